```python
import math
import jax
import jax.numpy as jnp
from jax import lax
import numpy as np

D_MODEL = 2048
BATCH = 2
SEQ = 8192
DEPTH = 4

GRID_W = 64
CTX_LEN = 256
EPS = 1e-6
HEAD_DIM = 128
A_WIDTH = D_MODEL // 2
A_CH = 128
A_GROUPS = A_WIDTH // A_CH
CHUNK = 128
B_HEADS = (D_MODEL // 2) // HEAD_DIM
B_KV_HEADS = max(1, B_HEADS // 4)
B_GROUP = B_HEADS // B_KV_HEADS
B_Q_W = B_HEADS * HEAD_DIM
B_KV_W = B_KV_HEADS * HEAD_DIM
Q_BLOCK = 128
ROPE_THETA = 10000.0
AXIS_DIM = HEAD_DIM // 2
AB_SPLITS = (A_WIDTH, 2 * A_WIDTH, 3 * A_WIDTH, 3 * A_WIDTH + B_Q_W,
             3 * A_WIDTH + B_Q_W + B_KV_W, 3 * A_WIDTH + B_Q_W + 2 * B_KV_W)
AB_IN_W = 3 * A_WIDTH + 2 * B_Q_W + 2 * B_KV_W
AB_OUT_W = A_WIDTH + B_Q_W
DN_K_HEADS = D_MODEL // HEAD_DIM
DN_V_HEADS = 2 * DN_K_HEADS
DN_K_W = DN_K_HEADS * HEAD_DIM
DN_V_W = DN_V_HEADS * HEAD_DIM
DN_QKV_W = 2 * DN_K_W + DN_V_W
DN_IN_W = DN_QKV_W + DN_V_W + 4 * DN_V_HEADS
DN_CHUNK = 64
CONV_K = 5

kernel_name = 'hybrid_prefix_gmlp_gqa_gdn'

F32 = jnp.float32


def rms_norm(x, w):
    xf = x.astype(F32)
    y = xf * lax.rsqrt(jnp.mean(xf * xf, axis=-1, keepdims=True) + EPS)
    return (y * w.astype(F32)).astype(x.dtype)


def l2_norm(x):
    xf = x.astype(F32)
    return (xf * lax.rsqrt(jnp.sum(xf * xf, axis=-1, keepdims=True) + EPS)).astype(x.dtype)


def ada_mod(cond, w, b):
    m = jax.nn.silu(cond) @ w + b
    return jnp.split(m[..., None, :], 3, axis=-1)


def axial_rope_tables(rows):
    row = jnp.repeat(jnp.arange(rows), GRID_W).astype(F32)
    col = jnp.tile(jnp.arange(GRID_W), rows).astype(F32)
    freqs = ROPE_THETA ** (-jnp.arange(0, AXIS_DIM, 2, dtype=F32) / AXIS_DIM)
    ang_r = row[:, None] * freqs[None, :]
    ang_c = col[:, None] * freqs[None, :]
    ang = jnp.concatenate([ang_r, ang_r, ang_c, ang_c], axis=-1)
    return jnp.cos(ang), jnp.sin(ang)


def apply_rope(x, cos, sin):
    xr = x.reshape(x.shape[:-1] + (2, 2, AXIS_DIM // 2))
    rot = jnp.stack([-xr[..., 1, :], xr[..., 0, :]], axis=-2).reshape(x.shape)
    return x * cos[None, :, None, :].astype(x.dtype) + rot * sin[None, :, None, :].astype(x.dtype)


def block_attention(q, k, v):
    bn, lq = q.shape[:2]
    nb = lq // Q_BLOCK
    qb = q.reshape(bn, nb, Q_BLOCK, B_KV_HEADS, B_GROUP, HEAD_DIM).transpose(1, 0, 2, 3, 4, 5)

    def attend(qblk):
        s = jnp.einsum('bqkgd,bskd->bkgqs', qblk, k).astype(F32) * (HEAD_DIM ** -0.5)
        p = jax.nn.softmax(s, axis=-1).astype(v.dtype)
        return jnp.einsum('bkgqs,bskd->bqkgd', p, v)

    o = lax.map(attend, qb)
    return o.transpose(1, 0, 2, 3, 4, 5).reshape(bn, lq, B_Q_W)


def chunk_sgu(u, v, w_s, b_s):
    bn, l, _ = u.shape
    vg = v.astype(F32).reshape(bn, l // CHUNK, CHUNK, A_GROUPS, A_CH)
    mu = jnp.mean(vg, axis=-1, keepdims=True)
    var = jnp.mean(jnp.square(vg - mu), axis=-1, keepdims=True)
    vn = ((vg - mu) * lax.rsqrt(var + 1e-5)).astype(u.dtype)
    mixed = jnp.einsum('gij,bnjgc->bnigc', w_s, vn) + b_s.T[None, None, :, :, None]
    return u * mixed.reshape(bn, l, A_WIDTH)


def ab_project(h, w_in, qn_w, kn_w):
    bn, l, _ = h.shape
    u, v, ga, q, k, vv, gb = jnp.split(h @ w_in, AB_SPLITS, axis=-1)
    q = rms_norm(q.reshape(bn, l, B_HEADS, HEAD_DIM), qn_w)
    k = rms_norm(k.reshape(bn, l, B_KV_HEADS, HEAD_DIM), kn_w)
    vv = vv.reshape(bn, l, B_KV_HEADS, HEAD_DIM)
    return u, v, ga, q, k, vv, gb


def ab_mixer(h_lat, h_ctx, w_in, w_out, w_s, b_s, qn_w, kn_w, cos, sin, need_ctx_out):
    lu, lv, lga, lq, lk, lvv, lgb = ab_project(h_lat, w_in, qn_w, kn_w)
    cu, cv, cga, cq, ck, cvv, cgb = ab_project(h_ctx, w_in, qn_w, kn_w)
    lq = apply_rope(lq, cos, sin)
    lk = apply_rope(lk, cos, sin)
    k_all = jnp.concatenate([ck, lk], axis=1)
    v_all = jnp.concatenate([cvv, lvv], axis=1)

    def merge(u, v, ga, attn_o, gb):
        a_out = chunk_sgu(u, v, w_s, b_s) * jax.nn.silu(ga)
        b_out = attn_o * jax.nn.silu(gb)
        return jnp.concatenate([a_out, b_out], axis=-1) @ w_out

    lat = merge(lu, lv, lga, block_attention(lq, k_all, v_all), lgb)
    ctx_out = merge(cu, cv, cga, block_attention(cq, ck, cvv), cgb) if need_ctx_out else None
    return lat, ctx_out


def centred_conv(x, w):
    return lax.conv_general_dilated(
        x, w.astype(x.dtype)[:, None, :], window_strides=(1,),
        padding=[(CONV_K // 2, CONV_K // 2)], dimension_numbers=('NWC', 'WIO', 'NWC'),
        feature_group_count=x.shape[-1])


def gated_delta_rule(q, k, v, g, beta, state):
    bn, l, h, _ = q.shape
    dv = v.shape[-1]
    n = l // DN_CHUNK

    def chunks(t):
        return t.astype(F32).reshape(bn, n, DN_CHUNK, h, -1).transpose(1, 0, 3, 2, 4)

    q, k, v = chunks(q), chunks(k), chunks(v)
    g = chunks(g[..., None])[..., 0]
    beta = chunks(beta[..., None])[..., 0]
    gc = jnp.cumsum(g, axis=-1)
    tril = jnp.tril(jnp.ones((DN_CHUNK, DN_CHUNK), dtype=bool))
    strict = jnp.tril(jnp.ones((DN_CHUNK, DN_CHUNK), dtype=bool), -1)
    decay = jnp.exp(jnp.where(tril, gc[..., :, None] - gc[..., None, :], -jnp.inf))
    kb = k * beta[..., None]
    lmat = jnp.where(strict, jnp.einsum('nbhid,nbhjd->nbhij', kb, k) * decay, 0.0)
    tmat = lmat + jnp.eye(DN_CHUNK, dtype=F32)
    rhs = jnp.concatenate([v * beta[..., None], kb * jnp.exp(gc)[..., None]], axis=-1)
    sol = lax.linalg.triangular_solve(tmat, rhs, left_side=True, lower=True, unit_diagonal=True)
    u_c, w_c = sol[..., :dv], sol[..., dv:]
    attn = jnp.einsum('nbhid,nbhjd->nbhij', q, k) * decay
    qg = q * jnp.exp(gc)[..., None]
    kd = k * jnp.exp(gc[..., -1:] - gc)[..., None]
    gl = jnp.exp(gc[..., -1])[..., None, None]

    def step(s, xs):
        attn_n, u_n, w_n, qg_n, kd_n, gl_n = xs
        v_new = u_n - jnp.einsum('bhcd,bhde->bhce', w_n, s)
        o = jnp.einsum('bhcd,bhde->bhce', qg_n, s) + jnp.einsum('bhij,bhje->bhie', attn_n, v_new)
        s = s * gl_n + jnp.einsum('bhcd,bhce->bhde', kd_n, v_new)
        return s, o

    state, o = lax.scan(step, state.astype(F32), (attn, u_c, w_c, qg, kd, gl))
    return o.transpose(1, 0, 3, 2, 4).reshape(bn, l, h, dv), state


def dn_prepare(h, w_in, conv_w, a_log, dt_bias):
    bn, l, _ = h.shape
    qkv, z, ba = jnp.split(h @ w_in, (DN_QKV_W, DN_QKV_W + DN_V_W), axis=-1)
    qkv = jax.nn.silu(centred_conv(qkv, conv_w))
    q, k, v = jnp.split(qkv, (DN_K_W, 2 * DN_K_W), axis=-1)
    rep = DN_V_HEADS // DN_K_HEADS
    q = l2_norm(q.reshape(bn, l, DN_K_HEADS, HEAD_DIM)) * (HEAD_DIM ** -0.5)
    k = l2_norm(k.reshape(bn, l, DN_K_HEADS, HEAD_DIM))
    q = jnp.repeat(q, rep, axis=2)
    k = jnp.repeat(k, rep, axis=2)
    v = v.reshape(bn, l, DN_V_HEADS, HEAD_DIM)
    ba = ba.astype(F32).reshape(bn, l, 2, 2, DN_V_HEADS)
    beta = jax.nn.sigmoid(ba[:, :, :, 0])
    g = -jnp.exp(a_log.astype(F32)) * jax.nn.softplus(ba[:, :, :, 1] + dt_bias.astype(F32))
    return q, k, v, z, beta, g


def dn_bidir(q, k, v, beta, g, s_fwd, s_bwd):
    o_f, s_f = gated_delta_rule(q, k, v, g[:, :, 0], beta[:, :, 0], s_fwd)
    fl = lambda t: jnp.flip(t, axis=1)
    o_b, s_b = gated_delta_rule(fl(q), fl(k), fl(v), fl(g[:, :, 1]), fl(beta[:, :, 1]), s_bwd)
    return o_f + fl(o_b), s_f, s_b


def dn_out(o, z, norm_w, w_out):
    bn, l, _ = z.shape
    zh = z.reshape(bn, l, DN_V_HEADS, HEAD_DIM)
    y = rms_norm(o.astype(z.dtype), norm_w) * jax.nn.silu(zh)
    return y.reshape(bn, l, DN_V_W) @ w_out


def dn_mixer(h_lat, h_ctx, w_in, conv_w, a_log, dt_bias, norm_w, w_out, need_ctx_out):
    cq, ck, cv, cz, cbeta, cg = dn_prepare(h_ctx, w_in, conv_w, a_log, dt_bias)
    zero = jnp.zeros((h_ctx.shape[0], DN_V_HEADS, HEAD_DIM, HEAD_DIM), F32)
    co, cs_f, cs_b = dn_bidir(cq, ck, cv, cbeta, cg, zero, zero)
    lq, lk, lv, lz, lbeta, lg = dn_prepare(h_lat, w_in, conv_w, a_log, dt_bias)
    lo, _, _ = dn_bidir(lq, lk, lv, lbeta, lg, cs_f, cs_b)
    lat = dn_out(lo, lz, norm_w, w_out)
    ctx_out = dn_out(co, cz, norm_w, w_out) if need_ctx_out else None
    return lat, ctx_out


def setup_inputs(seed: int = 0) -> dict:
    key = jax.random.key(seed)
    ks = jax.random.split(key, 20)
    n_even = (DEPTH + 1) // 2
    n_odd = DEPTH // 2

    def nrm(k, shape, scale):
        return jax.random.normal(k, shape, F32) * scale

    dt = jnp.exp(jax.random.uniform(ks[16], (n_odd, 2, DN_V_HEADS), F32,
                                    math.log(1e-3), math.log(1e-1)))
    return {
        'x': nrm(ks[0], (BATCH, SEQ, D_MODEL), 1.0),
        'c': nrm(ks[1], (BATCH, D_MODEL), 1.0),
        'ctx': nrm(ks[2], (BATCH, CTX_LEN, D_MODEL), 1.0),
        'c_ctx': nrm(ks[3], (D_MODEL,), 1.0),
        'norm_w': 1.0 + nrm(ks[4], (DEPTH, D_MODEL), 0.02),
        'ada_w': nrm(ks[5], (DEPTH, D_MODEL, 3 * D_MODEL), 0.5 * D_MODEL ** -0.5),
        'ada_b': nrm(ks[6], (DEPTH, 3 * D_MODEL), 0.02),
        'ab_w_in': nrm(ks[7], (n_even, D_MODEL, AB_IN_W), D_MODEL ** -0.5),
        'ab_w_out': nrm(ks[8], (n_even, AB_OUT_W, D_MODEL), AB_OUT_W ** -0.5),
        'sgu_w': nrm(ks[9], (n_even, A_GROUPS, CHUNK, CHUNK), CHUNK ** -0.5),
        'sgu_b': nrm(ks[10], (n_even, A_GROUPS, CHUNK), 0.02),
        'q_norm_w': 1.0 + nrm(ks[11], (n_even, HEAD_DIM), 0.02),
        'k_norm_w': 1.0 + nrm(ks[12], (n_even, HEAD_DIM), 0.02),
        'dn_w_in': nrm(ks[13], (n_odd, D_MODEL, DN_IN_W), D_MODEL ** -0.5),
        'dn_conv_w': nrm(ks[14], (n_odd, CONV_K, DN_QKV_W), CONV_K ** -0.5),
        'dn_a_log': jnp.log(jax.random.uniform(ks[15], (n_odd, 2, DN_V_HEADS), F32, 1.0, 16.0)),
        'dn_dt_bias': dt + jnp.log(-jnp.expm1(-dt)),
        'dn_norm_w': 1.0 + nrm(ks[17], (n_odd, HEAD_DIM), 0.02),
        'dn_w_out': nrm(ks[18], (n_odd, DN_V_W, D_MODEL), DN_V_W ** -0.5),
        'final_norm_w': 1.0 + nrm(ks[19], (D_MODEL,), 0.02),
    }


def reference(x, c, ctx, c_ctx, norm_w, ada_w, ada_b, ab_w_in, ab_w_out, sgu_w, sgu_b,
              q_norm_w, k_norm_w, dn_w_in, dn_conv_w, dn_a_log, dn_dt_bias, dn_norm_w,
              dn_w_out, final_norm_w):
    rows = x.shape[1] // GRID_W
    cos, sin = axial_rope_tables(rows)
    for i in range(DEPTH):
        need_ctx_out = i < DEPTH - 1
        shift, scale, gate = ada_mod(c, ada_w[i], ada_b[i])
        c_shift, c_scale, c_gate = ada_mod(c_ctx, ada_w[i], ada_b[i])
        h = rms_norm(x, norm_w[i]) * (1.0 + scale) + shift
        hc = rms_norm(ctx, norm_w[i]) * (1.0 + c_scale) + c_shift
        j = i // 2
        if i % 2 == 0:
            o, oc = ab_mixer(h, hc, ab_w_in[j], ab_w_out[j], sgu_w[j], sgu_b[j],
                             q_norm_w[j], k_norm_w[j], cos, sin, need_ctx_out)
        else:
            o, oc = dn_mixer(h, hc, dn_w_in[j], dn_conv_w[j], dn_a_log[j], dn_dt_bias[j],
                             dn_norm_w[j], dn_w_out[j], need_ctx_out)
        x = x + gate * o
        if need_ctx_out:
            ctx = ctx + c_gate * oc
    return rms_norm(x, final_norm_w)
```

```python
import functools

import jax
import jax.numpy as jnp
from jax import lax
from jax.experimental import pallas as pl
from jax.experimental.pallas import tpu as pltpu

F32 = jnp.float32
BF16 = jnp.bfloat16

D_MODEL = 2048
GRID_W = 64
EPS = 1e-6
HEAD_DIM = 128
A_WIDTH = D_MODEL // 2
A_GROUPS = A_WIDTH // 128
SGU_CHUNK = 128
B_HEADS = (D_MODEL // 2) // HEAD_DIM
B_KV_HEADS = B_HEADS // 4
B_GROUP = B_HEADS // B_KV_HEADS
B_Q_W = B_HEADS * HEAD_DIM
B_KV_W = B_KV_HEADS * HEAD_DIM
ROPE_THETA = 10000.0
AXIS_DIM = HEAD_DIM // 2
AB_IN_W = 3 * A_WIDTH + 2 * B_Q_W + 2 * B_KV_W
DN_K_HEADS = D_MODEL // HEAD_DIM
DN_V_HEADS = 2 * DN_K_HEADS
DN_K_W = DN_K_HEADS * HEAD_DIM
DN_V_W = DN_V_HEADS * HEAD_DIM
DN_QKV_W = 2 * DN_K_W + DN_V_W
DN_IN_W = DN_QKV_W + DN_V_W + 4 * DN_V_HEADS
DN_CHUNK = 64
CONV_K = 5

V7X_VMEM_LIMIT_BYTES = 56 * 1024 * 1024
LANES = 128
SUBLANES = 8

ROW_TILE = 512
CONV_ROW_TILE = 256


def _cparams(n_axes):
    return pltpu.CompilerParams(dimension_semantics=("arbitrary",) * n_axes,
                                vmem_limit_bytes=V7X_VMEM_LIMIT_BYTES)


def _silu(x):
    return x * jax.nn.sigmoid(x)


def _split_bf16(a):
    hi = a.astype(BF16)
    lo = (a - hi.astype(F32)).astype(BF16)
    return hi, lo


def _dot(a, b):
    return jnp.dot(a, b, preferred_element_type=F32)


def _dot_nt(a, b):
    return lax.dot_general(a, b, (((1,), (1,)), ((), ())), preferred_element_type=F32)


def _dot3(a, b):
    ah, al = _split_bf16(a)
    bh, bl = _split_bf16(b)
    return _dot(ah, bh) + _dot(ah, bl) + _dot(al, bh)


def _ada_kernel(c_ref, w_ref, b_ref, o_ref):
    s = _silu(c_ref[...])
    o_ref[0] = _dot3(s, w_ref[0]) + b_ref[0]


def _ada_mod(cond, ada_w, ada_b):
    depth = ada_w.shape[0]
    tn = 512
    return pl.pallas_call(
        _ada_kernel,
        grid=(depth, 3 * D_MODEL // tn),
        in_specs=[pl.BlockSpec((SUBLANES, D_MODEL), lambda l, j: (0, 0)),
                  pl.BlockSpec((1, D_MODEL, tn), lambda l, j: (l, 0, j)),
                  pl.BlockSpec((1, 1, tn), lambda l, j: (l, 0, j))],
        out_specs=pl.BlockSpec((1, SUBLANES, tn), lambda l, j: (l, 0, j)),
        out_shape=jax.ShapeDtypeStruct((depth, SUBLANES, 3 * D_MODEL), F32),
        compiler_params=_cparams(2),
        name="ada_mod",
    )(cond, ada_w, ada_b.reshape(depth, 1, 3 * D_MODEL))


def _inproj_kernel(x_ref, nw_ref, mod_ref, w_ref, *rest):
    h_ref = rest[-1]
    has_side = len(rest) == 4

    @pl.when(pl.program_id(1) == 0)
    def _():
        x = x_ref[...]
        y = x * lax.rsqrt(jnp.mean(x * x, axis=-1, keepdims=True) + EPS) * nw_ref[...]
        shift = mod_ref[:, 0:D_MODEL]
        scale = mod_ref[:, D_MODEL:2 * D_MODEL]
        h_ref[...] = (y * (1.0 + scale) + shift).astype(BF16)
        if has_side:
            rest[2][...] = _dot(h_ref[...], rest[0][...])

    o_ref = rest[1] if has_side else rest[0]
    o_ref[...] = _dot(h_ref[...], w_ref[...]).astype(o_ref.dtype)


def _mod_row(i, lat_tiles, batch):
    return jnp.minimum(i // lat_tiles, batch)


def _inproj(x, norm_w, mod, w_bf16, batch, seq, w_side_bf16=None, out_dtype=F32):
    m = x.shape[0]
    n = w_bf16.shape[1]
    tm, tn = ROW_TILE, 512
    assert n % tn == 0
    lat_tiles = seq // tm
    in_specs = [pl.BlockSpec((tm, D_MODEL), lambda i, j: (i, 0)),
                pl.BlockSpec((1, D_MODEL), lambda i, j: (0, 0)),
                pl.BlockSpec((None, 1, 3 * D_MODEL), lambda i, j: (_mod_row(i, lat_tiles, batch), 0, 0)),
                pl.BlockSpec((D_MODEL, tn), lambda i, j: (0, j))]
    out_specs = [pl.BlockSpec((tm, tn), lambda i, j: (i, j))]
    out_shape = [jax.ShapeDtypeStruct((m, n), out_dtype)]
    args = [x, norm_w.reshape(1, D_MODEL), mod, w_bf16]
    if w_side_bf16 is not None:
        ns = w_side_bf16.shape[1]
        in_specs.append(pl.BlockSpec((D_MODEL, ns), lambda i, j: (0, 0)))
        out_specs.append(pl.BlockSpec((tm, ns), lambda i, j: (i, 0)))
        out_shape.append(jax.ShapeDtypeStruct((m, ns), F32))
        args.append(w_side_bf16)
    res = pl.pallas_call(
        _inproj_kernel,
        grid=(m // tm, n // tn),
        in_specs=in_specs,
        out_specs=out_specs,
        out_shape=out_shape,
        scratch_shapes=[pltpu.VMEM((tm, D_MODEL), BF16)],
        compiler_params=_cparams(2),
        name="inproj",
    )(*args)
    return res if w_side_bf16 is not None else res[0]


def _qkprep_kernel(q_ref, k_ref, v_ref, cos_ref, sin_ref, qw_ref, kw_ref, qo_ref, ko_ref, vo_ref):
    cos = cos_ref[...]
    sin = sin_ref[...]
    lane = lax.broadcasted_iota(jnp.int32, cos.shape, 1)
    first = (lane % (AXIS_DIM)) < (AXIS_DIM // 2)

    def prep(x, w):
        y = x * lax.rsqrt(jnp.mean(x * x, axis=-1, keepdims=True) + EPS) * w
        rot = jnp.where(first, pltpu.roll(y, HEAD_DIM - AXIS_DIM // 2, 1), pltpu.roll(y, AXIS_DIM // 2, 1))
        return y * cos + rot * sin

    for h in range(B_HEADS):
        sl = slice(h * HEAD_DIM, (h + 1) * HEAD_DIM)
        qo_ref[:, sl] = prep(q_ref[:, sl], qw_ref[...]).astype(qo_ref.dtype)
    for h in range(B_KV_HEADS):
        sl = slice(h * HEAD_DIM, (h + 1) * HEAD_DIM)
        ko_ref[:, sl] = prep(k_ref[:, sl], kw_ref[...]).astype(ko_ref.dtype)
    vo_ref[...] = v_ref[...].astype(vo_ref.dtype)


def _qkprep(proj, cos_tab, sin_tab, qn_w, kn_w, seq):
    m = proj.shape[0]
    tm = ROW_TILE
    lat_tiles = seq // tm
    q_blk = (4 * A_WIDTH) // B_Q_W
    k_blk = (4 * A_WIDTH + B_Q_W) // B_KV_W

    def tab_idx(i):
        return (jnp.where(i < 2 * lat_tiles, i % lat_tiles, lat_tiles), 0)

    return pl.pallas_call(
        _qkprep_kernel,
        grid=(m // tm,),
        in_specs=[pl.BlockSpec((tm, B_Q_W), lambda i: (i, q_blk)),
                  pl.BlockSpec((tm, B_KV_W), lambda i: (i, k_blk)),
                  pl.BlockSpec((tm, B_KV_W), lambda i: (i, k_blk + 1)),
                  pl.BlockSpec((tm, HEAD_DIM), tab_idx),
                  pl.BlockSpec((tm, HEAD_DIM), tab_idx),
                  pl.BlockSpec((1, HEAD_DIM), lambda i: (0, 0)),
                  pl.BlockSpec((1, HEAD_DIM), lambda i: (0, 0))],
        out_specs=[pl.BlockSpec((tm, B_Q_W), lambda i: (i, 0)),
                   pl.BlockSpec((tm, B_KV_W), lambda i: (i, 0)),
                   pl.BlockSpec((tm, B_KV_W), lambda i: (i, 0))],
        out_shape=[jax.ShapeDtypeStruct((m, B_Q_W), BF16),
                   jax.ShapeDtypeStruct((m, B_KV_W), BF16),
                   jax.ShapeDtypeStruct((m, B_KV_W), BF16)],
        compiler_params=_cparams(1),
        name="qk_prep",
    )(proj, proj, proj, cos_tab, sin_tab, qn_w.reshape(1, HEAD_DIM), kn_w.reshape(1, HEAD_DIM))


def _rope_tables(seq, tm):
    rows = seq // GRID_W
    row = jnp.repeat(jnp.arange(rows), GRID_W).astype(F32)
    col = jnp.tile(jnp.arange(GRID_W), rows).astype(F32)
    freqs = ROPE_THETA ** (-jnp.arange(0, AXIS_DIM, 2, dtype=F32) / AXIS_DIM)
    ang_r = row[:, None] * freqs[None, :]
    ang_c = col[:, None] * freqs[None, :]
    ang = jnp.concatenate([ang_r, ang_r, ang_c, ang_c], axis=-1)
    sign = jnp.where((jnp.arange(HEAD_DIM) % AXIS_DIM) < AXIS_DIM // 2, -1.0, 1.0).astype(F32)
    cos = jnp.concatenate([jnp.cos(ang), jnp.ones((tm, HEAD_DIM), F32)], axis=0)
    sin = jnp.concatenate([jnp.sin(ang) * sign[None, :], jnp.zeros((tm, HEAD_DIM), F32)], axis=0)
    return cos, sin


def _attn_kernel(*refs, seg_chunks, tq):
    n_seg = len(seg_chunks)
    q_ref = refs[0]
    kv_refs = refs[1:1 + 2 * n_seg]
    o_ref = refs[1 + 2 * n_seg]
    m_scr, l_scr, acc_scr = refs[2 + 2 * n_seg:]
    q = jnp.concatenate([q_ref[:, g * HEAD_DIM:(g + 1) * HEAD_DIM] for g in range(B_GROUP)], axis=0)
    m_scr[...] = jnp.full(m_scr.shape, -jnp.inf, F32)
    l_scr[...] = jnp.zeros(l_scr.shape, F32)
    acc_scr[...] = jnp.zeros(acc_scr.shape, F32)

    def step(k, v):
        tk = k.shape[0]
        s = _dot_nt(q, k) * (HEAD_DIM ** -0.5)
        m_prev = m_scr[...]
        m_next = jnp.maximum(m_prev, jnp.max(s, axis=1, keepdims=True))
        p = jnp.exp(s - jnp.concatenate([m_next] * (tk // LANES), axis=1))
        alpha = jnp.exp(m_prev - m_next)
        l_scr[...] = alpha * l_scr[...] + jnp.sum(p, axis=1, keepdims=True)
        acc_scr[...] = acc_scr[...] * alpha + _dot(p.astype(BF16), v)
        m_scr[...] = m_next

    for s_i, (n_chunks, tk) in enumerate(seg_chunks):
        k_ref, v_ref = kv_refs[2 * s_i], kv_refs[2 * s_i + 1]
        if n_chunks == 1:
            step(k_ref[...], v_ref[...])
        else:
            def body(c, carry, k_ref=k_ref, v_ref=v_ref, tk=tk):
                r0 = pl.multiple_of(c * tk, tk)
                step(k_ref[pl.ds(r0, tk), :], v_ref[pl.ds(r0, tk), :])
                return carry
            lax.fori_loop(0, n_chunks, body, 0)

    out = acc_scr[...] / l_scr[...]
    for g in range(B_GROUP):
        o_ref[:, g * HEAD_DIM:(g + 1) * HEAD_DIM] = out[g * tq:(g + 1) * tq].astype(o_ref.dtype)


def _attention(qr, kr, vr, batch, seq, ctx_len, latent, out):
    tq = 128
    tk = 512
    gw = B_GROUP * HEAD_DIM
    ctx_blk0 = batch * seq // ctx_len
    q_tiles = (seq if latent else ctx_len) // tq
    q_row0 = 0 if latent else batch * seq // tq

    def q_idx(b, h, i):
        return (q_row0 + b * q_tiles + i, h)

    in_specs = [pl.BlockSpec((tq, gw), q_idx),
                pl.BlockSpec((ctx_len, HEAD_DIM), lambda b, h, i: (ctx_blk0 + b, h)),
                pl.BlockSpec((ctx_len, HEAD_DIM), lambda b, h, i: (ctx_blk0 + b, h))]
    args = [qr, kr, vr]
    seg_chunks = [(1, ctx_len)]
    if latent:
        in_specs += [pl.BlockSpec((seq, HEAD_DIM), lambda b, h, i: (b, h)),
                     pl.BlockSpec((seq, HEAD_DIM), lambda b, h, i: (b, h))]
        args += [kr, vr]
        seg_chunks.append((seq // tk, tk))
    n_in = len(args)
    aliases = {}
    if out is not None:
        in_specs.append(pl.BlockSpec(memory_space=pl.ANY))
        args.append(out)
        aliases = {n_in: 0}
    rows = B_GROUP * tq

    def kern(*refs):
        _attn_kernel(*refs[:n_in], *refs[len(args):], seg_chunks=tuple(seg_chunks), tq=tq)

    return pl.pallas_call(
        kern,
        grid=(batch, B_KV_HEADS, q_tiles),
        in_specs=in_specs,
        out_specs=pl.BlockSpec((tq, gw), q_idx),
        out_shape=jax.ShapeDtypeStruct((qr.shape[0], B_Q_W), F32),
        scratch_shapes=[pltpu.VMEM((rows, LANES), F32), pltpu.VMEM((rows, LANES), F32),
                        pltpu.VMEM((rows, HEAD_DIM), F32)],
        input_output_aliases=aliases,
        compiler_params=_cparams(3),
        name="attn_lat" if latent else "attn_ctx",
    )(*args)


def _ab_out_kernel(u_ref, v_ref, ga_ref, gb_ref, ao_ref, ws_ref, bs_ref, w_ref, x_ref, gate_ref, o_ref, y_ref):
    @pl.when(pl.program_id(1) == 0)
    def _():
        tm = u_ref.shape[0]
        for c in range(tm // SGU_CHUNK):
            rows = slice(c * SGU_CHUNK, (c + 1) * SGU_CHUNK)
            for g in range(A_GROUPS):
                cols = slice(g * LANES, (g + 1) * LANES)
                vg = v_ref[rows, cols]
                d = vg - jnp.mean(vg, axis=-1, keepdims=True)
                var = jnp.mean(d * d, axis=-1, keepdims=True)
                vn = (d * lax.rsqrt(var + 1e-5)).astype(BF16)
                mixed = _dot(ws_ref[g], vn) + bs_ref[:, g:g + 1]
                y_ref[rows, cols] = (u_ref[rows, cols] * mixed * _silu(ga_ref[rows, cols])).astype(BF16)
        y_ref[:, A_WIDTH:] = (ao_ref[...].astype(F32) * _silu(gb_ref[...])).astype(BF16)

    o_ref[...] = x_ref[...] + gate_ref[...] * _dot(y_ref[...], w_ref[...])


def _ab_out(proj, attn_o, sgu_w_bf16, sgu_b_t, w_out_bf16, x, mod, batch, seq):
    m = x.shape[0]
    tm, tn = ROW_TILE, 512
    lat_tiles = seq // tm
    gate_blk0 = 2 * D_MODEL // tn
    return pl.pallas_call(
        _ab_out_kernel,
        grid=(m // tm, D_MODEL // tn),
        in_specs=[pl.BlockSpec((tm, A_WIDTH), lambda i, j: (i, 0)),
                  pl.BlockSpec((tm, A_WIDTH), lambda i, j: (i, 1)),
                  pl.BlockSpec((tm, A_WIDTH), lambda i, j: (i, 2)),
                  pl.BlockSpec((tm, B_Q_W), lambda i, j: (i, 3)),
                  pl.BlockSpec((tm, B_Q_W), lambda i, j: (i, 0)),
                  pl.BlockSpec((A_GROUPS, SGU_CHUNK, SGU_CHUNK), lambda i, j: (0, 0, 0)),
                  pl.BlockSpec((SGU_CHUNK, A_GROUPS), lambda i, j: (0, 0)),
                  pl.BlockSpec((A_WIDTH + B_Q_W, tn), lambda i, j: (0, j)),
                  pl.BlockSpec((tm, tn), lambda i, j: (i, j)),
                  pl.BlockSpec((None, 1, tn), lambda i, j: (_mod_row(i, lat_tiles, batch), 0, gate_blk0 + j))],
        out_specs=pl.BlockSpec((tm, tn), lambda i, j: (i, j)),
        out_shape=jax.ShapeDtypeStruct((m, D_MODEL), F32),
        scratch_shapes=[pltpu.VMEM((tm, A_WIDTH + B_Q_W), BF16)],
        compiler_params=_cparams(2),
        name="ab_out",
    )(proj, proj, proj, proj, attn_o, sgu_w_bf16, sgu_b_t, w_out_bf16, x, mod)


def _dnconv_kernel(xp_ref, xc_ref, xn_ref, w_ref, o_ref, xe_ref, *, lat_tiles, n_lat, q_tiles, k_tiles):
    i = pl.program_id(0)
    j = pl.program_id(1)
    tm = xc_ref.shape[0]
    pos = i % lat_tiles
    is_lat = i < n_lat
    first = jnp.logical_or(jnp.logical_not(is_lat), pos == 0)
    last = jnp.logical_or(jnp.logical_not(is_lat), pos == lat_tiles - 1)
    xe_ref[0:SUBLANES, :] = jnp.where(first, 0.0, xp_ref[...])
    xe_ref[SUBLANES:SUBLANES + tm, :] = xc_ref[...]
    xe_ref[SUBLANES + tm:2 * SUBLANES + tm, :] = jnp.where(last, 0.0, xn_ref[...])
    half = CONV_K // 2
    acc = w_ref[0:1, :] * xe_ref[SUBLANES - half:SUBLANES - half + tm, :]
    for t in range(1, CONV_K):
        acc = acc + w_ref[t:t + 1, :] * xe_ref[SUBLANES - half + t:SUBLANES - half + t + tm, :]
    y = _silu(acc)

    def l2n(scale):
        for h in range(y.shape[1] // HEAD_DIM):
            sl = slice(h * HEAD_DIM, (h + 1) * HEAD_DIM)
            yh = y[:, sl]
            n = yh * lax.rsqrt(jnp.sum(yh * yh, axis=-1, keepdims=True) + EPS)
            o_ref[:, sl] = (n * scale).astype(o_ref.dtype) if scale != 1.0 else n.astype(o_ref.dtype)

    @pl.when(j < q_tiles)
    def _():
        l2n(HEAD_DIM ** -0.5)

    @pl.when(jnp.logical_and(j >= q_tiles, j < q_tiles + k_tiles))
    def _():
        l2n(1.0)

    @pl.when(j >= q_tiles + k_tiles)
    def _():
        o_ref[...] = y.astype(o_ref.dtype)


def _dn_conv(proj, conv_w, batch, seq, out_dtype=F32):
    m = proj.shape[0]
    tm, tc = CONV_ROW_TILE, 1024
    lat_tiles = seq // tm
    n_lat = batch * lat_tiles
    sub_per_tile = tm // SUBLANES
    n_sub = m // SUBLANES
    kern = functools.partial(_dnconv_kernel, lat_tiles=lat_tiles, n_lat=n_lat,
                             q_tiles=DN_K_W // tc, k_tiles=DN_K_W // tc)
    return pl.pallas_call(
        kern,
        grid=(m // tm, DN_QKV_W // tc),
        in_specs=[pl.BlockSpec((SUBLANES, tc), lambda i, j: (jnp.maximum(i * sub_per_tile - 1, 0), j)),
                  pl.BlockSpec((tm, tc), lambda i, j: (i, j)),
                  pl.BlockSpec((SUBLANES, tc), lambda i, j: (jnp.minimum((i + 1) * sub_per_tile, n_sub - 1), j)),
                  pl.BlockSpec((SUBLANES, tc), lambda i, j: (0, j))],
        out_specs=pl.BlockSpec((tm, tc), lambda i, j: (i, j)),
        out_shape=jax.ShapeDtypeStruct((m, DN_QKV_W), out_dtype),
        scratch_shapes=[pltpu.VMEM((tm + 2 * SUBLANES, tc), F32)],
        compiler_params=_cparams(2),
        name="dn_conv",
    )(proj, proj, proj, conv_w)


def _dngate_kernel(ba_ref, alog_ref, dtb_ref, o_ref):
    ba = ba_ref[...]
    tm = ba.shape[0]
    lane = lax.broadcasted_iota(jnp.int32, (DN_CHUNK, LANES), 1)
    is_beta = (lane // DN_V_HEADS) % 2 == 0
    is_fwd = lane < 2 * DN_V_HEADS
    z = ba + dtb_ref[...]
    softplus = jnp.maximum(z, 0.0) + jnp.log1p(jnp.exp(-jnp.abs(z)))
    g = -jnp.exp(alog_ref[...]) * softplus
    r = lax.broadcasted_iota(jnp.int32, (DN_CHUNK, DN_CHUNK), 0)
    c = lax.broadcasted_iota(jnp.int32, (DN_CHUNK, DN_CHUNK), 1)
    tri_lo = (r >= c).astype(BF16)
    tri_up = (r <= c).astype(BF16)
    beta = jax.nn.sigmoid(ba)
    for ch in range(tm // DN_CHUNK):
        rows = slice(ch * DN_CHUNK, (ch + 1) * DN_CHUNK)
        gch = g[rows]
        g1 = gch.astype(BF16)
        r1 = gch - g1.astype(F32)
        g2 = r1.astype(BF16)
        g3 = (r1 - g2.astype(F32)).astype(BF16)
        pre = _dot(tri_lo, g1) + _dot(tri_lo, g2) + _dot(tri_lo, g3)
        suf = _dot(tri_up, g1) + _dot(tri_up, g2) + _dot(tri_up, g3)
        gc = jnp.where(is_fwd, pre, suf)
        o_ref[rows, :] = jnp.where(is_beta, beta[rows], gc)


def _dn_gate(ba, alog_vec, dtb_vec):
    m = ba.shape[0]
    tm = ROW_TILE
    return pl.pallas_call(
        _dngate_kernel,
        grid=(m // tm,),
        in_specs=[pl.BlockSpec((tm, LANES), lambda i: (i, 0)),
                  pl.BlockSpec((1, LANES), lambda i: (0, 0)),
                  pl.BlockSpec((1, LANES), lambda i: (0, 0))],
        out_specs=pl.BlockSpec((tm, LANES), lambda i: (i, 0)),
        out_shape=jax.ShapeDtypeStruct((m, LANES), F32),
        compiler_params=_cparams(1),
        name="dn_gate",
    )(ba, alog_vec, dtb_vec)


DN_KH_PER_STEP = 4


def _dncore_kernel(qf_ref, kf_ref, vf_ref, gf_ref, qb_ref, kb_ref, vb_ref, gb_ref, of_ref, ob_ref, s_ref):
    C = DN_CHUNK
    nvh = 2 * DN_KH_PER_STEP

    @pl.when(pl.program_id(2) == 0)
    def _():
        s_ref[...] = jnp.zeros(s_ref.shape, F32)

    lane = lax.broadcasted_iota(jnp.int32, (C, 2 * C), 1)
    left = lane < C
    row = lax.broadcasted_iota(jnp.int32, (C, 2 * C), 0)
    colp = lane % C
    left_sq = lax.broadcasted_iota(jnp.int32, (2 * C, 2 * C), 1) < C
    eye2 = (row == colp).astype(F32)

    def blockdiag(p):
        z = jnp.zeros_like(p)
        return jnp.concatenate([jnp.where(left, p, z), jnp.where(left, z, p)], axis=0)

    def packed_mm(a_hi, a_lo, b_hi, b_lo):
        bh, bl = blockdiag(b_hi), blockdiag(b_lo)
        return _dot(a_hi, bh) + _dot(a_hi, bl) + _dot(a_lo, bh)

    for d, (q_ref, k_ref, v_ref, g_ref, o_ref) in enumerate(
            ((qf_ref, kf_ref, vf_ref, gf_ref, of_ref), (qb_ref, kb_ref, vb_ref, gb_ref, ob_ref))):
        G = g_ref[...]
        GT = jnp.concatenate([G, G], axis=0).T
        base_beta = d * 2 * nvh
        base_gc = base_beta + nvh
        incl = (row >= colp) if d == 0 else (row <= colp)
        strict = (row > colp) if d == 0 else (row < colp)
        last = C - 1 if d == 0 else 0
        for kh in range(DN_KH_PER_STEP):
            q = q_ref[:, kh * HEAD_DIM:(kh + 1) * HEAD_DIM]
            k = k_ref[:, kh * HEAD_DIM:(kh + 1) * HEAD_DIM]
            k2 = jnp.concatenate([k, k], axis=0)
            kT2 = k2.T
            qk2 = jnp.concatenate([q, k], axis=0).astype(BF16)
            gram = _dot(qk2, kT2.astype(BF16))
            lv0 = 2 * kh

            def col(idx):
                return G[:, idx:idx + 1]

            def rowvec(idx):
                return GT[idx:idx + 1, :]

            b0, b1 = col(base_beta + lv0), col(base_beta + lv0 + 1)
            c0, c1 = col(base_gc + lv0), col(base_gc + lv0 + 1)
            r0, r1 = rowvec(base_gc + lv0), rowvec(base_gc + lv0 + 1)
            gcol_p = jnp.where(left, c0, c1)
            grow_p = jnp.where(left[0:1], r0, r1)
            beta_p = jnp.where(left, b0, b1)
            dec = jnp.exp(jnp.where(incl, gcol_p - grow_p, -1e30))
            attn_p = gram[0:C] * dec
            L = jnp.where(strict, gram[C:2 * C] * dec, 0.0) * beta_p
            X = eye2 - L
            p_hi, p_lo = _split_bf16(L)
            P = packed_mm(p_hi, p_lo, p_hi, p_lo)
            n_sq = 5
            for it in range(n_sq):
                p_hi, p_lo = _split_bf16(P)
                x_hi, x_lo = _split_bf16(X)
                X = X + packed_mm(x_hi, x_lo, p_hi, p_lo)
                if it + 1 < n_sq:
                    P = packed_mm(p_hi, p_lo, p_hi, p_lo)
            beta_r = jnp.concatenate([b0, b1], axis=0)
            egc_r = jnp.exp(jnp.concatenate([c0, c1], axis=0))
            v2 = jnp.concatenate([v_ref[:, (lv0) * HEAD_DIM:(lv0 + 1) * HEAD_DIM],
                                  v_ref[:, (lv0 + 1) * HEAD_DIM:(lv0 + 2) * HEAD_DIM]], axis=0)
            rhs = jnp.concatenate([v2 * beta_r, k2 * (beta_r * egc_r)], axis=1)
            sol = _dot(blockdiag(X).astype(BF16), rhs.astype(BF16))
            u2, w2 = sol[:, 0:HEAD_DIM], sol[:, HEAD_DIM:]
            q2 = jnp.concatenate([q, q], axis=0)
            qg2 = q2 * egc_r
            kdT_p = kT2 * dec[last:last + 1, :]
            vnew = []
            ws_bot = []
            for r in range(2):
                S = s_ref[d, lv0 + r]
                lhs = jnp.concatenate([w2[r * C:(r + 1) * C], qg2[r * C:(r + 1) * C]], axis=0).astype(BF16)
                ws = _dot(lhs, S.astype(BF16))
                vnew.append(u2[r * C:(r + 1) * C] - ws[0:C])
                ws_bot.append(ws[C:2 * C])
            vn2 = jnp.concatenate(vnew, axis=0).astype(BF16)
            o2 = jnp.concatenate(ws_bot, axis=0) + _dot(blockdiag(attn_p).astype(BF16), vn2)
            zkd = jnp.zeros_like(kdT_p)
            for r in range(2):
                kd_r = jnp.where(left_sq if r == 0 else jnp.logical_not(left_sq), kdT_p, zkd).astype(BF16)
                glr = jnp.broadcast_to(egc_r[r * C + last:r * C + last + 1, :], (HEAD_DIM, HEAD_DIM))
                s_ref[d, lv0 + r] = s_ref[d, lv0 + r] * glr + _dot(kd_r, vn2)
                o_ref[:, (lv0 + r) * HEAD_DIM:(lv0 + r + 1) * HEAD_DIM] = o2[r * C:(r + 1) * C].astype(o_ref.dtype)


def _dn_core(qkv, gates, batch, seq, ctx_len):
    m = qkv.shape[0]
    C = DN_CHUNK
    n_lat = seq // C
    n_ctx = ctx_len // C
    n_steps = n_ctx + n_lat
    ctx0 = batch * n_lat
    khs = DN_KH_PER_STEP
    qw = khs * HEAD_DIM
    vw = 2 * khs * HEAD_DIM
    k_blk0 = DN_K_W // qw
    v_blk0 = 2 * DN_K_W // vw

    def rf(b, t):
        return jnp.where(t < n_ctx, ctx0 + b * n_ctx + t, b * n_lat + (t - n_ctx))

    def rb(b, t):
        return jnp.where(t < n_ctx, ctx0 + b * n_ctx + (n_ctx - 1 - t), b * n_lat + (n_lat - 1 - (t - n_ctx)))

    def specs(rfun):
        return [pl.BlockSpec((C, qw), lambda b, h, t: (rfun(b, t), h)),
                pl.BlockSpec((C, qw), lambda b, h, t: (rfun(b, t), k_blk0 + h)),
                pl.BlockSpec((C, vw), lambda b, h, t: (rfun(b, t), v_blk0 + h)),
                pl.BlockSpec((C, LANES), lambda b, h, t: (rfun(b, t), h))]

    return pl.pallas_call(
        _dncore_kernel,
        grid=(batch, DN_K_HEADS // khs, n_steps),
        in_specs=specs(rf) + specs(rb),
        out_specs=[pl.BlockSpec((C, vw), lambda b, h, t: (rf(b, t), h)),
                   pl.BlockSpec((C, vw), lambda b, h, t: (rb(b, t), h))],
        out_shape=[jax.ShapeDtypeStruct((m, DN_V_W), F32), jax.ShapeDtypeStruct((m, DN_V_W), F32)],
        scratch_shapes=[pltpu.VMEM((2, 2 * khs, HEAD_DIM, HEAD_DIM), F32)],
        compiler_params=_cparams(3),
        name="dn_core",
    )(qkv, qkv, qkv, gates, qkv, qkv, qkv, gates)


def _dn_out_kernel(of_ref, ob_ref, z_ref, nw_ref, w_ref, x_ref, gate_ref, o_ref, y_ref):
    @pl.when(pl.program_id(1) == 0)
    def _():
        for h in range(DN_V_HEADS):
            sl = slice(h * HEAD_DIM, (h + 1) * HEAD_DIM)
            o = of_ref[:, sl] + ob_ref[:, sl]
            n = o * lax.rsqrt(jnp.mean(o * o, axis=-1, keepdims=True) + EPS) * nw_ref[...]
            y_ref[:, sl] = (n * _silu(z_ref[:, sl])).astype(BF16)

    o_ref[...] = x_ref[...] + gate_ref[...] * _dot(y_ref[...], w_ref[...])


def _dn_out(o_f, o_b, proj, norm_w, w_out_bf16, x, mod, batch, seq):
    m = x.shape[0]
    tm, tn = CONV_ROW_TILE, 512
    lat_tiles = seq // tm
    gate_blk0 = 2 * D_MODEL // tn
    z_blk = DN_QKV_W // DN_V_W
    return pl.pallas_call(
        _dn_out_kernel,
        grid=(m // tm, D_MODEL // tn),
        in_specs=[pl.BlockSpec((tm, DN_V_W), lambda i, j: (i, 0)),
                  pl.BlockSpec((tm, DN_V_W), lambda i, j: (i, 0)),
                  pl.BlockSpec((tm, DN_V_W), lambda i, j: (i, z_blk)),
                  pl.BlockSpec((1, HEAD_DIM), lambda i, j: (0, 0)),
                  pl.BlockSpec((DN_V_W, tn), lambda i, j: (0, j)),
                  pl.BlockSpec((tm, tn), lambda i, j: (i, j)),
                  pl.BlockSpec((None, 1, tn), lambda i, j: (_mod_row(i, lat_tiles, batch), 0, gate_blk0 + j))],
        out_specs=pl.BlockSpec((tm, tn), lambda i, j: (i, j)),
        out_shape=jax.ShapeDtypeStruct((m, D_MODEL), F32),
        scratch_shapes=[pltpu.VMEM((tm, DN_V_W), BF16)],
        compiler_params=_cparams(2),
        name="dn_out",
    )(o_f, o_b, proj, norm_w.reshape(1, HEAD_DIM), w_out_bf16, x, mod)


def _final_norm_kernel(x_ref, w_ref, o_ref):
    x = x_ref[...]
    o_ref[...] = x * lax.rsqrt(jnp.mean(x * x, axis=-1, keepdims=True) + EPS) * w_ref[...]


def _final_norm(x, w, rows):
    tm = ROW_TILE
    return pl.pallas_call(
        _final_norm_kernel,
        grid=(rows // tm,),
        in_specs=[pl.BlockSpec((tm, D_MODEL), lambda i: (i, 0)),
                  pl.BlockSpec((1, D_MODEL), lambda i: (0, 0))],
        out_specs=pl.BlockSpec((tm, D_MODEL), lambda i: (i, 0)),
        out_shape=jax.ShapeDtypeStruct((rows, D_MODEL), F32),
        compiler_params=_cparams(1),
        name="final_norm",
    )(x, w.reshape(1, D_MODEL))


def kernel(x, c, ctx, c_ctx, norm_w, ada_w, ada_b, ab_w_in, ab_w_out, sgu_w, sgu_b, q_norm_w, k_norm_w,
           dn_w_in, dn_conv_w, dn_a_log, dn_dt_bias, dn_norm_w, dn_w_out, final_norm_w):
    batch, seq, _ = x.shape
    ctx_len = ctx.shape[1]
    depth = norm_w.shape[0]
    assert ctx_len == CONV_ROW_TILE and seq % ROW_TILE == 0 and (batch * ctx_len) % ROW_TILE == 0
    n_lat_rows = batch * seq

    xs = jnp.concatenate([x.reshape(n_lat_rows, D_MODEL), ctx.reshape(batch * ctx_len, D_MODEL)], axis=0)
    cond = jnp.zeros((SUBLANES, D_MODEL), F32).at[0:batch].set(c).at[batch].set(c_ctx)
    mods = _ada_mod(cond, ada_w, ada_b)
    cos_tab, sin_tab = _rope_tables(seq, ROW_TILE)

    for i in range(depth):
        j = i // 2
        mod = mods[i].reshape(SUBLANES, 1, 3 * D_MODEL)
        if i % 2 == 0:
            w = ab_w_in[j]
            a3 = 3 * A_WIDTH
            w_in = jnp.concatenate([w[:, :a3], w[:, a3 + B_Q_W + 2 * B_KV_W:], w[:, a3:a3 + B_Q_W + 2 * B_KV_W]],
                                   axis=1).astype(BF16)
            proj = _inproj(xs, norm_w[i], mod, w_in, batch, seq)
            qr, kr, vr = _qkprep(proj, cos_tab, sin_tab, q_norm_w[j], k_norm_w[j], seq)
            attn_o = _attention(qr, kr, vr, batch, seq, ctx_len, True, None)
            attn_o = _attention(qr, kr, vr, batch, seq, ctx_len, False, attn_o)
            xs = _ab_out(proj, attn_o, sgu_w[j].astype(BF16), sgu_b[j].T, ab_w_out[j].astype(BF16), xs, mod,
                         batch, seq)
        else:
            w = dn_w_in[j].astype(BF16)
            n_main = DN_QKV_W + DN_V_W
            proj, ba = _inproj(xs, norm_w[i], mod, w[:, :n_main], batch, seq, w_side_bf16=w[:, n_main:])
            conv_w = jnp.zeros((SUBLANES, DN_QKV_W), F32).at[0:CONV_K].set(dn_conv_w[j])
            qkv = _dn_conv(proj, conv_w, batch, seq)
            zeros = jnp.zeros((2, DN_V_HEADS), F32)
            alog_vec = jnp.concatenate([zeros, dn_a_log[j]], axis=1).reshape(1, LANES)
            dtb_vec = jnp.concatenate([zeros, dn_dt_bias[j]], axis=1).reshape(1, LANES)
            gates = _dn_gate(ba, alog_vec, dtb_vec)
            n_hg = DN_K_HEADS // DN_KH_PER_STEP
            nvh = 2 * DN_KH_PER_STEP
            g4 = gates.reshape(-1, 4, n_hg, nvh).transpose(0, 2, 1, 3).reshape(-1, n_hg, 4 * nvh)
            g4 = jnp.pad(g4, ((0, 0), (0, 0), (0, LANES - 4 * nvh))).reshape(-1, n_hg * LANES)
            o_f, o_b = _dn_core(qkv, g4, batch, seq, ctx_len)
            xs = _dn_out(o_f, o_b, proj, dn_norm_w[j], dn_w_out[j].astype(BF16), xs, mod, batch, seq)

    out = _final_norm(xs, final_norm_w, n_lat_rows)
    return out.reshape(batch, seq, D_MODEL)
```

```python
import functools

import jax
import jax.numpy as jnp
from jax import lax
from jax.experimental import pallas as pl
from jax.experimental.pallas import tpu as pltpu

F32 = jnp.float32
BF16 = jnp.bfloat16

D_MODEL = 2048
GRID_W = 64
EPS = 1e-6
HEAD_DIM = 128
A_WIDTH = D_MODEL // 2
A_GROUPS = A_WIDTH // 128
SGU_CHUNK = 128
B_HEADS = (D_MODEL // 2) // HEAD_DIM
B_KV_HEADS = B_HEADS // 4
B_GROUP = B_HEADS // B_KV_HEADS
B_Q_W = B_HEADS * HEAD_DIM
B_KV_W = B_KV_HEADS * HEAD_DIM
ROPE_THETA = 10000.0
AXIS_DIM = HEAD_DIM // 2
AB_IN_W = 3 * A_WIDTH + 2 * B_Q_W + 2 * B_KV_W
DN_K_HEADS = D_MODEL // HEAD_DIM
DN_V_HEADS = 2 * DN_K_HEADS
DN_K_W = DN_K_HEADS * HEAD_DIM
DN_V_W = DN_V_HEADS * HEAD_DIM
DN_QKV_W = 2 * DN_K_W + DN_V_W
DN_IN_W = DN_QKV_W + DN_V_W + 4 * DN_V_HEADS
DN_CHUNK = 64
CONV_K = 5

V7X_VMEM_LIMIT_BYTES = 56 * 1024 * 1024
LANES = 128
SUBLANES = 8

ROW_TILE = 512
CONV_ROW_TILE = 256


def _cparams(n_axes):
    return pltpu.CompilerParams(dimension_semantics=("arbitrary",) * n_axes,
                                vmem_limit_bytes=V7X_VMEM_LIMIT_BYTES)


def _silu(x):
    return x * jax.nn.sigmoid(x)


def _split_bf16(a):
    hi = a.astype(BF16)
    lo = (a - hi.astype(F32)).astype(BF16)
    return hi, lo


def _dot(a, b):
    return jnp.dot(a, b, preferred_element_type=F32)


def _dot_nt(a, b):
    return lax.dot_general(a, b, (((1,), (1,)), ((), ())), preferred_element_type=F32)


def _dot3(a, b):
    ah, al = _split_bf16(a)
    bh, bl = _split_bf16(b)
    return _dot(ah, bh) + _dot(ah, bl) + _dot(al, bh)


def _ada_kernel(c_ref, w_ref, b_ref, o_ref):
    s = _silu(c_ref[...])
    o_ref[0] = _dot3(s, w_ref[0]) + b_ref[0]


def _ada_mod(cond, ada_w, ada_b):
    depth = ada_w.shape[0]
    tn = 512
    return pl.pallas_call(
        _ada_kernel,
        grid=(depth, 3 * D_MODEL // tn),
        in_specs=[pl.BlockSpec((SUBLANES, D_MODEL), lambda l, j: (0, 0)),
                  pl.BlockSpec((1, D_MODEL, tn), lambda l, j: (l, 0, j)),
                  pl.BlockSpec((1, 1, tn), lambda l, j: (l, 0, j))],
        out_specs=pl.BlockSpec((1, SUBLANES, tn), lambda l, j: (l, 0, j)),
        out_shape=jax.ShapeDtypeStruct((depth, SUBLANES, 3 * D_MODEL), F32),
        compiler_params=_cparams(2),
        name="ada_mod",
    )(cond, ada_w, ada_b.reshape(depth, 1, 3 * D_MODEL))


def _inproj_kernel(x_ref, nw_ref, mod_ref, w_ref, *rest):
    h_ref = rest[-1]
    has_side = len(rest) == 4

    @pl.when(pl.program_id(1) == 0)
    def _():
        x = x_ref[...]
        y = x * lax.rsqrt(jnp.mean(x * x, axis=-1, keepdims=True) + EPS) * nw_ref[...]
        shift = mod_ref[:, 0:D_MODEL]
        scale = mod_ref[:, D_MODEL:2 * D_MODEL]
        h_ref[...] = (y * (1.0 + scale) + shift).astype(BF16)
        if has_side:
            rest[2][...] = _dot(h_ref[...], rest[0][...])

    o_ref = rest[1] if has_side else rest[0]
    o_ref[...] = _dot(h_ref[...], w_ref[...]).astype(o_ref.dtype)


def _mod_row(i, lat_tiles, batch):
    return jnp.minimum(i // lat_tiles, batch)


def _inproj(x, norm_w, mod, w_bf16, batch, seq, w_side_bf16=None, out_dtype=F32):
    m = x.shape[0]
    n = w_bf16.shape[1]
    tm, tn = ROW_TILE, 512
    assert n % tn == 0
    lat_tiles = seq // tm
    in_specs = [pl.BlockSpec((tm, D_MODEL), lambda i, j: (i, 0)),
                pl.BlockSpec((1, D_MODEL), lambda i, j: (0, 0)),
                pl.BlockSpec((None, 1, 3 * D_MODEL), lambda i, j: (_mod_row(i, lat_tiles, batch), 0, 0)),
                pl.BlockSpec((D_MODEL, tn), lambda i, j: (0, j))]
    out_specs = [pl.BlockSpec((tm, tn), lambda i, j: (i, j))]
    out_shape = [jax.ShapeDtypeStruct((m, n), out_dtype)]
    args = [x, norm_w.reshape(1, D_MODEL), mod, w_bf16]
    if w_side_bf16 is not None:
        ns = w_side_bf16.shape[1]
        in_specs.append(pl.BlockSpec((D_MODEL, ns), lambda i, j: (0, 0)))
        out_specs.append(pl.BlockSpec((tm, ns), lambda i, j: (i, 0)))
        out_shape.append(jax.ShapeDtypeStruct((m, ns), F32))
        args.append(w_side_bf16)
    res = pl.pallas_call(
        _inproj_kernel,
        grid=(m // tm, n // tn),
        in_specs=in_specs,
        out_specs=out_specs,
        out_shape=out_shape,
        scratch_shapes=[pltpu.VMEM((tm, D_MODEL), BF16)],
        compiler_params=_cparams(2),
        name="inproj",
    )(*args)
    return res if w_side_bf16 is not None else res[0]


def _qkprep_kernel(q_ref, k_ref, v_ref, cos_ref, sin_ref, qw_ref, kw_ref, qo_ref, ko_ref, vo_ref):
    cos = cos_ref[...]
    sin = sin_ref[...]
    lane = lax.broadcasted_iota(jnp.int32, cos.shape, 1)
    first = (lane % (AXIS_DIM)) < (AXIS_DIM // 2)

    def prep(x, w):
        y = x * lax.rsqrt(jnp.mean(x * x, axis=-1, keepdims=True) + EPS) * w
        rot = jnp.where(first, pltpu.roll(y, HEAD_DIM - AXIS_DIM // 2, 1), pltpu.roll(y, AXIS_DIM // 2, 1))
        return y * cos + rot * sin

    for h in range(B_HEADS):
        sl = slice(h * HEAD_DIM, (h + 1) * HEAD_DIM)
        qo_ref[:, sl] = prep(q_ref[:, sl], qw_ref[...]).astype(qo_ref.dtype)
    for h in range(B_KV_HEADS):
        sl = slice(h * HEAD_DIM, (h + 1) * HEAD_DIM)
        ko_ref[:, sl] = prep(k_ref[:, sl], kw_ref[...]).astype(ko_ref.dtype)
    vo_ref[...] = v_ref[...].astype(vo_ref.dtype)


def _qkprep(proj, cos_tab, sin_tab, qn_w, kn_w, seq):
    m = proj.shape[0]
    tm = ROW_TILE
    lat_tiles = seq // tm
    q_blk = (4 * A_WIDTH) // B_Q_W
    k_blk = (4 * A_WIDTH + B_Q_W) // B_KV_W

    def tab_idx(i):
        return (jnp.where(i < 2 * lat_tiles, i % lat_tiles, lat_tiles), 0)

    return pl.pallas_call(
        _qkprep_kernel,
        grid=(m // tm,),
        in_specs=[pl.BlockSpec((tm, B_Q_W), lambda i: (i, q_blk)),
                  pl.BlockSpec((tm, B_KV_W), lambda i: (i, k_blk)),
                  pl.BlockSpec((tm, B_KV_W), lambda i: (i, k_blk + 1)),
                  pl.BlockSpec((tm, HEAD_DIM), tab_idx),
                  pl.BlockSpec((tm, HEAD_DIM), tab_idx),
                  pl.BlockSpec((1, HEAD_DIM), lambda i: (0, 0)),
                  pl.BlockSpec((1, HEAD_DIM), lambda i: (0, 0))],
        out_specs=[pl.BlockSpec((tm, B_Q_W), lambda i: (i, 0)),
                   pl.BlockSpec((tm, B_KV_W), lambda i: (i, 0)),
                   pl.BlockSpec((tm, B_KV_W), lambda i: (i, 0))],
        out_shape=[jax.ShapeDtypeStruct((m, B_Q_W), BF16),
                   jax.ShapeDtypeStruct((m, B_KV_W), BF16),
                   jax.ShapeDtypeStruct((m, B_KV_W), BF16)],
        compiler_params=_cparams(1),
        name="qk_prep",
    )(proj, proj, proj, cos_tab, sin_tab, qn_w.reshape(1, HEAD_DIM), kn_w.reshape(1, HEAD_DIM))


def _rope_tables(seq, tm):
    rows = seq // GRID_W
    row = jnp.repeat(jnp.arange(rows), GRID_W).astype(F32)
    col = jnp.tile(jnp.arange(GRID_W), rows).astype(F32)
    freqs = ROPE_THETA ** (-jnp.arange(0, AXIS_DIM, 2, dtype=F32) / AXIS_DIM)
    ang_r = row[:, None] * freqs[None, :]
    ang_c = col[:, None] * freqs[None, :]
    ang = jnp.concatenate([ang_r, ang_r, ang_c, ang_c], axis=-1)
    sign = jnp.where((jnp.arange(HEAD_DIM) % AXIS_DIM) < AXIS_DIM // 2, -1.0, 1.0).astype(F32)
    cos = jnp.concatenate([jnp.cos(ang), jnp.ones((tm, HEAD_DIM), F32)], axis=0)
    sin = jnp.concatenate([jnp.sin(ang) * sign[None, :], jnp.zeros((tm, HEAD_DIM), F32)], axis=0)
    return cos, sin


def _attn_kernel(*refs, seg_chunks, tq):
    n_seg = len(seg_chunks)
    q_ref = refs[0]
    kv_refs = refs[1:1 + 2 * n_seg]
    o_ref = refs[1 + 2 * n_seg]
    m_scr, l_scr, acc_scr = refs[2 + 2 * n_seg:]
    q = jnp.concatenate([q_ref[:, g * HEAD_DIM:(g + 1) * HEAD_DIM] for g in range(B_GROUP)], axis=0)
    m_scr[...] = jnp.full(m_scr.shape, -jnp.inf, F32)
    l_scr[...] = jnp.zeros(l_scr.shape, F32)
    acc_scr[...] = jnp.zeros(acc_scr.shape, F32)

    def step(k, v):
        tk = k.shape[0]
        s = _dot_nt(q, k) * (HEAD_DIM ** -0.5)
        m_prev = m_scr[...]
        m_next = jnp.maximum(m_prev, jnp.max(s, axis=1, keepdims=True))
        p = jnp.exp(s - jnp.concatenate([m_next] * (tk // LANES), axis=1))
        alpha = jnp.exp(m_prev - m_next)
        l_scr[...] = alpha * l_scr[...] + jnp.sum(p, axis=1, keepdims=True)
        acc_scr[...] = acc_scr[...] * alpha + _dot(p.astype(BF16), v)
        m_scr[...] = m_next

    for s_i, (n_chunks, tk) in enumerate(seg_chunks):
        k_ref, v_ref = kv_refs[2 * s_i], kv_refs[2 * s_i + 1]
        if n_chunks == 1:
            step(k_ref[...], v_ref[...])
        else:
            def body(c, carry, k_ref=k_ref, v_ref=v_ref, tk=tk):
                r0 = pl.multiple_of(c * tk, tk)
                step(k_ref[pl.ds(r0, tk), :], v_ref[pl.ds(r0, tk), :])
                return carry
            lax.fori_loop(0, n_chunks, body, 0)

    out = acc_scr[...] / l_scr[...]
    for g in range(B_GROUP):
        o_ref[:, g * HEAD_DIM:(g + 1) * HEAD_DIM] = out[g * tq:(g + 1) * tq].astype(o_ref.dtype)


def _attention(qr, kr, vr, batch, seq, ctx_len, latent, out):
    tq = 128
    tk = 512
    gw = B_GROUP * HEAD_DIM
    ctx_blk0 = batch * seq // ctx_len
    q_tiles = (seq if latent else ctx_len) // tq
    q_row0 = 0 if latent else batch * seq // tq

    def q_idx(b, h, i):
        return (q_row0 + b * q_tiles + i, h)

    in_specs = [pl.BlockSpec((tq, gw), q_idx),
                pl.BlockSpec((ctx_len, HEAD_DIM), lambda b, h, i: (ctx_blk0 + b, h)),
                pl.BlockSpec((ctx_len, HEAD_DIM), lambda b, h, i: (ctx_blk0 + b, h))]
    args = [qr, kr, vr]
    seg_chunks = [(1, ctx_len)]
    if latent:
        in_specs += [pl.BlockSpec((seq, HEAD_DIM), lambda b, h, i: (b, h)),
                     pl.BlockSpec((seq, HEAD_DIM), lambda b, h, i: (b, h))]
        args += [kr, vr]
        seg_chunks.append((seq // tk, tk))
    n_in = len(args)
    aliases = {}
    if out is not None:
        in_specs.append(pl.BlockSpec(memory_space=pl.ANY))
        args.append(out)
        aliases = {n_in: 0}
    rows = B_GROUP * tq

    def kern(*refs):
        _attn_kernel(*refs[:n_in], *refs[len(args):], seg_chunks=tuple(seg_chunks), tq=tq)

    return pl.pallas_call(
        kern,
        grid=(batch, B_KV_HEADS, q_tiles),
        in_specs=in_specs,
        out_specs=pl.BlockSpec((tq, gw), q_idx),
        out_shape=jax.ShapeDtypeStruct((qr.shape[0], B_Q_W), F32),
        scratch_shapes=[pltpu.VMEM((rows, LANES), F32), pltpu.VMEM((rows, LANES), F32),
                        pltpu.VMEM((rows, HEAD_DIM), F32)],
        input_output_aliases=aliases,
        compiler_params=_cparams(3),
        name="attn_lat" if latent else "attn_ctx",
    )(*args)


def _ab_out_kernel(u_ref, v_ref, ga_ref, gb_ref, ao_ref, ws_ref, bs_ref, w_ref, x_ref, gate_ref, o_ref, y_ref):
    @pl.when(pl.program_id(1) == 0)
    def _():
        tm = u_ref.shape[0]
        for c in range(tm // SGU_CHUNK):
            rows = slice(c * SGU_CHUNK, (c + 1) * SGU_CHUNK)
            for g in range(A_GROUPS):
                cols = slice(g * LANES, (g + 1) * LANES)
                vg = v_ref[rows, cols]
                d = vg - jnp.mean(vg, axis=-1, keepdims=True)
                var = jnp.mean(d * d, axis=-1, keepdims=True)
                vn = (d * lax.rsqrt(var + 1e-5)).astype(BF16)
                mixed = _dot(ws_ref[g], vn) + bs_ref[:, g:g + 1]
                y_ref[rows, cols] = (u_ref[rows, cols] * mixed * _silu(ga_ref[rows, cols])).astype(BF16)
        y_ref[:, A_WIDTH:] = (ao_ref[...].astype(F32) * _silu(gb_ref[...])).astype(BF16)

    o_ref[...] = x_ref[...] + gate_ref[...] * _dot(y_ref[...], w_ref[...])


def _ab_out(proj, attn_o, sgu_w_bf16, sgu_b_t, w_out_bf16, x, mod, batch, seq):
    m = x.shape[0]
    tm, tn = ROW_TILE, 512
    lat_tiles = seq // tm
    gate_blk0 = 2 * D_MODEL // tn
    return pl.pallas_call(
        _ab_out_kernel,
        grid=(m // tm, D_MODEL // tn),
        in_specs=[pl.BlockSpec((tm, A_WIDTH), lambda i, j: (i, 0)),
                  pl.BlockSpec((tm, A_WIDTH), lambda i, j: (i, 1)),
                  pl.BlockSpec((tm, A_WIDTH), lambda i, j: (i, 2)),
                  pl.BlockSpec((tm, B_Q_W), lambda i, j: (i, 3)),
                  pl.BlockSpec((tm, B_Q_W), lambda i, j: (i, 0)),
                  pl.BlockSpec((A_GROUPS, SGU_CHUNK, SGU_CHUNK), lambda i, j: (0, 0, 0)),
                  pl.BlockSpec((SGU_CHUNK, A_GROUPS), lambda i, j: (0, 0)),
                  pl.BlockSpec((A_WIDTH + B_Q_W, tn), lambda i, j: (0, j)),
                  pl.BlockSpec((tm, tn), lambda i, j: (i, j)),
                  pl.BlockSpec((None, 1, tn), lambda i, j: (_mod_row(i, lat_tiles, batch), 0, gate_blk0 + j))],
        out_specs=pl.BlockSpec((tm, tn), lambda i, j: (i, j)),
        out_shape=jax.ShapeDtypeStruct((m, D_MODEL), F32),
        scratch_shapes=[pltpu.VMEM((tm, A_WIDTH + B_Q_W), BF16)],
        compiler_params=_cparams(2),
        name="ab_out",
    )(proj, proj, proj, proj, attn_o, sgu_w_bf16, sgu_b_t, w_out_bf16, x, mod)


def _dnconv_kernel(xp_ref, xc_ref, xn_ref, w_ref, o_ref, xe_ref, *, lat_tiles, n_lat, q_tiles, k_tiles):
    i = pl.program_id(0)
    j = pl.program_id(1)
    tm = xc_ref.shape[0]
    pos = i % lat_tiles
    is_lat = i < n_lat
    first = jnp.logical_or(jnp.logical_not(is_lat), pos == 0)
    last = jnp.logical_or(jnp.logical_not(is_lat), pos == lat_tiles - 1)
    xe_ref[0:SUBLANES, :] = jnp.where(first, 0.0, xp_ref[...])
    xe_ref[SUBLANES:SUBLANES + tm, :] = xc_ref[...]
    xe_ref[SUBLANES + tm:2 * SUBLANES + tm, :] = jnp.where(last, 0.0, xn_ref[...])
    half = CONV_K // 2
    acc = w_ref[0:1, :] * xe_ref[SUBLANES - half:SUBLANES - half + tm, :]
    for t in range(1, CONV_K):
        acc = acc + w_ref[t:t + 1, :] * xe_ref[SUBLANES - half + t:SUBLANES - half + t + tm, :]
    y = _silu(acc)

    def l2n(scale):
        for h in range(y.shape[1] // HEAD_DIM):
            sl = slice(h * HEAD_DIM, (h + 1) * HEAD_DIM)
            yh = y[:, sl]
            n = yh * lax.rsqrt(jnp.sum(yh * yh, axis=-1, keepdims=True) + EPS)
            o_ref[:, sl] = (n * scale).astype(o_ref.dtype) if scale != 1.0 else n.astype(o_ref.dtype)

    @pl.when(j < q_tiles)
    def _():
        l2n(HEAD_DIM ** -0.5)

    @pl.when(jnp.logical_and(j >= q_tiles, j < q_tiles + k_tiles))
    def _():
        l2n(1.0)

    @pl.when(j >= q_tiles + k_tiles)
    def _():
        o_ref[...] = y.astype(o_ref.dtype)


def _dn_conv(proj, conv_w, batch, seq, out_dtype=F32):
    m = proj.shape[0]
    tm, tc = CONV_ROW_TILE, 1024
    lat_tiles = seq // tm
    n_lat = batch * lat_tiles
    sub_per_tile = tm // SUBLANES
    n_sub = m // SUBLANES
    kern = functools.partial(_dnconv_kernel, lat_tiles=lat_tiles, n_lat=n_lat,
                             q_tiles=DN_K_W // tc, k_tiles=DN_K_W // tc)
    return pl.pallas_call(
        kern,
        grid=(m // tm, DN_QKV_W // tc),
        in_specs=[pl.BlockSpec((SUBLANES, tc), lambda i, j: (jnp.maximum(i * sub_per_tile - 1, 0), j)),
                  pl.BlockSpec((tm, tc), lambda i, j: (i, j)),
                  pl.BlockSpec((SUBLANES, tc), lambda i, j: (jnp.minimum((i + 1) * sub_per_tile, n_sub - 1), j)),
                  pl.BlockSpec((SUBLANES, tc), lambda i, j: (0, j))],
        out_specs=pl.BlockSpec((tm, tc), lambda i, j: (i, j)),
        out_shape=jax.ShapeDtypeStruct((m, DN_QKV_W), out_dtype),
        scratch_shapes=[pltpu.VMEM((tm + 2 * SUBLANES, tc), F32)],
        compiler_params=_cparams(2),
        name="dn_conv",
    )(proj, proj, proj, conv_w)


def _dngate_kernel(ba_ref, alog_ref, dtb_ref, o_ref):
    ba = ba_ref[...]
    tm = ba.shape[0]
    lane = lax.broadcasted_iota(jnp.int32, (DN_CHUNK, LANES), 1)
    is_beta = (lane // DN_V_HEADS) % 2 == 0
    is_fwd = lane < 2 * DN_V_HEADS
    z = ba + dtb_ref[...]
    softplus = jnp.maximum(z, 0.0) + jnp.log1p(jnp.exp(-jnp.abs(z)))
    g = -jnp.exp(alog_ref[...]) * softplus
    r = lax.broadcasted_iota(jnp.int32, (DN_CHUNK, DN_CHUNK), 0)
    c = lax.broadcasted_iota(jnp.int32, (DN_CHUNK, DN_CHUNK), 1)
    tri_lo = (r >= c).astype(BF16)
    tri_up = (r <= c).astype(BF16)
    beta = jax.nn.sigmoid(ba)
    for ch in range(tm // DN_CHUNK):
        rows = slice(ch * DN_CHUNK, (ch + 1) * DN_CHUNK)
        gch = g[rows]
        g1 = gch.astype(BF16)
        r1 = gch - g1.astype(F32)
        g2 = r1.astype(BF16)
        g3 = (r1 - g2.astype(F32)).astype(BF16)
        pre = _dot(tri_lo, g1) + _dot(tri_lo, g2) + _dot(tri_lo, g3)
        suf = _dot(tri_up, g1) + _dot(tri_up, g2) + _dot(tri_up, g3)
        gc = jnp.where(is_fwd, pre, suf)
        o_ref[rows, :] = jnp.where(is_beta, beta[rows], gc)


def _dn_gate(ba, alog_vec, dtb_vec):
    m = ba.shape[0]
    tm = ROW_TILE
    return pl.pallas_call(
        _dngate_kernel,
        grid=(m // tm,),
        in_specs=[pl.BlockSpec((tm, LANES), lambda i: (i, 0)),
                  pl.BlockSpec((1, LANES), lambda i: (0, 0)),
                  pl.BlockSpec((1, LANES), lambda i: (0, 0))],
        out_specs=pl.BlockSpec((tm, LANES), lambda i: (i, 0)),
        out_shape=jax.ShapeDtypeStruct((m, LANES), F32),
        compiler_params=_cparams(1),
        name="dn_gate",
    )(ba, alog_vec, dtb_vec)


DN_KH_PER_STEP = 4


def _dncore_kernel(qf_ref, kf_ref, vf_ref, gf_ref, qb_ref, kb_ref, vb_ref, gb_ref, of_ref, ob_ref, s_ref):
    C = DN_CHUNK
    nvh = 2 * DN_KH_PER_STEP

    @pl.when(pl.program_id(2) == 0)
    def _():
        s_ref[...] = jnp.zeros(s_ref.shape, F32)

    lane = lax.broadcasted_iota(jnp.int32, (C, 2 * C), 1)
    left = lane < C
    row = lax.broadcasted_iota(jnp.int32, (C, 2 * C), 0)
    colp = lane % C
    left_sq = lax.broadcasted_iota(jnp.int32, (2 * C, 2 * C), 1) < C
    eye2 = (row == colp).astype(F32)

    def blockdiag(p):
        z = jnp.zeros_like(p)
        return jnp.concatenate([jnp.where(left, p, z), jnp.where(left, z, p)], axis=0)

    def packed_mm(a, b):
        return _dot(a.astype(BF16), blockdiag(b.astype(BF16)))

    dirs = ((qf_ref, kf_ref, vf_ref, gf_ref, of_ref), (qb_ref, kb_ref, vb_ref, gb_ref, ob_ref))
    chains = [(d, kh) for d in range(2) for kh in range(DN_KH_PER_STEP)]
    G = [dirs[d][3][...] for d in range(2)]
    GT = [jnp.concatenate([g, g], axis=0).T for g in G]

    st = []
    for d, kh in chains:
        q_ref, k_ref, v_ref, _, _ = dirs[d]
        base_beta = d * 2 * nvh
        base_gc = base_beta + nvh
        lv0 = 2 * kh
        q = q_ref[:, kh * HEAD_DIM:(kh + 1) * HEAD_DIM]
        k = k_ref[:, kh * HEAD_DIM:(kh + 1) * HEAD_DIM]
        k2 = jnp.concatenate([k, k], axis=0)
        kT2 = k2.T
        gram = _dot(jnp.concatenate([q, k], axis=0).astype(BF16), kT2.astype(BF16))
        b0, b1 = G[d][:, base_beta + lv0:base_beta + lv0 + 1], G[d][:, base_beta + lv0 + 1:base_beta + lv0 + 2]
        c0, c1 = G[d][:, base_gc + lv0:base_gc + lv0 + 1], G[d][:, base_gc + lv0 + 1:base_gc + lv0 + 2]
        r0, r1 = GT[d][base_gc + lv0:base_gc + lv0 + 1, :], GT[d][base_gc + lv0 + 1:base_gc + lv0 + 2, :]
        st.append(dict(d=d, lv0=lv0, q=q, k2=k2, kT2=kT2, gram=gram, b0=b0, b1=b1, c0=c0, c1=c1, r0=r0, r1=r1))

    for s in st:
        d = s["d"]
        incl = (row >= colp) if d == 0 else (row <= colp)
        strict = (row > colp) if d == 0 else (row < colp)
        gcol_p = jnp.where(left, s["c0"], s["c1"])
        grow_p = jnp.where(left[0:1], s["r0"], s["r1"])
        beta_p = jnp.where(left, s["b0"], s["b1"])
        dec = jnp.exp(jnp.where(incl, gcol_p - grow_p, -1e30))
        s["dec"] = dec
        s["attn"] = s["gram"][0:C] * dec
        s["L"] = jnp.where(strict, s["gram"][C:2 * C] * dec, 0.0) * beta_p

    for s in st:
        n1 = jnp.where(jnp.logical_and(row // 2 == colp // 2, row != colp), s["L"], 0.0)
        s["X"] = eye2 - n1
    blk = 2
    while blk < C:
        mask = jnp.logical_and(row // (2 * blk) == colp // (2 * blk), row // blk != colp // blk)
        for s in st:
            s["Y"] = packed_mm(s["X"], jnp.where(mask, s["L"], 0.0))
        for s in st:
            s["X"] = s["X"] - packed_mm(s["Y"], s["X"])
        blk *= 2

    for s in st:
        d, lv0 = s["d"], s["lv0"]
        v_ref = dirs[d][2]
        beta_r = jnp.concatenate([s["b0"], s["b1"]], axis=0)
        egc_r = jnp.exp(jnp.concatenate([s["c0"], s["c1"]], axis=0))
        v2 = jnp.concatenate([v_ref[:, lv0 * HEAD_DIM:(lv0 + 1) * HEAD_DIM],
                              v_ref[:, (lv0 + 1) * HEAD_DIM:(lv0 + 2) * HEAD_DIM]], axis=0)
        rhs = jnp.concatenate([v2 * beta_r, s["k2"] * (beta_r * egc_r)], axis=1)
        s["sol"] = _dot(blockdiag(s["X"]).astype(BF16), rhs.astype(BF16))
        s["egc"] = egc_r
        s["qg2"] = jnp.concatenate([s["q"], s["q"]], axis=0) * egc_r

    for s in st:
        d, lv0 = s["d"], s["lv0"]
        w2 = s["sol"][:, HEAD_DIM:]
        s["ws"] = []
        for r in range(2):
            lhs = jnp.concatenate([w2[r * C:(r + 1) * C], s["qg2"][r * C:(r + 1) * C]], axis=0).astype(BF16)
            s["ws"].append(_dot(lhs, s_ref[d, lv0 + r].astype(BF16)))

    for s in st:
        u2 = s["sol"][:, 0:HEAD_DIM]
        vn2 = jnp.concatenate([u2[r * C:(r + 1) * C] - s["ws"][r][0:C] for r in range(2)], axis=0).astype(BF16)
        s["vn2"] = vn2
        s["o2"] = jnp.concatenate([s["ws"][r][C:2 * C] for r in range(2)], axis=0) + _dot(
            blockdiag(s["attn"]).astype(BF16), vn2)

    for s in st:
        d, lv0 = s["d"], s["lv0"]
        o_ref = dirs[d][4]
        last = C - 1 if d == 0 else 0
        kdT_p = s["kT2"] * s["dec"][last:last + 1, :]
        zkd = jnp.zeros_like(kdT_p)
        for r in range(2):
            kd_r = jnp.where(left_sq if r == 0 else jnp.logical_not(left_sq), kdT_p, zkd).astype(BF16)
            glr = jnp.broadcast_to(s["egc"][r * C + last:r * C + last + 1, :], (HEAD_DIM, HEAD_DIM))
            s_ref[d, lv0 + r] = s_ref[d, lv0 + r] * glr + _dot(kd_r, s["vn2"])
            o_ref[:, (lv0 + r) * HEAD_DIM:(lv0 + r + 1) * HEAD_DIM] = s["o2"][r * C:(r + 1) * C].astype(o_ref.dtype)


def _dn_core(qkv, gates, batch, seq, ctx_len):
    m = qkv.shape[0]
    C = DN_CHUNK
    n_lat = seq // C
    n_ctx = ctx_len // C
    n_steps = n_ctx + n_lat
    ctx0 = batch * n_lat
    khs = DN_KH_PER_STEP
    qw = khs * HEAD_DIM
    vw = 2 * khs * HEAD_DIM
    k_blk0 = DN_K_W // qw
    v_blk0 = 2 * DN_K_W // vw

    def rf(b, t):
        return jnp.where(t < n_ctx, ctx0 + b * n_ctx + t, b * n_lat + (t - n_ctx))

    def rb(b, t):
        return jnp.where(t < n_ctx, ctx0 + b * n_ctx + (n_ctx - 1 - t), b * n_lat + (n_lat - 1 - (t - n_ctx)))

    def specs(rfun):
        return [pl.BlockSpec((C, qw), lambda b, h, t: (rfun(b, t), h)),
                pl.BlockSpec((C, qw), lambda b, h, t: (rfun(b, t), k_blk0 + h)),
                pl.BlockSpec((C, vw), lambda b, h, t: (rfun(b, t), v_blk0 + h)),
                pl.BlockSpec((C, LANES), lambda b, h, t: (rfun(b, t), h))]

    return pl.pallas_call(
        _dncore_kernel,
        grid=(batch, DN_K_HEADS // khs, n_steps),
        in_specs=specs(rf) + specs(rb),
        out_specs=[pl.BlockSpec((C, vw), lambda b, h, t: (rf(b, t), h)),
                   pl.BlockSpec((C, vw), lambda b, h, t: (rb(b, t), h))],
        out_shape=[jax.ShapeDtypeStruct((m, DN_V_W), F32), jax.ShapeDtypeStruct((m, DN_V_W), F32)],
        scratch_shapes=[pltpu.VMEM((2, 2 * khs, HEAD_DIM, HEAD_DIM), F32)],
        compiler_params=_cparams(3),
        name="dn_core",
    )(qkv, qkv, qkv, gates, qkv, qkv, qkv, gates)


def _dn_out_kernel(of_ref, ob_ref, z_ref, nw_ref, w_ref, x_ref, gate_ref, o_ref, y_ref):
    @pl.when(pl.program_id(1) == 0)
    def _():
        for h in range(DN_V_HEADS):
            sl = slice(h * HEAD_DIM, (h + 1) * HEAD_DIM)
            o = of_ref[:, sl] + ob_ref[:, sl]
            n = o * lax.rsqrt(jnp.mean(o * o, axis=-1, keepdims=True) + EPS) * nw_ref[...]
            y_ref[:, sl] = (n * _silu(z_ref[:, sl])).astype(BF16)

    o_ref[...] = x_ref[...] + gate_ref[...] * _dot(y_ref[...], w_ref[...])


def _dn_out(o_f, o_b, proj, norm_w, w_out_bf16, x, mod, batch, seq):
    m = x.shape[0]
    tm, tn = CONV_ROW_TILE, 512
    lat_tiles = seq // tm
    gate_blk0 = 2 * D_MODEL // tn
    z_blk = DN_QKV_W // DN_V_W
    return pl.pallas_call(
        _dn_out_kernel,
        grid=(m // tm, D_MODEL // tn),
        in_specs=[pl.BlockSpec((tm, DN_V_W), lambda i, j: (i, 0)),
                  pl.BlockSpec((tm, DN_V_W), lambda i, j: (i, 0)),
                  pl.BlockSpec((tm, DN_V_W), lambda i, j: (i, z_blk)),
                  pl.BlockSpec((1, HEAD_DIM), lambda i, j: (0, 0)),
                  pl.BlockSpec((DN_V_W, tn), lambda i, j: (0, j)),
                  pl.BlockSpec((tm, tn), lambda i, j: (i, j)),
                  pl.BlockSpec((None, 1, tn), lambda i, j: (_mod_row(i, lat_tiles, batch), 0, gate_blk0 + j))],
        out_specs=pl.BlockSpec((tm, tn), lambda i, j: (i, j)),
        out_shape=jax.ShapeDtypeStruct((m, D_MODEL), F32),
        scratch_shapes=[pltpu.VMEM((tm, DN_V_W), BF16)],
        compiler_params=_cparams(2),
        name="dn_out",
    )(o_f, o_b, proj, norm_w.reshape(1, HEAD_DIM), w_out_bf16, x, mod)


def _final_norm_kernel(x_ref, w_ref, o_ref):
    x = x_ref[...]
    o_ref[...] = x * lax.rsqrt(jnp.mean(x * x, axis=-1, keepdims=True) + EPS) * w_ref[...]


def _final_norm(x, w, rows):
    tm = ROW_TILE
    return pl.pallas_call(
        _final_norm_kernel,
        grid=(rows // tm,),
        in_specs=[pl.BlockSpec((tm, D_MODEL), lambda i: (i, 0)),
                  pl.BlockSpec((1, D_MODEL), lambda i: (0, 0))],
        out_specs=pl.BlockSpec((tm, D_MODEL), lambda i: (i, 0)),
        out_shape=jax.ShapeDtypeStruct((rows, D_MODEL), F32),
        compiler_params=_cparams(1),
        name="final_norm",
    )(x, w.reshape(1, D_MODEL))


def kernel(x, c, ctx, c_ctx, norm_w, ada_w, ada_b, ab_w_in, ab_w_out, sgu_w, sgu_b, q_norm_w, k_norm_w,
           dn_w_in, dn_conv_w, dn_a_log, dn_dt_bias, dn_norm_w, dn_w_out, final_norm_w):
    batch, seq, _ = x.shape
    ctx_len = ctx.shape[1]
    depth = norm_w.shape[0]
    assert ctx_len == CONV_ROW_TILE and seq % ROW_TILE == 0 and (batch * ctx_len) % ROW_TILE == 0
    n_lat_rows = batch * seq

    xs = jnp.concatenate([x.reshape(n_lat_rows, D_MODEL), ctx.reshape(batch * ctx_len, D_MODEL)], axis=0)
    cond = jnp.zeros((SUBLANES, D_MODEL), F32).at[0:batch].set(c).at[batch].set(c_ctx)
    mods = _ada_mod(cond, ada_w, ada_b)
    cos_tab, sin_tab = _rope_tables(seq, ROW_TILE)

    for i in range(depth):
        j = i // 2
        mod = mods[i].reshape(SUBLANES, 1, 3 * D_MODEL)
        if i % 2 == 0:
            w = ab_w_in[j]
            a3 = 3 * A_WIDTH
            w_in = jnp.concatenate([w[:, :a3], w[:, a3 + B_Q_W + 2 * B_KV_W:], w[:, a3:a3 + B_Q_W + 2 * B_KV_W]],
                                   axis=1).astype(BF16)
            proj = _inproj(xs, norm_w[i], mod, w_in, batch, seq)
            qr, kr, vr = _qkprep(proj, cos_tab, sin_tab, q_norm_w[j], k_norm_w[j], seq)
            attn_o = _attention(qr, kr, vr, batch, seq, ctx_len, True, None)
            attn_o = _attention(qr, kr, vr, batch, seq, ctx_len, False, attn_o)
            xs = _ab_out(proj, attn_o, sgu_w[j].astype(BF16), sgu_b[j].T, ab_w_out[j].astype(BF16), xs, mod,
                         batch, seq)
        else:
            w = dn_w_in[j].astype(BF16)
            n_main = DN_QKV_W + DN_V_W
            proj, ba = _inproj(xs, norm_w[i], mod, w[:, :n_main], batch, seq, w_side_bf16=w[:, n_main:])
            conv_w = jnp.zeros((SUBLANES, DN_QKV_W), F32).at[0:CONV_K].set(dn_conv_w[j])
            qkv = _dn_conv(proj, conv_w, batch, seq)
            zeros = jnp.zeros((2, DN_V_HEADS), F32)
            alog_vec = jnp.concatenate([zeros, dn_a_log[j]], axis=1).reshape(1, LANES)
            dtb_vec = jnp.concatenate([zeros, dn_dt_bias[j]], axis=1).reshape(1, LANES)
            gates = _dn_gate(ba, alog_vec, dtb_vec)
            n_hg = DN_K_HEADS // DN_KH_PER_STEP
            nvh = 2 * DN_KH_PER_STEP
            g4 = gates.reshape(-1, 4, n_hg, nvh).transpose(0, 2, 1, 3).reshape(-1, n_hg, 4 * nvh)
            g4 = jnp.pad(g4, ((0, 0), (0, 0), (0, LANES - 4 * nvh))).reshape(-1, n_hg * LANES)
            o_f, o_b = _dn_core(qkv, g4, batch, seq, ctx_len)
            xs = _dn_out(o_f, o_b, proj, dn_norm_w[j], dn_w_out[j].astype(BF16), xs, mod, batch, seq)

    out = _final_norm(xs, final_norm_w, n_lat_rows)
    return out.reshape(batch, seq, D_MODEL)
```

```python
import functools

import jax
import jax.numpy as jnp
from jax import lax
from jax.experimental import pallas as pl
from jax.experimental.pallas import tpu as pltpu

F32 = jnp.float32
BF16 = jnp.bfloat16

D_MODEL = 2048
GRID_W = 64
EPS = 1e-6
HEAD_DIM = 128
A_WIDTH = D_MODEL // 2
A_GROUPS = A_WIDTH // 128
SGU_CHUNK = 128
B_HEADS = (D_MODEL // 2) // HEAD_DIM
B_KV_HEADS = B_HEADS // 4
B_GROUP = B_HEADS // B_KV_HEADS
B_Q_W = B_HEADS * HEAD_DIM
B_KV_W = B_KV_HEADS * HEAD_DIM
ROPE_THETA = 10000.0
AXIS_DIM = HEAD_DIM // 2
AB_IN_W = 3 * A_WIDTH + 2 * B_Q_W + 2 * B_KV_W
DN_K_HEADS = D_MODEL // HEAD_DIM
DN_V_HEADS = 2 * DN_K_HEADS
DN_K_W = DN_K_HEADS * HEAD_DIM
DN_V_W = DN_V_HEADS * HEAD_DIM
DN_QKV_W = 2 * DN_K_W + DN_V_W
DN_IN_W = DN_QKV_W + DN_V_W + 4 * DN_V_HEADS
DN_CHUNK = 64
CONV_K = 5

V7X_VMEM_LIMIT_BYTES = 56 * 1024 * 1024
LANES = 128
SUBLANES = 8

ROW_TILE = 512
CONV_ROW_TILE = 256


def _cparams(n_axes):
    return pltpu.CompilerParams(dimension_semantics=("arbitrary",) * n_axes,
                                vmem_limit_bytes=V7X_VMEM_LIMIT_BYTES)


def _silu(x):
    return x * jax.nn.sigmoid(x)


def _split_bf16(a):
    hi = a.astype(BF16)
    lo = (a - hi.astype(F32)).astype(BF16)
    return hi, lo


def _dot(a, b):
    return jnp.dot(a, b, preferred_element_type=F32)


def _dot_nt(a, b):
    return lax.dot_general(a, b, (((1,), (1,)), ((), ())), preferred_element_type=F32)


def _dot3(a, b):
    ah, al = _split_bf16(a)
    bh, bl = _split_bf16(b)
    return _dot(ah, bh) + _dot(ah, bl) + _dot(al, bh)


def _ada_kernel(c_ref, w_ref, b_ref, o_ref):
    s = _silu(c_ref[...])
    o_ref[0] = _dot3(s, w_ref[0]) + b_ref[0]


def _ada_mod(cond, ada_w, ada_b):
    depth = ada_w.shape[0]
    tn = 512
    return pl.pallas_call(
        _ada_kernel,
        grid=(depth, 3 * D_MODEL // tn),
        in_specs=[pl.BlockSpec((SUBLANES, D_MODEL), lambda l, j: (0, 0)),
                  pl.BlockSpec((1, D_MODEL, tn), lambda l, j: (l, 0, j)),
                  pl.BlockSpec((1, 1, tn), lambda l, j: (l, 0, j))],
        out_specs=pl.BlockSpec((1, SUBLANES, tn), lambda l, j: (l, 0, j)),
        out_shape=jax.ShapeDtypeStruct((depth, SUBLANES, 3 * D_MODEL), F32),
        compiler_params=_cparams(2),
        name="ada_mod",
    )(cond, ada_w, ada_b.reshape(depth, 1, 3 * D_MODEL))


INPROJ_COL_TILES = (512, 1024, 1408, 1536)


def _inproj_kernel(x_ref, nw_ref, mod_ref, w_ref, *rest):
    h_ref = rest[-1]
    has_side = len(rest) == 4

    @pl.when(pl.program_id(1) == 0)
    def _():
        x = x_ref[...]
        y = x * lax.rsqrt(jnp.mean(x * x, axis=-1, keepdims=True) + EPS) * nw_ref[...]
        shift = mod_ref[:, 0:D_MODEL]
        scale = mod_ref[:, D_MODEL:2 * D_MODEL]
        h_ref[...] = (y * (1.0 + scale) + shift).astype(BF16)
        if has_side:
            rest[2][...] = _dot(h_ref[...], rest[0][...])

    o_ref = rest[1] if has_side else rest[0]
    o_ref[...] = _dot(h_ref[...], w_ref[...]).astype(o_ref.dtype)


def _mod_row(i, lat_tiles, batch):
    return jnp.minimum(i // lat_tiles, batch)


def _inproj(x, norm_w, mod, w_bf16, batch, seq, w_side_bf16=None, out_dtype=F32):
    m = x.shape[0]
    n = w_bf16.shape[1]
    tm = ROW_TILE
    tn = max(t for t in INPROJ_COL_TILES if n % t == 0)
    lat_tiles = seq // tm
    in_specs = [pl.BlockSpec((tm, D_MODEL), lambda i, j: (i, 0)),
                pl.BlockSpec((1, D_MODEL), lambda i, j: (0, 0)),
                pl.BlockSpec((None, 1, 3 * D_MODEL), lambda i, j: (_mod_row(i, lat_tiles, batch), 0, 0)),
                pl.BlockSpec((D_MODEL, tn), lambda i, j: (0, j))]
    out_specs = [pl.BlockSpec((tm, tn), lambda i, j: (i, j))]
    out_shape = [jax.ShapeDtypeStruct((m, n), out_dtype)]
    args = [x, norm_w.reshape(1, D_MODEL), mod, w_bf16]
    if w_side_bf16 is not None:
        ns = w_side_bf16.shape[1]
        in_specs.append(pl.BlockSpec((D_MODEL, ns), lambda i, j: (0, 0)))
        out_specs.append(pl.BlockSpec((tm, ns), lambda i, j: (i, 0)))
        out_shape.append(jax.ShapeDtypeStruct((m, ns), F32))
        args.append(w_side_bf16)
    res = pl.pallas_call(
        _inproj_kernel,
        grid=(m // tm, n // tn),
        in_specs=in_specs,
        out_specs=out_specs,
        out_shape=out_shape,
        scratch_shapes=[pltpu.VMEM((tm, D_MODEL), BF16)],
        compiler_params=_cparams(2),
        name="inproj",
    )(*args)
    return res if w_side_bf16 is not None else res[0]


Q_PRESCALE = (HEAD_DIM ** -0.5) * 1.4426950408889634


def _qkprep_kernel(q_ref, k_ref, v_ref, cos_ref, sin_ref, qw_ref, kw_ref, qo_ref, ko_ref, vo_ref):
    cos = cos_ref[...]
    sin = sin_ref[...]
    lane = lax.broadcasted_iota(jnp.int32, cos.shape, 1)
    first = (lane % (AXIS_DIM)) < (AXIS_DIM // 2)

    def prep(x, w):
        y = x * lax.rsqrt(jnp.mean(x * x, axis=-1, keepdims=True) + EPS) * w
        rot = jnp.where(first, pltpu.roll(y, HEAD_DIM - AXIS_DIM // 2, 1), pltpu.roll(y, AXIS_DIM // 2, 1))
        return y * cos + rot * sin

    for h in range(B_HEADS):
        sl = slice(h * HEAD_DIM, (h + 1) * HEAD_DIM)
        qo_ref[:, sl] = (prep(q_ref[:, sl], qw_ref[...]) * Q_PRESCALE).astype(qo_ref.dtype)
    for h in range(B_KV_HEADS):
        sl = slice(h * HEAD_DIM, (h + 1) * HEAD_DIM)
        ko_ref[:, sl] = prep(k_ref[:, sl], kw_ref[...]).astype(ko_ref.dtype)
    vo_ref[...] = v_ref[...].astype(vo_ref.dtype)


def _qkprep(proj, cos_tab, sin_tab, qn_w, kn_w, batch, seq):
    m = proj.shape[0]
    tm = CONV_ROW_TILE
    lat_tiles = seq // tm
    n_lat = batch * lat_tiles
    q_blk = (4 * A_WIDTH) // B_Q_W
    k_blk = (4 * A_WIDTH + B_Q_W) // B_KV_W

    def tab_idx(i):
        return (jnp.where(i < n_lat, i % lat_tiles, lat_tiles), 0)

    def kv_idx(i):
        lat_blk = (i // lat_tiles) * (lat_tiles + 1) + 1 + i % lat_tiles
        return (jnp.where(i < n_lat, lat_blk, (i - n_lat) * (lat_tiles + 1)), 0)

    return pl.pallas_call(
        _qkprep_kernel,
        grid=(m // tm,),
        in_specs=[pl.BlockSpec((tm, B_Q_W), lambda i: (i, q_blk)),
                  pl.BlockSpec((tm, B_KV_W), lambda i: (i, k_blk)),
                  pl.BlockSpec((tm, B_KV_W), lambda i: (i, k_blk + 1)),
                  pl.BlockSpec((tm, HEAD_DIM), tab_idx),
                  pl.BlockSpec((tm, HEAD_DIM), tab_idx),
                  pl.BlockSpec((1, HEAD_DIM), lambda i: (0, 0)),
                  pl.BlockSpec((1, HEAD_DIM), lambda i: (0, 0))],
        out_specs=[pl.BlockSpec((tm, B_Q_W), lambda i: (i, 0)),
                   pl.BlockSpec((tm, B_KV_W), kv_idx),
                   pl.BlockSpec((tm, B_KV_W), kv_idx)],
        out_shape=[jax.ShapeDtypeStruct((m, B_Q_W), BF16),
                   jax.ShapeDtypeStruct((m, B_KV_W), BF16),
                   jax.ShapeDtypeStruct((m, B_KV_W), BF16)],
        compiler_params=_cparams(1),
        name="qk_prep",
    )(proj, proj, proj, cos_tab, sin_tab, qn_w.reshape(1, HEAD_DIM), kn_w.reshape(1, HEAD_DIM))


def _rope_tables(seq, tm):
    rows = seq // GRID_W
    row = jnp.repeat(jnp.arange(rows), GRID_W).astype(F32)
    col = jnp.tile(jnp.arange(GRID_W), rows).astype(F32)
    freqs = ROPE_THETA ** (-jnp.arange(0, AXIS_DIM, 2, dtype=F32) / AXIS_DIM)
    ang_r = row[:, None] * freqs[None, :]
    ang_c = col[:, None] * freqs[None, :]
    ang = jnp.concatenate([ang_r, ang_r, ang_c, ang_c], axis=-1)
    sign = jnp.where((jnp.arange(HEAD_DIM) % AXIS_DIM) < AXIS_DIM // 2, -1.0, 1.0).astype(F32)
    cos = jnp.concatenate([jnp.cos(ang), jnp.ones((tm, HEAD_DIM), F32)], axis=0)
    sin = jnp.concatenate([jnp.sin(ang) * sign[None, :], jnp.zeros((tm, HEAD_DIM), F32)], axis=0)
    return cos, sin


ATTN_Q_TILE = 256
ATTN_KV_CHUNK = 768


def _attn_kernel(q_ref, k_ref, v_ref, o_ref, m_scr, l_scr, acc_scr, *, n_chunks, tk, tq):
    q = jnp.concatenate([q_ref[:, g * HEAD_DIM:(g + 1) * HEAD_DIM] for g in range(B_GROUP)], axis=0)
    m_scr[...] = jnp.full(m_scr.shape, -jnp.inf, F32)
    l_scr[...] = jnp.zeros(l_scr.shape, F32)
    acc_scr[...] = jnp.zeros(acc_scr.shape, F32)
    for c in range(n_chunks):
        k = k_ref[c * tk:(c + 1) * tk, :]
        v = v_ref[c * tk:(c + 1) * tk, :]
        s = _dot_nt(q, k)
        m_prev = m_scr[...]
        m_next = jnp.maximum(m_prev, jnp.max(s, axis=1, keepdims=True))
        p = jnp.exp2(s - jnp.concatenate([m_next] * (tk // LANES), axis=1))
        alpha = jnp.exp2(m_prev - m_next)
        l_scr[...] = alpha * l_scr[...] + jnp.sum(p, axis=1, keepdims=True)
        acc_scr[...] = acc_scr[...] * alpha + _dot(p.astype(BF16), v)
        m_scr[...] = m_next
    out = acc_scr[...] / l_scr[...]
    for g in range(B_GROUP):
        o_ref[:, g * HEAD_DIM:(g + 1) * HEAD_DIM] = out[g * tq:(g + 1) * tq].astype(o_ref.dtype)


def _attention(qr, kr, vr, batch, seq, ctx_len, latent, out):
    tq = ATTN_Q_TILE
    gw = B_GROUP * HEAD_DIM
    q_tiles = (seq if latent else ctx_len) // tq
    q_row0 = 0 if latent else batch * seq // tq
    kv_len = ctx_len + seq
    if latent:
        kv_rows, tk = kv_len, ATTN_KV_CHUNK
        kv_idx = lambda b, h, i: (b, h)
    else:
        kv_rows, tk = ctx_len, ctx_len
        kv_idx = lambda b, h, i: (b * (kv_len // ctx_len), h)
    assert kv_rows % tk == 0

    def q_idx(b, h, i):
        return (q_row0 + b * q_tiles + i, h)

    in_specs = [pl.BlockSpec((tq, gw), q_idx),
                pl.BlockSpec((kv_rows, HEAD_DIM), kv_idx),
                pl.BlockSpec((kv_rows, HEAD_DIM), kv_idx)]
    args = [qr, kr, vr]
    aliases = {}
    if out is not None:
        in_specs.append(pl.BlockSpec(memory_space=pl.ANY))
        args.append(out)
        aliases = {3: 0}
    rows = B_GROUP * tq
    body = functools.partial(_attn_kernel, n_chunks=kv_rows // tk, tk=tk, tq=tq)

    def kern(*refs):
        body(*refs[:3], *refs[len(args):])

    return pl.pallas_call(
        kern,
        grid=(batch, B_KV_HEADS, q_tiles),
        in_specs=in_specs,
        out_specs=pl.BlockSpec((tq, gw), q_idx),
        out_shape=jax.ShapeDtypeStruct((qr.shape[0], B_Q_W), F32),
        scratch_shapes=[pltpu.VMEM((rows, LANES), F32), pltpu.VMEM((rows, LANES), F32),
                        pltpu.VMEM((rows, HEAD_DIM), F32)],
        input_output_aliases=aliases,
        compiler_params=_cparams(3),
        name="attn_lat" if latent else "attn_ctx",
    )(*args)


def _ab_out_kernel(u_ref, v_ref, ga_ref, gb_ref, ao_ref, ws_ref, bs_ref, w_ref, x_ref, gate_ref, o_ref, y_ref):
    @pl.when(pl.program_id(1) == 0)
    def _():
        tm = u_ref.shape[0]
        for c in range(tm // SGU_CHUNK):
            rows = slice(c * SGU_CHUNK, (c + 1) * SGU_CHUNK)
            for g in range(A_GROUPS):
                cols = slice(g * LANES, (g + 1) * LANES)
                vg = v_ref[rows, cols]
                d = vg - jnp.mean(vg, axis=-1, keepdims=True)
                var = jnp.mean(d * d, axis=-1, keepdims=True)
                vn = (d * lax.rsqrt(var + 1e-5)).astype(BF16)
                mixed = _dot(ws_ref[g], vn) + bs_ref[:, g:g + 1]
                y_ref[rows, cols] = (u_ref[rows, cols] * mixed * _silu(ga_ref[rows, cols])).astype(BF16)
        y_ref[:, A_WIDTH:] = (ao_ref[...].astype(F32) * _silu(gb_ref[...])).astype(BF16)

    o_ref[...] = x_ref[...] + gate_ref[...] * _dot(y_ref[...], w_ref[...])


def _ab_out(proj, attn_o, sgu_w_bf16, sgu_b_t, w_out_bf16, x, mod, batch, seq):
    m = x.shape[0]
    tm, tn = ROW_TILE, 512
    lat_tiles = seq // tm
    gate_blk0 = 2 * D_MODEL // tn
    return pl.pallas_call(
        _ab_out_kernel,
        grid=(m // tm, D_MODEL // tn),
        in_specs=[pl.BlockSpec((tm, A_WIDTH), lambda i, j: (i, 0)),
                  pl.BlockSpec((tm, A_WIDTH), lambda i, j: (i, 1)),
                  pl.BlockSpec((tm, A_WIDTH), lambda i, j: (i, 2)),
                  pl.BlockSpec((tm, B_Q_W), lambda i, j: (i, 3)),
                  pl.BlockSpec((tm, B_Q_W), lambda i, j: (i, 0)),
                  pl.BlockSpec((A_GROUPS, SGU_CHUNK, SGU_CHUNK), lambda i, j: (0, 0, 0)),
                  pl.BlockSpec((SGU_CHUNK, A_GROUPS), lambda i, j: (0, 0)),
                  pl.BlockSpec((A_WIDTH + B_Q_W, tn), lambda i, j: (0, j)),
                  pl.BlockSpec((tm, tn), lambda i, j: (i, j)),
                  pl.BlockSpec((None, 1, tn), lambda i, j: (_mod_row(i, lat_tiles, batch), 0, gate_blk0 + j))],
        out_specs=pl.BlockSpec((tm, tn), lambda i, j: (i, j)),
        out_shape=jax.ShapeDtypeStruct((m, D_MODEL), F32),
        scratch_shapes=[pltpu.VMEM((tm, A_WIDTH + B_Q_W), BF16)],
        compiler_params=_cparams(2),
        name="ab_out",
    )(proj, proj, proj, proj, attn_o, sgu_w_bf16, sgu_b_t, w_out_bf16, x, mod)


def _dnconv_kernel(xp_ref, xc_ref, xn_ref, w_ref, o_ref, xe_ref, *, lat_tiles, n_lat, q_tiles, k_tiles):
    i = pl.program_id(0)
    j = pl.program_id(1)
    tm = xc_ref.shape[0]
    pos = i % lat_tiles
    is_lat = i < n_lat
    first = jnp.logical_or(jnp.logical_not(is_lat), pos == 0)
    last = jnp.logical_or(jnp.logical_not(is_lat), pos == lat_tiles - 1)
    xe_ref[0:SUBLANES, :] = jnp.where(first, 0.0, xp_ref[...])
    xe_ref[SUBLANES:SUBLANES + tm, :] = xc_ref[...]
    xe_ref[SUBLANES + tm:2 * SUBLANES + tm, :] = jnp.where(last, 0.0, xn_ref[...])
    half = CONV_K // 2
    acc = w_ref[0:1, :] * xe_ref[SUBLANES - half:SUBLANES - half + tm, :]
    for t in range(1, CONV_K):
        acc = acc + w_ref[t:t + 1, :] * xe_ref[SUBLANES - half + t:SUBLANES - half + t + tm, :]
    y = _silu(acc)

    def l2n(scale):
        for h in range(y.shape[1] // HEAD_DIM):
            sl = slice(h * HEAD_DIM, (h + 1) * HEAD_DIM)
            yh = y[:, sl]
            n = yh * lax.rsqrt(jnp.sum(yh * yh, axis=-1, keepdims=True) + EPS)
            o_ref[:, sl] = (n * scale).astype(o_ref.dtype) if scale != 1.0 else n.astype(o_ref.dtype)

    @pl.when(j < q_tiles)
    def _():
        l2n(HEAD_DIM ** -0.5)

    @pl.when(jnp.logical_and(j >= q_tiles, j < q_tiles + k_tiles))
    def _():
        l2n(1.0)

    @pl.when(j >= q_tiles + k_tiles)
    def _():
        o_ref[...] = y.astype(o_ref.dtype)


def _dn_conv(proj, conv_w, batch, seq, out_dtype=F32):
    m = proj.shape[0]
    tm, tc = CONV_ROW_TILE, 1024
    lat_tiles = seq // tm
    n_lat = batch * lat_tiles
    sub_per_tile = tm // SUBLANES
    n_sub = m // SUBLANES
    kern = functools.partial(_dnconv_kernel, lat_tiles=lat_tiles, n_lat=n_lat,
                             q_tiles=DN_K_W // tc, k_tiles=DN_K_W // tc)
    return pl.pallas_call(
        kern,
        grid=(m // tm, DN_QKV_W // tc),
        in_specs=[pl.BlockSpec((SUBLANES, tc), lambda i, j: (jnp.maximum(i * sub_per_tile - 1, 0), j)),
                  pl.BlockSpec((tm, tc), lambda i, j: (i, j)),
                  pl.BlockSpec((SUBLANES, tc), lambda i, j: (jnp.minimum((i + 1) * sub_per_tile, n_sub - 1), j)),
                  pl.BlockSpec((SUBLANES, tc), lambda i, j: (0, j))],
        out_specs=pl.BlockSpec((tm, tc), lambda i, j: (i, j)),
        out_shape=jax.ShapeDtypeStruct((m, DN_QKV_W), out_dtype),
        scratch_shapes=[pltpu.VMEM((tm + 2 * SUBLANES, tc), F32)],
        compiler_params=_cparams(2),
        name="dn_conv",
    )(proj, proj, proj, conv_w)


def _dngate_kernel(ba_ref, alog_ref, dtb_ref, o_ref):
    ba = ba_ref[...]
    tm = ba.shape[0]
    lane = lax.broadcasted_iota(jnp.int32, (DN_CHUNK, LANES), 1)
    is_beta = (lane // DN_V_HEADS) % 2 == 0
    is_fwd = lane < 2 * DN_V_HEADS
    z = ba + dtb_ref[...]
    softplus = jnp.maximum(z, 0.0) + jnp.log1p(jnp.exp(-jnp.abs(z)))
    g = -jnp.exp(alog_ref[...]) * softplus
    r = lax.broadcasted_iota(jnp.int32, (DN_CHUNK, DN_CHUNK), 0)
    c = lax.broadcasted_iota(jnp.int32, (DN_CHUNK, DN_CHUNK), 1)
    tri_lo = (r >= c).astype(BF16)
    tri_up = (r <= c).astype(BF16)
    beta = jax.nn.sigmoid(ba)
    for ch in range(tm // DN_CHUNK):
        rows = slice(ch * DN_CHUNK, (ch + 1) * DN_CHUNK)
        gch = g[rows]
        g1 = gch.astype(BF16)
        r1 = gch - g1.astype(F32)
        g2 = r1.astype(BF16)
        g3 = (r1 - g2.astype(F32)).astype(BF16)
        pre = _dot(tri_lo, g1) + _dot(tri_lo, g2) + _dot(tri_lo, g3)
        suf = _dot(tri_up, g1) + _dot(tri_up, g2) + _dot(tri_up, g3)
        gc = jnp.where(is_fwd, pre, suf)
        o_ref[rows, :] = jnp.where(is_beta, beta[rows], gc)


def _dn_gate(ba, alog_vec, dtb_vec):
    m = ba.shape[0]
    tm = ROW_TILE
    return pl.pallas_call(
        _dngate_kernel,
        grid=(m // tm,),
        in_specs=[pl.BlockSpec((tm, LANES), lambda i: (i, 0)),
                  pl.BlockSpec((1, LANES), lambda i: (0, 0)),
                  pl.BlockSpec((1, LANES), lambda i: (0, 0))],
        out_specs=pl.BlockSpec((tm, LANES), lambda i: (i, 0)),
        out_shape=jax.ShapeDtypeStruct((m, LANES), F32),
        compiler_params=_cparams(1),
        name="dn_gate",
    )(ba, alog_vec, dtb_vec)


DN_KH_PER_STEP = 16


def _dncore_kernel(qf_ref, kf_ref, vf_ref, gf_ref, qb_ref, kb_ref, vb_ref, gb_ref, of_ref, ob_ref, s_ref):
    C = DN_CHUNK
    nvh = 2 * DN_KH_PER_STEP

    @pl.when(pl.program_id(2) == 0)
    def _():
        s_ref[...] = jnp.zeros(s_ref.shape, F32)

    lane = lax.broadcasted_iota(jnp.int32, (C, 2 * C), 1)
    left = lane < C
    row = lax.broadcasted_iota(jnp.int32, (C, 2 * C), 0)
    colp = lane % C
    left_sq = lax.broadcasted_iota(jnp.int32, (2 * C, 2 * C), 1) < C
    eye2 = (row == colp).astype(F32)

    def blockdiag(p):
        z = jnp.zeros_like(p)
        return jnp.concatenate([jnp.where(left, p, z), jnp.where(left, z, p)], axis=0)

    def packed_mm(a, b):
        return _dot(a.astype(BF16), blockdiag(b.astype(BF16)))

    dirs = ((qf_ref, kf_ref, vf_ref, gf_ref, of_ref), (qb_ref, kb_ref, vb_ref, gb_ref, ob_ref))
    chains = [(d, kh) for d in range(2) for kh in range(DN_KH_PER_STEP)]
    G = [dirs[d][3][...] for d in range(2)]
    GT = [jnp.concatenate([g, g], axis=0).T for g in G]

    st = []
    for d, kh in chains:
        q_ref, k_ref, v_ref, _, _ = dirs[d]
        base_beta = d * 2 * nvh
        base_gc = base_beta + nvh
        lv0 = 2 * kh
        q = q_ref[:, kh * HEAD_DIM:(kh + 1) * HEAD_DIM]
        k = k_ref[:, kh * HEAD_DIM:(kh + 1) * HEAD_DIM]
        k2 = jnp.concatenate([k, k], axis=0)
        kT2 = k2.T
        gram = _dot(jnp.concatenate([q, k], axis=0).astype(BF16), kT2.astype(BF16))
        b0, b1 = G[d][:, base_beta + lv0:base_beta + lv0 + 1], G[d][:, base_beta + lv0 + 1:base_beta + lv0 + 2]
        c0, c1 = G[d][:, base_gc + lv0:base_gc + lv0 + 1], G[d][:, base_gc + lv0 + 1:base_gc + lv0 + 2]
        r0, r1 = GT[d][base_gc + lv0:base_gc + lv0 + 1, :], GT[d][base_gc + lv0 + 1:base_gc + lv0 + 2, :]
        st.append(dict(d=d, lv0=lv0, q=q, k2=k2, kT2=kT2, gram=gram, b0=b0, b1=b1, c0=c0, c1=c1, r0=r0, r1=r1))

    for s in st:
        d = s["d"]
        incl = (row >= colp) if d == 0 else (row <= colp)
        strict = (row > colp) if d == 0 else (row < colp)
        gcol_p = jnp.where(left, s["c0"], s["c1"])
        grow_p = jnp.where(left[0:1], s["r0"], s["r1"])
        beta_p = jnp.where(left, s["b0"], s["b1"])
        dec = jnp.exp(jnp.where(incl, gcol_p - grow_p, -1e30))
        s["dec"] = dec
        s["attn"] = s["gram"][0:C] * dec
        s["L"] = jnp.where(strict, s["gram"][C:2 * C] * dec, 0.0) * beta_p

    for s in st:
        n1 = jnp.where(jnp.logical_and(row // 2 == colp // 2, row != colp), s["L"], 0.0)
        s["X"] = eye2 - n1
    blk = 2
    while blk < C:
        mask = jnp.logical_and(row // (2 * blk) == colp // (2 * blk), row // blk != colp // blk)
        for s in st:
            s["Y"] = packed_mm(s["X"], jnp.where(mask, s["L"], 0.0))
        for s in st:
            s["X"] = s["X"] - packed_mm(s["Y"], s["X"])
        blk *= 2

    for s in st:
        d, lv0 = s["d"], s["lv0"]
        v_ref = dirs[d][2]
        beta_r = jnp.concatenate([s["b0"], s["b1"]], axis=0)
        egc_r = jnp.exp(jnp.concatenate([s["c0"], s["c1"]], axis=0))
        v2 = jnp.concatenate([v_ref[:, lv0 * HEAD_DIM:(lv0 + 1) * HEAD_DIM],
                              v_ref[:, (lv0 + 1) * HEAD_DIM:(lv0 + 2) * HEAD_DIM]], axis=0)
        rhs = jnp.concatenate([v2 * beta_r, s["k2"] * (beta_r * egc_r)], axis=1)
        s["sol"] = _dot(blockdiag(s["X"]).astype(BF16), rhs.astype(BF16))
        s["egc"] = egc_r
        s["qg2"] = jnp.concatenate([s["q"], s["q"]], axis=0) * egc_r

    for s in st:
        d, lv0 = s["d"], s["lv0"]
        w2 = s["sol"][:, HEAD_DIM:]
        s["ws"] = []
        for r in range(2):
            lhs = jnp.concatenate([w2[r * C:(r + 1) * C], s["qg2"][r * C:(r + 1) * C]], axis=0).astype(BF16)
            s["ws"].append(_dot(lhs, s_ref[d, lv0 + r].astype(BF16)))

    for s in st:
        u2 = s["sol"][:, 0:HEAD_DIM]
        vn2 = jnp.concatenate([u2[r * C:(r + 1) * C] - s["ws"][r][0:C] for r in range(2)], axis=0).astype(BF16)
        s["vn2"] = vn2
        s["o2"] = jnp.concatenate([s["ws"][r][C:2 * C] for r in range(2)], axis=0) + _dot(
            blockdiag(s["attn"]).astype(BF16), vn2)

    for s in st:
        d, lv0 = s["d"], s["lv0"]
        o_ref = dirs[d][4]
        last = C - 1 if d == 0 else 0
        kdT_p = s["kT2"] * s["dec"][last:last + 1, :]
        zkd = jnp.zeros_like(kdT_p)
        for r in range(2):
            kd_r = jnp.where(left_sq if r == 0 else jnp.logical_not(left_sq), kdT_p, zkd).astype(BF16)
            glr = jnp.broadcast_to(s["egc"][r * C + last:r * C + last + 1, :], (HEAD_DIM, HEAD_DIM))
            s_ref[d, lv0 + r] = s_ref[d, lv0 + r] * glr + _dot(kd_r, s["vn2"])
            o_ref[:, (lv0 + r) * HEAD_DIM:(lv0 + r + 1) * HEAD_DIM] = s["o2"][r * C:(r + 1) * C].astype(o_ref.dtype)


def _dn_core(qkv, gates, batch, seq, ctx_len):
    m = qkv.shape[0]
    C = DN_CHUNK
    n_lat = seq // C
    n_ctx = ctx_len // C
    n_steps = n_ctx + n_lat
    ctx0 = batch * n_lat
    khs = DN_KH_PER_STEP
    qw = khs * HEAD_DIM
    vw = 2 * khs * HEAD_DIM
    k_blk0 = DN_K_W // qw
    v_blk0 = 2 * DN_K_W // vw

    def rf(b, t):
        return jnp.where(t < n_ctx, ctx0 + b * n_ctx + t, b * n_lat + (t - n_ctx))

    def rb(b, t):
        return jnp.where(t < n_ctx, ctx0 + b * n_ctx + (n_ctx - 1 - t), b * n_lat + (n_lat - 1 - (t - n_ctx)))

    def specs(rfun):
        return [pl.BlockSpec((C, qw), lambda b, h, t: (rfun(b, t), h)),
                pl.BlockSpec((C, qw), lambda b, h, t: (rfun(b, t), k_blk0 + h)),
                pl.BlockSpec((C, vw), lambda b, h, t: (rfun(b, t), v_blk0 + h)),
                pl.BlockSpec((C, LANES), lambda b, h, t: (rfun(b, t), h))]

    return pl.pallas_call(
        _dncore_kernel,
        grid=(batch, DN_K_HEADS // khs, n_steps),
        in_specs=specs(rf) + specs(rb),
        out_specs=[pl.BlockSpec((C, vw), lambda b, h, t: (rf(b, t), h)),
                   pl.BlockSpec((C, vw), lambda b, h, t: (rb(b, t), h))],
        out_shape=[jax.ShapeDtypeStruct((m, DN_V_W), F32), jax.ShapeDtypeStruct((m, DN_V_W), F32)],
        scratch_shapes=[pltpu.VMEM((2, 2 * khs, HEAD_DIM, HEAD_DIM), F32)],
        compiler_params=_cparams(3),
        name="dn_core",
    )(qkv, qkv, qkv, gates, qkv, qkv, qkv, gates)


def _dn_out_kernel(of_ref, ob_ref, z_ref, nw_ref, w_ref, x_ref, gate_ref, o_ref, y_ref):
    @pl.when(pl.program_id(1) == 0)
    def _():
        for h in range(DN_V_HEADS):
            sl = slice(h * HEAD_DIM, (h + 1) * HEAD_DIM)
            o = of_ref[:, sl] + ob_ref[:, sl]
            n = o * lax.rsqrt(jnp.mean(o * o, axis=-1, keepdims=True) + EPS) * nw_ref[...]
            y_ref[:, sl] = (n * _silu(z_ref[:, sl])).astype(BF16)

    o_ref[...] = x_ref[...] + gate_ref[...] * _dot(y_ref[...], w_ref[...])


def _dn_out(o_f, o_b, proj, norm_w, w_out_bf16, x, mod, batch, seq):
    m = x.shape[0]
    tm, tn = CONV_ROW_TILE, 512
    lat_tiles = seq // tm
    gate_blk0 = 2 * D_MODEL // tn
    z_blk = DN_QKV_W // DN_V_W
    return pl.pallas_call(
        _dn_out_kernel,
        grid=(m // tm, D_MODEL // tn),
        in_specs=[pl.BlockSpec((tm, DN_V_W), lambda i, j: (i, 0)),
                  pl.BlockSpec((tm, DN_V_W), lambda i, j: (i, 0)),
                  pl.BlockSpec((tm, DN_V_W), lambda i, j: (i, z_blk)),
                  pl.BlockSpec((1, HEAD_DIM), lambda i, j: (0, 0)),
                  pl.BlockSpec((DN_V_W, tn), lambda i, j: (0, j)),
                  pl.BlockSpec((tm, tn), lambda i, j: (i, j)),
                  pl.BlockSpec((None, 1, tn), lambda i, j: (_mod_row(i, lat_tiles, batch), 0, gate_blk0 + j))],
        out_specs=pl.BlockSpec((tm, tn), lambda i, j: (i, j)),
        out_shape=jax.ShapeDtypeStruct((m, D_MODEL), F32),
        scratch_shapes=[pltpu.VMEM((tm, DN_V_W), BF16)],
        compiler_params=_cparams(2),
        name="dn_out",
    )(o_f, o_b, proj, norm_w.reshape(1, HEAD_DIM), w_out_bf16, x, mod)


def _final_norm_kernel(x_ref, w_ref, o_ref):
    x = x_ref[...]
    o_ref[...] = x * lax.rsqrt(jnp.mean(x * x, axis=-1, keepdims=True) + EPS) * w_ref[...]


def _final_norm(x, w, rows):
    tm = ROW_TILE
    return pl.pallas_call(
        _final_norm_kernel,
        grid=(rows // tm,),
        in_specs=[pl.BlockSpec((tm, D_MODEL), lambda i: (i, 0)),
                  pl.BlockSpec((1, D_MODEL), lambda i: (0, 0))],
        out_specs=pl.BlockSpec((tm, D_MODEL), lambda i: (i, 0)),
        out_shape=jax.ShapeDtypeStruct((rows, D_MODEL), F32),
        compiler_params=_cparams(1),
        name="final_norm",
    )(x, w.reshape(1, D_MODEL))


def kernel(x, c, ctx, c_ctx, norm_w, ada_w, ada_b, ab_w_in, ab_w_out, sgu_w, sgu_b, q_norm_w, k_norm_w,
           dn_w_in, dn_conv_w, dn_a_log, dn_dt_bias, dn_norm_w, dn_w_out, final_norm_w):
    batch, seq, _ = x.shape
    ctx_len = ctx.shape[1]
    depth = norm_w.shape[0]
    assert ctx_len == CONV_ROW_TILE and seq % ROW_TILE == 0 and (batch * ctx_len) % ROW_TILE == 0
    n_lat_rows = batch * seq

    xs = jnp.concatenate([x.reshape(n_lat_rows, D_MODEL), ctx.reshape(batch * ctx_len, D_MODEL)], axis=0)
    cond = jnp.zeros((SUBLANES, D_MODEL), F32).at[0:batch].set(c).at[batch].set(c_ctx)
    mods = _ada_mod(cond, ada_w, ada_b)
    cos_tab, sin_tab = _rope_tables(seq, CONV_ROW_TILE)

    for i in range(depth):
        j = i // 2
        mod = mods[i].reshape(SUBLANES, 1, 3 * D_MODEL)
        if i % 2 == 0:
            w = ab_w_in[j]
            a3 = 3 * A_WIDTH
            w_in = jnp.concatenate([w[:, :a3], w[:, a3 + B_Q_W + 2 * B_KV_W:], w[:, a3:a3 + B_Q_W + 2 * B_KV_W]],
                                   axis=1).astype(BF16)
            proj = _inproj(xs, norm_w[i], mod, w_in, batch, seq)
            qr, kr, vr = _qkprep(proj, cos_tab, sin_tab, q_norm_w[j], k_norm_w[j], batch, seq)
            attn_o = _attention(qr, kr, vr, batch, seq, ctx_len, True, None)
            attn_o = _attention(qr, kr, vr, batch, seq, ctx_len, False, attn_o)
            xs = _ab_out(proj, attn_o, sgu_w[j].astype(BF16), sgu_b[j].T, ab_w_out[j].astype(BF16), xs, mod,
                         batch, seq)
        else:
            w = dn_w_in[j].astype(BF16)
            n_main = DN_QKV_W + DN_V_W
            proj, ba = _inproj(xs, norm_w[i], mod, w[:, :n_main], batch, seq, w_side_bf16=w[:, n_main:])
            conv_w = jnp.zeros((SUBLANES, DN_QKV_W), F32).at[0:CONV_K].set(dn_conv_w[j])
            qkv = _dn_conv(proj, conv_w, batch, seq)
            zeros = jnp.zeros((2, DN_V_HEADS), F32)
            alog_vec = jnp.concatenate([zeros, dn_a_log[j]], axis=1).reshape(1, LANES)
            dtb_vec = jnp.concatenate([zeros, dn_dt_bias[j]], axis=1).reshape(1, LANES)
            gates = _dn_gate(ba, alog_vec, dtb_vec)
            n_hg = DN_K_HEADS // DN_KH_PER_STEP
            nvh = 2 * DN_KH_PER_STEP
            g4 = gates.reshape(-1, 4, n_hg, nvh).transpose(0, 2, 1, 3).reshape(-1, n_hg, 4 * nvh)
            g4 = jnp.pad(g4, ((0, 0), (0, 0), (0, LANES - 4 * nvh))).reshape(-1, n_hg * LANES)
            o_f, o_b = _dn_core(qkv, g4, batch, seq, ctx_len)
            xs = _dn_out(o_f, o_b, proj, dn_norm_w[j], dn_w_out[j].astype(BF16), xs, mod, batch, seq)

    out = _final_norm(xs, final_norm_w, n_lat_rows)
    return out.reshape(batch, seq, D_MODEL)
```

```python
import functools

import jax
import jax.numpy as jnp
from jax import lax
from jax.experimental import pallas as pl
from jax.experimental.pallas import tpu as pltpu

F32 = jnp.float32
BF16 = jnp.bfloat16

D_MODEL = 2048
GRID_W = 64
EPS = 1e-6
HEAD_DIM = 128
A_WIDTH = D_MODEL // 2
A_GROUPS = A_WIDTH // 128
SGU_CHUNK = 128
B_HEADS = (D_MODEL // 2) // HEAD_DIM
B_KV_HEADS = B_HEADS // 4
B_GROUP = B_HEADS // B_KV_HEADS
B_Q_W = B_HEADS * HEAD_DIM
B_KV_W = B_KV_HEADS * HEAD_DIM
ROPE_THETA = 10000.0
AXIS_DIM = HEAD_DIM // 2
AB_IN_W = 3 * A_WIDTH + 2 * B_Q_W + 2 * B_KV_W
DN_K_HEADS = D_MODEL // HEAD_DIM
DN_V_HEADS = 2 * DN_K_HEADS
DN_K_W = DN_K_HEADS * HEAD_DIM
DN_V_W = DN_V_HEADS * HEAD_DIM
DN_QKV_W = 2 * DN_K_W + DN_V_W
DN_IN_W = DN_QKV_W + DN_V_W + 4 * DN_V_HEADS
DN_CHUNK = 64
CONV_K = 5

V7X_VMEM_LIMIT_BYTES = 56 * 1024 * 1024
LANES = 128
SUBLANES = 8

ROW_TILE = 512
CONV_ROW_TILE = 256


def _cparams(n_axes):
    return pltpu.CompilerParams(dimension_semantics=("arbitrary",) * n_axes,
                                vmem_limit_bytes=V7X_VMEM_LIMIT_BYTES)


def _silu(x):
    return x * jax.nn.sigmoid(x)


def _split_bf16(a):
    hi = a.astype(BF16)
    lo = (a - hi.astype(F32)).astype(BF16)
    return hi, lo


def _dot(a, b):
    return jnp.dot(a, b, preferred_element_type=F32)


def _dot_nt(a, b):
    return lax.dot_general(a, b, (((1,), (1,)), ((), ())), preferred_element_type=F32)


def _dot3(a, b):
    ah, al = _split_bf16(a)
    bh, bl = _split_bf16(b)
    return _dot(ah, bh) + _dot(ah, bl) + _dot(al, bh)


def _ada_kernel(c_ref, w_ref, b_ref, o_ref):
    s = _silu(c_ref[...])
    o_ref[0] = _dot3(s, w_ref[0]) + b_ref[0]


def _ada_mod(cond, ada_w, ada_b):
    depth = ada_w.shape[0]
    tn = 512
    return pl.pallas_call(
        _ada_kernel,
        grid=(depth, 3 * D_MODEL // tn),
        in_specs=[pl.BlockSpec((SUBLANES, D_MODEL), lambda l, j: (0, 0)),
                  pl.BlockSpec((1, D_MODEL, tn), lambda l, j: (l, 0, j)),
                  pl.BlockSpec((1, 1, tn), lambda l, j: (l, 0, j))],
        out_specs=pl.BlockSpec((1, SUBLANES, tn), lambda l, j: (l, 0, j)),
        out_shape=jax.ShapeDtypeStruct((depth, SUBLANES, 3 * D_MODEL), F32),
        compiler_params=_cparams(2),
        name="ada_mod",
    )(cond, ada_w, ada_b.reshape(depth, 1, 3 * D_MODEL))


INPROJ_COL_TILES = (512, 1024, 1408, 1536)


def _inproj_kernel(x_ref, nw_ref, mod_ref, w_ref, *rest):
    h_ref = rest[-1]
    has_side = len(rest) == 4

    @pl.when(pl.program_id(1) == 0)
    def _():
        x = x_ref[...]
        y = x * lax.rsqrt(jnp.mean(x * x, axis=-1, keepdims=True) + EPS) * nw_ref[...]
        shift = mod_ref[:, 0:D_MODEL]
        scale = mod_ref[:, D_MODEL:2 * D_MODEL]
        h_ref[...] = (y * (1.0 + scale) + shift).astype(BF16)
        if has_side:
            rest[2][...] = _dot(h_ref[...], rest[0][...])

    o_ref = rest[1] if has_side else rest[0]
    o_ref[...] = _dot(h_ref[...], w_ref[...]).astype(o_ref.dtype)


def _mod_row(i, lat_tiles, batch):
    return jnp.minimum(i // lat_tiles, batch)


def _inproj(x, norm_w, mod, w_bf16, batch, seq, w_side_bf16=None, out_dtype=F32):
    m = x.shape[0]
    n = w_bf16.shape[1]
    tm = ROW_TILE
    tn = max(t for t in INPROJ_COL_TILES if n % t == 0)
    lat_tiles = seq // tm
    in_specs = [pl.BlockSpec((tm, D_MODEL), lambda i, j: (i, 0)),
                pl.BlockSpec((1, D_MODEL), lambda i, j: (0, 0)),
                pl.BlockSpec((None, 1, 3 * D_MODEL), lambda i, j: (_mod_row(i, lat_tiles, batch), 0, 0)),
                pl.BlockSpec((D_MODEL, tn), lambda i, j: (0, j))]
    out_specs = [pl.BlockSpec((tm, tn), lambda i, j: (i, j))]
    out_shape = [jax.ShapeDtypeStruct((m, n), out_dtype)]
    args = [x, norm_w.reshape(1, D_MODEL), mod, w_bf16]
    if w_side_bf16 is not None:
        ns = w_side_bf16.shape[1]
        in_specs.append(pl.BlockSpec((D_MODEL, ns), lambda i, j: (0, 0)))
        out_specs.append(pl.BlockSpec((tm, ns), lambda i, j: (i, 0)))
        out_shape.append(jax.ShapeDtypeStruct((m, ns), F32))
        args.append(w_side_bf16)
    res = pl.pallas_call(
        _inproj_kernel,
        grid=(m // tm, n // tn),
        in_specs=in_specs,
        out_specs=out_specs,
        out_shape=out_shape,
        scratch_shapes=[pltpu.VMEM((tm, D_MODEL), BF16)],
        compiler_params=_cparams(2),
        name="inproj",
    )(*args)
    return res if w_side_bf16 is not None else res[0]


Q_PRESCALE = (HEAD_DIM ** -0.5) * 1.4426950408889634


def _qkprep_kernel(q_ref, k_ref, v_ref, cos_ref, sin_ref, qw_ref, kw_ref, qo_ref, ko_ref, vo_ref):
    cos = cos_ref[...]
    sin = sin_ref[...]
    lane = lax.broadcasted_iota(jnp.int32, cos.shape, 1)
    first = (lane % (AXIS_DIM)) < (AXIS_DIM // 2)

    def prep(x, w):
        y = x * lax.rsqrt(jnp.mean(x * x, axis=-1, keepdims=True) + EPS) * w
        rot = jnp.where(first, pltpu.roll(y, HEAD_DIM - AXIS_DIM // 2, 1), pltpu.roll(y, AXIS_DIM // 2, 1))
        return y * cos + rot * sin

    for h in range(B_HEADS):
        sl = slice(h * HEAD_DIM, (h + 1) * HEAD_DIM)
        qo_ref[:, sl] = (prep(q_ref[:, sl], qw_ref[...]) * Q_PRESCALE).astype(qo_ref.dtype)
    for h in range(B_KV_HEADS):
        sl = slice(h * HEAD_DIM, (h + 1) * HEAD_DIM)
        ko_ref[:, sl] = prep(k_ref[:, sl], kw_ref[...]).astype(ko_ref.dtype)
    vo_ref[...] = v_ref[...].astype(vo_ref.dtype)


def _qkprep(proj, cos_tab, sin_tab, qn_w, kn_w, batch, seq):
    m = proj.shape[0]
    tm = CONV_ROW_TILE
    lat_tiles = seq // tm
    n_lat = batch * lat_tiles
    q_blk = (4 * A_WIDTH) // B_Q_W
    k_blk = (4 * A_WIDTH + B_Q_W) // B_KV_W

    def tab_idx(i):
        return (jnp.where(i < n_lat, i % lat_tiles, lat_tiles), 0)

    def kv_idx(i):
        lat_blk = (i // lat_tiles) * (lat_tiles + 1) + 1 + i % lat_tiles
        return (jnp.where(i < n_lat, lat_blk, (i - n_lat) * (lat_tiles + 1)), 0)

    return pl.pallas_call(
        _qkprep_kernel,
        grid=(m // tm,),
        in_specs=[pl.BlockSpec((tm, B_Q_W), lambda i: (i, q_blk)),
                  pl.BlockSpec((tm, B_KV_W), lambda i: (i, k_blk)),
                  pl.BlockSpec((tm, B_KV_W), lambda i: (i, k_blk + 1)),
                  pl.BlockSpec((tm, HEAD_DIM), tab_idx),
                  pl.BlockSpec((tm, HEAD_DIM), tab_idx),
                  pl.BlockSpec((1, HEAD_DIM), lambda i: (0, 0)),
                  pl.BlockSpec((1, HEAD_DIM), lambda i: (0, 0))],
        out_specs=[pl.BlockSpec((tm, B_Q_W), lambda i: (i, 0)),
                   pl.BlockSpec((tm, B_KV_W), kv_idx),
                   pl.BlockSpec((tm, B_KV_W), kv_idx)],
        out_shape=[jax.ShapeDtypeStruct((m, B_Q_W), BF16),
                   jax.ShapeDtypeStruct((m, B_KV_W), BF16),
                   jax.ShapeDtypeStruct((m, B_KV_W), BF16)],
        compiler_params=_cparams(1),
        name="qk_prep",
    )(proj, proj, proj, cos_tab, sin_tab, qn_w.reshape(1, HEAD_DIM), kn_w.reshape(1, HEAD_DIM))


def _rope_tables(seq, tm):
    rows = seq // GRID_W
    row = jnp.repeat(jnp.arange(rows), GRID_W).astype(F32)
    col = jnp.tile(jnp.arange(GRID_W), rows).astype(F32)
    freqs = ROPE_THETA ** (-jnp.arange(0, AXIS_DIM, 2, dtype=F32) / AXIS_DIM)
    ang_r = row[:, None] * freqs[None, :]
    ang_c = col[:, None] * freqs[None, :]
    ang = jnp.concatenate([ang_r, ang_r, ang_c, ang_c], axis=-1)
    sign = jnp.where((jnp.arange(HEAD_DIM) % AXIS_DIM) < AXIS_DIM // 2, -1.0, 1.0).astype(F32)
    cos = jnp.concatenate([jnp.cos(ang), jnp.ones((tm, HEAD_DIM), F32)], axis=0)
    sin = jnp.concatenate([jnp.sin(ang) * sign[None, :], jnp.zeros((tm, HEAD_DIM), F32)], axis=0)
    return cos, sin


ATTN_Q_TILE = 256
ATTN_KV_CHUNK = 768


def _attn_kernel(q_ref, k_ref, v_ref, o_ref, m_scr, l_scr, acc_scr, *, n_chunks, tk, tq):
    q = jnp.concatenate([q_ref[:, g * HEAD_DIM:(g + 1) * HEAD_DIM] for g in range(B_GROUP)], axis=0)
    m_scr[...] = jnp.full(m_scr.shape, -jnp.inf, F32)
    l_scr[...] = jnp.zeros(l_scr.shape, F32)
    acc_scr[...] = jnp.zeros(acc_scr.shape, F32)
    for c in range(n_chunks):
        k = k_ref[c * tk:(c + 1) * tk, :]
        v = v_ref[c * tk:(c + 1) * tk, :]
        s = _dot_nt(q, k)
        m_prev = m_scr[...]
        m_next = jnp.maximum(m_prev, jnp.max(s, axis=1, keepdims=True))
        p = jnp.exp2(s - jnp.concatenate([m_next] * (tk // LANES), axis=1))
        alpha = jnp.exp2(m_prev - m_next)
        l_scr[...] = alpha * l_scr[...] + jnp.sum(p, axis=1, keepdims=True)
        acc_scr[...] = acc_scr[...] * alpha + _dot(p.astype(BF16), v)
        m_scr[...] = m_next
    out = acc_scr[...] / l_scr[...]
    for g in range(B_GROUP):
        o_ref[:, g * HEAD_DIM:(g + 1) * HEAD_DIM] = out[g * tq:(g + 1) * tq].astype(o_ref.dtype)


def _attention(qr, kr, vr, batch, seq, ctx_len, latent):
    tq = ATTN_Q_TILE
    gw = B_GROUP * HEAD_DIM
    q_len = seq if latent else ctx_len
    q_tiles = q_len // tq
    q_row0 = 0 if latent else batch * seq // tq
    kv_len = ctx_len + seq
    if latent:
        kv_rows, tk = kv_len, ATTN_KV_CHUNK
        kv_idx = lambda b, h, i: (b, h)
    else:
        kv_rows, tk = ctx_len, ctx_len
        kv_idx = lambda b, h, i: (b * (kv_len // ctx_len), h)
    assert kv_rows % tk == 0
    rows = B_GROUP * tq
    return pl.pallas_call(
        functools.partial(_attn_kernel, n_chunks=kv_rows // tk, tk=tk, tq=tq),
        grid=(batch, B_KV_HEADS, q_tiles),
        in_specs=[pl.BlockSpec((tq, gw), lambda b, h, i: (q_row0 + b * q_tiles + i, h)),
                  pl.BlockSpec((kv_rows, HEAD_DIM), kv_idx),
                  pl.BlockSpec((kv_rows, HEAD_DIM), kv_idx)],
        out_specs=pl.BlockSpec((tq, gw), lambda b, h, i: (b * q_tiles + i, h)),
        out_shape=jax.ShapeDtypeStruct((batch * q_len, B_Q_W), F32),
        scratch_shapes=[pltpu.VMEM((rows, LANES), F32), pltpu.VMEM((rows, LANES), F32),
                        pltpu.VMEM((rows, HEAD_DIM), F32)],
        compiler_params=_cparams(3),
        name="attn_lat" if latent else "attn_ctx",
    )(qr, kr, vr)


def _ab_out_kernel(u_ref, v_ref, ga_ref, gb_ref, aol_ref, aoc_ref, ws_ref, bs_ref, w_ref, x_ref, gate_ref, o_ref,
                   y_ref, acc_ref, *, n_lat_tiles):
    k = pl.program_id(1)

    @pl.when(k == 0)
    def _():
        tm = u_ref.shape[0]
        for c in range(tm // SGU_CHUNK):
            rows = slice(c * SGU_CHUNK, (c + 1) * SGU_CHUNK)
            for g in range(A_GROUPS):
                cols = slice(g * LANES, (g + 1) * LANES)
                vg = v_ref[rows, cols]
                d = vg - jnp.mean(vg, axis=-1, keepdims=True)
                var = jnp.mean(d * d, axis=-1, keepdims=True)
                vn = (d * lax.rsqrt(var + 1e-5)).astype(BF16)
                mixed = _dot(ws_ref[g], vn) + bs_ref[:, g:g + 1]
                y_ref[rows, cols] = (u_ref[rows, cols] * mixed * _silu(ga_ref[rows, cols])).astype(BF16)
        acc_ref[...] = _dot(y_ref[...], w_ref[...])

    @pl.when(k == 1)
    def _():
        ao = jnp.where(pl.program_id(0) < n_lat_tiles, aol_ref[...], aoc_ref[...]).astype(F32)
        y = (ao * _silu(gb_ref[...])).astype(BF16)
        o_ref[...] = x_ref[...] + gate_ref[...] * (acc_ref[...] + _dot(y, w_ref[...]))


def _ab_out(proj, ao_lat, ao_ctx, sgu_w_bf16, sgu_b_t, w_out_bf16, x, mod, batch, seq):
    m = x.shape[0]
    tm = ROW_TILE
    lat_tiles = seq // tm
    n_lat_tiles = batch * lat_tiles
    assert ao_ctx.shape[0] == tm
    return pl.pallas_call(
        functools.partial(_ab_out_kernel, n_lat_tiles=n_lat_tiles),
        grid=(m // tm, 2),
        in_specs=[pl.BlockSpec((tm, A_WIDTH), lambda i, k: (i, 0)),
                  pl.BlockSpec((tm, A_WIDTH), lambda i, k: (i, 1)),
                  pl.BlockSpec((tm, A_WIDTH), lambda i, k: (i, 2)),
                  pl.BlockSpec((tm, B_Q_W), lambda i, k: (i, 3)),
                  pl.BlockSpec((tm, B_Q_W), lambda i, k: (jnp.minimum(i, n_lat_tiles - 1), 0)),
                  pl.BlockSpec((tm, B_Q_W), lambda i, k: (0, 0)),
                  pl.BlockSpec((A_GROUPS, SGU_CHUNK, SGU_CHUNK), lambda i, k: (0, 0, 0)),
                  pl.BlockSpec((SGU_CHUNK, A_GROUPS), lambda i, k: (0, 0)),
                  pl.BlockSpec((A_WIDTH, D_MODEL), lambda i, k: (k, 0)),
                  pl.BlockSpec((tm, D_MODEL), lambda i, k: (i, 0)),
                  pl.BlockSpec((None, 1, D_MODEL), lambda i, k: (_mod_row(i, lat_tiles, batch), 0, 2))],
        out_specs=pl.BlockSpec((tm, D_MODEL), lambda i, k: (i, 0)),
        out_shape=jax.ShapeDtypeStruct((m, D_MODEL), F32),
        scratch_shapes=[pltpu.VMEM((tm, A_WIDTH), BF16), pltpu.VMEM((tm, D_MODEL), F32)],
        compiler_params=_cparams(2),
        name="ab_out",
    )(proj, proj, proj, proj, ao_lat, ao_ctx, sgu_w_bf16, sgu_b_t, w_out_bf16, x, mod)


def _dnconv_kernel(xp_ref, xc_ref, xn_ref, w_ref, o_ref, *, lat_tiles, n_lat, q_tiles, k_tiles):
    i = pl.program_id(0)
    j = pl.program_id(1)
    tm = xc_ref.shape[0]
    pos = i % lat_tiles
    is_lat = i < n_lat
    first = jnp.logical_or(jnp.logical_not(is_lat), pos == 0)
    last = jnp.logical_or(jnp.logical_not(is_lat), pos == lat_tiles - 1)
    half = CONV_K // 2
    sub = lax.broadcasted_iota(jnp.int32, (SUBLANES, HEAD_DIM), 0)

    def conv_silu(sl):
        x = xc_ref[:, sl]
        prev = jnp.where(first, 0.0, xp_ref[:, sl])
        nxt = jnp.where(last, 0.0, xn_ref[:, sl])
        acc = w_ref[half:half + 1, sl] * x
        for t in range(CONV_K):
            s_rows = half - t
            if s_rows == 0:
                continue
            r = pltpu.roll(x, s_rows % tm, 0)
            if s_rows > 0:
                head = jnp.where(sub < s_rows, pltpu.roll(prev, s_rows, 0), r[0:SUBLANES])
                shifted = jnp.concatenate([head, r[SUBLANES:]], axis=0)
            else:
                tail = jnp.where(sub >= SUBLANES + s_rows, pltpu.roll(nxt, SUBLANES + s_rows, 0), r[tm - SUBLANES:])
                shifted = jnp.concatenate([r[:tm - SUBLANES], tail], axis=0)
            acc = acc + w_ref[t:t + 1, sl] * shifted
        return _silu(acc)

    heads = [slice(h * HEAD_DIM, (h + 1) * HEAD_DIM) for h in range(xc_ref.shape[1] // HEAD_DIM)]

    @pl.when(j < q_tiles + k_tiles)
    def _():
        qk_scale = jnp.where(j < q_tiles, HEAD_DIM ** -0.5, 1.0)
        for sl in heads:
            y = conv_silu(sl)
            o_ref[:, sl] = (y * (lax.rsqrt(jnp.sum(y * y, axis=-1, keepdims=True) + EPS) * qk_scale)).astype(o_ref.dtype)

    @pl.when(j >= q_tiles + k_tiles)
    def _():
        for sl in heads:
            o_ref[:, sl] = conv_silu(sl).astype(o_ref.dtype)


def _dn_conv(proj, conv_w, batch, seq, out_dtype=F32):
    m = proj.shape[0]
    tm, tc = CONV_ROW_TILE, 1024
    lat_tiles = seq // tm
    n_lat = batch * lat_tiles
    sub_per_tile = tm // SUBLANES
    n_sub = m // SUBLANES
    kern = functools.partial(_dnconv_kernel, lat_tiles=lat_tiles, n_lat=n_lat,
                             q_tiles=DN_K_W // tc, k_tiles=DN_K_W // tc)
    return pl.pallas_call(
        kern,
        grid=(m // tm, DN_QKV_W // tc),
        in_specs=[pl.BlockSpec((SUBLANES, tc), lambda i, j: (jnp.maximum(i * sub_per_tile - 1, 0), j)),
                  pl.BlockSpec((tm, tc), lambda i, j: (i, j)),
                  pl.BlockSpec((SUBLANES, tc), lambda i, j: (jnp.minimum((i + 1) * sub_per_tile, n_sub - 1), j)),
                  pl.BlockSpec((SUBLANES, tc), lambda i, j: (0, j))],
        out_specs=pl.BlockSpec((tm, tc), lambda i, j: (i, j)),
        out_shape=jax.ShapeDtypeStruct((m, DN_QKV_W), out_dtype),
        compiler_params=_cparams(2),
        name="dn_conv",
    )(proj, proj, proj, conv_w)


def _dngate_kernel(ba_ref, alog_ref, dtb_ref, o_ref):
    ba = ba_ref[...]
    tm = ba.shape[0]
    lane = lax.broadcasted_iota(jnp.int32, (DN_CHUNK, LANES), 1)
    is_beta = (lane // DN_V_HEADS) % 2 == 0
    is_fwd = lane < 2 * DN_V_HEADS
    z = ba + dtb_ref[...]
    softplus = jnp.maximum(z, 0.0) + jnp.log1p(jnp.exp(-jnp.abs(z)))
    g = -jnp.exp(alog_ref[...]) * softplus
    r = lax.broadcasted_iota(jnp.int32, (DN_CHUNK, DN_CHUNK), 0)
    c = lax.broadcasted_iota(jnp.int32, (DN_CHUNK, DN_CHUNK), 1)
    tri_lo = (r >= c).astype(BF16)
    tri_up = (r <= c).astype(BF16)
    beta = jax.nn.sigmoid(ba)
    for ch in range(tm // DN_CHUNK):
        rows = slice(ch * DN_CHUNK, (ch + 1) * DN_CHUNK)
        gch = g[rows]
        g1 = gch.astype(BF16)
        r1 = gch - g1.astype(F32)
        g2 = r1.astype(BF16)
        g3 = (r1 - g2.astype(F32)).astype(BF16)
        pre = _dot(tri_lo, g1) + _dot(tri_lo, g2) + _dot(tri_lo, g3)
        suf = _dot(tri_up, g1) + _dot(tri_up, g2) + _dot(tri_up, g3)
        gc = jnp.where(is_fwd, pre, suf)
        o_ref[rows, :] = jnp.where(is_beta, beta[rows], gc)


def _dn_gate(ba, alog_vec, dtb_vec):
    m = ba.shape[0]
    tm = ROW_TILE
    return pl.pallas_call(
        _dngate_kernel,
        grid=(m // tm,),
        in_specs=[pl.BlockSpec((tm, LANES), lambda i: (i, 0)),
                  pl.BlockSpec((1, LANES), lambda i: (0, 0)),
                  pl.BlockSpec((1, LANES), lambda i: (0, 0))],
        out_specs=pl.BlockSpec((tm, LANES), lambda i: (i, 0)),
        out_shape=jax.ShapeDtypeStruct((m, LANES), F32),
        compiler_params=_cparams(1),
        name="dn_gate",
    )(ba, alog_vec, dtb_vec)


DN_KH_PER_STEP = 16


def _dncore_kernel(qf_ref, kf_ref, vf_ref, gf_ref, qb_ref, kb_ref, vb_ref, gb_ref, of_ref, ob_ref, s_ref):
    C = DN_CHUNK
    nvh = 2 * DN_KH_PER_STEP

    @pl.when(pl.program_id(2) == 0)
    def _():
        s_ref[...] = jnp.zeros(s_ref.shape, F32)

    lane = lax.broadcasted_iota(jnp.int32, (C, 2 * C), 1)
    left = lane < C
    row = lax.broadcasted_iota(jnp.int32, (C, 2 * C), 0)
    colp = lane % C
    left_sq = lax.broadcasted_iota(jnp.int32, (2 * C, 2 * C), 1) < C
    eye2 = (row == colp).astype(F32)

    def blockdiag(p):
        z = jnp.zeros_like(p)
        return jnp.concatenate([jnp.where(left, p, z), jnp.where(left, z, p)], axis=0)

    def packed_mm(a, b):
        return _dot(a.astype(BF16), blockdiag(b.astype(BF16)))

    dirs = ((qf_ref, kf_ref, vf_ref, gf_ref, of_ref), (qb_ref, kb_ref, vb_ref, gb_ref, ob_ref))
    chains = [(d, kh) for d in range(2) for kh in range(DN_KH_PER_STEP)]
    G = [dirs[d][3][...] for d in range(2)]
    GT = [jnp.concatenate([g, g], axis=0).T for g in G]

    st = []
    for d, kh in chains:
        q_ref, k_ref, v_ref, _, _ = dirs[d]
        base_beta = d * 2 * nvh
        base_gc = base_beta + nvh
        lv0 = 2 * kh
        q = q_ref[:, kh * HEAD_DIM:(kh + 1) * HEAD_DIM]
        k = k_ref[:, kh * HEAD_DIM:(kh + 1) * HEAD_DIM]
        k2 = jnp.concatenate([k, k], axis=0)
        kT2 = k2.T
        gram = _dot(jnp.concatenate([q, k], axis=0).astype(BF16), kT2.astype(BF16))
        b0, b1 = G[d][:, base_beta + lv0:base_beta + lv0 + 1], G[d][:, base_beta + lv0 + 1:base_beta + lv0 + 2]
        c0, c1 = G[d][:, base_gc + lv0:base_gc + lv0 + 1], G[d][:, base_gc + lv0 + 1:base_gc + lv0 + 2]
        r0, r1 = GT[d][base_gc + lv0:base_gc + lv0 + 1, :], GT[d][base_gc + lv0 + 1:base_gc + lv0 + 2, :]
        st.append(dict(d=d, lv0=lv0, q=q, k2=k2, kT2=kT2, gram=gram, b0=b0, b1=b1, c0=c0, c1=c1, r0=r0, r1=r1))

    for s in st:
        d = s["d"]
        incl = (row >= colp) if d == 0 else (row <= colp)
        strict = (row > colp) if d == 0 else (row < colp)
        gcol_p = jnp.where(left, s["c0"], s["c1"])
        grow_p = jnp.where(left[0:1], s["r0"], s["r1"])
        beta_p = jnp.where(left, s["b0"], s["b1"])
        dec = jnp.exp(jnp.where(incl, gcol_p - grow_p, -1e30))
        s["dec"] = dec
        s["attn"] = s["gram"][0:C] * dec
        s["L"] = jnp.where(strict, s["gram"][C:2 * C] * dec, 0.0) * beta_p

    for s in st:
        n1 = jnp.where(jnp.logical_and(row // 2 == colp // 2, row != colp), s["L"], 0.0)
        s["X"] = eye2 - n1
    blk = 2
    while blk < C:
        mask = jnp.logical_and(row // (2 * blk) == colp // (2 * blk), row // blk != colp // blk)
        for s in st:
            s["Y"] = packed_mm(s["X"], jnp.where(mask, s["L"], 0.0))
        for s in st:
            s["X"] = s["X"] - packed_mm(s["Y"], s["X"])
        blk *= 2

    for s in st:
        d, lv0 = s["d"], s["lv0"]
        v_ref = dirs[d][2]
        beta_r = jnp.concatenate([s["b0"], s["b1"]], axis=0)
        egc_r = jnp.exp(jnp.concatenate([s["c0"], s["c1"]], axis=0))
        v2 = jnp.concatenate([v_ref[:, lv0 * HEAD_DIM:(lv0 + 1) * HEAD_DIM],
                              v_ref[:, (lv0 + 1) * HEAD_DIM:(lv0 + 2) * HEAD_DIM]], axis=0)
        rhs = jnp.concatenate([v2 * beta_r, s["k2"] * (beta_r * egc_r)], axis=1)
        s["sol"] = _dot(blockdiag(s["X"]).astype(BF16), rhs.astype(BF16))
        s["egc"] = egc_r
        s["qg2"] = jnp.concatenate([s["q"], s["q"]], axis=0) * egc_r

    for s in st:
        d, lv0 = s["d"], s["lv0"]
        w2 = s["sol"][:, HEAD_DIM:]
        s["ws"] = []
        for r in range(2):
            lhs = jnp.concatenate([w2[r * C:(r + 1) * C], s["qg2"][r * C:(r + 1) * C]], axis=0).astype(BF16)
            s["ws"].append(_dot(lhs, s_ref[d, lv0 + r].astype(BF16)))

    for s in st:
        u2 = s["sol"][:, 0:HEAD_DIM]
        vn2 = jnp.concatenate([u2[r * C:(r + 1) * C] - s["ws"][r][0:C] for r in range(2)], axis=0).astype(BF16)
        s["vn2"] = vn2
        s["o2"] = jnp.concatenate([s["ws"][r][C:2 * C] for r in range(2)], axis=0) + _dot(
            blockdiag(s["attn"]).astype(BF16), vn2)

    for s in st:
        d, lv0 = s["d"], s["lv0"]
        o_ref = dirs[d][4]
        last = C - 1 if d == 0 else 0
        kdT_p = s["kT2"] * s["dec"][last:last + 1, :]
        zkd = jnp.zeros_like(kdT_p)
        for r in range(2):
            kd_r = jnp.where(left_sq if r == 0 else jnp.logical_not(left_sq), kdT_p, zkd).astype(BF16)
            glr = jnp.broadcast_to(s["egc"][r * C + last:r * C + last + 1, :], (HEAD_DIM, HEAD_DIM))
            s_ref[d, lv0 + r] = s_ref[d, lv0 + r] * glr + _dot(kd_r, s["vn2"])
            o_ref[:, (lv0 + r) * HEAD_DIM:(lv0 + r + 1) * HEAD_DIM] = s["o2"][r * C:(r + 1) * C].astype(o_ref.dtype)


def _dn_core(qkv, gates, batch, seq, ctx_len):
    m = qkv.shape[0]
    C = DN_CHUNK
    n_lat = seq // C
    n_ctx = ctx_len // C
    n_steps = n_ctx + n_lat
    ctx0 = batch * n_lat
    khs = DN_KH_PER_STEP
    qw = khs * HEAD_DIM
    vw = 2 * khs * HEAD_DIM
    k_blk0 = DN_K_W // qw
    v_blk0 = 2 * DN_K_W // vw

    def rf(b, t):
        return jnp.where(t < n_ctx, ctx0 + b * n_ctx + t, b * n_lat + (t - n_ctx))

    def rb(b, t):
        return jnp.where(t < n_ctx, ctx0 + b * n_ctx + (n_ctx - 1 - t), b * n_lat + (n_lat - 1 - (t - n_ctx)))

    def specs(rfun):
        return [pl.BlockSpec((C, qw), lambda b, h, t: (rfun(b, t), h)),
                pl.BlockSpec((C, qw), lambda b, h, t: (rfun(b, t), k_blk0 + h)),
                pl.BlockSpec((C, vw), lambda b, h, t: (rfun(b, t), v_blk0 + h)),
                pl.BlockSpec((C, LANES), lambda b, h, t: (rfun(b, t), h))]

    return pl.pallas_call(
        _dncore_kernel,
        grid=(batch, DN_K_HEADS // khs, n_steps),
        in_specs=specs(rf) + specs(rb),
        out_specs=[pl.BlockSpec((C, vw), lambda b, h, t: (rf(b, t), h)),
                   pl.BlockSpec((C, vw), lambda b, h, t: (rb(b, t), h))],
        out_shape=[jax.ShapeDtypeStruct((m, DN_V_W), F32), jax.ShapeDtypeStruct((m, DN_V_W), F32)],
        scratch_shapes=[pltpu.VMEM((2, 2 * khs, HEAD_DIM, HEAD_DIM), F32)],
        compiler_params=_cparams(3),
        name="dn_core",
    )(qkv, qkv, qkv, gates, qkv, qkv, qkv, gates)


DN_OUT_K_TILE = 1024


def _dn_out_kernel(of_ref, ob_ref, z_ref, nw_ref, w_ref, x_ref, gate_ref, o_ref, acc_ref):
    k = pl.program_id(1)
    ys = []
    for h in range(of_ref.shape[1] // HEAD_DIM):
        sl = slice(h * HEAD_DIM, (h + 1) * HEAD_DIM)
        o = of_ref[:, sl].astype(F32) + ob_ref[:, sl].astype(F32)
        n = o * lax.rsqrt(jnp.mean(o * o, axis=-1, keepdims=True) + EPS) * nw_ref[...]
        ys.append((n * _silu(z_ref[:, sl].astype(F32))).astype(BF16))
    part = _dot(jnp.concatenate(ys, axis=1), w_ref[...])

    @pl.when(k == 0)
    def _():
        acc_ref[...] = part

    @pl.when(jnp.logical_and(k > 0, k < pl.num_programs(1) - 1))
    def _():
        acc_ref[...] += part

    @pl.when(k == pl.num_programs(1) - 1)
    def _():
        o_ref[...] = x_ref[...] + gate_ref[...] * (acc_ref[...] + part)


def _dn_out(o_f, o_b, proj, norm_w, w_out_bf16, x, mod, batch, seq):
    m = x.shape[0]
    tm, tk = ROW_TILE, DN_OUT_K_TILE
    lat_tiles = seq // tm
    z_blk0 = DN_QKV_W // tk
    return pl.pallas_call(
        _dn_out_kernel,
        grid=(m // tm, DN_V_W // tk),
        in_specs=[pl.BlockSpec((tm, tk), lambda i, k: (i, k)),
                  pl.BlockSpec((tm, tk), lambda i, k: (i, k)),
                  pl.BlockSpec((tm, tk), lambda i, k: (i, z_blk0 + k)),
                  pl.BlockSpec((1, HEAD_DIM), lambda i, k: (0, 0)),
                  pl.BlockSpec((tk, D_MODEL), lambda i, k: (k, 0)),
                  pl.BlockSpec((tm, D_MODEL), lambda i, k: (i, 0)),
                  pl.BlockSpec((None, 1, D_MODEL), lambda i, k: (_mod_row(i, lat_tiles, batch), 0, 2))],
        out_specs=pl.BlockSpec((tm, D_MODEL), lambda i, k: (i, 0)),
        out_shape=jax.ShapeDtypeStruct((m, D_MODEL), F32),
        scratch_shapes=[pltpu.VMEM((tm, D_MODEL), F32)],
        compiler_params=_cparams(2),
        name="dn_out",
    )(o_f, o_b, proj, norm_w.reshape(1, HEAD_DIM), w_out_bf16, x, mod)


def _final_norm_kernel(x_ref, w_ref, o_ref):
    x = x_ref[...]
    o_ref[...] = x * lax.rsqrt(jnp.mean(x * x, axis=-1, keepdims=True) + EPS) * w_ref[...]


def _final_norm(x, w, rows):
    tm = ROW_TILE
    return pl.pallas_call(
        _final_norm_kernel,
        grid=(rows // tm,),
        in_specs=[pl.BlockSpec((tm, D_MODEL), lambda i: (i, 0)),
                  pl.BlockSpec((1, D_MODEL), lambda i: (0, 0))],
        out_specs=pl.BlockSpec((tm, D_MODEL), lambda i: (i, 0)),
        out_shape=jax.ShapeDtypeStruct((rows, D_MODEL), F32),
        compiler_params=_cparams(1),
        name="final_norm",
    )(x, w.reshape(1, D_MODEL))


def kernel(x, c, ctx, c_ctx, norm_w, ada_w, ada_b, ab_w_in, ab_w_out, sgu_w, sgu_b, q_norm_w, k_norm_w,
           dn_w_in, dn_conv_w, dn_a_log, dn_dt_bias, dn_norm_w, dn_w_out, final_norm_w):
    batch, seq, _ = x.shape
    ctx_len = ctx.shape[1]
    depth = norm_w.shape[0]
    assert ctx_len == CONV_ROW_TILE and seq % ROW_TILE == 0 and (batch * ctx_len) % ROW_TILE == 0
    n_lat_rows = batch * seq

    xs = jnp.concatenate([x.reshape(n_lat_rows, D_MODEL), ctx.reshape(batch * ctx_len, D_MODEL)], axis=0)
    cond = jnp.zeros((SUBLANES, D_MODEL), F32).at[0:batch].set(c).at[batch].set(c_ctx)
    mods = _ada_mod(cond, ada_w, ada_b)
    cos_tab, sin_tab = _rope_tables(seq, CONV_ROW_TILE)

    for i in range(depth):
        j = i // 2
        mod = mods[i].reshape(SUBLANES, 1, 3 * D_MODEL)
        if i % 2 == 0:
            w = ab_w_in[j]
            a3 = 3 * A_WIDTH
            w_in = jnp.concatenate([w[:, :a3], w[:, a3 + B_Q_W + 2 * B_KV_W:], w[:, a3:a3 + B_Q_W + 2 * B_KV_W]],
                                   axis=1).astype(BF16)
            proj = _inproj(xs, norm_w[i], mod, w_in, batch, seq)
            qr, kr, vr = _qkprep(proj, cos_tab, sin_tab, q_norm_w[j], k_norm_w[j], batch, seq)
            ao_lat = _attention(qr, kr, vr, batch, seq, ctx_len, True)
            ao_ctx = _attention(qr, kr, vr, batch, seq, ctx_len, False)
            xs = _ab_out(proj, ao_lat, ao_ctx, sgu_w[j].astype(BF16), sgu_b[j].T, ab_w_out[j].astype(BF16), xs, mod,
                         batch, seq)
        else:
            w = dn_w_in[j].astype(BF16)
            n_main = DN_QKV_W + DN_V_W
            proj, ba = _inproj(xs, norm_w[i], mod, w[:, :n_main], batch, seq, w_side_bf16=w[:, n_main:])
            conv_w = jnp.zeros((SUBLANES, DN_QKV_W), F32).at[0:CONV_K].set(dn_conv_w[j])
            qkv = _dn_conv(proj, conv_w, batch, seq)
            zeros = jnp.zeros((2, DN_V_HEADS), F32)
            alog_vec = jnp.concatenate([zeros, dn_a_log[j]], axis=1).reshape(1, LANES)
            dtb_vec = jnp.concatenate([zeros, dn_dt_bias[j]], axis=1).reshape(1, LANES)
            gates = _dn_gate(ba, alog_vec, dtb_vec)
            n_hg = DN_K_HEADS // DN_KH_PER_STEP
            nvh = 2 * DN_KH_PER_STEP
            g4 = gates.reshape(-1, 4, n_hg, nvh).transpose(0, 2, 1, 3).reshape(-1, n_hg, 4 * nvh)
            g4 = jnp.pad(g4, ((0, 0), (0, 0), (0, LANES - 4 * nvh))).reshape(-1, n_hg * LANES)
            o_f, o_b = _dn_core(qkv, g4, batch, seq, ctx_len)
            xs = _dn_out(o_f, o_b, proj, dn_norm_w[j], dn_w_out[j].astype(BF16), xs, mod, batch, seq)

    out = _final_norm(xs, final_norm_w, n_lat_rows)
    return out.reshape(batch, seq, D_MODEL)
```

```python
import functools

import jax
import jax.numpy as jnp
from jax import lax
from jax.experimental import pallas as pl
from jax.experimental.pallas import tpu as pltpu

F32 = jnp.float32
BF16 = jnp.bfloat16

D_MODEL = 2048
GRID_W = 64
EPS = 1e-6
HEAD_DIM = 128
A_WIDTH = D_MODEL // 2
A_GROUPS = A_WIDTH // 128
SGU_CHUNK = 128
B_HEADS = (D_MODEL // 2) // HEAD_DIM
B_KV_HEADS = B_HEADS // 4
B_GROUP = B_HEADS // B_KV_HEADS
B_Q_W = B_HEADS * HEAD_DIM
B_KV_W = B_KV_HEADS * HEAD_DIM
ROPE_THETA = 10000.0
AXIS_DIM = HEAD_DIM // 2
AB_IN_W = 3 * A_WIDTH + 2 * B_Q_W + 2 * B_KV_W
DN_K_HEADS = D_MODEL // HEAD_DIM
DN_V_HEADS = 2 * DN_K_HEADS
DN_K_W = DN_K_HEADS * HEAD_DIM
DN_V_W = DN_V_HEADS * HEAD_DIM
DN_QKV_W = 2 * DN_K_W + DN_V_W
DN_IN_W = DN_QKV_W + DN_V_W + 4 * DN_V_HEADS
DN_CHUNK = 64
CONV_K = 5

V7X_VMEM_LIMIT_BYTES = 56 * 1024 * 1024
LANES = 128
SUBLANES = 8

ROW_TILE = 512
CONV_ROW_TILE = 256


def _cparams(n_axes):
    return pltpu.CompilerParams(dimension_semantics=("arbitrary",) * n_axes,
                                vmem_limit_bytes=V7X_VMEM_LIMIT_BYTES)


def _silu(x):
    return x * jax.nn.sigmoid(x)


def _split_bf16(a):
    hi = a.astype(BF16)
    lo = (a - hi.astype(F32)).astype(BF16)
    return hi, lo


def _dot(a, b):
    return jnp.dot(a, b, preferred_element_type=F32)


def _dot_nt(a, b):
    return lax.dot_general(a, b, (((1,), (1,)), ((), ())), preferred_element_type=F32)


def _dot3(a, b):
    ah, al = _split_bf16(a)
    bh, bl = _split_bf16(b)
    return _dot(ah, bh) + _dot(ah, bl) + _dot(al, bh)


def _ada_kernel(c_ref, w_ref, b_ref, o_ref):
    s = _silu(c_ref[...])
    o_ref[0] = _dot3(s, w_ref[0]) + b_ref[0]


def _ada_mod(cond, ada_w, ada_b):
    depth = ada_w.shape[0]
    tn = 512
    return pl.pallas_call(
        _ada_kernel,
        grid=(depth, 3 * D_MODEL // tn),
        in_specs=[pl.BlockSpec((SUBLANES, D_MODEL), lambda l, j: (0, 0)),
                  pl.BlockSpec((1, D_MODEL, tn), lambda l, j: (l, 0, j)),
                  pl.BlockSpec((1, 1, tn), lambda l, j: (l, 0, j))],
        out_specs=pl.BlockSpec((1, SUBLANES, tn), lambda l, j: (l, 0, j)),
        out_shape=jax.ShapeDtypeStruct((depth, SUBLANES, 3 * D_MODEL), F32),
        compiler_params=_cparams(2),
        name="ada_mod",
    )(cond, ada_w, ada_b.reshape(depth, 1, 3 * D_MODEL))


INPROJ_COL_TILES = (512, 1024, 1408, 1536)


def _inproj_kernel(x_ref, nw_ref, mod_ref, w_ref, *rest):
    h_ref = rest[-1]
    has_side = len(rest) == 4

    @pl.when(pl.program_id(1) == 0)
    def _():
        x = x_ref[...]
        y = x * lax.rsqrt(jnp.mean(x * x, axis=-1, keepdims=True) + EPS) * nw_ref[...]
        shift = mod_ref[:, 0:D_MODEL]
        scale = mod_ref[:, D_MODEL:2 * D_MODEL]
        h_ref[...] = (y * (1.0 + scale) + shift).astype(BF16)
        if has_side:
            rest[2][...] = _dot(h_ref[...], rest[0][...])

    o_ref = rest[1] if has_side else rest[0]
    o_ref[...] = _dot(h_ref[...], w_ref[...]).astype(o_ref.dtype)


def _mod_row(i, lat_tiles, batch):
    return jnp.minimum(i // lat_tiles, batch)


def _inproj(x, norm_w, mod, w_bf16, batch, seq, w_side_bf16=None, out_dtype=BF16):
    m = x.shape[0]
    n = w_bf16.shape[1]
    tm = ROW_TILE
    tn = max(t for t in INPROJ_COL_TILES if n % t == 0)
    lat_tiles = seq // tm
    in_specs = [pl.BlockSpec((tm, D_MODEL), lambda i, j: (i, 0)),
                pl.BlockSpec((1, D_MODEL), lambda i, j: (0, 0)),
                pl.BlockSpec((None, 1, 3 * D_MODEL), lambda i, j: (_mod_row(i, lat_tiles, batch), 0, 0)),
                pl.BlockSpec((D_MODEL, tn), lambda i, j: (0, j))]
    out_specs = [pl.BlockSpec((tm, tn), lambda i, j: (i, j))]
    out_shape = [jax.ShapeDtypeStruct((m, n), out_dtype)]
    args = [x, norm_w.reshape(1, D_MODEL), mod, w_bf16]
    if w_side_bf16 is not None:
        ns = w_side_bf16.shape[1]
        in_specs.append(pl.BlockSpec((D_MODEL, ns), lambda i, j: (0, 0)))
        out_specs.append(pl.BlockSpec((tm, ns), lambda i, j: (i, 0)))
        out_shape.append(jax.ShapeDtypeStruct((m, ns), F32))
        args.append(w_side_bf16)
    res = pl.pallas_call(
        _inproj_kernel,
        grid=(m // tm, n // tn),
        in_specs=in_specs,
        out_specs=out_specs,
        out_shape=out_shape,
        scratch_shapes=[pltpu.VMEM((tm, D_MODEL), BF16)],
        compiler_params=_cparams(2),
        name="inproj",
    )(*args)
    return res if w_side_bf16 is not None else res[0]


Q_PRESCALE = (HEAD_DIM ** -0.5) * 1.4426950408889634


def _qkprep_kernel(q_ref, k_ref, v_ref, cos_ref, sin_ref, qw_ref, kw_ref, qo_ref, ko_ref, vo_ref):
    cos = cos_ref[...]
    sin = sin_ref[...]
    lane = lax.broadcasted_iota(jnp.int32, cos.shape, 1)
    first = (lane % (AXIS_DIM)) < (AXIS_DIM // 2)

    def prep(x, w):
        y = x * lax.rsqrt(jnp.mean(x * x, axis=-1, keepdims=True) + EPS) * w
        rot = jnp.where(first, pltpu.roll(y, HEAD_DIM - AXIS_DIM // 2, 1), pltpu.roll(y, AXIS_DIM // 2, 1))
        return y * cos + rot * sin

    for h in range(B_HEADS):
        sl = slice(h * HEAD_DIM, (h + 1) * HEAD_DIM)
        qo_ref[:, sl] = (prep(q_ref[:, sl].astype(F32), qw_ref[...]) * Q_PRESCALE).astype(qo_ref.dtype)
    for h in range(B_KV_HEADS):
        sl = slice(h * HEAD_DIM, (h + 1) * HEAD_DIM)
        ko_ref[:, sl] = prep(k_ref[:, sl].astype(F32), kw_ref[...]).astype(ko_ref.dtype)
    vo_ref[...] = v_ref[...].astype(vo_ref.dtype)


def _qkprep(proj, cos_tab, sin_tab, qn_w, kn_w, batch, seq):
    m = proj.shape[0]
    tm = CONV_ROW_TILE
    lat_tiles = seq // tm
    n_lat = batch * lat_tiles
    q_blk = (4 * A_WIDTH) // B_Q_W
    k_blk = (4 * A_WIDTH + B_Q_W) // B_KV_W

    def tab_idx(i):
        return (jnp.where(i < n_lat, i % lat_tiles, lat_tiles), 0)

    def kv_idx(i):
        lat_blk = (i // lat_tiles) * (lat_tiles + 1) + 1 + i % lat_tiles
        return (jnp.where(i < n_lat, lat_blk, (i - n_lat) * (lat_tiles + 1)), 0)

    return pl.pallas_call(
        _qkprep_kernel,
        grid=(m // tm,),
        in_specs=[pl.BlockSpec((tm, B_Q_W), lambda i: (i, q_blk)),
                  pl.BlockSpec((tm, B_KV_W), lambda i: (i, k_blk)),
                  pl.BlockSpec((tm, B_KV_W), lambda i: (i, k_blk + 1)),
                  pl.BlockSpec((tm, HEAD_DIM), tab_idx),
                  pl.BlockSpec((tm, HEAD_DIM), tab_idx),
                  pl.BlockSpec((1, HEAD_DIM), lambda i: (0, 0)),
                  pl.BlockSpec((1, HEAD_DIM), lambda i: (0, 0))],
        out_specs=[pl.BlockSpec((tm, B_Q_W), lambda i: (i, 0)),
                   pl.BlockSpec((tm, B_KV_W), kv_idx),
                   pl.BlockSpec((tm, B_KV_W), kv_idx)],
        out_shape=[jax.ShapeDtypeStruct((m, B_Q_W), BF16),
                   jax.ShapeDtypeStruct((m, B_KV_W), BF16),
                   jax.ShapeDtypeStruct((m, B_KV_W), BF16)],
        compiler_params=_cparams(1),
        name="qk_prep",
    )(proj, proj, proj, cos_tab, sin_tab, qn_w.reshape(1, HEAD_DIM), kn_w.reshape(1, HEAD_DIM))


def _rope_tables(seq, tm):
    rows = seq // GRID_W
    row = jnp.repeat(jnp.arange(rows), GRID_W).astype(F32)
    col = jnp.tile(jnp.arange(GRID_W), rows).astype(F32)
    freqs = ROPE_THETA ** (-jnp.arange(0, AXIS_DIM, 2, dtype=F32) / AXIS_DIM)
    ang_r = row[:, None] * freqs[None, :]
    ang_c = col[:, None] * freqs[None, :]
    ang = jnp.concatenate([ang_r, ang_r, ang_c, ang_c], axis=-1)
    sign = jnp.where((jnp.arange(HEAD_DIM) % AXIS_DIM) < AXIS_DIM // 2, -1.0, 1.0).astype(F32)
    cos = jnp.concatenate([jnp.cos(ang), jnp.ones((tm, HEAD_DIM), F32)], axis=0)
    sin = jnp.concatenate([jnp.sin(ang) * sign[None, :], jnp.zeros((tm, HEAD_DIM), F32)], axis=0)
    return cos, sin


ATTN_Q_TILE = 256
ATTN_KV_CHUNK = 768


def _attn_kernel(q_ref, k_ref, v_ref, o_ref, m_scr, l_scr, acc_scr, *, n_chunks, tk, tq):
    q = jnp.concatenate([q_ref[:, g * HEAD_DIM:(g + 1) * HEAD_DIM] for g in range(B_GROUP)], axis=0)
    m_scr[...] = jnp.full(m_scr.shape, -jnp.inf, F32)
    l_scr[...] = jnp.zeros(l_scr.shape, F32)
    acc_scr[...] = jnp.zeros(acc_scr.shape, F32)
    for c in range(n_chunks):
        k = k_ref[c * tk:(c + 1) * tk, :]
        v = v_ref[c * tk:(c + 1) * tk, :]
        s = _dot_nt(q, k)
        m_prev = m_scr[...]
        m_next = jnp.maximum(m_prev, jnp.max(s, axis=1, keepdims=True))
        p = jnp.exp2(s - jnp.concatenate([m_next] * (tk // LANES), axis=1))
        alpha = jnp.exp2(m_prev - m_next)
        l_scr[...] = alpha * l_scr[...] + jnp.sum(p, axis=1, keepdims=True)
        acc_scr[...] = acc_scr[...] * alpha + _dot(p.astype(BF16), v)
        m_scr[...] = m_next
    out = acc_scr[...] / l_scr[...]
    for g in range(B_GROUP):
        o_ref[:, g * HEAD_DIM:(g + 1) * HEAD_DIM] = out[g * tq:(g + 1) * tq].astype(o_ref.dtype)


def _attention(qr, kr, vr, batch, seq, ctx_len, latent):
    tq = ATTN_Q_TILE
    gw = B_GROUP * HEAD_DIM
    q_len = seq if latent else ctx_len
    q_tiles = q_len // tq
    q_row0 = 0 if latent else batch * seq // tq
    kv_len = ctx_len + seq
    if latent:
        kv_rows, tk = kv_len, ATTN_KV_CHUNK
        kv_idx = lambda b, h, i: (b, h)
    else:
        kv_rows, tk = ctx_len, ctx_len
        kv_idx = lambda b, h, i: (b * (kv_len // ctx_len), h)
    assert kv_rows % tk == 0
    rows = B_GROUP * tq
    return pl.pallas_call(
        functools.partial(_attn_kernel, n_chunks=kv_rows // tk, tk=tk, tq=tq),
        grid=(batch, B_KV_HEADS, q_tiles),
        in_specs=[pl.BlockSpec((tq, gw), lambda b, h, i: (q_row0 + b * q_tiles + i, h)),
                  pl.BlockSpec((kv_rows, HEAD_DIM), kv_idx),
                  pl.BlockSpec((kv_rows, HEAD_DIM), kv_idx)],
        out_specs=pl.BlockSpec((tq, gw), lambda b, h, i: (b * q_tiles + i, h)),
        out_shape=jax.ShapeDtypeStruct((batch * q_len, B_Q_W), BF16),
        scratch_shapes=[pltpu.VMEM((rows, LANES), F32), pltpu.VMEM((rows, LANES), F32),
                        pltpu.VMEM((rows, HEAD_DIM), F32)],
        compiler_params=_cparams(3),
        name="attn_lat" if latent else "attn_ctx",
    )(qr, kr, vr)


def _ab_out_kernel(u_ref, v_ref, ga_ref, gb_ref, aol_ref, aoc_ref, ws_ref, bs_ref, w_ref, x_ref, gate_ref, o_ref,
                   y_ref, acc_ref, *, n_lat_tiles):
    k = pl.program_id(1)

    @pl.when(k == 0)
    def _():
        tm = u_ref.shape[0]
        for c in range(tm // SGU_CHUNK):
            rows = slice(c * SGU_CHUNK, (c + 1) * SGU_CHUNK)
            for g in range(A_GROUPS):
                cols = slice(g * LANES, (g + 1) * LANES)
                vg = v_ref[rows, cols].astype(F32)
                d = vg - jnp.mean(vg, axis=-1, keepdims=True)
                var = jnp.mean(d * d, axis=-1, keepdims=True)
                vn = (d * lax.rsqrt(var + 1e-5)).astype(BF16)
                mixed = _dot(ws_ref[g], vn) + bs_ref[:, g:g + 1]
                y_ref[rows, cols] = (u_ref[rows, cols].astype(F32) * mixed
                                     * _silu(ga_ref[rows, cols].astype(F32))).astype(BF16)
        acc_ref[...] = _dot(y_ref[...], w_ref[0:A_WIDTH, :])

    @pl.when(k == 1)
    def _():
        ao = jnp.where(pl.program_id(0) < n_lat_tiles, aol_ref[...], aoc_ref[...]).astype(F32)
        y = (ao * _silu(gb_ref[...].astype(F32))).astype(BF16)
        o_ref[...] = x_ref[...] + gate_ref[...] * (acc_ref[...] + _dot(y, w_ref[A_WIDTH:, :]))


def _ab_out(proj, ao_lat, ao_ctx, sgu_w_bf16, sgu_b_t, w_out_bf16, x, mod, batch, seq):
    m = x.shape[0]
    tm = ROW_TILE
    lat_tiles = seq // tm
    n_lat_tiles = batch * lat_tiles
    assert ao_ctx.shape[0] == tm
    return pl.pallas_call(
        functools.partial(_ab_out_kernel, n_lat_tiles=n_lat_tiles),
        grid=(m // tm, 2),
        in_specs=[pl.BlockSpec((tm, A_WIDTH), lambda i, k: (i, 0)),
                  pl.BlockSpec((tm, A_WIDTH), lambda i, k: (i, 1)),
                  pl.BlockSpec((tm, A_WIDTH), lambda i, k: (i, 2)),
                  pl.BlockSpec((tm, B_Q_W), lambda i, k: (i, 3)),
                  pl.BlockSpec((tm, B_Q_W), lambda i, k: (jnp.minimum(i, n_lat_tiles - 1), 0)),
                  pl.BlockSpec((tm, B_Q_W), lambda i, k: (0, 0)),
                  pl.BlockSpec((A_GROUPS, SGU_CHUNK, SGU_CHUNK), lambda i, k: (0, 0, 0)),
                  pl.BlockSpec((SGU_CHUNK, A_GROUPS), lambda i, k: (0, 0)),
                  pl.BlockSpec((A_WIDTH + B_Q_W, D_MODEL), lambda i, k: (0, 0), pipeline_mode=pl.Buffered(1)),
                  pl.BlockSpec((tm, D_MODEL), lambda i, k: (i, 0)),
                  pl.BlockSpec((None, 1, D_MODEL), lambda i, k: (_mod_row(i, lat_tiles, batch), 0, 2))],
        out_specs=pl.BlockSpec((tm, D_MODEL), lambda i, k: (i, 0)),
        out_shape=jax.ShapeDtypeStruct((m, D_MODEL), F32),
        scratch_shapes=[pltpu.VMEM((tm, A_WIDTH), BF16), pltpu.VMEM((tm, D_MODEL), F32)],
        compiler_params=_cparams(2),
        name="ab_out",
    )(proj, proj, proj, proj, ao_lat, ao_ctx, sgu_w_bf16, sgu_b_t, w_out_bf16, x, mod)


def _dnconv_kernel(xp_ref, xc_ref, xn_ref, w_ref, o_ref, *, lat_tiles, n_lat, q_tiles, k_tiles):
    i = pl.program_id(0)
    j = pl.program_id(1)
    tm = xc_ref.shape[0]
    pos = i % lat_tiles
    is_lat = i < n_lat
    first = jnp.logical_or(jnp.logical_not(is_lat), pos == 0)
    last = jnp.logical_or(jnp.logical_not(is_lat), pos == lat_tiles - 1)
    half = CONV_K // 2
    sub = lax.broadcasted_iota(jnp.int32, (SUBLANES, HEAD_DIM), 0)

    def conv_silu(sl):
        x = xc_ref[:, sl].astype(F32)
        prev = jnp.where(first, 0.0, xp_ref[:, sl].astype(F32)[SUBLANES:])
        nxt = jnp.where(last, 0.0, xn_ref[:, sl].astype(F32)[:SUBLANES])
        acc = w_ref[half:half + 1, sl] * x
        for t in range(CONV_K):
            s_rows = half - t
            if s_rows == 0:
                continue
            r = pltpu.roll(x, s_rows % tm, 0)
            if s_rows > 0:
                head = jnp.where(sub < s_rows, pltpu.roll(prev, s_rows, 0), r[0:SUBLANES])
                shifted = jnp.concatenate([head, r[SUBLANES:]], axis=0)
            else:
                tail = jnp.where(sub >= SUBLANES + s_rows, pltpu.roll(nxt, SUBLANES + s_rows, 0), r[tm - SUBLANES:])
                shifted = jnp.concatenate([r[:tm - SUBLANES], tail], axis=0)
            acc = acc + w_ref[t:t + 1, sl] * shifted
        return _silu(acc)

    heads = [slice(h * HEAD_DIM, (h + 1) * HEAD_DIM) for h in range(xc_ref.shape[1] // HEAD_DIM)]

    @pl.when(j < q_tiles + k_tiles)
    def _():
        qk_scale = jnp.where(j < q_tiles, HEAD_DIM ** -0.5, 1.0)
        for sl in heads:
            y = conv_silu(sl)
            o_ref[:, sl] = (y * (lax.rsqrt(jnp.sum(y * y, axis=-1, keepdims=True) + EPS) * qk_scale)).astype(o_ref.dtype)

    @pl.when(j >= q_tiles + k_tiles)
    def _():
        for sl in heads:
            o_ref[:, sl] = conv_silu(sl).astype(o_ref.dtype)


def _dn_conv(proj, conv_w, batch, seq, out_dtype=F32):
    m = proj.shape[0]
    tm, tc = CONV_ROW_TILE, 1024
    lat_tiles = seq // tm
    n_lat = batch * lat_tiles
    halo = 2 * SUBLANES
    sub_per_tile = tm // halo
    n_sub = m // halo
    kern = functools.partial(_dnconv_kernel, lat_tiles=lat_tiles, n_lat=n_lat,
                             q_tiles=DN_K_W // tc, k_tiles=DN_K_W // tc)
    return pl.pallas_call(
        kern,
        grid=(m // tm, DN_QKV_W // tc),
        in_specs=[pl.BlockSpec((halo, tc), lambda i, j: (jnp.maximum(i * sub_per_tile - 1, 0), j)),
                  pl.BlockSpec((tm, tc), lambda i, j: (i, j)),
                  pl.BlockSpec((halo, tc), lambda i, j: (jnp.minimum((i + 1) * sub_per_tile, n_sub - 1), j)),
                  pl.BlockSpec((SUBLANES, tc), lambda i, j: (0, j))],
        out_specs=pl.BlockSpec((tm, tc), lambda i, j: (i, j)),
        out_shape=jax.ShapeDtypeStruct((m, DN_QKV_W), out_dtype),
        compiler_params=_cparams(2),
        name="dn_conv",
    )(proj, proj, proj, conv_w)


def _dngate_kernel(ba_ref, alog_ref, dtb_ref, o_ref):
    ba = ba_ref[...]
    tm = ba.shape[0]
    lane = lax.broadcasted_iota(jnp.int32, (DN_CHUNK, LANES), 1)
    is_beta = (lane // DN_V_HEADS) % 2 == 0
    is_fwd = lane < 2 * DN_V_HEADS
    z = ba + dtb_ref[...]
    softplus = jnp.maximum(z, 0.0) + jnp.log1p(jnp.exp(-jnp.abs(z)))
    g = -jnp.exp(alog_ref[...]) * softplus
    r = lax.broadcasted_iota(jnp.int32, (DN_CHUNK, DN_CHUNK), 0)
    c = lax.broadcasted_iota(jnp.int32, (DN_CHUNK, DN_CHUNK), 1)
    tri_lo = (r >= c).astype(BF16)
    tri_up = (r <= c).astype(BF16)
    beta = jax.nn.sigmoid(ba)
    for ch in range(tm // DN_CHUNK):
        rows = slice(ch * DN_CHUNK, (ch + 1) * DN_CHUNK)
        gch = g[rows]
        g1 = gch.astype(BF16)
        r1 = gch - g1.astype(F32)
        g2 = r1.astype(BF16)
        g3 = (r1 - g2.astype(F32)).astype(BF16)
        pre = _dot(tri_lo, g1) + _dot(tri_lo, g2) + _dot(tri_lo, g3)
        suf = _dot(tri_up, g1) + _dot(tri_up, g2) + _dot(tri_up, g3)
        gc = jnp.where(is_fwd, pre, suf)
        o_ref[rows, :] = jnp.where(is_beta, beta[rows], gc)


def _dn_gate(ba, alog_vec, dtb_vec):
    m = ba.shape[0]
    tm = ROW_TILE
    return pl.pallas_call(
        _dngate_kernel,
        grid=(m // tm,),
        in_specs=[pl.BlockSpec((tm, LANES), lambda i: (i, 0)),
                  pl.BlockSpec((1, LANES), lambda i: (0, 0)),
                  pl.BlockSpec((1, LANES), lambda i: (0, 0))],
        out_specs=pl.BlockSpec((tm, LANES), lambda i: (i, 0)),
        out_shape=jax.ShapeDtypeStruct((m, LANES), F32),
        compiler_params=_cparams(1),
        name="dn_gate",
    )(ba, alog_vec, dtb_vec)


DN_KH_PER_STEP = 16


def _dncore_kernel(qf_ref, kf_ref, vf_ref, gf_ref, qb_ref, kb_ref, vb_ref, gb_ref, of_ref, ob_ref, s_ref):
    C = DN_CHUNK
    nvh = 2 * DN_KH_PER_STEP

    @pl.when(pl.program_id(2) == 0)
    def _():
        s_ref[...] = jnp.zeros(s_ref.shape, F32)

    lane = lax.broadcasted_iota(jnp.int32, (C, 2 * C), 1)
    left = lane < C
    row = lax.broadcasted_iota(jnp.int32, (C, 2 * C), 0)
    colp = lane % C
    left_sq = lax.broadcasted_iota(jnp.int32, (2 * C, 2 * C), 1) < C
    eye2 = (row == colp).astype(F32)

    def blockdiag(p):
        z = jnp.zeros_like(p)
        return jnp.concatenate([jnp.where(left, p, z), jnp.where(left, z, p)], axis=0)

    def packed_mm(a, b):
        return _dot(a.astype(BF16), blockdiag(b.astype(BF16)))

    dirs = ((qf_ref, kf_ref, vf_ref, gf_ref, of_ref), (qb_ref, kb_ref, vb_ref, gb_ref, ob_ref))
    chains = [(d, kh) for d in range(2) for kh in range(DN_KH_PER_STEP)]
    G = [dirs[d][3][...] for d in range(2)]
    GT = [jnp.concatenate([g, g], axis=0).T for g in G]

    st = []
    for d, kh in chains:
        q_ref, k_ref, v_ref, _, _ = dirs[d]
        base_beta = d * 2 * nvh
        base_gc = base_beta + nvh
        lv0 = 2 * kh
        q = q_ref[:, kh * HEAD_DIM:(kh + 1) * HEAD_DIM]
        k = k_ref[:, kh * HEAD_DIM:(kh + 1) * HEAD_DIM]
        k2 = jnp.concatenate([k, k], axis=0)
        kT2 = k2.T
        gram = _dot(jnp.concatenate([q, k], axis=0).astype(BF16), kT2.astype(BF16))
        b0, b1 = G[d][:, base_beta + lv0:base_beta + lv0 + 1], G[d][:, base_beta + lv0 + 1:base_beta + lv0 + 2]
        c0, c1 = G[d][:, base_gc + lv0:base_gc + lv0 + 1], G[d][:, base_gc + lv0 + 1:base_gc + lv0 + 2]
        r0, r1 = GT[d][base_gc + lv0:base_gc + lv0 + 1, :], GT[d][base_gc + lv0 + 1:base_gc + lv0 + 2, :]
        st.append(dict(d=d, lv0=lv0, q=q, k2=k2, kT2=kT2, gram=gram, b0=b0, b1=b1, c0=c0, c1=c1, r0=r0, r1=r1))

    for s in st:
        d = s["d"]
        incl = (row >= colp) if d == 0 else (row <= colp)
        strict = (row > colp) if d == 0 else (row < colp)
        gcol_p = jnp.where(left, s["c0"], s["c1"])
        grow_p = jnp.where(left[0:1], s["r0"], s["r1"])
        beta_p = jnp.where(left, s["b0"], s["b1"])
        dec = jnp.exp(jnp.where(incl, gcol_p - grow_p, -1e30))
        s["dec"] = dec
        s["attn"] = s["gram"][0:C] * dec
        s["L"] = jnp.where(strict, s["gram"][C:2 * C] * dec, 0.0) * beta_p

    for s in st:
        n1 = jnp.where(jnp.logical_and(row // 2 == colp // 2, row != colp), s["L"], 0.0)
        s["X"] = eye2 - n1
    blk = 2
    while blk < C:
        mask = jnp.logical_and(row // (2 * blk) == colp // (2 * blk), row // blk != colp // blk)
        for s in st:
            s["Y"] = packed_mm(s["X"], jnp.where(mask, s["L"], 0.0))
        for s in st:
            s["X"] = s["X"] - packed_mm(s["Y"], s["X"])
        blk *= 2

    for s in st:
        d, lv0 = s["d"], s["lv0"]
        v_ref = dirs[d][2]
        beta_r = jnp.concatenate([s["b0"], s["b1"]], axis=0)
        egc_r = jnp.exp(jnp.concatenate([s["c0"], s["c1"]], axis=0))
        v2 = jnp.concatenate([v_ref[:, lv0 * HEAD_DIM:(lv0 + 1) * HEAD_DIM],
                              v_ref[:, (lv0 + 1) * HEAD_DIM:(lv0 + 2) * HEAD_DIM]], axis=0)
        rhs = jnp.concatenate([v2 * beta_r, s["k2"] * (beta_r * egc_r)], axis=1)
        s["sol"] = _dot(blockdiag(s["X"]).astype(BF16), rhs.astype(BF16))
        s["egc"] = egc_r
        s["qg2"] = jnp.concatenate([s["q"], s["q"]], axis=0) * egc_r

    for s in st:
        d, lv0 = s["d"], s["lv0"]
        w2 = s["sol"][:, HEAD_DIM:]
        s["ws"] = []
        for r in range(2):
            lhs = jnp.concatenate([w2[r * C:(r + 1) * C], s["qg2"][r * C:(r + 1) * C]], axis=0).astype(BF16)
            s["ws"].append(_dot(lhs, s_ref[d, lv0 + r].astype(BF16)))

    for s in st:
        u2 = s["sol"][:, 0:HEAD_DIM]
        vn2 = jnp.concatenate([u2[r * C:(r + 1) * C] - s["ws"][r][0:C] for r in range(2)], axis=0).astype(BF16)
        s["vn2"] = vn2
        s["o2"] = jnp.concatenate([s["ws"][r][C:2 * C] for r in range(2)], axis=0) + _dot(
            blockdiag(s["attn"]).astype(BF16), vn2)

    for s in st:
        d, lv0 = s["d"], s["lv0"]
        o_ref = dirs[d][4]
        last = C - 1 if d == 0 else 0
        kdT_p = s["kT2"] * s["dec"][last:last + 1, :]
        zkd = jnp.zeros_like(kdT_p)
        for r in range(2):
            kd_r = jnp.where(left_sq if r == 0 else jnp.logical_not(left_sq), kdT_p, zkd).astype(BF16)
            glr = jnp.broadcast_to(s["egc"][r * C + last:r * C + last + 1, :], (HEAD_DIM, HEAD_DIM))
            s_ref[d, lv0 + r] = s_ref[d, lv0 + r] * glr + _dot(kd_r, s["vn2"])
            o_ref[:, (lv0 + r) * HEAD_DIM:(lv0 + r + 1) * HEAD_DIM] = s["o2"][r * C:(r + 1) * C].astype(o_ref.dtype)


def _dn_core(qkv, gates, batch, seq, ctx_len):
    m = qkv.shape[0]
    C = DN_CHUNK
    n_lat = seq // C
    n_ctx = ctx_len // C
    n_steps = n_ctx + n_lat
    ctx0 = batch * n_lat
    khs = DN_KH_PER_STEP
    qw = khs * HEAD_DIM
    vw = 2 * khs * HEAD_DIM
    k_blk0 = DN_K_W // qw
    v_blk0 = 2 * DN_K_W // vw

    def rf(b, t):
        return jnp.where(t < n_ctx, ctx0 + b * n_ctx + t, b * n_lat + (t - n_ctx))

    def rb(b, t):
        return jnp.where(t < n_ctx, ctx0 + b * n_ctx + (n_ctx - 1 - t), b * n_lat + (n_lat - 1 - (t - n_ctx)))

    def specs(rfun):
        return [pl.BlockSpec((C, qw), lambda b, h, t: (rfun(b, t), h)),
                pl.BlockSpec((C, qw), lambda b, h, t: (rfun(b, t), k_blk0 + h)),
                pl.BlockSpec((C, vw), lambda b, h, t: (rfun(b, t), v_blk0 + h)),
                pl.BlockSpec((C, LANES), lambda b, h, t: (rfun(b, t), h))]

    return pl.pallas_call(
        _dncore_kernel,
        grid=(batch, DN_K_HEADS // khs, n_steps),
        in_specs=specs(rf) + specs(rb),
        out_specs=[pl.BlockSpec((C, vw), lambda b, h, t: (rf(b, t), h)),
                   pl.BlockSpec((C, vw), lambda b, h, t: (rb(b, t), h))],
        out_shape=[jax.ShapeDtypeStruct((m, DN_V_W), BF16), jax.ShapeDtypeStruct((m, DN_V_W), BF16)],
        scratch_shapes=[pltpu.VMEM((2, 2 * khs, HEAD_DIM, HEAD_DIM), F32)],
        compiler_params=_cparams(3),
        name="dn_core",
    )(qkv, qkv, qkv, gates, qkv, qkv, qkv, gates)


DN_OUT_K_TILE = 1024


def _dn_out_kernel(of_ref, ob_ref, z_ref, nw_ref, w_ref, x_ref, gate_ref, o_ref, acc_ref):
    k = pl.program_id(1)
    ys = []
    for h in range(of_ref.shape[1] // HEAD_DIM):
        sl = slice(h * HEAD_DIM, (h + 1) * HEAD_DIM)
        o = of_ref[:, sl].astype(F32) + ob_ref[:, sl].astype(F32)
        n = o * lax.rsqrt(jnp.mean(o * o, axis=-1, keepdims=True) + EPS) * nw_ref[...]
        ys.append((n * _silu(z_ref[:, sl].astype(F32))).astype(BF16))
    tk = of_ref.shape[1]
    part = _dot(jnp.concatenate(ys, axis=1), w_ref[pl.ds(pl.multiple_of(k * tk, tk), tk), :])

    @pl.when(k == 0)
    def _():
        acc_ref[...] = part

    @pl.when(jnp.logical_and(k > 0, k < pl.num_programs(1) - 1))
    def _():
        acc_ref[...] += part

    @pl.when(k == pl.num_programs(1) - 1)
    def _():
        o_ref[...] = x_ref[...] + gate_ref[...] * (acc_ref[...] + part)


def _dn_out(o_f, o_b, proj, norm_w, w_out_bf16, x, mod, batch, seq):
    m = x.shape[0]
    tm, tk = ROW_TILE, DN_OUT_K_TILE
    lat_tiles = seq // tm
    z_blk0 = DN_QKV_W // tk
    return pl.pallas_call(
        _dn_out_kernel,
        grid=(m // tm, DN_V_W // tk),
        in_specs=[pl.BlockSpec((tm, tk), lambda i, k: (i, k)),
                  pl.BlockSpec((tm, tk), lambda i, k: (i, k)),
                  pl.BlockSpec((tm, tk), lambda i, k: (i, z_blk0 + k)),
                  pl.BlockSpec((1, HEAD_DIM), lambda i, k: (0, 0)),
                  pl.BlockSpec((DN_V_W, D_MODEL), lambda i, k: (0, 0), pipeline_mode=pl.Buffered(1)),
                  pl.BlockSpec((tm, D_MODEL), lambda i, k: (i, 0)),
                  pl.BlockSpec((None, 1, D_MODEL), lambda i, k: (_mod_row(i, lat_tiles, batch), 0, 2))],
        out_specs=pl.BlockSpec((tm, D_MODEL), lambda i, k: (i, 0)),
        out_shape=jax.ShapeDtypeStruct((m, D_MODEL), F32),
        scratch_shapes=[pltpu.VMEM((tm, D_MODEL), F32)],
        compiler_params=_cparams(2),
        name="dn_out",
    )(o_f, o_b, proj, norm_w.reshape(1, HEAD_DIM), w_out_bf16, x, mod)


def _final_norm_kernel(x_ref, w_ref, o_ref):
    x = x_ref[...]
    o_ref[...] = x * lax.rsqrt(jnp.mean(x * x, axis=-1, keepdims=True) + EPS) * w_ref[...]


def _final_norm(x, w, rows):
    tm = ROW_TILE
    return pl.pallas_call(
        _final_norm_kernel,
        grid=(rows // tm,),
        in_specs=[pl.BlockSpec((tm, D_MODEL), lambda i: (i, 0)),
                  pl.BlockSpec((1, D_MODEL), lambda i: (0, 0))],
        out_specs=pl.BlockSpec((tm, D_MODEL), lambda i: (i, 0)),
        out_shape=jax.ShapeDtypeStruct((rows, D_MODEL), F32),
        compiler_params=_cparams(1),
        name="final_norm",
    )(x, w.reshape(1, D_MODEL))


def kernel(x, c, ctx, c_ctx, norm_w, ada_w, ada_b, ab_w_in, ab_w_out, sgu_w, sgu_b, q_norm_w, k_norm_w,
           dn_w_in, dn_conv_w, dn_a_log, dn_dt_bias, dn_norm_w, dn_w_out, final_norm_w):
    batch, seq, _ = x.shape
    ctx_len = ctx.shape[1]
    depth = norm_w.shape[0]
    assert ctx_len == CONV_ROW_TILE and seq % ROW_TILE == 0 and (batch * ctx_len) % ROW_TILE == 0
    n_lat_rows = batch * seq

    xs = jnp.concatenate([x.reshape(n_lat_rows, D_MODEL), ctx.reshape(batch * ctx_len, D_MODEL)], axis=0)
    cond = jnp.zeros((SUBLANES, D_MODEL), F32).at[0:batch].set(c).at[batch].set(c_ctx)
    mods = _ada_mod(cond, ada_w, ada_b)
    cos_tab, sin_tab = _rope_tables(seq, CONV_ROW_TILE)

    for i in range(depth):
        j = i // 2
        mod = mods[i].reshape(SUBLANES, 1, 3 * D_MODEL)
        if i % 2 == 0:
            w = ab_w_in[j]
            a3 = 3 * A_WIDTH
            w_in = jnp.concatenate([w[:, :a3], w[:, a3 + B_Q_W + 2 * B_KV_W:], w[:, a3:a3 + B_Q_W + 2 * B_KV_W]],
                                   axis=1).astype(BF16)
            proj = _inproj(xs, norm_w[i], mod, w_in, batch, seq)
            qr, kr, vr = _qkprep(proj, cos_tab, sin_tab, q_norm_w[j], k_norm_w[j], batch, seq)
            ao_lat = _attention(qr, kr, vr, batch, seq, ctx_len, True)
            ao_ctx = _attention(qr, kr, vr, batch, seq, ctx_len, False)
            xs = _ab_out(proj, ao_lat, ao_ctx, sgu_w[j].astype(BF16), sgu_b[j].T, ab_w_out[j].astype(BF16), xs, mod,
                         batch, seq)
        else:
            w = dn_w_in[j].astype(BF16)
            n_main = DN_QKV_W + DN_V_W
            proj, ba = _inproj(xs, norm_w[i], mod, w[:, :n_main], batch, seq, w_side_bf16=w[:, n_main:])
            conv_w = jnp.zeros((SUBLANES, DN_QKV_W), F32).at[0:CONV_K].set(dn_conv_w[j])
            qkv = _dn_conv(proj, conv_w, batch, seq)
            zeros = jnp.zeros((2, DN_V_HEADS), F32)
            alog_vec = jnp.concatenate([zeros, dn_a_log[j]], axis=1).reshape(1, LANES)
            dtb_vec = jnp.concatenate([zeros, dn_dt_bias[j]], axis=1).reshape(1, LANES)
            gates = _dn_gate(ba, alog_vec, dtb_vec)
            n_hg = DN_K_HEADS // DN_KH_PER_STEP
            nvh = 2 * DN_KH_PER_STEP
            g4 = gates.reshape(-1, 4, n_hg, nvh).transpose(0, 2, 1, 3).reshape(-1, n_hg, 4 * nvh)
            g4 = jnp.pad(g4, ((0, 0), (0, 0), (0, LANES - 4 * nvh))).reshape(-1, n_hg * LANES)
            o_f, o_b = _dn_core(qkv, g4, batch, seq, ctx_len)
            xs = _dn_out(o_f, o_b, proj, dn_norm_w[j], dn_w_out[j].astype(BF16), xs, mod, batch, seq)

    out = _final_norm(xs, final_norm_w, n_lat_rows)
    return out.reshape(batch, seq, D_MODEL)
```

```python
import functools

import jax
import jax.numpy as jnp
from jax import lax
from jax.experimental import pallas as pl
from jax.experimental.pallas import tpu as pltpu

F32 = jnp.float32
BF16 = jnp.bfloat16

D_MODEL = 2048
GRID_W = 64
EPS = 1e-6
HEAD_DIM = 128
A_WIDTH = D_MODEL // 2
A_GROUPS = A_WIDTH // 128
SGU_CHUNK = 128
B_HEADS = (D_MODEL // 2) // HEAD_DIM
B_KV_HEADS = B_HEADS // 4
B_GROUP = B_HEADS // B_KV_HEADS
B_Q_W = B_HEADS * HEAD_DIM
B_KV_W = B_KV_HEADS * HEAD_DIM
ROPE_THETA = 10000.0
AXIS_DIM = HEAD_DIM // 2
AB_IN_W = 3 * A_WIDTH + 2 * B_Q_W + 2 * B_KV_W
DN_K_HEADS = D_MODEL // HEAD_DIM
DN_V_HEADS = 2 * DN_K_HEADS
DN_K_W = DN_K_HEADS * HEAD_DIM
DN_V_W = DN_V_HEADS * HEAD_DIM
DN_QKV_W = 2 * DN_K_W + DN_V_W
DN_IN_W = DN_QKV_W + DN_V_W + 4 * DN_V_HEADS
DN_CHUNK = 64
CONV_K = 5

V7X_VMEM_LIMIT_BYTES = 56 * 1024 * 1024
LANES = 128
SUBLANES = 8

ROW_TILE = 512
CONV_ROW_TILE = 256


def _cparams(n_axes):
    return pltpu.CompilerParams(dimension_semantics=("arbitrary",) * n_axes,
                                vmem_limit_bytes=V7X_VMEM_LIMIT_BYTES)


def _silu(x):
    return x * jax.nn.sigmoid(x)


def _split_bf16(a):
    hi = a.astype(BF16)
    lo = (a - hi.astype(F32)).astype(BF16)
    return hi, lo


def _dot(a, b):
    return jnp.dot(a, b, preferred_element_type=F32)


def _dot_nt(a, b):
    return lax.dot_general(a, b, (((1,), (1,)), ((), ())), preferred_element_type=F32)


def _dot3(a, b):
    ah, al = _split_bf16(a)
    bh, bl = _split_bf16(b)
    return _dot(ah, bh) + _dot(ah, bl) + _dot(al, bh)


def _ada_kernel(c_ref, w_ref, b_ref, o_ref):
    s = _silu(c_ref[...])
    o_ref[0] = _dot3(s, w_ref[0]) + b_ref[0]


def _ada_mod(cond, ada_w, ada_b):
    depth = ada_w.shape[0]
    tn = 512
    return pl.pallas_call(
        _ada_kernel,
        grid=(depth, 3 * D_MODEL // tn),
        in_specs=[pl.BlockSpec((SUBLANES, D_MODEL), lambda l, j: (0, 0)),
                  pl.BlockSpec((1, D_MODEL, tn), lambda l, j: (l, 0, j)),
                  pl.BlockSpec((1, 1, tn), lambda l, j: (l, 0, j))],
        out_specs=pl.BlockSpec((1, SUBLANES, tn), lambda l, j: (l, 0, j)),
        out_shape=jax.ShapeDtypeStruct((depth, SUBLANES, 3 * D_MODEL), F32),
        compiler_params=_cparams(2),
        name="ada_mod",
    )(cond, ada_w, ada_b.reshape(depth, 1, 3 * D_MODEL))


INPROJ_COL_TILES = (512, 1024, 1408, 1536)


def _inproj_kernel(x_ref, nw_ref, mod_ref, w_ref, *rest):
    h_ref = rest[-1]
    has_side = len(rest) == 4

    @pl.when(pl.program_id(1) == 0)
    def _():
        x = x_ref[...]
        y = x * lax.rsqrt(jnp.mean(x * x, axis=-1, keepdims=True) + EPS) * nw_ref[...]
        shift = mod_ref[:, 0:D_MODEL]
        scale = mod_ref[:, D_MODEL:2 * D_MODEL]
        h_ref[...] = (y * (1.0 + scale) + shift).astype(BF16)
        if has_side:
            rest[2][...] = _dot(h_ref[...], rest[0][...])

    o_ref = rest[1] if has_side else rest[0]
    o_ref[...] = _dot(h_ref[...], w_ref[...]).astype(o_ref.dtype)


def _mod_row(i, lat_tiles, batch):
    return jnp.minimum(i // lat_tiles, batch)


def _inproj(x, norm_w, mod, w_bf16, batch, seq, n_side=0, out_dtype=BF16):
    m = x.shape[0]
    n = w_bf16.shape[1] - n_side
    tm = ROW_TILE
    tn = max(t for t in INPROJ_COL_TILES if n % t == 0)
    lat_tiles = seq // tm
    in_specs = [pl.BlockSpec((tm, D_MODEL), lambda i, j: (i, 0)),
                pl.BlockSpec((1, D_MODEL), lambda i, j: (0, 0)),
                pl.BlockSpec((None, 1, 3 * D_MODEL), lambda i, j: (_mod_row(i, lat_tiles, batch), 0, 0)),
                pl.BlockSpec((D_MODEL, tn), lambda i, j: (0, j))]
    out_specs = [pl.BlockSpec((tm, tn), lambda i, j: (i, j))]
    out_shape = [jax.ShapeDtypeStruct((m, n), out_dtype)]
    args = [x, norm_w.reshape(1, D_MODEL), mod, w_bf16]
    if n_side:
        assert n % n_side == 0
        in_specs.append(pl.BlockSpec((D_MODEL, n_side), lambda i, j: (0, n // n_side)))
        out_specs.append(pl.BlockSpec((tm, n_side), lambda i, j: (i, 0)))
        out_shape.append(jax.ShapeDtypeStruct((m, n_side), F32))
        args.append(w_bf16)
    res = pl.pallas_call(
        _inproj_kernel,
        grid=(m // tm, n // tn),
        in_specs=in_specs,
        out_specs=out_specs,
        out_shape=out_shape,
        scratch_shapes=[pltpu.VMEM((tm, D_MODEL), BF16)],
        compiler_params=_cparams(2),
        name="inproj",
    )(*args)
    return res if n_side else res[0]


Q_PRESCALE = (HEAD_DIM ** -0.5) * 1.4426950408889634


def _qkprep_kernel(q_ref, k_ref, v_ref, cos_ref, sin_ref, qw_ref, kw_ref, qo_ref, ko_ref, vo_ref):
    cos = cos_ref[...]
    sin = sin_ref[...]
    lane = lax.broadcasted_iota(jnp.int32, cos.shape, 1)
    first = (lane % (AXIS_DIM)) < (AXIS_DIM // 2)

    def prep(x, w):
        y = x * lax.rsqrt(jnp.mean(x * x, axis=-1, keepdims=True) + EPS) * w
        rot = jnp.where(first, pltpu.roll(y, HEAD_DIM - AXIS_DIM // 2, 1), pltpu.roll(y, AXIS_DIM // 2, 1))
        return y * cos + rot * sin

    for h in range(B_HEADS):
        sl = slice(h * HEAD_DIM, (h + 1) * HEAD_DIM)
        qo_ref[:, sl] = (prep(q_ref[:, sl].astype(F32), qw_ref[...]) * Q_PRESCALE).astype(qo_ref.dtype)
    for h in range(B_KV_HEADS):
        sl = slice(h * HEAD_DIM, (h + 1) * HEAD_DIM)
        ko_ref[:, sl] = prep(k_ref[:, sl].astype(F32), kw_ref[...]).astype(ko_ref.dtype)
    vo_ref[...] = v_ref[...].astype(vo_ref.dtype)


def _qkprep(proj, cos_tab, sin_tab, qn_w, kn_w, batch, seq):
    m = proj.shape[0]
    tm = CONV_ROW_TILE
    lat_tiles = seq // tm
    n_lat = batch * lat_tiles
    q_blk = (3 * A_WIDTH) // B_Q_W
    k_blk = (3 * A_WIDTH + B_Q_W) // B_KV_W

    def tab_idx(i):
        return (jnp.where(i < n_lat, i % lat_tiles, lat_tiles), 0)

    def kv_idx(i):
        lat_blk = (i // lat_tiles) * (lat_tiles + 1) + 1 + i % lat_tiles
        return (jnp.where(i < n_lat, lat_blk, (i - n_lat) * (lat_tiles + 1)), 0)

    return pl.pallas_call(
        _qkprep_kernel,
        grid=(m // tm,),
        in_specs=[pl.BlockSpec((tm, B_Q_W), lambda i: (i, q_blk)),
                  pl.BlockSpec((tm, B_KV_W), lambda i: (i, k_blk)),
                  pl.BlockSpec((tm, B_KV_W), lambda i: (i, k_blk + 1)),
                  pl.BlockSpec((tm, HEAD_DIM), tab_idx),
                  pl.BlockSpec((tm, HEAD_DIM), tab_idx),
                  pl.BlockSpec((1, HEAD_DIM), lambda i: (0, 0)),
                  pl.BlockSpec((1, HEAD_DIM), lambda i: (0, 0))],
        out_specs=[pl.BlockSpec((tm, B_Q_W), lambda i: (i, 0)),
                   pl.BlockSpec((tm, B_KV_W), kv_idx),
                   pl.BlockSpec((tm, B_KV_W), kv_idx)],
        out_shape=[jax.ShapeDtypeStruct((m, B_Q_W), BF16),
                   jax.ShapeDtypeStruct((m, B_KV_W), BF16),
                   jax.ShapeDtypeStruct((m, B_KV_W), BF16)],
        compiler_params=_cparams(1),
        name="qk_prep",
    )(proj, proj, proj, cos_tab, sin_tab, qn_w.reshape(1, HEAD_DIM), kn_w.reshape(1, HEAD_DIM))


def _rope_tables(seq, tm):
    rows = seq // GRID_W
    row = jnp.repeat(jnp.arange(rows), GRID_W).astype(F32)
    col = jnp.tile(jnp.arange(GRID_W), rows).astype(F32)
    freqs = ROPE_THETA ** (-jnp.arange(0, AXIS_DIM, 2, dtype=F32) / AXIS_DIM)
    ang_r = row[:, None] * freqs[None, :]
    ang_c = col[:, None] * freqs[None, :]
    ang = jnp.concatenate([ang_r, ang_r, ang_c, ang_c], axis=-1)
    sign = jnp.where((jnp.arange(HEAD_DIM) % AXIS_DIM) < AXIS_DIM // 2, -1.0, 1.0).astype(F32)
    cos = jnp.concatenate([jnp.cos(ang), jnp.ones((tm, HEAD_DIM), F32)], axis=0)
    sin = jnp.concatenate([jnp.sin(ang) * sign[None, :], jnp.zeros((tm, HEAD_DIM), F32)], axis=0)
    return cos, sin


ATTN_Q_TILE = 256
ATTN_KV_CHUNK = 768


def _attn_kernel(q_ref, k_ref, v_ref, o_ref, m_scr, l_scr, acc_scr, *, n_chunks, tk, tq):
    q = jnp.concatenate([q_ref[:, g * HEAD_DIM:(g + 1) * HEAD_DIM] for g in range(B_GROUP)], axis=0)
    m_scr[...] = jnp.full(m_scr.shape, -jnp.inf, F32)
    l_scr[...] = jnp.zeros(l_scr.shape, F32)
    acc_scr[...] = jnp.zeros(acc_scr.shape, F32)
    for c in range(n_chunks):
        k = k_ref[c * tk:(c + 1) * tk, :]
        v = v_ref[c * tk:(c + 1) * tk, :]
        s = _dot_nt(q, k)
        m_prev = m_scr[...]
        m_next = jnp.maximum(m_prev, jnp.max(s, axis=1, keepdims=True))
        p = jnp.exp2(s - jnp.concatenate([m_next] * (tk // LANES), axis=1))
        alpha = jnp.exp2(m_prev - m_next)
        l_scr[...] = alpha * l_scr[...] + jnp.sum(p, axis=1, keepdims=True)
        acc_scr[...] = acc_scr[...] * alpha + _dot(p.astype(BF16), v)
        m_scr[...] = m_next
    out = acc_scr[...] / l_scr[...]
    for g in range(B_GROUP):
        o_ref[:, g * HEAD_DIM:(g + 1) * HEAD_DIM] = out[g * tq:(g + 1) * tq].astype(o_ref.dtype)


def _attention(qr, kr, vr, batch, seq, ctx_len, latent):
    tq = ATTN_Q_TILE
    gw = B_GROUP * HEAD_DIM
    q_len = seq if latent else ctx_len
    q_tiles = q_len // tq
    q_row0 = 0 if latent else batch * seq // tq
    kv_len = ctx_len + seq
    if latent:
        kv_rows, tk = kv_len, ATTN_KV_CHUNK
        kv_idx = lambda b, h, i: (b, h)
    else:
        kv_rows, tk = ctx_len, ctx_len
        kv_idx = lambda b, h, i: (b * (kv_len // ctx_len), h)
    assert kv_rows % tk == 0
    rows = B_GROUP * tq
    return pl.pallas_call(
        functools.partial(_attn_kernel, n_chunks=kv_rows // tk, tk=tk, tq=tq),
        grid=(batch, B_KV_HEADS, q_tiles),
        in_specs=[pl.BlockSpec((tq, gw), lambda b, h, i: (q_row0 + b * q_tiles + i, h)),
                  pl.BlockSpec((kv_rows, HEAD_DIM), kv_idx),
                  pl.BlockSpec((kv_rows, HEAD_DIM), kv_idx)],
        out_specs=pl.BlockSpec((tq, gw), lambda b, h, i: (b * q_tiles + i, h)),
        out_shape=jax.ShapeDtypeStruct((batch * q_len, B_Q_W), BF16),
        scratch_shapes=[pltpu.VMEM((rows, LANES), F32), pltpu.VMEM((rows, LANES), F32),
                        pltpu.VMEM((rows, HEAD_DIM), F32)],
        compiler_params=_cparams(3),
        name="attn_lat" if latent else "attn_ctx",
    )(qr, kr, vr)


def _ab_out_kernel(u_ref, v_ref, ga_ref, gb0_ref, gb1_ref, aol_ref, aoc_ref, ws_ref, bs_ref, w_ref, x_ref, gate_ref, o_ref,
                   y_ref, acc_ref, *, n_lat_tiles):
    k = pl.program_id(1)

    @pl.when(k == 0)
    def _():
        tm = u_ref.shape[0]
        for c in range(tm // SGU_CHUNK):
            rows = slice(c * SGU_CHUNK, (c + 1) * SGU_CHUNK)
            for g in range(A_GROUPS):
                cols = slice(g * LANES, (g + 1) * LANES)
                vg = v_ref[rows, cols].astype(F32)
                d = vg - jnp.mean(vg, axis=-1, keepdims=True)
                var = jnp.mean(d * d, axis=-1, keepdims=True)
                vn = (d * lax.rsqrt(var + 1e-5)).astype(BF16)
                mixed = _dot(ws_ref[g], vn) + bs_ref[:, g:g + 1]
                y_ref[rows, cols] = (u_ref[rows, cols].astype(F32) * mixed
                                     * _silu(ga_ref[rows, cols].astype(F32))).astype(BF16)
        acc_ref[...] = _dot(y_ref[...], w_ref[0:A_WIDTH, :])

    @pl.when(k == 1)
    def _():
        ao = jnp.where(pl.program_id(0) < n_lat_tiles, aol_ref[...], aoc_ref[...]).astype(F32)
        gb = jnp.concatenate([gb0_ref[...], gb1_ref[...]], axis=1).astype(F32)
        y = (ao * _silu(gb)).astype(BF16)
        o_ref[...] = x_ref[...] + gate_ref[...] * (acc_ref[...] + _dot(y, w_ref[A_WIDTH:, :]))


def _ab_out(proj, ao_lat, ao_ctx, sgu_w_bf16, sgu_b_t, w_out_bf16, x, mod, batch, seq):
    m = x.shape[0]
    tm = ROW_TILE
    lat_tiles = seq // tm
    n_lat_tiles = batch * lat_tiles
    assert ao_ctx.shape[0] == tm
    gb_blk0 = (3 * A_WIDTH + B_Q_W + 2 * B_KV_W) // (B_Q_W // 2)
    return pl.pallas_call(
        functools.partial(_ab_out_kernel, n_lat_tiles=n_lat_tiles),
        grid=(m // tm, 2),
        in_specs=[pl.BlockSpec((tm, A_WIDTH), lambda i, k: (i, 0)),
                  pl.BlockSpec((tm, A_WIDTH), lambda i, k: (i, 1)),
                  pl.BlockSpec((tm, A_WIDTH), lambda i, k: (i, 2)),
                  pl.BlockSpec((tm, B_Q_W // 2), lambda i, k: (i, gb_blk0)),
                  pl.BlockSpec((tm, B_Q_W // 2), lambda i, k: (i, gb_blk0 + 1)),
                  pl.BlockSpec((tm, B_Q_W), lambda i, k: (jnp.minimum(i, n_lat_tiles - 1), 0)),
                  pl.BlockSpec((tm, B_Q_W), lambda i, k: (0, 0)),
                  pl.BlockSpec((A_GROUPS, SGU_CHUNK, SGU_CHUNK), lambda i, k: (0, 0, 0)),
                  pl.BlockSpec((SGU_CHUNK, A_GROUPS), lambda i, k: (0, 0)),
                  pl.BlockSpec((A_WIDTH + B_Q_W, D_MODEL), lambda i, k: (0, 0), pipeline_mode=pl.Buffered(1)),
                  pl.BlockSpec((tm, D_MODEL), lambda i, k: (i, 0)),
                  pl.BlockSpec((None, 1, D_MODEL), lambda i, k: (_mod_row(i, lat_tiles, batch), 0, 2))],
        out_specs=pl.BlockSpec((tm, D_MODEL), lambda i, k: (i, 0)),
        out_shape=jax.ShapeDtypeStruct((m, D_MODEL), F32),
        scratch_shapes=[pltpu.VMEM((tm, A_WIDTH), BF16), pltpu.VMEM((tm, D_MODEL), F32)],
        compiler_params=_cparams(2),
        name="ab_out",
    )(proj, proj, proj, proj, proj, ao_lat, ao_ctx, sgu_w_bf16, sgu_b_t, w_out_bf16, x, mod)


def _dnconv_kernel(xp_ref, xc_ref, xn_ref, w_ref, o_ref, *, lat_tiles, n_lat, q_tiles, k_tiles):
    i = pl.program_id(0)
    j = pl.program_id(1)
    tm = xc_ref.shape[0]
    pos = i % lat_tiles
    is_lat = i < n_lat
    first = jnp.logical_or(jnp.logical_not(is_lat), pos == 0)
    last = jnp.logical_or(jnp.logical_not(is_lat), pos == lat_tiles - 1)
    half = CONV_K // 2
    sub = lax.broadcasted_iota(jnp.int32, (SUBLANES, HEAD_DIM), 0)

    def conv_silu(sl):
        x = xc_ref[:, sl].astype(F32)
        prev = jnp.where(first, 0.0, xp_ref[:, sl].astype(F32)[SUBLANES:])
        nxt = jnp.where(last, 0.0, xn_ref[:, sl].astype(F32)[:SUBLANES])
        acc = w_ref[half:half + 1, sl] * x
        for t in range(CONV_K):
            s_rows = half - t
            if s_rows == 0:
                continue
            r = pltpu.roll(x, s_rows % tm, 0)
            if s_rows > 0:
                head = jnp.where(sub < s_rows, pltpu.roll(prev, s_rows, 0), r[0:SUBLANES])
                shifted = jnp.concatenate([head, r[SUBLANES:]], axis=0)
            else:
                tail = jnp.where(sub >= SUBLANES + s_rows, pltpu.roll(nxt, SUBLANES + s_rows, 0), r[tm - SUBLANES:])
                shifted = jnp.concatenate([r[:tm - SUBLANES], tail], axis=0)
            acc = acc + w_ref[t:t + 1, sl] * shifted
        return _silu(acc)

    heads = [slice(h * HEAD_DIM, (h + 1) * HEAD_DIM) for h in range(xc_ref.shape[1] // HEAD_DIM)]

    @pl.when(j < q_tiles + k_tiles)
    def _():
        qk_scale = jnp.where(j < q_tiles, HEAD_DIM ** -0.5, 1.0)
        for sl in heads:
            y = conv_silu(sl)
            o_ref[:, sl] = (y * (lax.rsqrt(jnp.sum(y * y, axis=-1, keepdims=True) + EPS) * qk_scale)).astype(o_ref.dtype)

    @pl.when(j >= q_tiles + k_tiles)
    def _():
        for sl in heads:
            o_ref[:, sl] = conv_silu(sl).astype(o_ref.dtype)


def _dn_conv(proj, conv_w, batch, seq, out_dtype=F32):
    m = proj.shape[0]
    tm, tc = CONV_ROW_TILE, 1024
    lat_tiles = seq // tm
    n_lat = batch * lat_tiles
    halo = 2 * SUBLANES
    sub_per_tile = tm // halo
    n_sub = m // halo
    kern = functools.partial(_dnconv_kernel, lat_tiles=lat_tiles, n_lat=n_lat,
                             q_tiles=DN_K_W // tc, k_tiles=DN_K_W // tc)
    return pl.pallas_call(
        kern,
        grid=(m // tm, DN_QKV_W // tc),
        in_specs=[pl.BlockSpec((halo, tc), lambda i, j: (jnp.maximum(i * sub_per_tile - 1, 0), j)),
                  pl.BlockSpec((tm, tc), lambda i, j: (i, j)),
                  pl.BlockSpec((halo, tc), lambda i, j: (jnp.minimum((i + 1) * sub_per_tile, n_sub - 1), j)),
                  pl.BlockSpec((SUBLANES, tc), lambda i, j: (0, j))],
        out_specs=pl.BlockSpec((tm, tc), lambda i, j: (i, j)),
        out_shape=jax.ShapeDtypeStruct((m, DN_QKV_W), out_dtype),
        compiler_params=_cparams(2),
        name="dn_conv",
    )(proj, proj, proj, conv_w)


def _dngate_kernel(ba_ref, alog_ref, dtb_ref, o_ref):
    ba = ba_ref[...]
    tm = ba.shape[0]
    lane = lax.broadcasted_iota(jnp.int32, (DN_CHUNK, LANES), 1)
    is_beta = (lane // DN_V_HEADS) % 2 == 0
    is_fwd = lane < 2 * DN_V_HEADS
    z = ba + dtb_ref[...]
    softplus = jnp.maximum(z, 0.0) + jnp.log1p(jnp.exp(-jnp.abs(z)))
    g = -jnp.exp(alog_ref[...]) * softplus
    r = lax.broadcasted_iota(jnp.int32, (DN_CHUNK, DN_CHUNK), 0)
    c = lax.broadcasted_iota(jnp.int32, (DN_CHUNK, DN_CHUNK), 1)
    tri_lo = (r >= c).astype(BF16)
    tri_up = (r <= c).astype(BF16)
    beta = jax.nn.sigmoid(ba)
    for ch in range(tm // DN_CHUNK):
        rows = slice(ch * DN_CHUNK, (ch + 1) * DN_CHUNK)
        gch = g[rows]
        g1 = gch.astype(BF16)
        r1 = gch - g1.astype(F32)
        g2 = r1.astype(BF16)
        g3 = (r1 - g2.astype(F32)).astype(BF16)
        pre = _dot(tri_lo, g1) + _dot(tri_lo, g2) + _dot(tri_lo, g3)
        suf = _dot(tri_up, g1) + _dot(tri_up, g2) + _dot(tri_up, g3)
        gc = jnp.where(is_fwd, pre, suf)
        o_ref[rows, :] = jnp.where(is_beta, beta[rows], gc)


def _dn_gate(ba, alog_vec, dtb_vec):
    m = ba.shape[0]
    tm = ROW_TILE
    return pl.pallas_call(
        _dngate_kernel,
        grid=(m // tm,),
        in_specs=[pl.BlockSpec((tm, LANES), lambda i: (i, 0)),
                  pl.BlockSpec((1, LANES), lambda i: (0, 0)),
                  pl.BlockSpec((1, LANES), lambda i: (0, 0))],
        out_specs=pl.BlockSpec((tm, LANES), lambda i: (i, 0)),
        out_shape=jax.ShapeDtypeStruct((m, LANES), F32),
        compiler_params=_cparams(1),
        name="dn_gate",
    )(ba, alog_vec, dtb_vec)


DN_CHAIN_GROUP = 16
DN_KH_PER_STEP = 16


def _dncore_kernel(qf_ref, kf_ref, vf_ref, gf_ref, qb_ref, kb_ref, vb_ref, gb_ref, of_ref, ob_ref, s_ref):
    C = DN_CHUNK
    nvh = 2 * DN_KH_PER_STEP

    @pl.when(pl.program_id(2) == 0)
    def _():
        s_ref[...] = jnp.zeros(s_ref.shape, F32)

    lane = lax.broadcasted_iota(jnp.int32, (C, 2 * C), 1)
    left = lane < C
    row = lax.broadcasted_iota(jnp.int32, (C, 2 * C), 0)
    colp = lane % C
    left_sq = lax.broadcasted_iota(jnp.int32, (2 * C, 2 * C), 1) < C
    eye2 = (row == colp).astype(F32)

    def blockdiag(p):
        z = jnp.zeros_like(p)
        return jnp.concatenate([jnp.where(left, p, z), jnp.where(left, z, p)], axis=0)

    def packed_mm(a, b):
        return _dot(a.astype(BF16), blockdiag(b.astype(BF16)))

    dirs = ((qf_ref, kf_ref, vf_ref, gf_ref, of_ref), (qb_ref, kb_ref, vb_ref, gb_ref, ob_ref))
    chains = [(d, kh) for d in range(2) for kh in range(DN_KH_PER_STEP)]
    G = [dirs[d][3][...] for d in range(2)]
    GT = [jnp.concatenate([g, g], axis=0).T for g in G]

    for g0 in range(0, len(chains), DN_CHAIN_GROUP):
        group = chains[g0:g0 + DN_CHAIN_GROUP]
        st = []
        for d, kh in group:
            q_ref, k_ref, v_ref, _, _ = dirs[d]
            base_beta = d * 2 * nvh
            base_gc = base_beta + nvh
            lv0 = 2 * kh
            q = q_ref[:, kh * HEAD_DIM:(kh + 1) * HEAD_DIM]
            k = k_ref[:, kh * HEAD_DIM:(kh + 1) * HEAD_DIM]
            k2 = jnp.concatenate([k, k], axis=0)
            kT2 = k2.T
            gram = _dot(jnp.concatenate([q, k], axis=0).astype(BF16), kT2.astype(BF16))
            b0, b1 = G[d][:, base_beta + lv0:base_beta + lv0 + 1], G[d][:, base_beta + lv0 + 1:base_beta + lv0 + 2]
            c0, c1 = G[d][:, base_gc + lv0:base_gc + lv0 + 1], G[d][:, base_gc + lv0 + 1:base_gc + lv0 + 2]
            r0, r1 = GT[d][base_gc + lv0:base_gc + lv0 + 1, :], GT[d][base_gc + lv0 + 1:base_gc + lv0 + 2, :]
            st.append(dict(d=d, lv0=lv0, q=q, k2=k2, kT2=kT2, gram=gram, b0=b0, b1=b1, c0=c0, c1=c1, r0=r0, r1=r1))

        for s in st:
            d = s["d"]
            incl = (row >= colp) if d == 0 else (row <= colp)
            strict = (row > colp) if d == 0 else (row < colp)
            gcol_p = jnp.where(left, s["c0"], s["c1"])
            grow_p = jnp.where(left[0:1], s["r0"], s["r1"])
            beta_p = jnp.where(left, s["b0"], s["b1"])
            dec = jnp.exp(jnp.where(incl, gcol_p - grow_p, -1e30))
            s["dec"] = dec
            s["attn"] = s["gram"][0:C] * dec
            s["L"] = jnp.where(strict, s["gram"][C:2 * C] * dec, 0.0) * beta_p

        for s in st:
            n1 = jnp.where(jnp.logical_and(row // 2 == colp // 2, row != colp), s["L"], 0.0)
            s["X"] = eye2 - n1
        blk = 2
        while blk < C:
            mask = jnp.logical_and(row // (2 * blk) == colp // (2 * blk), row // blk != colp // blk)
            for s in st:
                s["Y"] = packed_mm(s["X"], jnp.where(mask, s["L"], 0.0))
            for s in st:
                s["X"] = s["X"] - packed_mm(s["Y"], s["X"])
            blk *= 2

        for s in st:
            d, lv0 = s["d"], s["lv0"]
            v_ref = dirs[d][2]
            beta_r = jnp.concatenate([s["b0"], s["b1"]], axis=0)
            egc_r = jnp.exp(jnp.concatenate([s["c0"], s["c1"]], axis=0))
            v2 = jnp.concatenate([v_ref[:, lv0 * HEAD_DIM:(lv0 + 1) * HEAD_DIM],
                                  v_ref[:, (lv0 + 1) * HEAD_DIM:(lv0 + 2) * HEAD_DIM]], axis=0)
            rhs = jnp.concatenate([v2 * beta_r, s["k2"] * (beta_r * egc_r)], axis=1)
            s["sol"] = _dot(blockdiag(s["X"]).astype(BF16), rhs.astype(BF16))
            s["egc"] = egc_r
            s["qg2"] = jnp.concatenate([s["q"], s["q"]], axis=0) * egc_r

        for s in st:
            d, lv0 = s["d"], s["lv0"]
            w2 = s["sol"][:, HEAD_DIM:]
            s["ws"] = []
            for r in range(2):
                lhs = jnp.concatenate([w2[r * C:(r + 1) * C], s["qg2"][r * C:(r + 1) * C]], axis=0).astype(BF16)
                s["ws"].append(_dot(lhs, s_ref[d, lv0 + r].astype(BF16)))

        for s in st:
            u2 = s["sol"][:, 0:HEAD_DIM]
            vn2 = jnp.concatenate([u2[r * C:(r + 1) * C] - s["ws"][r][0:C] for r in range(2)], axis=0).astype(BF16)
            s["vn2"] = vn2
            s["o2"] = jnp.concatenate([s["ws"][r][C:2 * C] for r in range(2)], axis=0) + _dot(
                blockdiag(s["attn"]).astype(BF16), vn2)

        for s in st:
            d, lv0 = s["d"], s["lv0"]
            o_ref = dirs[d][4]
            last = C - 1 if d == 0 else 0
            kdT_p = s["kT2"] * s["dec"][last:last + 1, :]
            zkd = jnp.zeros_like(kdT_p)
            for r in range(2):
                kd_r = jnp.where(left_sq if r == 0 else jnp.logical_not(left_sq), kdT_p, zkd).astype(BF16)
                glr = jnp.broadcast_to(s["egc"][r * C + last:r * C + last + 1, :], (HEAD_DIM, HEAD_DIM))
                s_ref[d, lv0 + r] = s_ref[d, lv0 + r] * glr + _dot(kd_r, s["vn2"])
                o_ref[:, (lv0 + r) * HEAD_DIM:(lv0 + r + 1) * HEAD_DIM] = s["o2"][r * C:(r + 1) * C].astype(o_ref.dtype)


def _dn_core(qkv, gates, batch, seq, ctx_len):
    m = qkv.shape[0]
    C = DN_CHUNK
    n_lat = seq // C
    n_ctx = ctx_len // C
    n_steps = n_ctx + n_lat
    ctx0 = batch * n_lat
    khs = DN_KH_PER_STEP
    qw = khs * HEAD_DIM
    vw = 2 * khs * HEAD_DIM
    k_blk0 = DN_K_W // qw
    v_blk0 = 2 * DN_K_W // vw

    def rf(b, t):
        return jnp.where(t < n_ctx, ctx0 + b * n_ctx + t, b * n_lat + (t - n_ctx))

    def rb(b, t):
        return jnp.where(t < n_ctx, ctx0 + b * n_ctx + (n_ctx - 1 - t), b * n_lat + (n_lat - 1 - (t - n_ctx)))

    def specs(rfun):
        return [pl.BlockSpec((C, qw), lambda b, h, t: (rfun(b, t), h)),
                pl.BlockSpec((C, qw), lambda b, h, t: (rfun(b, t), k_blk0 + h)),
                pl.BlockSpec((C, vw), lambda b, h, t: (rfun(b, t), v_blk0 + h)),
                pl.BlockSpec((C, LANES), lambda b, h, t: (rfun(b, t), h))]

    return pl.pallas_call(
        _dncore_kernel,
        grid=(batch, DN_K_HEADS // khs, n_steps),
        in_specs=specs(rf) + specs(rb),
        out_specs=[pl.BlockSpec((C, vw), lambda b, h, t: (rf(b, t), h)),
                   pl.BlockSpec((C, vw), lambda b, h, t: (rb(b, t), h))],
        out_shape=[jax.ShapeDtypeStruct((m, DN_V_W), BF16), jax.ShapeDtypeStruct((m, DN_V_W), BF16)],
        scratch_shapes=[pltpu.VMEM((2, 2 * khs, HEAD_DIM, HEAD_DIM), F32)],
        compiler_params=_cparams(3),
        name="dn_core",
    )(qkv, qkv, qkv, gates, qkv, qkv, qkv, gates)


DN_OUT_K_TILE = 1024


def _dn_out_kernel(of_ref, ob_ref, z_ref, nw_ref, w_ref, x_ref, gate_ref, *rest):
    o_ref, acc_ref = rest[-2:]
    k = pl.program_id(1)
    ys = []
    for h in range(of_ref.shape[1] // HEAD_DIM):
        sl = slice(h * HEAD_DIM, (h + 1) * HEAD_DIM)
        o = of_ref[:, sl].astype(F32) + ob_ref[:, sl].astype(F32)
        n = o * lax.rsqrt(jnp.mean(o * o, axis=-1, keepdims=True) + EPS) * nw_ref[...]
        ys.append((n * _silu(z_ref[:, sl].astype(F32))).astype(BF16))
    tk = of_ref.shape[1]
    part = _dot(jnp.concatenate(ys, axis=1), w_ref[pl.ds(pl.multiple_of(k * tk, tk), tk), :])

    @pl.when(k == 0)
    def _():
        acc_ref[...] = part

    @pl.when(jnp.logical_and(k > 0, k < pl.num_programs(1) - 1))
    def _():
        acc_ref[...] += part

    @pl.when(k == pl.num_programs(1) - 1)
    def _():
        y = x_ref[...] + gate_ref[...] * (acc_ref[...] + part)
        if len(rest) == 3:
            y = y * lax.rsqrt(jnp.mean(y * y, axis=-1, keepdims=True) + EPS) * rest[0][...]
        o_ref[...] = y


def _dn_out(o_f, o_b, proj, norm_w, w_out_bf16, x, mod, batch, seq, final_norm_w=None):
    tm, tk = ROW_TILE, DN_OUT_K_TILE
    m = x.shape[0] if final_norm_w is None else batch * seq
    lat_tiles = seq // tm
    z_blk0 = DN_QKV_W // tk
    extra_specs, extra_args = [], []
    if final_norm_w is not None:
        extra_specs = [pl.BlockSpec((1, D_MODEL), lambda i, k: (0, 0))]
        extra_args = [final_norm_w.reshape(1, D_MODEL)]
    return pl.pallas_call(
        _dn_out_kernel,
        grid=(m // tm, DN_V_W // tk),
        in_specs=[pl.BlockSpec((tm, tk), lambda i, k: (i, k)),
                  pl.BlockSpec((tm, tk), lambda i, k: (i, k)),
                  pl.BlockSpec((tm, tk), lambda i, k: (i, z_blk0 + k)),
                  pl.BlockSpec((1, HEAD_DIM), lambda i, k: (0, 0)),
                  pl.BlockSpec((DN_V_W, D_MODEL), lambda i, k: (0, 0), pipeline_mode=pl.Buffered(1)),
                  pl.BlockSpec((tm, D_MODEL), lambda i, k: (i, 0)),
                  pl.BlockSpec((None, 1, D_MODEL), lambda i, k: (_mod_row(i, lat_tiles, batch), 0, 2))] + extra_specs,
        out_specs=pl.BlockSpec((tm, D_MODEL), lambda i, k: (i, 0)),
        out_shape=jax.ShapeDtypeStruct((m, D_MODEL), F32),
        scratch_shapes=[pltpu.VMEM((tm, D_MODEL), F32)],
        compiler_params=_cparams(2),
        name="dn_out",
    )(o_f, o_b, proj, norm_w.reshape(1, HEAD_DIM), w_out_bf16, x, mod, *extra_args)


def _final_norm_kernel(x_ref, w_ref, o_ref):
    x = x_ref[...]
    o_ref[...] = x * lax.rsqrt(jnp.mean(x * x, axis=-1, keepdims=True) + EPS) * w_ref[...]


def _final_norm(x, w, rows):
    tm = ROW_TILE
    return pl.pallas_call(
        _final_norm_kernel,
        grid=(rows // tm,),
        in_specs=[pl.BlockSpec((tm, D_MODEL), lambda i: (i, 0)),
                  pl.BlockSpec((1, D_MODEL), lambda i: (0, 0))],
        out_specs=pl.BlockSpec((tm, D_MODEL), lambda i: (i, 0)),
        out_shape=jax.ShapeDtypeStruct((rows, D_MODEL), F32),
        compiler_params=_cparams(1),
        name="final_norm",
    )(x, w.reshape(1, D_MODEL))


def kernel(x, c, ctx, c_ctx, norm_w, ada_w, ada_b, ab_w_in, ab_w_out, sgu_w, sgu_b, q_norm_w, k_norm_w,
           dn_w_in, dn_conv_w, dn_a_log, dn_dt_bias, dn_norm_w, dn_w_out, final_norm_w):
    batch, seq, _ = x.shape
    ctx_len = ctx.shape[1]
    depth = norm_w.shape[0]
    assert ctx_len == CONV_ROW_TILE and seq % ROW_TILE == 0 and (batch * ctx_len) % ROW_TILE == 0
    n_lat_rows = batch * seq

    xs = jnp.concatenate([x.reshape(n_lat_rows, D_MODEL), ctx.reshape(batch * ctx_len, D_MODEL)], axis=0)
    cond = jnp.zeros((SUBLANES, D_MODEL), F32).at[0:batch].set(c).at[batch].set(c_ctx)
    mods = _ada_mod(cond, ada_w, ada_b)
    cos_tab, sin_tab = _rope_tables(seq, CONV_ROW_TILE)

    for i in range(depth):
        j = i // 2
        mod = mods[i].reshape(SUBLANES, 1, 3 * D_MODEL)
        if i % 2 == 0:
            proj = _inproj(xs, norm_w[i], mod, ab_w_in[j].astype(BF16), batch, seq)
            qr, kr, vr = _qkprep(proj, cos_tab, sin_tab, q_norm_w[j], k_norm_w[j], batch, seq)
            ao_lat = _attention(qr, kr, vr, batch, seq, ctx_len, True)
            ao_ctx = _attention(qr, kr, vr, batch, seq, ctx_len, False)
            xs = _ab_out(proj, ao_lat, ao_ctx, sgu_w[j].astype(BF16), sgu_b[j].T, ab_w_out[j].astype(BF16), xs, mod,
                         batch, seq)
        else:
            proj, ba = _inproj(xs, norm_w[i], mod, dn_w_in[j].astype(BF16), batch, seq, n_side=4 * DN_V_HEADS)
            conv_w = jnp.zeros((SUBLANES, DN_QKV_W), F32).at[0:CONV_K].set(dn_conv_w[j])
            qkv = _dn_conv(proj, conv_w, batch, seq)
            zeros = jnp.zeros((2, DN_V_HEADS), F32)
            alog_vec = jnp.concatenate([zeros, dn_a_log[j]], axis=1).reshape(1, LANES)
            dtb_vec = jnp.concatenate([zeros, dn_dt_bias[j]], axis=1).reshape(1, LANES)
            gates = _dn_gate(ba, alog_vec, dtb_vec)
            n_hg = DN_K_HEADS // DN_KH_PER_STEP
            nvh = 2 * DN_KH_PER_STEP
            g4 = gates.reshape(-1, 4, n_hg, nvh).transpose(0, 2, 1, 3).reshape(-1, n_hg, 4 * nvh)
            g4 = jnp.pad(g4, ((0, 0), (0, 0), (0, LANES - 4 * nvh))).reshape(-1, n_hg * LANES)
            o_f, o_b = _dn_core(qkv, g4, batch, seq, ctx_len)
            xs = _dn_out(o_f, o_b, proj, dn_norm_w[j], dn_w_out[j].astype(BF16), xs, mod, batch, seq,
                         final_norm_w=final_norm_w if i == depth - 1 else None)

    if depth % 2 == 1:
        xs = _final_norm(xs, final_norm_w, n_lat_rows)
    return xs.reshape(batch, seq, D_MODEL)
```

```python
import functools

import jax
import jax.numpy as jnp
from jax import lax
from jax.experimental import pallas as pl
from jax.experimental.pallas import tpu as pltpu

F32 = jnp.float32
BF16 = jnp.bfloat16

D_MODEL = 2048
GRID_W = 64
EPS = 1e-6
HEAD_DIM = 128
A_WIDTH = D_MODEL // 2
A_GROUPS = A_WIDTH // 128
SGU_CHUNK = 128
B_HEADS = (D_MODEL // 2) // HEAD_DIM
B_KV_HEADS = B_HEADS // 4
B_GROUP = B_HEADS // B_KV_HEADS
B_Q_W = B_HEADS * HEAD_DIM
B_KV_W = B_KV_HEADS * HEAD_DIM
ROPE_THETA = 10000.0
AXIS_DIM = HEAD_DIM // 2
AB_IN_W = 3 * A_WIDTH + 2 * B_Q_W + 2 * B_KV_W
DN_K_HEADS = D_MODEL // HEAD_DIM
DN_V_HEADS = 2 * DN_K_HEADS
DN_K_W = DN_K_HEADS * HEAD_DIM
DN_V_W = DN_V_HEADS * HEAD_DIM
DN_QKV_W = 2 * DN_K_W + DN_V_W
DN_IN_W = DN_QKV_W + DN_V_W + 4 * DN_V_HEADS
DN_CHUNK = 64
CONV_K = 5

V7X_VMEM_LIMIT_BYTES = 56 * 1024 * 1024
LANES = 128
SUBLANES = 8

ROW_TILE = 512
CONV_ROW_TILE = 256


def _cparams(n_axes):
    return pltpu.CompilerParams(dimension_semantics=("arbitrary",) * n_axes,
                                vmem_limit_bytes=V7X_VMEM_LIMIT_BYTES)


def _silu(x):
    return x * jax.nn.sigmoid(x)


def _split_bf16(a):
    hi = a.astype(BF16)
    lo = (a - hi.astype(F32)).astype(BF16)
    return hi, lo


def _dot(a, b):
    return jnp.dot(a, b, preferred_element_type=F32)


def _dot_nt(a, b):
    return lax.dot_general(a, b, (((1,), (1,)), ((), ())), preferred_element_type=F32)


def _dot3(a, b):
    ah, al = _split_bf16(a)
    bh, bl = _split_bf16(b)
    return _dot(ah, bh) + _dot(ah, bl) + _dot(al, bh)


def _ada_kernel(c_ref, w_ref, b_ref, o_ref):
    s = _silu(c_ref[...])
    o_ref[0] = _dot3(s, w_ref[0]) + b_ref[0]


def _ada_mod(cond, ada_w, ada_b):
    depth = ada_w.shape[0]
    tn = 512
    return pl.pallas_call(
        _ada_kernel,
        grid=(depth, 3 * D_MODEL // tn),
        in_specs=[pl.BlockSpec((SUBLANES, D_MODEL), lambda l, j: (0, 0)),
                  pl.BlockSpec((1, D_MODEL, tn), lambda l, j: (l, 0, j)),
                  pl.BlockSpec((1, 1, tn), lambda l, j: (l, 0, j))],
        out_specs=pl.BlockSpec((1, SUBLANES, tn), lambda l, j: (l, 0, j)),
        out_shape=jax.ShapeDtypeStruct((depth, SUBLANES, 3 * D_MODEL), F32),
        compiler_params=_cparams(2),
        name="ada_mod",
    )(cond, ada_w, ada_b.reshape(depth, 1, 3 * D_MODEL))


INPROJ_COL_TILES = (512, 1024, 1408, 1536)


def _inproj_kernel(x_ref, nw_ref, mod_ref, w_ref, *rest):
    h_ref = rest[-1]
    has_side = len(rest) == 4

    @pl.when(pl.program_id(1) == 0)
    def _():
        x = x_ref[...]
        y = x * lax.rsqrt(jnp.mean(x * x, axis=-1, keepdims=True) + EPS) * nw_ref[...]
        shift = mod_ref[:, 0:D_MODEL]
        scale = mod_ref[:, D_MODEL:2 * D_MODEL]
        h_ref[...] = (y * (1.0 + scale) + shift).astype(BF16)
        if has_side:
            rest[2][...] = _dot(h_ref[...], rest[0][...])

    o_ref = rest[1] if has_side else rest[0]
    o_ref[...] = _dot(h_ref[...], w_ref[...]).astype(o_ref.dtype)


def _mod_row(i, lat_tiles, batch):
    return jnp.minimum(i // lat_tiles, batch)


def _inproj(x, norm_w, mod, w_bf16, batch, seq, n_side=0, out_dtype=BF16):
    m = x.shape[0]
    n = w_bf16.shape[1] - n_side
    tm = ROW_TILE
    tn = max(t for t in INPROJ_COL_TILES if n % t == 0)
    lat_tiles = seq // tm
    in_specs = [pl.BlockSpec((tm, D_MODEL), lambda i, j: (i, 0)),
                pl.BlockSpec((1, D_MODEL), lambda i, j: (0, 0)),
                pl.BlockSpec((None, 1, 3 * D_MODEL), lambda i, j: (_mod_row(i, lat_tiles, batch), 0, 0)),
                pl.BlockSpec((D_MODEL, tn), lambda i, j: (0, j))]
    out_specs = [pl.BlockSpec((tm, tn), lambda i, j: (i, j))]
    out_shape = [jax.ShapeDtypeStruct((m, n), out_dtype)]
    args = [x, norm_w.reshape(1, D_MODEL), mod, w_bf16]
    if n_side:
        assert n % n_side == 0
        in_specs.append(pl.BlockSpec((D_MODEL, n_side), lambda i, j: (0, n // n_side)))
        out_specs.append(pl.BlockSpec((tm, n_side), lambda i, j: (i, 0)))
        out_shape.append(jax.ShapeDtypeStruct((m, n_side), F32))
        args.append(w_bf16)
    res = pl.pallas_call(
        _inproj_kernel,
        grid=(m // tm, n // tn),
        in_specs=in_specs,
        out_specs=out_specs,
        out_shape=out_shape,
        scratch_shapes=[pltpu.VMEM((tm, D_MODEL), BF16)],
        compiler_params=_cparams(2),
        name="inproj",
    )(*args)
    return res if n_side else res[0]


Q_PRESCALE = (HEAD_DIM ** -0.5) * 1.4426950408889634


def _qkprep_kernel(q_ref, k_ref, v_ref, cos_ref, sin_ref, qw_ref, kw_ref, qo_ref, ko_ref, vo_ref):
    cos = cos_ref[...]
    sin = sin_ref[...]
    lane = lax.broadcasted_iota(jnp.int32, cos.shape, 1)
    first = (lane % (AXIS_DIM)) < (AXIS_DIM // 2)

    def prep(x, w):
        y = x * lax.rsqrt(jnp.mean(x * x, axis=-1, keepdims=True) + EPS) * w
        rot = jnp.where(first, pltpu.roll(y, HEAD_DIM - AXIS_DIM // 2, 1), pltpu.roll(y, AXIS_DIM // 2, 1))
        return y * cos + rot * sin

    for h in range(B_HEADS):
        sl = slice(h * HEAD_DIM, (h + 1) * HEAD_DIM)
        qo_ref[:, sl] = (prep(q_ref[:, sl].astype(F32), qw_ref[...]) * Q_PRESCALE).astype(qo_ref.dtype)
    for h in range(B_KV_HEADS):
        sl = slice(h * HEAD_DIM, (h + 1) * HEAD_DIM)
        ko_ref[:, sl] = prep(k_ref[:, sl].astype(F32), kw_ref[...]).astype(ko_ref.dtype)
    vo_ref[...] = v_ref[...].astype(vo_ref.dtype)


def _qkprep(proj, cos_tab, sin_tab, qn_w, kn_w, batch, seq):
    m = proj.shape[0]
    tm = CONV_ROW_TILE
    lat_tiles = seq // tm
    n_lat = batch * lat_tiles
    q_blk = (3 * A_WIDTH) // B_Q_W
    k_blk = (3 * A_WIDTH + B_Q_W) // B_KV_W

    def tab_idx(i):
        return (jnp.where(i < n_lat, i % lat_tiles, lat_tiles), 0)

    def kv_idx(i):
        lat_blk = (i // lat_tiles) * (lat_tiles + 1) + 1 + i % lat_tiles
        return (jnp.where(i < n_lat, lat_blk, (i - n_lat) * (lat_tiles + 1)), 0)

    return pl.pallas_call(
        _qkprep_kernel,
        grid=(m // tm,),
        in_specs=[pl.BlockSpec((tm, B_Q_W), lambda i: (i, q_blk)),
                  pl.BlockSpec((tm, B_KV_W), lambda i: (i, k_blk)),
                  pl.BlockSpec((tm, B_KV_W), lambda i: (i, k_blk + 1)),
                  pl.BlockSpec((tm, HEAD_DIM), tab_idx),
                  pl.BlockSpec((tm, HEAD_DIM), tab_idx),
                  pl.BlockSpec((1, HEAD_DIM), lambda i: (0, 0)),
                  pl.BlockSpec((1, HEAD_DIM), lambda i: (0, 0))],
        out_specs=[pl.BlockSpec((tm, B_Q_W), lambda i: (i, 0)),
                   pl.BlockSpec((tm, B_KV_W), kv_idx),
                   pl.BlockSpec((tm, B_KV_W), kv_idx)],
        out_shape=[jax.ShapeDtypeStruct((m, B_Q_W), BF16),
                   jax.ShapeDtypeStruct((m, B_KV_W), BF16),
                   jax.ShapeDtypeStruct((m, B_KV_W), BF16)],
        compiler_params=_cparams(1),
        name="qk_prep",
    )(proj, proj, proj, cos_tab, sin_tab, qn_w.reshape(1, HEAD_DIM), kn_w.reshape(1, HEAD_DIM))


def _rope_tables(seq, tm):
    rows = seq // GRID_W
    row = jnp.repeat(jnp.arange(rows), GRID_W).astype(F32)
    col = jnp.tile(jnp.arange(GRID_W), rows).astype(F32)
    freqs = ROPE_THETA ** (-jnp.arange(0, AXIS_DIM, 2, dtype=F32) / AXIS_DIM)
    ang_r = row[:, None] * freqs[None, :]
    ang_c = col[:, None] * freqs[None, :]
    ang = jnp.concatenate([ang_r, ang_r, ang_c, ang_c], axis=-1)
    sign = jnp.where((jnp.arange(HEAD_DIM) % AXIS_DIM) < AXIS_DIM // 2, -1.0, 1.0).astype(F32)
    cos = jnp.concatenate([jnp.cos(ang), jnp.ones((tm, HEAD_DIM), F32)], axis=0)
    sin = jnp.concatenate([jnp.sin(ang) * sign[None, :], jnp.zeros((tm, HEAD_DIM), F32)], axis=0)
    return cos, sin


ATTN_Q_TILE = 256
ATTN_KV_CHUNK = 2816


def _attn_kernel(q_ref, k_ref, v_ref, o_ref, m_scr, l_scr, acc_scr, *, n_chunks, tk, tq):
    q = jnp.concatenate([q_ref[:, g * HEAD_DIM:(g + 1) * HEAD_DIM] for g in range(B_GROUP)], axis=0)
    m_scr[...] = jnp.full(m_scr.shape, -jnp.inf, F32)
    l_scr[...] = jnp.zeros(l_scr.shape, F32)
    acc_scr[...] = jnp.zeros(acc_scr.shape, F32)
    for c in range(n_chunks):
        k = k_ref[c * tk:(c + 1) * tk, :]
        v = v_ref[c * tk:(c + 1) * tk, :]
        s = _dot_nt(q, k)
        m_prev = m_scr[...]
        m_next = jnp.maximum(m_prev, jnp.max(s, axis=1, keepdims=True))
        p = jnp.exp2(s - jnp.concatenate([m_next] * (tk // LANES), axis=1))
        alpha = jnp.exp2(m_prev - m_next)
        l_scr[...] = alpha * l_scr[...] + jnp.sum(p, axis=1, keepdims=True)
        acc_scr[...] = acc_scr[...] * alpha + _dot(p.astype(BF16), v)
        m_scr[...] = m_next
    out = acc_scr[...] / l_scr[...]
    for g in range(B_GROUP):
        o_ref[:, g * HEAD_DIM:(g + 1) * HEAD_DIM] = out[g * tq:(g + 1) * tq].astype(o_ref.dtype)


def _attention(qr, kr, vr, batch, seq, ctx_len, latent):
    tq = ATTN_Q_TILE
    gw = B_GROUP * HEAD_DIM
    q_len = seq if latent else ctx_len
    q_tiles = q_len // tq
    q_row0 = 0 if latent else batch * seq // tq
    kv_len = ctx_len + seq
    if latent:
        kv_rows, tk = kv_len, ATTN_KV_CHUNK
        kv_idx = lambda b, h, i: (b, h)
    else:
        kv_rows, tk = ctx_len, ctx_len
        kv_idx = lambda b, h, i: (b * (kv_len // ctx_len), h)
    assert kv_rows % tk == 0
    rows = B_GROUP * tq
    return pl.pallas_call(
        functools.partial(_attn_kernel, n_chunks=kv_rows // tk, tk=tk, tq=tq),
        grid=(batch, B_KV_HEADS, q_tiles),
        in_specs=[pl.BlockSpec((tq, gw), lambda b, h, i: (q_row0 + b * q_tiles + i, h)),
                  pl.BlockSpec((kv_rows, HEAD_DIM), kv_idx),
                  pl.BlockSpec((kv_rows, HEAD_DIM), kv_idx)],
        out_specs=pl.BlockSpec((tq, gw), lambda b, h, i: (b * q_tiles + i, h)),
        out_shape=jax.ShapeDtypeStruct((batch * q_len, B_Q_W), BF16),
        scratch_shapes=[pltpu.VMEM((rows, LANES), F32), pltpu.VMEM((rows, LANES), F32),
                        pltpu.VMEM((rows, HEAD_DIM), F32)],
        compiler_params=_cparams(3),
        name="attn_lat" if latent else "attn_ctx",
    )(qr, kr, vr)


def _ab_out_kernel(u_ref, v_ref, ga_ref, gb0_ref, gb1_ref, aol_ref, aoc_ref, ws_ref, bs_ref, w_ref, x_ref, gate_ref, o_ref,
                   y_ref, acc_ref, *, n_lat_tiles):
    k = pl.program_id(1)

    @pl.when(k == 0)
    def _():
        tm = u_ref.shape[0]
        for c in range(tm // SGU_CHUNK):
            rows = slice(c * SGU_CHUNK, (c + 1) * SGU_CHUNK)
            for g in range(A_GROUPS):
                cols = slice(g * LANES, (g + 1) * LANES)
                vg = v_ref[rows, cols].astype(F32)
                d = vg - jnp.mean(vg, axis=-1, keepdims=True)
                var = jnp.mean(d * d, axis=-1, keepdims=True)
                vn = (d * lax.rsqrt(var + 1e-5)).astype(BF16)
                mixed = _dot(ws_ref[g], vn) + bs_ref[:, g:g + 1]
                y_ref[rows, cols] = (u_ref[rows, cols].astype(F32) * mixed
                                     * _silu(ga_ref[rows, cols].astype(F32))).astype(BF16)
        acc_ref[...] = _dot(y_ref[...], w_ref[0:A_WIDTH, :])

    @pl.when(k == 1)
    def _():
        ao = jnp.where(pl.program_id(0) < n_lat_tiles, aol_ref[...], aoc_ref[...]).astype(F32)
        gb = jnp.concatenate([gb0_ref[...], gb1_ref[...]], axis=1).astype(F32)
        y = (ao * _silu(gb)).astype(BF16)
        o_ref[...] = x_ref[...] + gate_ref[...] * (acc_ref[...] + _dot(y, w_ref[A_WIDTH:, :]))


def _ab_out(proj, ao_lat, ao_ctx, sgu_w_bf16, sgu_b_t, w_out_bf16, x, mod, batch, seq):
    m = x.shape[0]
    tm = ROW_TILE
    lat_tiles = seq // tm
    n_lat_tiles = batch * lat_tiles
    assert ao_ctx.shape[0] == tm
    gb_blk0 = (3 * A_WIDTH + B_Q_W + 2 * B_KV_W) // (B_Q_W // 2)
    return pl.pallas_call(
        functools.partial(_ab_out_kernel, n_lat_tiles=n_lat_tiles),
        grid=(m // tm, 2),
        in_specs=[pl.BlockSpec((tm, A_WIDTH), lambda i, k: (i, 0)),
                  pl.BlockSpec((tm, A_WIDTH), lambda i, k: (i, 1)),
                  pl.BlockSpec((tm, A_WIDTH), lambda i, k: (i, 2)),
                  pl.BlockSpec((tm, B_Q_W // 2), lambda i, k: (i, gb_blk0)),
                  pl.BlockSpec((tm, B_Q_W // 2), lambda i, k: (i, gb_blk0 + 1)),
                  pl.BlockSpec((tm, B_Q_W), lambda i, k: (jnp.minimum(i, n_lat_tiles - 1), 0)),
                  pl.BlockSpec((tm, B_Q_W), lambda i, k: (0, 0)),
                  pl.BlockSpec((A_GROUPS, SGU_CHUNK, SGU_CHUNK), lambda i, k: (0, 0, 0)),
                  pl.BlockSpec((SGU_CHUNK, A_GROUPS), lambda i, k: (0, 0)),
                  pl.BlockSpec((A_WIDTH + B_Q_W, D_MODEL), lambda i, k: (0, 0), pipeline_mode=pl.Buffered(1)),
                  pl.BlockSpec((tm, D_MODEL), lambda i, k: (i, 0)),
                  pl.BlockSpec((None, 1, D_MODEL), lambda i, k: (_mod_row(i, lat_tiles, batch), 0, 2))],
        out_specs=pl.BlockSpec((tm, D_MODEL), lambda i, k: (i, 0)),
        out_shape=jax.ShapeDtypeStruct((m, D_MODEL), F32),
        scratch_shapes=[pltpu.VMEM((tm, A_WIDTH), BF16), pltpu.VMEM((tm, D_MODEL), F32)],
        compiler_params=_cparams(2),
        name="ab_out",
    )(proj, proj, proj, proj, proj, ao_lat, ao_ctx, sgu_w_bf16, sgu_b_t, w_out_bf16, x, mod)


DN_HALO = 2 * SUBLANES
DN_PROJ_COL_TILE = 1024


def _inproj_dn_kernel(xp_ref, x_ref, xn_ref, nw_ref, mod_ref, w_ref, ws_ref, cw_ref, qkv_ref, z_ref, ba_ref, h_ref,
                      *, lat_tiles, n_lat, q_tiles, qk_tiles, qkv_tiles, ctx_len):
    i = pl.program_id(0)
    j = pl.program_id(1)
    tm = x_ref.shape[0]
    is_lat = i < n_lat
    pos = i % lat_tiles
    first = jnp.logical_or(jnp.logical_not(is_lat), pos == 0)
    last = jnp.logical_or(jnp.logical_not(is_lat), pos == lat_tiles - 1)
    half = CONV_K // 2

    @pl.when(j == 0)
    def _():
        shift = mod_ref[:, 0:D_MODEL]
        scale = mod_ref[:, D_MODEL:2 * D_MODEL]

        def norm_mod(x):
            y = x * lax.rsqrt(jnp.mean(x * x, axis=-1, keepdims=True) + EPS) * nw_ref[...]
            return y * (1.0 + scale) + shift

        h_ref[0:DN_HALO, :] = jnp.where(first, 0.0, norm_mod(xp_ref[...])).astype(BF16)
        h_ref[DN_HALO:DN_HALO + tm, :] = norm_mod(x_ref[...]).astype(BF16)
        h_ref[DN_HALO + tm:, :] = jnp.where(last, 0.0, norm_mod(xn_ref[...])).astype(BF16)
        ba_ref[...] = _dot(h_ref[DN_HALO:DN_HALO + tm, :], ws_ref[...])

    n_ext = h_ref.shape[0]
    tn = w_ref.shape[1]
    pair_w = 2 * HEAD_DIM
    row = lax.broadcasted_iota(jnp.int32, (tm, HEAD_DIM), 0)
    is_ctx = jnp.logical_not(is_lat)

    def conv_silu(ph, sl):
        acc = cw_ref[half:half + 1, sl] * ph[DN_HALO:DN_HALO + tm]
        for t in range(CONV_K):
            off = t - half
            if off == 0:
                continue
            tap = pltpu.roll(ph, (-off) % n_ext, 0)[DN_HALO:DN_HALO + tm]
            crosses = (jnp.logical_and(row >= ctx_len - off, row < ctx_len) if off > 0
                       else jnp.logical_and(row >= ctx_len, row < ctx_len - off))
            tap = jnp.where(jnp.logical_and(is_ctx, crosses), 0.0, tap)
            acc = acc + cw_ref[t:t + 1, sl] * tap
        return _silu(acc)

    def for_each_head(store):
        for c0 in range(0, tn, pair_w):
            p = _dot(h_ref[...], w_ref[:, c0:c0 + pair_w])
            for h0 in range(0, pair_w, HEAD_DIM):
                sl = slice(c0 + h0, c0 + h0 + HEAD_DIM)
                store(sl, conv_silu(p[:, h0:h0 + HEAD_DIM], sl))

    @pl.when(j < qk_tiles)
    def _():
        qk_scale = jnp.where(j < q_tiles, HEAD_DIM ** -0.5, 1.0)

        def store(sl, y):
            qkv_ref[:, sl] = (y * (lax.rsqrt(jnp.sum(y * y, axis=-1, keepdims=True) + EPS) * qk_scale)).astype(qkv_ref.dtype)
        for_each_head(store)

    @pl.when(jnp.logical_and(j >= qk_tiles, j < qkv_tiles))
    def _():
        def store(sl, y):
            qkv_ref[:, sl] = y.astype(qkv_ref.dtype)
        for_each_head(store)

    @pl.when(j >= qkv_tiles)
    def _():
        z_ref[...] = _dot(h_ref[DN_HALO:DN_HALO + tm, :], w_ref[...]).astype(z_ref.dtype)


def _inproj_dn(x, norm_w, mod, w_bf16, conv_w, batch, seq, ctx_len):
    m = x.shape[0]
    tm, tn = ROW_TILE, DN_PROJ_COL_TILE
    n_side = 4 * DN_V_HEADS
    n = w_bf16.shape[1] - n_side
    assert n == DN_QKV_W + DN_V_W and tm == 2 * ctx_len
    lat_tiles = seq // tm
    n_lat = batch * lat_tiles
    halo_per_tile = tm // DN_HALO
    n_halo = m // DN_HALO
    qkv_tiles = DN_QKV_W // tn
    kern = functools.partial(_inproj_dn_kernel, lat_tiles=lat_tiles, n_lat=n_lat, q_tiles=DN_K_W // tn,
                             qk_tiles=2 * DN_K_W // tn, qkv_tiles=qkv_tiles, ctx_len=ctx_len)
    return pl.pallas_call(
        kern,
        grid=(m // tm, n // tn),
        in_specs=[pl.BlockSpec((DN_HALO, D_MODEL), lambda i, j: (jnp.maximum(i * halo_per_tile - 1, 0), 0)),
                  pl.BlockSpec((tm, D_MODEL), lambda i, j: (i, 0)),
                  pl.BlockSpec((DN_HALO, D_MODEL), lambda i, j: (jnp.minimum((i + 1) * halo_per_tile, n_halo - 1), 0)),
                  pl.BlockSpec((1, D_MODEL), lambda i, j: (0, 0)),
                  pl.BlockSpec((None, 1, 3 * D_MODEL), lambda i, j: (_mod_row(i, lat_tiles, batch), 0, 0)),
                  pl.BlockSpec((D_MODEL, tn), lambda i, j: (0, j)),
                  pl.BlockSpec((D_MODEL, n_side), lambda i, j: (0, n // n_side)),
                  pl.BlockSpec((SUBLANES, tn), lambda i, j: (0, jnp.minimum(j, qkv_tiles - 1)))],
        out_specs=[pl.BlockSpec((tm, tn), lambda i, j: (i, jnp.minimum(j, qkv_tiles - 1))),
                   pl.BlockSpec((tm, tn), lambda i, j: (i, jnp.maximum(j - qkv_tiles, 0))),
                   pl.BlockSpec((tm, n_side), lambda i, j: (i, 0))],
        out_shape=[jax.ShapeDtypeStruct((m, DN_QKV_W), F32),
                   jax.ShapeDtypeStruct((m, DN_V_W), BF16),
                   jax.ShapeDtypeStruct((m, n_side), F32)],
        scratch_shapes=[pltpu.VMEM((tm + 2 * DN_HALO, D_MODEL), BF16)],
        compiler_params=_cparams(2),
        name="inproj_dn",
    )(x, x, x, norm_w.reshape(1, D_MODEL), mod, w_bf16, w_bf16, conv_w)


def _dngate_kernel(ba_ref, alog_ref, dtb_ref, o_ref):
    ba = ba_ref[...]
    tm = ba.shape[0]
    lane = lax.broadcasted_iota(jnp.int32, (DN_CHUNK, LANES), 1)
    is_beta = (lane // DN_V_HEADS) % 2 == 0
    is_fwd = lane < 2 * DN_V_HEADS
    z = ba + dtb_ref[...]
    softplus = jnp.maximum(z, 0.0) + jnp.log1p(jnp.exp(-jnp.abs(z)))
    g = -jnp.exp(alog_ref[...]) * softplus
    r = lax.broadcasted_iota(jnp.int32, (DN_CHUNK, DN_CHUNK), 0)
    c = lax.broadcasted_iota(jnp.int32, (DN_CHUNK, DN_CHUNK), 1)
    tri_lo = (r >= c).astype(BF16)
    tri_up = (r <= c).astype(BF16)
    beta = jax.nn.sigmoid(ba)
    for ch in range(tm // DN_CHUNK):
        rows = slice(ch * DN_CHUNK, (ch + 1) * DN_CHUNK)
        gch = g[rows]
        g1 = gch.astype(BF16)
        r1 = gch - g1.astype(F32)
        g2 = r1.astype(BF16)
        g3 = (r1 - g2.astype(F32)).astype(BF16)
        pre = _dot(tri_lo, g1) + _dot(tri_lo, g2) + _dot(tri_lo, g3)
        suf = _dot(tri_up, g1) + _dot(tri_up, g2) + _dot(tri_up, g3)
        gc = jnp.where(is_fwd, pre, suf)
        o_ref[rows, :] = jnp.where(is_beta, beta[rows], gc)


def _dn_gate(ba, alog_vec, dtb_vec):
    m = ba.shape[0]
    tm = ROW_TILE
    return pl.pallas_call(
        _dngate_kernel,
        grid=(m // tm,),
        in_specs=[pl.BlockSpec((tm, LANES), lambda i: (i, 0)),
                  pl.BlockSpec((1, LANES), lambda i: (0, 0)),
                  pl.BlockSpec((1, LANES), lambda i: (0, 0))],
        out_specs=pl.BlockSpec((tm, LANES), lambda i: (i, 0)),
        out_shape=jax.ShapeDtypeStruct((m, LANES), F32),
        compiler_params=_cparams(1),
        name="dn_gate",
    )(ba, alog_vec, dtb_vec)


DN_CHAIN_GROUP = 16
DN_KH_PER_STEP = 16


def _dncore_kernel(qf_ref, kf_ref, vf_ref, gf_ref, qb_ref, kb_ref, vb_ref, gb_ref, of_ref, ob_ref, s_ref):
    C = DN_CHUNK
    nvh = 2 * DN_KH_PER_STEP

    @pl.when(pl.program_id(2) == 0)
    def _():
        s_ref[...] = jnp.zeros(s_ref.shape, F32)

    lane = lax.broadcasted_iota(jnp.int32, (C, 2 * C), 1)
    left = lane < C
    row = lax.broadcasted_iota(jnp.int32, (C, 2 * C), 0)
    colp = lane % C
    left_sq = lax.broadcasted_iota(jnp.int32, (2 * C, 2 * C), 1) < C
    eye2 = (row == colp).astype(F32)

    def blockdiag(p):
        z = jnp.zeros_like(p)
        return jnp.concatenate([jnp.where(left, p, z), jnp.where(left, z, p)], axis=0)

    def packed_mm(a, b):
        return _dot(a.astype(BF16), blockdiag(b.astype(BF16)))

    dirs = ((qf_ref, kf_ref, vf_ref, gf_ref, of_ref), (qb_ref, kb_ref, vb_ref, gb_ref, ob_ref))
    chains = [(d, kh) for d in range(2) for kh in range(DN_KH_PER_STEP)]
    G = [dirs[d][3][...] for d in range(2)]
    GT = [jnp.concatenate([g, g], axis=0).T for g in G]

    for g0 in range(0, len(chains), DN_CHAIN_GROUP):
        group = chains[g0:g0 + DN_CHAIN_GROUP]
        st = []
        for d, kh in group:
            q_ref, k_ref, v_ref, _, _ = dirs[d]
            base_beta = d * 2 * nvh
            base_gc = base_beta + nvh
            lv0 = 2 * kh
            q = q_ref[:, kh * HEAD_DIM:(kh + 1) * HEAD_DIM]
            k = k_ref[:, kh * HEAD_DIM:(kh + 1) * HEAD_DIM]
            k2 = jnp.concatenate([k, k], axis=0)
            kT2 = k2.T
            gram = _dot(jnp.concatenate([q, k], axis=0).astype(BF16), kT2.astype(BF16))
            b0, b1 = G[d][:, base_beta + lv0:base_beta + lv0 + 1], G[d][:, base_beta + lv0 + 1:base_beta + lv0 + 2]
            c0, c1 = G[d][:, base_gc + lv0:base_gc + lv0 + 1], G[d][:, base_gc + lv0 + 1:base_gc + lv0 + 2]
            r0, r1 = GT[d][base_gc + lv0:base_gc + lv0 + 1, :], GT[d][base_gc + lv0 + 1:base_gc + lv0 + 2, :]
            st.append(dict(d=d, lv0=lv0, q=q, k2=k2, kT2=kT2, gram=gram, b0=b0, b1=b1, c0=c0, c1=c1, r0=r0, r1=r1))

        for s in st:
            d = s["d"]
            incl = (row >= colp) if d == 0 else (row <= colp)
            strict = (row > colp) if d == 0 else (row < colp)
            gcol_p = jnp.where(left, s["c0"], s["c1"])
            grow_p = jnp.where(left[0:1], s["r0"], s["r1"])
            beta_p = jnp.where(left, s["b0"], s["b1"])
            dec = jnp.exp(jnp.where(incl, gcol_p - grow_p, -1e30))
            s["dec"] = dec
            s["attn"] = s["gram"][0:C] * dec
            s["L"] = jnp.where(strict, s["gram"][C:2 * C] * dec, 0.0) * beta_p

        for s in st:
            n1 = jnp.where(jnp.logical_and(row // 2 == colp // 2, row != colp), s["L"], 0.0)
            s["X"] = eye2 - n1
        blk = 2
        while blk < C:
            mask = jnp.logical_and(row // (2 * blk) == colp // (2 * blk), row // blk != colp // blk)
            for s in st:
                s["Y"] = packed_mm(s["X"], jnp.where(mask, s["L"], 0.0))
            for s in st:
                s["X"] = s["X"] - packed_mm(s["Y"], s["X"])
            blk *= 2

        for s in st:
            d, lv0 = s["d"], s["lv0"]
            v_ref = dirs[d][2]
            beta_r = jnp.concatenate([s["b0"], s["b1"]], axis=0)
            egc_r = jnp.exp(jnp.concatenate([s["c0"], s["c1"]], axis=0))
            v2 = jnp.concatenate([v_ref[:, lv0 * HEAD_DIM:(lv0 + 1) * HEAD_DIM],
                                  v_ref[:, (lv0 + 1) * HEAD_DIM:(lv0 + 2) * HEAD_DIM]], axis=0)
            rhs = jnp.concatenate([v2 * beta_r, s["k2"] * (beta_r * egc_r)], axis=1)
            s["sol"] = _dot(blockdiag(s["X"]).astype(BF16), rhs.astype(BF16))
            s["egc"] = egc_r
            s["qg2"] = jnp.concatenate([s["q"], s["q"]], axis=0) * egc_r

        for s in st:
            d, lv0 = s["d"], s["lv0"]
            w2 = s["sol"][:, HEAD_DIM:]
            s["ws"] = []
            for r in range(2):
                lhs = jnp.concatenate([w2[r * C:(r + 1) * C], s["qg2"][r * C:(r + 1) * C]], axis=0).astype(BF16)
                s["ws"].append(_dot(lhs, s_ref[d, lv0 + r].astype(BF16)))

        for s in st:
            u2 = s["sol"][:, 0:HEAD_DIM]
            vn2 = jnp.concatenate([u2[r * C:(r + 1) * C] - s["ws"][r][0:C] for r in range(2)], axis=0).astype(BF16)
            s["vn2"] = vn2
            s["o2"] = jnp.concatenate([s["ws"][r][C:2 * C] for r in range(2)], axis=0) + _dot(
                blockdiag(s["attn"]).astype(BF16), vn2)

        for s in st:
            d, lv0 = s["d"], s["lv0"]
            o_ref = dirs[d][4]
            last = C - 1 if d == 0 else 0
            kdT_p = s["kT2"] * s["dec"][last:last + 1, :]
            zkd = jnp.zeros_like(kdT_p)
            for r in range(2):
                kd_r = jnp.where(left_sq if r == 0 else jnp.logical_not(left_sq), kdT_p, zkd).astype(BF16)
                glr = jnp.broadcast_to(s["egc"][r * C + last:r * C + last + 1, :], (HEAD_DIM, HEAD_DIM))
                s_ref[d, lv0 + r] = s_ref[d, lv0 + r] * glr + _dot(kd_r, s["vn2"])
                o_ref[:, (lv0 + r) * HEAD_DIM:(lv0 + r + 1) * HEAD_DIM] = s["o2"][r * C:(r + 1) * C].astype(o_ref.dtype)


def _dn_core(qkv, gates, batch, seq, ctx_len):
    m = qkv.shape[0]
    C = DN_CHUNK
    n_lat = seq // C
    n_ctx = ctx_len // C
    n_steps = n_ctx + n_lat
    ctx0 = batch * n_lat
    khs = DN_KH_PER_STEP
    qw = khs * HEAD_DIM
    vw = 2 * khs * HEAD_DIM
    k_blk0 = DN_K_W // qw
    v_blk0 = 2 * DN_K_W // vw

    def rf(b, t):
        return jnp.where(t < n_ctx, ctx0 + b * n_ctx + t, b * n_lat + (t - n_ctx))

    def rb(b, t):
        return jnp.where(t < n_ctx, ctx0 + b * n_ctx + (n_ctx - 1 - t), b * n_lat + (n_lat - 1 - (t - n_ctx)))

    def specs(rfun):
        return [pl.BlockSpec((C, qw), lambda b, h, t: (rfun(b, t), h)),
                pl.BlockSpec((C, qw), lambda b, h, t: (rfun(b, t), k_blk0 + h)),
                pl.BlockSpec((C, vw), lambda b, h, t: (rfun(b, t), v_blk0 + h)),
                pl.BlockSpec((C, LANES), lambda b, h, t: (rfun(b, t), h))]

    return pl.pallas_call(
        _dncore_kernel,
        grid=(batch, DN_K_HEADS // khs, n_steps),
        in_specs=specs(rf) + specs(rb),
        out_specs=[pl.BlockSpec((C, vw), lambda b, h, t: (rf(b, t), h)),
                   pl.BlockSpec((C, vw), lambda b, h, t: (rb(b, t), h))],
        out_shape=[jax.ShapeDtypeStruct((m, DN_V_W), BF16), jax.ShapeDtypeStruct((m, DN_V_W), BF16)],
        scratch_shapes=[pltpu.VMEM((2, 2 * khs, HEAD_DIM, HEAD_DIM), F32)],
        compiler_params=_cparams(3),
        name="dn_core",
    )(qkv, qkv, qkv, gates, qkv, qkv, qkv, gates)


DN_OUT_K_TILE = 1024


def _dn_out_kernel(of_ref, ob_ref, z_ref, nw_ref, w_ref, x_ref, gate_ref, *rest):
    o_ref, acc_ref = rest[-2:]
    k = pl.program_id(1)
    ys = []
    for h in range(of_ref.shape[1] // HEAD_DIM):
        sl = slice(h * HEAD_DIM, (h + 1) * HEAD_DIM)
        o = of_ref[:, sl].astype(F32) + ob_ref[:, sl].astype(F32)
        n = o * lax.rsqrt(jnp.mean(o * o, axis=-1, keepdims=True) + EPS) * nw_ref[...]
        ys.append((n * _silu(z_ref[:, sl].astype(F32))).astype(BF16))
    tk = of_ref.shape[1]
    part = _dot(jnp.concatenate(ys, axis=1), w_ref[pl.ds(pl.multiple_of(k * tk, tk), tk), :])

    @pl.when(k == 0)
    def _():
        acc_ref[...] = part

    @pl.when(jnp.logical_and(k > 0, k < pl.num_programs(1) - 1))
    def _():
        acc_ref[...] += part

    @pl.when(k == pl.num_programs(1) - 1)
    def _():
        y = x_ref[...] + gate_ref[...] * (acc_ref[...] + part)
        if len(rest) == 3:
            y = y * lax.rsqrt(jnp.mean(y * y, axis=-1, keepdims=True) + EPS) * rest[0][...]
        o_ref[...] = y


def _dn_out(o_f, o_b, z, norm_w, w_out_bf16, x, mod, batch, seq, final_norm_w=None):
    tm, tk = ROW_TILE, DN_OUT_K_TILE
    m = x.shape[0] if final_norm_w is None else batch * seq
    lat_tiles = seq // tm
    extra_specs, extra_args = [], []
    if final_norm_w is not None:
        extra_specs = [pl.BlockSpec((1, D_MODEL), lambda i, k: (0, 0))]
        extra_args = [final_norm_w.reshape(1, D_MODEL)]
    return pl.pallas_call(
        _dn_out_kernel,
        grid=(m // tm, DN_V_W // tk),
        in_specs=[pl.BlockSpec((tm, tk), lambda i, k: (i, k)),
                  pl.BlockSpec((tm, tk), lambda i, k: (i, k)),
                  pl.BlockSpec((tm, tk), lambda i, k: (i, k)),
                  pl.BlockSpec((1, HEAD_DIM), lambda i, k: (0, 0)),
                  pl.BlockSpec((DN_V_W, D_MODEL), lambda i, k: (0, 0), pipeline_mode=pl.Buffered(1)),
                  pl.BlockSpec((tm, D_MODEL), lambda i, k: (i, 0)),
                  pl.BlockSpec((None, 1, D_MODEL), lambda i, k: (_mod_row(i, lat_tiles, batch), 0, 2))] + extra_specs,
        out_specs=pl.BlockSpec((tm, D_MODEL), lambda i, k: (i, 0)),
        out_shape=jax.ShapeDtypeStruct((m, D_MODEL), F32),
        scratch_shapes=[pltpu.VMEM((tm, D_MODEL), F32)],
        compiler_params=_cparams(2),
        name="dn_out",
    )(o_f, o_b, z, norm_w.reshape(1, HEAD_DIM), w_out_bf16, x, mod, *extra_args)


def _final_norm_kernel(x_ref, w_ref, o_ref):
    x = x_ref[...]
    o_ref[...] = x * lax.rsqrt(jnp.mean(x * x, axis=-1, keepdims=True) + EPS) * w_ref[...]


def _final_norm(x, w, rows):
    tm = ROW_TILE
    return pl.pallas_call(
        _final_norm_kernel,
        grid=(rows // tm,),
        in_specs=[pl.BlockSpec((tm, D_MODEL), lambda i: (i, 0)),
                  pl.BlockSpec((1, D_MODEL), lambda i: (0, 0))],
        out_specs=pl.BlockSpec((tm, D_MODEL), lambda i: (i, 0)),
        out_shape=jax.ShapeDtypeStruct((rows, D_MODEL), F32),
        compiler_params=_cparams(1),
        name="final_norm",
    )(x, w.reshape(1, D_MODEL))


def kernel(x, c, ctx, c_ctx, norm_w, ada_w, ada_b, ab_w_in, ab_w_out, sgu_w, sgu_b, q_norm_w, k_norm_w,
           dn_w_in, dn_conv_w, dn_a_log, dn_dt_bias, dn_norm_w, dn_w_out, final_norm_w):
    batch, seq, _ = x.shape
    ctx_len = ctx.shape[1]
    depth = norm_w.shape[0]
    assert ctx_len == CONV_ROW_TILE and seq % ROW_TILE == 0 and batch * ctx_len == ROW_TILE
    n_lat_rows = batch * seq

    xs = jnp.concatenate([x.reshape(n_lat_rows, D_MODEL), ctx.reshape(batch * ctx_len, D_MODEL)], axis=0)
    cond = jnp.zeros((SUBLANES, D_MODEL), F32).at[0:batch].set(c).at[batch].set(c_ctx)
    mods = _ada_mod(cond, ada_w, ada_b)
    cos_tab, sin_tab = _rope_tables(seq, CONV_ROW_TILE)

    for i in range(depth):
        j = i // 2
        mod = mods[i].reshape(SUBLANES, 1, 3 * D_MODEL)
        if i % 2 == 0:
            proj = _inproj(xs, norm_w[i], mod, ab_w_in[j].astype(BF16), batch, seq)
            qr, kr, vr = _qkprep(proj, cos_tab, sin_tab, q_norm_w[j], k_norm_w[j], batch, seq)
            ao_lat = _attention(qr, kr, vr, batch, seq, ctx_len, True)
            ao_ctx = _attention(qr, kr, vr, batch, seq, ctx_len, False)
            xs = _ab_out(proj, ao_lat, ao_ctx, sgu_w[j].astype(BF16), sgu_b[j].T, ab_w_out[j].astype(BF16), xs, mod,
                         batch, seq)
        else:
            conv_w = jnp.zeros((SUBLANES, DN_QKV_W), F32).at[0:CONV_K].set(dn_conv_w[j])
            qkv, z, ba = _inproj_dn(xs, norm_w[i], mod, dn_w_in[j].astype(BF16), conv_w, batch, seq, ctx_len)
            zeros = jnp.zeros((2, DN_V_HEADS), F32)
            alog_vec = jnp.concatenate([zeros, dn_a_log[j]], axis=1).reshape(1, LANES)
            dtb_vec = jnp.concatenate([zeros, dn_dt_bias[j]], axis=1).reshape(1, LANES)
            gates = _dn_gate(ba, alog_vec, dtb_vec)
            n_hg = DN_K_HEADS // DN_KH_PER_STEP
            nvh = 2 * DN_KH_PER_STEP
            g4 = gates.reshape(-1, 4, n_hg, nvh).transpose(0, 2, 1, 3).reshape(-1, n_hg, 4 * nvh)
            g4 = jnp.pad(g4, ((0, 0), (0, 0), (0, LANES - 4 * nvh))).reshape(-1, n_hg * LANES)
            o_f, o_b = _dn_core(qkv, g4, batch, seq, ctx_len)
            xs = _dn_out(o_f, o_b, z, dn_norm_w[j], dn_w_out[j].astype(BF16), xs, mod, batch, seq,
                         final_norm_w=final_norm_w if i == depth - 1 else None)

    if depth % 2 == 1:
        xs = _final_norm(xs, final_norm_w, n_lat_rows)
    return xs.reshape(batch, seq, D_MODEL)
```

```python
import functools

import jax
import jax.numpy as jnp
from jax import lax
from jax.experimental import pallas as pl
from jax.experimental.pallas import tpu as pltpu

F32 = jnp.float32
BF16 = jnp.bfloat16

D_MODEL = 2048
GRID_W = 64
EPS = 1e-6
HEAD_DIM = 128
A_WIDTH = D_MODEL // 2
A_GROUPS = A_WIDTH // 128
SGU_CHUNK = 128
B_HEADS = (D_MODEL // 2) // HEAD_DIM
B_KV_HEADS = B_HEADS // 4
B_GROUP = B_HEADS // B_KV_HEADS
B_Q_W = B_HEADS * HEAD_DIM
B_KV_W = B_KV_HEADS * HEAD_DIM
ROPE_THETA = 10000.0
AXIS_DIM = HEAD_DIM // 2
AB_IN_W = 3 * A_WIDTH + 2 * B_Q_W + 2 * B_KV_W
DN_K_HEADS = D_MODEL // HEAD_DIM
DN_V_HEADS = 2 * DN_K_HEADS
DN_K_W = DN_K_HEADS * HEAD_DIM
DN_V_W = DN_V_HEADS * HEAD_DIM
DN_QKV_W = 2 * DN_K_W + DN_V_W
DN_IN_W = DN_QKV_W + DN_V_W + 4 * DN_V_HEADS
DN_CHUNK = 64
CONV_K = 5

V7X_VMEM_LIMIT_BYTES = 56 * 1024 * 1024
LANES = 128
SUBLANES = 8

ROW_TILE = 512
CONV_ROW_TILE = 256


def _cparams(n_axes):
    return pltpu.CompilerParams(dimension_semantics=("arbitrary",) * n_axes,
                                vmem_limit_bytes=V7X_VMEM_LIMIT_BYTES)


def _silu(x):
    return x * jax.nn.sigmoid(x)


def _split_bf16(a):
    hi = a.astype(BF16)
    lo = (a - hi.astype(F32)).astype(BF16)
    return hi, lo


def _dot(a, b):
    return jnp.dot(a, b, preferred_element_type=F32)


def _dot_nt(a, b):
    return lax.dot_general(a, b, (((1,), (1,)), ((), ())), preferred_element_type=F32)


def _dot3(a, b):
    ah, al = _split_bf16(a)
    bh, bl = _split_bf16(b)
    return _dot(ah, bh) + _dot(ah, bl) + _dot(al, bh)


def _ada_kernel(c_ref, w_ref, b_ref, o_ref):
    s = _silu(c_ref[...])
    o_ref[0] = _dot3(s, w_ref[0]) + b_ref[0]


def _ada_mod(cond, ada_w, ada_b):
    depth = ada_w.shape[0]
    tn = 512
    return pl.pallas_call(
        _ada_kernel,
        grid=(depth, 3 * D_MODEL // tn),
        in_specs=[pl.BlockSpec((SUBLANES, D_MODEL), lambda l, j: (0, 0)),
                  pl.BlockSpec((1, D_MODEL, tn), lambda l, j: (l, 0, j)),
                  pl.BlockSpec((1, 1, tn), lambda l, j: (l, 0, j))],
        out_specs=pl.BlockSpec((1, SUBLANES, tn), lambda l, j: (l, 0, j)),
        out_shape=jax.ShapeDtypeStruct((depth, SUBLANES, 3 * D_MODEL), F32),
        compiler_params=_cparams(2),
        name="ada_mod",
    )(cond, ada_w, ada_b.reshape(depth, 1, 3 * D_MODEL))


INPROJ_COL_TILES = (512, 1024, 1408, 1536)


def _inproj_kernel(x_ref, nw_ref, mod_ref, w_ref, *rest):
    h_ref = rest[-1]
    has_side = len(rest) == 4

    @pl.when(pl.program_id(1) == 0)
    def _():
        x = x_ref[...]
        y = x * lax.rsqrt(jnp.mean(x * x, axis=-1, keepdims=True) + EPS) * nw_ref[...]
        shift = mod_ref[:, 0:D_MODEL]
        scale = mod_ref[:, D_MODEL:2 * D_MODEL]
        h_ref[...] = (y * (1.0 + scale) + shift).astype(BF16)
        if has_side:
            rest[2][...] = _dot(h_ref[...], rest[0][...])

    o_ref = rest[1] if has_side else rest[0]
    o_ref[...] = _dot(h_ref[...], w_ref[...]).astype(o_ref.dtype)


def _mod_row(i, lat_tiles, batch):
    return jnp.minimum(i // lat_tiles, batch)


def _inproj(x, norm_w, mod, w_bf16, batch, seq, n_side=0, out_dtype=BF16):
    m = x.shape[0]
    n = w_bf16.shape[1] - n_side
    tm = ROW_TILE
    tn = max(t for t in INPROJ_COL_TILES if n % t == 0)
    lat_tiles = seq // tm
    in_specs = [pl.BlockSpec((tm, D_MODEL), lambda i, j: (i, 0)),
                pl.BlockSpec((1, D_MODEL), lambda i, j: (0, 0)),
                pl.BlockSpec((None, 1, 3 * D_MODEL), lambda i, j: (_mod_row(i, lat_tiles, batch), 0, 0)),
                pl.BlockSpec((D_MODEL, tn), lambda i, j: (0, j))]
    out_specs = [pl.BlockSpec((tm, tn), lambda i, j: (i, j))]
    out_shape = [jax.ShapeDtypeStruct((m, n), out_dtype)]
    args = [x, norm_w.reshape(1, D_MODEL), mod, w_bf16]
    if n_side:
        assert n % n_side == 0
        in_specs.append(pl.BlockSpec((D_MODEL, n_side), lambda i, j: (0, n // n_side)))
        out_specs.append(pl.BlockSpec((tm, n_side), lambda i, j: (i, 0)))
        out_shape.append(jax.ShapeDtypeStruct((m, n_side), F32))
        args.append(w_bf16)
    res = pl.pallas_call(
        _inproj_kernel,
        grid=(m // tm, n // tn),
        in_specs=in_specs,
        out_specs=out_specs,
        out_shape=out_shape,
        scratch_shapes=[pltpu.VMEM((tm, D_MODEL), BF16)],
        compiler_params=_cparams(2),
        name="inproj",
    )(*args)
    return res if n_side else res[0]


Q_PRESCALE = (HEAD_DIM ** -0.5) * 1.4426950408889634


def _qkprep_kernel(q_ref, k_ref, v_ref, cos_ref, sin_ref, qw_ref, kw_ref, qo_ref, ko_ref, vo_ref):
    cos = cos_ref[...]
    sin = sin_ref[...]
    lane = lax.broadcasted_iota(jnp.int32, cos.shape, 1)
    first = (lane % (AXIS_DIM)) < (AXIS_DIM // 2)

    def prep(x, w):
        y = x * lax.rsqrt(jnp.mean(x * x, axis=-1, keepdims=True) + EPS) * w
        rot = jnp.where(first, pltpu.roll(y, HEAD_DIM - AXIS_DIM // 2, 1), pltpu.roll(y, AXIS_DIM // 2, 1))
        return y * cos + rot * sin

    for h in range(B_HEADS):
        sl = slice(h * HEAD_DIM, (h + 1) * HEAD_DIM)
        qo_ref[:, sl] = (prep(q_ref[:, sl].astype(F32), qw_ref[...]) * Q_PRESCALE).astype(qo_ref.dtype)
    for h in range(B_KV_HEADS):
        sl = slice(h * HEAD_DIM, (h + 1) * HEAD_DIM)
        ko_ref[:, sl] = prep(k_ref[:, sl].astype(F32), kw_ref[...]).astype(ko_ref.dtype)
    vo_ref[...] = v_ref[...].astype(vo_ref.dtype)


def _qkprep(proj, cos_tab, sin_tab, qn_w, kn_w, batch, seq):
    m = proj.shape[0]
    tm = CONV_ROW_TILE
    lat_tiles = seq // tm
    n_lat = batch * lat_tiles
    q_blk = (3 * A_WIDTH) // B_Q_W
    k_blk = (3 * A_WIDTH + B_Q_W) // B_KV_W

    def tab_idx(i):
        return (jnp.where(i < n_lat, i % lat_tiles, lat_tiles), 0)

    def kv_idx(i):
        lat_blk = (i // lat_tiles) * (lat_tiles + 1) + 1 + i % lat_tiles
        return (jnp.where(i < n_lat, lat_blk, (i - n_lat) * (lat_tiles + 1)), 0)

    return pl.pallas_call(
        _qkprep_kernel,
        grid=(m // tm,),
        in_specs=[pl.BlockSpec((tm, B_Q_W), lambda i: (i, q_blk)),
                  pl.BlockSpec((tm, B_KV_W), lambda i: (i, k_blk)),
                  pl.BlockSpec((tm, B_KV_W), lambda i: (i, k_blk + 1)),
                  pl.BlockSpec((tm, HEAD_DIM), tab_idx),
                  pl.BlockSpec((tm, HEAD_DIM), tab_idx),
                  pl.BlockSpec((1, HEAD_DIM), lambda i: (0, 0)),
                  pl.BlockSpec((1, HEAD_DIM), lambda i: (0, 0))],
        out_specs=[pl.BlockSpec((tm, B_Q_W), lambda i: (i, 0)),
                   pl.BlockSpec((tm, B_KV_W), kv_idx),
                   pl.BlockSpec((tm, B_KV_W), kv_idx)],
        out_shape=[jax.ShapeDtypeStruct((m, B_Q_W), BF16),
                   jax.ShapeDtypeStruct((m, B_KV_W), BF16),
                   jax.ShapeDtypeStruct((m, B_KV_W), BF16)],
        compiler_params=_cparams(1),
        name="qk_prep",
    )(proj, proj, proj, cos_tab, sin_tab, qn_w.reshape(1, HEAD_DIM), kn_w.reshape(1, HEAD_DIM))


def _rope_tables(seq, tm):
    rows = seq // GRID_W
    row = jnp.repeat(jnp.arange(rows), GRID_W).astype(F32)
    col = jnp.tile(jnp.arange(GRID_W), rows).astype(F32)
    freqs = ROPE_THETA ** (-jnp.arange(0, AXIS_DIM, 2, dtype=F32) / AXIS_DIM)
    ang_r = row[:, None] * freqs[None, :]
    ang_c = col[:, None] * freqs[None, :]
    ang = jnp.concatenate([ang_r, ang_r, ang_c, ang_c], axis=-1)
    sign = jnp.where((jnp.arange(HEAD_DIM) % AXIS_DIM) < AXIS_DIM // 2, -1.0, 1.0).astype(F32)
    cos = jnp.concatenate([jnp.cos(ang), jnp.ones((tm, HEAD_DIM), F32)], axis=0)
    sin = jnp.concatenate([jnp.sin(ang) * sign[None, :], jnp.zeros((tm, HEAD_DIM), F32)], axis=0)
    return cos, sin


ATTN_Q_TILE = 256
ATTN_KV_BODY = 4096
ATTN_KV_TAIL = 768


def _kv_chunks(kv_len):
    tail = min(ATTN_KV_TAIL, kv_len)
    body = kv_len - tail
    sizes = [ATTN_KV_BODY] * (body // ATTN_KV_BODY) + ([body % ATTN_KV_BODY] if body % ATTN_KV_BODY else []) + [tail]
    assert all(c % LANES == 0 for c in sizes) and sum(sizes) == kv_len
    return tuple(sizes)


def _attn_kernel(q_ref, k_ref, v_ref, o_ref, m_scr, l_scr, acc_scr, *, chunks, tq):
    q = jnp.concatenate([q_ref[:, g * HEAD_DIM:(g + 1) * HEAD_DIM] for g in range(B_GROUP)], axis=0)
    m_scr[...] = jnp.full(m_scr.shape, -jnp.inf, F32)
    l_scr[...] = jnp.zeros(l_scr.shape, F32)
    acc_scr[...] = jnp.zeros(acc_scr.shape, F32)
    start = 0
    for tk in chunks:
        k = k_ref[start:start + tk, :]
        v = v_ref[start:start + tk, :]
        start += tk
        s = _dot_nt(q, k)
        m_prev = m_scr[...]
        m_next = jnp.maximum(m_prev, jnp.max(s, axis=1, keepdims=True))
        p = jnp.exp2(s - jnp.concatenate([m_next] * (tk // LANES), axis=1))
        alpha = jnp.exp2(m_prev - m_next)
        l_scr[...] = alpha * l_scr[...] + jnp.sum(p, axis=1, keepdims=True)
        acc_scr[...] = acc_scr[...] * alpha + _dot(p.astype(BF16), v)
        m_scr[...] = m_next
    out = acc_scr[...] / l_scr[...]
    for g in range(B_GROUP):
        o_ref[:, g * HEAD_DIM:(g + 1) * HEAD_DIM] = out[g * tq:(g + 1) * tq].astype(o_ref.dtype)


def _attention(qr, kr, vr, batch, seq, ctx_len, latent):
    tq = ATTN_Q_TILE
    gw = B_GROUP * HEAD_DIM
    q_len = seq if latent else ctx_len
    q_tiles = q_len // tq
    q_row0 = 0 if latent else batch * seq // tq
    kv_len = ctx_len + seq
    if latent:
        kv_rows = kv_len
        kv_idx = lambda b, h, i: (b, h)
    else:
        kv_rows = ctx_len
        kv_idx = lambda b, h, i: (b * (kv_len // ctx_len), h)
    rows = B_GROUP * tq
    return pl.pallas_call(
        functools.partial(_attn_kernel, chunks=_kv_chunks(kv_rows), tq=tq),
        grid=(batch, B_KV_HEADS, q_tiles),
        in_specs=[pl.BlockSpec((tq, gw), lambda b, h, i: (q_row0 + b * q_tiles + i, h)),
                  pl.BlockSpec((kv_rows, HEAD_DIM), kv_idx),
                  pl.BlockSpec((kv_rows, HEAD_DIM), kv_idx)],
        out_specs=pl.BlockSpec((tq, gw), lambda b, h, i: (b * q_tiles + i, h)),
        out_shape=jax.ShapeDtypeStruct((batch * q_len, B_Q_W), BF16),
        scratch_shapes=[pltpu.VMEM((rows, LANES), F32), pltpu.VMEM((rows, LANES), F32),
                        pltpu.VMEM((rows, HEAD_DIM), F32)],
        compiler_params=_cparams(3),
        name="attn_lat" if latent else "attn_ctx",
    )(qr, kr, vr)


def _ab_out_kernel(u_ref, v_ref, ga_ref, gb0_ref, gb1_ref, aol_ref, aoc_ref, ws_ref, bs_ref, w_ref, x_ref, gate_ref, o_ref,
                   y_ref, acc_ref, *, n_lat_tiles):
    k = pl.program_id(1)

    @pl.when(k == 0)
    def _():
        tm = u_ref.shape[0]
        for c in range(tm // SGU_CHUNK):
            rows = slice(c * SGU_CHUNK, (c + 1) * SGU_CHUNK)
            for g in range(A_GROUPS):
                cols = slice(g * LANES, (g + 1) * LANES)
                vg = v_ref[rows, cols].astype(F32)
                d = vg - jnp.mean(vg, axis=-1, keepdims=True)
                var = jnp.mean(d * d, axis=-1, keepdims=True)
                vn = (d * lax.rsqrt(var + 1e-5)).astype(BF16)
                mixed = _dot(ws_ref[g], vn) + bs_ref[:, g:g + 1]
                y_ref[rows, cols] = (u_ref[rows, cols].astype(F32) * mixed
                                     * _silu(ga_ref[rows, cols].astype(F32))).astype(BF16)
        acc_ref[...] = _dot(y_ref[...], w_ref[0:A_WIDTH, :])

    @pl.when(k == 1)
    def _():
        ao = jnp.where(pl.program_id(0) < n_lat_tiles, aol_ref[...], aoc_ref[...]).astype(F32)
        gb = jnp.concatenate([gb0_ref[...], gb1_ref[...]], axis=1).astype(F32)
        y = (ao * _silu(gb)).astype(BF16)
        o_ref[...] = x_ref[...] + gate_ref[...] * (acc_ref[...] + _dot(y, w_ref[A_WIDTH:, :]))


def _ab_out(proj, ao_lat, ao_ctx, sgu_w_bf16, sgu_b_t, w_out_bf16, x, mod, batch, seq):
    m = x.shape[0]
    tm = ROW_TILE
    lat_tiles = seq // tm
    n_lat_tiles = batch * lat_tiles
    assert ao_ctx.shape[0] == tm
    gb_blk0 = (3 * A_WIDTH + B_Q_W + 2 * B_KV_W) // (B_Q_W // 2)
    return pl.pallas_call(
        functools.partial(_ab_out_kernel, n_lat_tiles=n_lat_tiles),
        grid=(m // tm, 2),
        in_specs=[pl.BlockSpec((tm, A_WIDTH), lambda i, k: (i, 0)),
                  pl.BlockSpec((tm, A_WIDTH), lambda i, k: (i, 1)),
                  pl.BlockSpec((tm, A_WIDTH), lambda i, k: (i, 2)),
                  pl.BlockSpec((tm, B_Q_W // 2), lambda i, k: (i, gb_blk0)),
                  pl.BlockSpec((tm, B_Q_W // 2), lambda i, k: (i, gb_blk0 + 1)),
                  pl.BlockSpec((tm, B_Q_W), lambda i, k: (jnp.minimum(i, n_lat_tiles - 1), 0)),
                  pl.BlockSpec((tm, B_Q_W), lambda i, k: (0, 0)),
                  pl.BlockSpec((A_GROUPS, SGU_CHUNK, SGU_CHUNK), lambda i, k: (0, 0, 0)),
                  pl.BlockSpec((SGU_CHUNK, A_GROUPS), lambda i, k: (0, 0)),
                  pl.BlockSpec((A_WIDTH + B_Q_W, D_MODEL), lambda i, k: (0, 0), pipeline_mode=pl.Buffered(1)),
                  pl.BlockSpec((tm, D_MODEL), lambda i, k: (i, 0)),
                  pl.BlockSpec((None, 1, D_MODEL), lambda i, k: (_mod_row(i, lat_tiles, batch), 0, 2))],
        out_specs=pl.BlockSpec((tm, D_MODEL), lambda i, k: (i, 0)),
        out_shape=jax.ShapeDtypeStruct((m, D_MODEL), F32),
        scratch_shapes=[pltpu.VMEM((tm, A_WIDTH), BF16), pltpu.VMEM((tm, D_MODEL), F32)],
        compiler_params=_cparams(2),
        name="ab_out",
    )(proj, proj, proj, proj, proj, ao_lat, ao_ctx, sgu_w_bf16, sgu_b_t, w_out_bf16, x, mod)


DN_HALO = 2 * SUBLANES
DN_PROJ_COL_TILE = 1024


def _inproj_dn_kernel(xp_ref, x_ref, xn_ref, nw_ref, mod_ref, w_ref, ws_ref, cw_ref, qkv_ref, z_ref, ba_ref, h_ref, p_ref,
                      *, lat_tiles, n_lat, q_tiles, qk_tiles, qkv_tiles, ctx_len):
    i = pl.program_id(0)
    j = pl.program_id(1)
    tm = x_ref.shape[0]
    is_lat = i < n_lat
    pos = i % lat_tiles
    first = jnp.logical_or(jnp.logical_not(is_lat), pos == 0)
    last = jnp.logical_or(jnp.logical_not(is_lat), pos == lat_tiles - 1)
    half = CONV_K // 2

    @pl.when(j == 0)
    def _():
        shift = mod_ref[:, 0:D_MODEL]
        scale = mod_ref[:, D_MODEL:2 * D_MODEL]

        def norm_mod(x):
            y = x * lax.rsqrt(jnp.mean(x * x, axis=-1, keepdims=True) + EPS) * nw_ref[...]
            return y * (1.0 + scale) + shift

        h_ref[0:DN_HALO, :] = jnp.where(first, 0.0, norm_mod(xp_ref[...])).astype(BF16)
        h_ref[DN_HALO:DN_HALO + tm, :] = norm_mod(x_ref[...]).astype(BF16)
        h_ref[DN_HALO + tm:, :] = jnp.where(last, 0.0, norm_mod(xn_ref[...])).astype(BF16)
        ba_ref[...] = _dot(h_ref[DN_HALO:DN_HALO + tm, :], ws_ref[...])

    n_ext = h_ref.shape[0]
    tn = w_ref.shape[1]
    pair_w = 2 * HEAD_DIM
    row = lax.broadcasted_iota(jnp.int32, (tm, HEAD_DIM), 0)
    is_ctx = jnp.logical_not(is_lat)

    def conv_silu(ph, sl):
        acc = cw_ref[half:half + 1, sl] * ph[DN_HALO:DN_HALO + tm]
        for t in range(CONV_K):
            off = t - half
            if off == 0:
                continue
            tap = pltpu.roll(ph, (-off) % n_ext, 0)[DN_HALO:DN_HALO + tm]
            crosses = (jnp.logical_and(row >= ctx_len - off, row < ctx_len) if off > 0
                       else jnp.logical_and(row >= ctx_len, row < ctx_len - off))
            tap = jnp.where(jnp.logical_and(is_ctx, crosses), 0.0, tap)
            acc = acc + cw_ref[t:t + 1, sl] * tap
        return _silu(acc)

    def for_each_head(store):
        for n, c0 in enumerate(range(0, tn, pair_w)):
            p_ref[n % 2] = _dot(h_ref[...], w_ref[:, c0:c0 + pair_w])
            for h0 in range(0, pair_w, HEAD_DIM):
                sl = slice(c0 + h0, c0 + h0 + HEAD_DIM)
                store(sl, conv_silu(p_ref[n % 2, :, h0:h0 + HEAD_DIM], sl))

    @pl.when(j < qk_tiles)
    def _():
        qk_scale = jnp.where(j < q_tiles, HEAD_DIM ** -0.5, 1.0)

        def store(sl, y):
            qkv_ref[:, sl] = (y * (lax.rsqrt(jnp.sum(y * y, axis=-1, keepdims=True) + EPS) * qk_scale)).astype(qkv_ref.dtype)
        for_each_head(store)

    @pl.when(jnp.logical_and(j >= qk_tiles, j < qkv_tiles))
    def _():
        def store(sl, y):
            qkv_ref[:, sl] = y.astype(qkv_ref.dtype)
        for_each_head(store)

    @pl.when(j >= qkv_tiles)
    def _():
        z_ref[...] = _dot(h_ref[DN_HALO:DN_HALO + tm, :], w_ref[...]).astype(z_ref.dtype)


def _inproj_dn(x, norm_w, mod, w_bf16, conv_w, batch, seq, ctx_len):
    m = x.shape[0]
    tm, tn = ROW_TILE, DN_PROJ_COL_TILE
    n_side = 4 * DN_V_HEADS
    n = w_bf16.shape[1] - n_side
    assert n == DN_QKV_W + DN_V_W and tm == 2 * ctx_len
    lat_tiles = seq // tm
    n_lat = batch * lat_tiles
    halo_per_tile = tm // DN_HALO
    n_halo = m // DN_HALO
    qkv_tiles = DN_QKV_W // tn
    kern = functools.partial(_inproj_dn_kernel, lat_tiles=lat_tiles, n_lat=n_lat, q_tiles=DN_K_W // tn,
                             qk_tiles=2 * DN_K_W // tn, qkv_tiles=qkv_tiles, ctx_len=ctx_len)
    return pl.pallas_call(
        kern,
        grid=(m // tm, n // tn),
        in_specs=[pl.BlockSpec((DN_HALO, D_MODEL), lambda i, j: (jnp.maximum(i * halo_per_tile - 1, 0), 0)),
                  pl.BlockSpec((tm, D_MODEL), lambda i, j: (i, 0)),
                  pl.BlockSpec((DN_HALO, D_MODEL), lambda i, j: (jnp.minimum((i + 1) * halo_per_tile, n_halo - 1), 0)),
                  pl.BlockSpec((1, D_MODEL), lambda i, j: (0, 0)),
                  pl.BlockSpec((None, 1, 3 * D_MODEL), lambda i, j: (_mod_row(i, lat_tiles, batch), 0, 0)),
                  pl.BlockSpec((D_MODEL, tn), lambda i, j: (0, j)),
                  pl.BlockSpec((D_MODEL, n_side), lambda i, j: (0, n // n_side)),
                  pl.BlockSpec((SUBLANES, tn), lambda i, j: (0, jnp.minimum(j, qkv_tiles - 1)))],
        out_specs=[pl.BlockSpec((tm, tn), lambda i, j: (i, jnp.minimum(j, qkv_tiles - 1))),
                   pl.BlockSpec((tm, tn), lambda i, j: (i, jnp.maximum(j - qkv_tiles, 0))),
                   pl.BlockSpec((tm, n_side), lambda i, j: (i, 0))],
        out_shape=[jax.ShapeDtypeStruct((m, DN_QKV_W), F32),
                   jax.ShapeDtypeStruct((m, DN_V_W), BF16),
                   jax.ShapeDtypeStruct((m, n_side), F32)],
        scratch_shapes=[pltpu.VMEM((tm + 2 * DN_HALO, D_MODEL), BF16),
                        pltpu.VMEM((2, tm + 2 * DN_HALO, 2 * HEAD_DIM), F32)],
        compiler_params=_cparams(2),
        name="inproj_dn",
    )(x, x, x, norm_w.reshape(1, D_MODEL), mod, w_bf16, w_bf16, conv_w)


def _dngate_kernel(ba_ref, alog_ref, dtb_ref, o_ref):
    ba = ba_ref[...]
    tm = ba.shape[0]
    lane = lax.broadcasted_iota(jnp.int32, (DN_CHUNK, LANES), 1)
    is_beta = (lane // DN_V_HEADS) % 2 == 0
    is_fwd = lane < 2 * DN_V_HEADS
    z = ba + dtb_ref[...]
    softplus = jnp.maximum(z, 0.0) + jnp.log1p(jnp.exp(-jnp.abs(z)))
    g = -jnp.exp(alog_ref[...]) * softplus
    r = lax.broadcasted_iota(jnp.int32, (DN_CHUNK, DN_CHUNK), 0)
    c = lax.broadcasted_iota(jnp.int32, (DN_CHUNK, DN_CHUNK), 1)
    tri_lo = (r >= c).astype(BF16)
    tri_up = (r <= c).astype(BF16)
    beta = jax.nn.sigmoid(ba)
    for ch in range(tm // DN_CHUNK):
        rows = slice(ch * DN_CHUNK, (ch + 1) * DN_CHUNK)
        gch = g[rows]
        g1 = gch.astype(BF16)
        r1 = gch - g1.astype(F32)
        g2 = r1.astype(BF16)
        g3 = (r1 - g2.astype(F32)).astype(BF16)
        pre = _dot(tri_lo, g1) + _dot(tri_lo, g2) + _dot(tri_lo, g3)
        suf = _dot(tri_up, g1) + _dot(tri_up, g2) + _dot(tri_up, g3)
        gc = jnp.where(is_fwd, pre, suf)
        o_ref[rows, :] = jnp.where(is_beta, beta[rows], gc)


def _dn_gate(ba, alog_vec, dtb_vec):
    m = ba.shape[0]
    tm = ROW_TILE
    return pl.pallas_call(
        _dngate_kernel,
        grid=(m // tm,),
        in_specs=[pl.BlockSpec((tm, LANES), lambda i: (i, 0)),
                  pl.BlockSpec((1, LANES), lambda i: (0, 0)),
                  pl.BlockSpec((1, LANES), lambda i: (0, 0))],
        out_specs=pl.BlockSpec((tm, LANES), lambda i: (i, 0)),
        out_shape=jax.ShapeDtypeStruct((m, LANES), F32),
        compiler_params=_cparams(1),
        name="dn_gate",
    )(ba, alog_vec, dtb_vec)


DN_CHAIN_GROUP = 16
DN_KH_PER_STEP = 16


def _dncore_kernel(qf_ref, kf_ref, vf_ref, gf_ref, qb_ref, kb_ref, vb_ref, gb_ref, of_ref, ob_ref, s_ref):
    C = DN_CHUNK
    nvh = 2 * DN_KH_PER_STEP

    @pl.when(pl.program_id(2) == 0)
    def _():
        s_ref[...] = jnp.zeros(s_ref.shape, F32)

    lane = lax.broadcasted_iota(jnp.int32, (C, 2 * C), 1)
    left = lane < C
    row = lax.broadcasted_iota(jnp.int32, (C, 2 * C), 0)
    colp = lane % C
    left_sq = lax.broadcasted_iota(jnp.int32, (2 * C, 2 * C), 1) < C
    eye2 = (row == colp).astype(F32)

    def blockdiag(p):
        z = jnp.zeros_like(p)
        return jnp.concatenate([jnp.where(left, p, z), jnp.where(left, z, p)], axis=0)

    def packed_mm(a, b):
        return _dot(a.astype(BF16), blockdiag(b.astype(BF16)))

    dirs = ((qf_ref, kf_ref, vf_ref, gf_ref, of_ref), (qb_ref, kb_ref, vb_ref, gb_ref, ob_ref))
    chains = [(d, kh) for d in range(2) for kh in range(DN_KH_PER_STEP)]
    G = [dirs[d][3][...] for d in range(2)]
    GT = [jnp.concatenate([g, g], axis=0).T for g in G]

    for g0 in range(0, len(chains), DN_CHAIN_GROUP):
        group = chains[g0:g0 + DN_CHAIN_GROUP]
        st = []
        for d, kh in group:
            q_ref, k_ref, v_ref, _, _ = dirs[d]
            base_beta = d * 2 * nvh
            base_gc = base_beta + nvh
            lv0 = 2 * kh
            q = q_ref[:, kh * HEAD_DIM:(kh + 1) * HEAD_DIM]
            k = k_ref[:, kh * HEAD_DIM:(kh + 1) * HEAD_DIM]
            k2 = jnp.concatenate([k, k], axis=0)
            kT2 = k2.T
            gram = _dot(jnp.concatenate([q, k], axis=0).astype(BF16), kT2.astype(BF16))
            b0, b1 = G[d][:, base_beta + lv0:base_beta + lv0 + 1], G[d][:, base_beta + lv0 + 1:base_beta + lv0 + 2]
            c0, c1 = G[d][:, base_gc + lv0:base_gc + lv0 + 1], G[d][:, base_gc + lv0 + 1:base_gc + lv0 + 2]
            r0, r1 = GT[d][base_gc + lv0:base_gc + lv0 + 1, :], GT[d][base_gc + lv0 + 1:base_gc + lv0 + 2, :]
            st.append(dict(d=d, lv0=lv0, q=q, k2=k2, kT2=kT2, gram=gram, b0=b0, b1=b1, c0=c0, c1=c1, r0=r0, r1=r1))

        for s in st:
            d = s["d"]
            incl = (row >= colp) if d == 0 else (row <= colp)
            strict = (row > colp) if d == 0 else (row < colp)
            gcol_p = jnp.where(left, s["c0"], s["c1"])
            grow_p = jnp.where(left[0:1], s["r0"], s["r1"])
            beta_p = jnp.where(left, s["b0"], s["b1"])
            dec = jnp.exp(jnp.where(incl, gcol_p - grow_p, -1e30))
            s["dec"] = dec
            s["attn"] = s["gram"][0:C] * dec
            s["L"] = jnp.where(strict, s["gram"][C:2 * C] * dec, 0.0) * beta_p

        for s in st:
            n1 = jnp.where(jnp.logical_and(row // 2 == colp // 2, row != colp), s["L"], 0.0)
            s["X"] = eye2 - n1
        blk = 2
        while blk < C:
            mask = jnp.logical_and(row // (2 * blk) == colp // (2 * blk), row // blk != colp // blk)
            for s in st:
                s["Y"] = packed_mm(s["X"], jnp.where(mask, s["L"], 0.0))
            for s in st:
                s["X"] = s["X"] - packed_mm(s["Y"], s["X"])
            blk *= 2

        for s in st:
            d, lv0 = s["d"], s["lv0"]
            v_ref = dirs[d][2]
            beta_r = jnp.concatenate([s["b0"], s["b1"]], axis=0)
            egc_r = jnp.exp(jnp.concatenate([s["c0"], s["c1"]], axis=0))
            v2 = jnp.concatenate([v_ref[:, lv0 * HEAD_DIM:(lv0 + 1) * HEAD_DIM],
                                  v_ref[:, (lv0 + 1) * HEAD_DIM:(lv0 + 2) * HEAD_DIM]], axis=0)
            rhs = jnp.concatenate([v2 * beta_r, s["k2"] * (beta_r * egc_r)], axis=1)
            s["sol"] = _dot(blockdiag(s["X"]).astype(BF16), rhs.astype(BF16))
            s["egc"] = egc_r
            s["qg2"] = jnp.concatenate([s["q"], s["q"]], axis=0) * egc_r

        for s in st:
            d, lv0 = s["d"], s["lv0"]
            w2 = s["sol"][:, HEAD_DIM:]
            s["ws"] = []
            for r in range(2):
                lhs = jnp.concatenate([w2[r * C:(r + 1) * C], s["qg2"][r * C:(r + 1) * C]], axis=0).astype(BF16)
                s["ws"].append(_dot(lhs, s_ref[d, lv0 + r].astype(BF16)))

        for s in st:
            u2 = s["sol"][:, 0:HEAD_DIM]
            vn2 = jnp.concatenate([u2[r * C:(r + 1) * C] - s["ws"][r][0:C] for r in range(2)], axis=0).astype(BF16)
            s["vn2"] = vn2
            s["o2"] = jnp.concatenate([s["ws"][r][C:2 * C] for r in range(2)], axis=0) + _dot(
                blockdiag(s["attn"]).astype(BF16), vn2)

        for s in st:
            d, lv0 = s["d"], s["lv0"]
            o_ref = dirs[d][4]
            last = C - 1 if d == 0 else 0
            kdT_p = s["kT2"] * s["dec"][last:last + 1, :]
            zkd = jnp.zeros_like(kdT_p)
            for r in range(2):
                kd_r = jnp.where(left_sq if r == 0 else jnp.logical_not(left_sq), kdT_p, zkd).astype(BF16)
                glr = jnp.broadcast_to(s["egc"][r * C + last:r * C + last + 1, :], (HEAD_DIM, HEAD_DIM))
                s_ref[d, lv0 + r] = s_ref[d, lv0 + r] * glr + _dot(kd_r, s["vn2"])
                o_ref[:, (lv0 + r) * HEAD_DIM:(lv0 + r + 1) * HEAD_DIM] = s["o2"][r * C:(r + 1) * C].astype(o_ref.dtype)


def _dn_core(qkv, gates, batch, seq, ctx_len):
    m = qkv.shape[0]
    C = DN_CHUNK
    n_lat = seq // C
    n_ctx = ctx_len // C
    n_steps = n_ctx + n_lat
    ctx0 = batch * n_lat
    khs = DN_KH_PER_STEP
    qw = khs * HEAD_DIM
    vw = 2 * khs * HEAD_DIM
    k_blk0 = DN_K_W // qw
    v_blk0 = 2 * DN_K_W // vw

    def rf(b, t):
        return jnp.where(t < n_ctx, ctx0 + b * n_ctx + t, b * n_lat + (t - n_ctx))

    def rb(b, t):
        return jnp.where(t < n_ctx, ctx0 + b * n_ctx + (n_ctx - 1 - t), b * n_lat + (n_lat - 1 - (t - n_ctx)))

    def specs(rfun):
        return [pl.BlockSpec((C, qw), lambda b, h, t: (rfun(b, t), h)),
                pl.BlockSpec((C, qw), lambda b, h, t: (rfun(b, t), k_blk0 + h)),
                pl.BlockSpec((C, vw), lambda b, h, t: (rfun(b, t), v_blk0 + h)),
                pl.BlockSpec((C, LANES), lambda b, h, t: (rfun(b, t), h))]

    return pl.pallas_call(
        _dncore_kernel,
        grid=(batch, DN_K_HEADS // khs, n_steps),
        in_specs=specs(rf) + specs(rb),
        out_specs=[pl.BlockSpec((C, vw), lambda b, h, t: (rf(b, t), h)),
                   pl.BlockSpec((C, vw), lambda b, h, t: (rb(b, t), h))],
        out_shape=[jax.ShapeDtypeStruct((m, DN_V_W), BF16), jax.ShapeDtypeStruct((m, DN_V_W), BF16)],
        scratch_shapes=[pltpu.VMEM((2, 2 * khs, HEAD_DIM, HEAD_DIM), F32)],
        compiler_params=_cparams(3),
        name="dn_core",
    )(qkv, qkv, qkv, gates, qkv, qkv, qkv, gates)


DN_OUT_K_TILE = 1024


def _dn_out_kernel(of_ref, ob_ref, z_ref, nw_ref, w_ref, x_ref, gate_ref, *rest):
    o_ref, acc_ref = rest[-2:]
    k = pl.program_id(1)
    ys = []
    for h in range(of_ref.shape[1] // HEAD_DIM):
        sl = slice(h * HEAD_DIM, (h + 1) * HEAD_DIM)
        o = of_ref[:, sl].astype(F32) + ob_ref[:, sl].astype(F32)
        n = o * lax.rsqrt(jnp.mean(o * o, axis=-1, keepdims=True) + EPS) * nw_ref[...]
        ys.append((n * _silu(z_ref[:, sl].astype(F32))).astype(BF16))
    tk = of_ref.shape[1]
    part = _dot(jnp.concatenate(ys, axis=1), w_ref[pl.ds(pl.multiple_of(k * tk, tk), tk), :])

    @pl.when(k == 0)
    def _():
        acc_ref[...] = part

    @pl.when(jnp.logical_and(k > 0, k < pl.num_programs(1) - 1))
    def _():
        acc_ref[...] += part

    @pl.when(k == pl.num_programs(1) - 1)
    def _():
        y = x_ref[...] + gate_ref[...] * (acc_ref[...] + part)
        if len(rest) == 3:
            y = y * lax.rsqrt(jnp.mean(y * y, axis=-1, keepdims=True) + EPS) * rest[0][...]
        o_ref[...] = y


def _dn_out(o_f, o_b, z, norm_w, w_out_bf16, x, mod, batch, seq, final_norm_w=None):
    tm, tk = ROW_TILE, DN_OUT_K_TILE
    m = x.shape[0] if final_norm_w is None else batch * seq
    lat_tiles = seq // tm
    extra_specs, extra_args = [], []
    if final_norm_w is not None:
        extra_specs = [pl.BlockSpec((1, D_MODEL), lambda i, k: (0, 0))]
        extra_args = [final_norm_w.reshape(1, D_MODEL)]
    return pl.pallas_call(
        _dn_out_kernel,
        grid=(m // tm, DN_V_W // tk),
        in_specs=[pl.BlockSpec((tm, tk), lambda i, k: (i, k)),
                  pl.BlockSpec((tm, tk), lambda i, k: (i, k)),
                  pl.BlockSpec((tm, tk), lambda i, k: (i, k)),
                  pl.BlockSpec((1, HEAD_DIM), lambda i, k: (0, 0)),
                  pl.BlockSpec((DN_V_W, D_MODEL), lambda i, k: (0, 0), pipeline_mode=pl.Buffered(1)),
                  pl.BlockSpec((tm, D_MODEL), lambda i, k: (i, 0)),
                  pl.BlockSpec((None, 1, D_MODEL), lambda i, k: (_mod_row(i, lat_tiles, batch), 0, 2))] + extra_specs,
        out_specs=pl.BlockSpec((tm, D_MODEL), lambda i, k: (i, 0)),
        out_shape=jax.ShapeDtypeStruct((m, D_MODEL), F32),
        scratch_shapes=[pltpu.VMEM((tm, D_MODEL), F32)],
        compiler_params=_cparams(2),
        name="dn_out",
    )(o_f, o_b, z, norm_w.reshape(1, HEAD_DIM), w_out_bf16, x, mod, *extra_args)


def _final_norm_kernel(x_ref, w_ref, o_ref):
    x = x_ref[...]
    o_ref[...] = x * lax.rsqrt(jnp.mean(x * x, axis=-1, keepdims=True) + EPS) * w_ref[...]


def _final_norm(x, w, rows):
    tm = ROW_TILE
    return pl.pallas_call(
        _final_norm_kernel,
        grid=(rows // tm,),
        in_specs=[pl.BlockSpec((tm, D_MODEL), lambda i: (i, 0)),
                  pl.BlockSpec((1, D_MODEL), lambda i: (0, 0))],
        out_specs=pl.BlockSpec((tm, D_MODEL), lambda i: (i, 0)),
        out_shape=jax.ShapeDtypeStruct((rows, D_MODEL), F32),
        compiler_params=_cparams(1),
        name="final_norm",
    )(x, w.reshape(1, D_MODEL))


def kernel(x, c, ctx, c_ctx, norm_w, ada_w, ada_b, ab_w_in, ab_w_out, sgu_w, sgu_b, q_norm_w, k_norm_w,
           dn_w_in, dn_conv_w, dn_a_log, dn_dt_bias, dn_norm_w, dn_w_out, final_norm_w):
    batch, seq, _ = x.shape
    ctx_len = ctx.shape[1]
    depth = norm_w.shape[0]
    assert ctx_len == CONV_ROW_TILE and seq % ROW_TILE == 0 and batch * ctx_len == ROW_TILE
    n_lat_rows = batch * seq

    xs = jnp.concatenate([x.reshape(n_lat_rows, D_MODEL), ctx.reshape(batch * ctx_len, D_MODEL)], axis=0)
    cond = jnp.zeros((SUBLANES, D_MODEL), F32).at[0:batch].set(c).at[batch].set(c_ctx)
    mods = _ada_mod(cond, ada_w, ada_b)
    cos_tab, sin_tab = _rope_tables(seq, CONV_ROW_TILE)

    for i in range(depth):
        j = i // 2
        mod = mods[i].reshape(SUBLANES, 1, 3 * D_MODEL)
        if i % 2 == 0:
            proj = _inproj(xs, norm_w[i], mod, ab_w_in[j].astype(BF16), batch, seq)
            qr, kr, vr = _qkprep(proj, cos_tab, sin_tab, q_norm_w[j], k_norm_w[j], batch, seq)
            ao_lat = _attention(qr, kr, vr, batch, seq, ctx_len, True)
            ao_ctx = _attention(qr, kr, vr, batch, seq, ctx_len, False)
            xs = _ab_out(proj, ao_lat, ao_ctx, sgu_w[j].astype(BF16), sgu_b[j].T, ab_w_out[j].astype(BF16), xs, mod,
                         batch, seq)
        else:
            conv_w = jnp.zeros((SUBLANES, DN_QKV_W), F32).at[0:CONV_K].set(dn_conv_w[j])
            qkv, z, ba = _inproj_dn(xs, norm_w[i], mod, dn_w_in[j].astype(BF16), conv_w, batch, seq, ctx_len)
            zeros = jnp.zeros((2, DN_V_HEADS), F32)
            alog_vec = jnp.concatenate([zeros, dn_a_log[j]], axis=1).reshape(1, LANES)
            dtb_vec = jnp.concatenate([zeros, dn_dt_bias[j]], axis=1).reshape(1, LANES)
            gates = _dn_gate(ba, alog_vec, dtb_vec)
            n_hg = DN_K_HEADS // DN_KH_PER_STEP
            nvh = 2 * DN_KH_PER_STEP
            g4 = gates.reshape(-1, 4, n_hg, nvh).transpose(0, 2, 1, 3).reshape(-1, n_hg, 4 * nvh)
            g4 = jnp.pad(g4, ((0, 0), (0, 0), (0, LANES - 4 * nvh))).reshape(-1, n_hg * LANES)
            o_f, o_b = _dn_core(qkv, g4, batch, seq, ctx_len)
            xs = _dn_out(o_f, o_b, z, dn_norm_w[j], dn_w_out[j].astype(BF16), xs, mod, batch, seq,
                         final_norm_w=final_norm_w if i == depth - 1 else None)

    if depth % 2 == 1:
        xs = _final_norm(xs, final_norm_w, n_lat_rows)
    return xs.reshape(batch, seq, D_MODEL)
```

```python
import functools

import jax
import jax.numpy as jnp
from jax import lax
from jax.experimental import pallas as pl
from jax.experimental.pallas import tpu as pltpu

F32 = jnp.float32
BF16 = jnp.bfloat16

D_MODEL = 2048
GRID_W = 64
EPS = 1e-6
HEAD_DIM = 128
A_WIDTH = D_MODEL // 2
A_GROUPS = A_WIDTH // 128
SGU_CHUNK = 128
B_HEADS = (D_MODEL // 2) // HEAD_DIM
B_KV_HEADS = B_HEADS // 4
B_GROUP = B_HEADS // B_KV_HEADS
B_Q_W = B_HEADS * HEAD_DIM
B_KV_W = B_KV_HEADS * HEAD_DIM
ROPE_THETA = 10000.0
AXIS_DIM = HEAD_DIM // 2
AB_IN_W = 3 * A_WIDTH + 2 * B_Q_W + 2 * B_KV_W
DN_K_HEADS = D_MODEL // HEAD_DIM
DN_V_HEADS = 2 * DN_K_HEADS
DN_K_W = DN_K_HEADS * HEAD_DIM
DN_V_W = DN_V_HEADS * HEAD_DIM
DN_QKV_W = 2 * DN_K_W + DN_V_W
DN_IN_W = DN_QKV_W + DN_V_W + 4 * DN_V_HEADS
DN_CHUNK = 64
CONV_K = 5

V7X_VMEM_LIMIT_BYTES = 56 * 1024 * 1024
LANES = 128
SUBLANES = 8

ROW_TILE = 512
CONV_ROW_TILE = 256


def _cparams(n_axes):
    return pltpu.CompilerParams(dimension_semantics=("arbitrary",) * n_axes,
                                vmem_limit_bytes=V7X_VMEM_LIMIT_BYTES)


def _silu(x):
    return x * jax.nn.sigmoid(x)


def _split_bf16(a):
    hi = a.astype(BF16)
    lo = (a - hi.astype(F32)).astype(BF16)
    return hi, lo


def _dot(a, b):
    return jnp.dot(a, b, preferred_element_type=F32)


def _dot_nt(a, b):
    return lax.dot_general(a, b, (((1,), (1,)), ((), ())), preferred_element_type=F32)


def _dot3(a, b):
    ah, al = _split_bf16(a)
    bh, bl = _split_bf16(b)
    return _dot(ah, bh) + _dot(ah, bl) + _dot(al, bh)


def _ada_kernel(c_ref, w_ref, b_ref, o_ref):
    s = _silu(c_ref[...])
    o_ref[0] = _dot3(s, w_ref[0]) + b_ref[0]


def _ada_mod(cond, ada_w, ada_b):
    depth = ada_w.shape[0]
    tn = 512
    return pl.pallas_call(
        _ada_kernel,
        grid=(depth, 3 * D_MODEL // tn),
        in_specs=[pl.BlockSpec((SUBLANES, D_MODEL), lambda l, j: (0, 0)),
                  pl.BlockSpec((1, D_MODEL, tn), lambda l, j: (l, 0, j)),
                  pl.BlockSpec((1, 1, tn), lambda l, j: (l, 0, j))],
        out_specs=pl.BlockSpec((1, SUBLANES, tn), lambda l, j: (l, 0, j)),
        out_shape=jax.ShapeDtypeStruct((depth, SUBLANES, 3 * D_MODEL), F32),
        compiler_params=_cparams(2),
        name="ada_mod",
    )(cond, ada_w, ada_b.reshape(depth, 1, 3 * D_MODEL))


INPROJ_COL_TILES = (512, 1024, 1408, 1536, 2816)


def _inproj_kernel(x_ref, nw_ref, mod_ref, w_ref, *rest):
    h_ref = rest[-1]
    has_side = len(rest) == 4

    @pl.when(pl.program_id(1) == 0)
    def _():
        x = x_ref[...]
        y = x * lax.rsqrt(jnp.mean(x * x, axis=-1, keepdims=True) + EPS) * nw_ref[...]
        shift = mod_ref[:, 0:D_MODEL]
        scale = mod_ref[:, D_MODEL:2 * D_MODEL]
        h_ref[...] = (y * (1.0 + scale) + shift).astype(BF16)
        if has_side:
            rest[2][...] = _dot(h_ref[...], rest[0][...])

    o_ref = rest[1] if has_side else rest[0]
    o_ref[...] = _dot(h_ref[...], w_ref[...]).astype(o_ref.dtype)


def _mod_row(i, lat_tiles, batch):
    return jnp.minimum(i // lat_tiles, batch)


def _inproj(x, norm_w, mod, w_bf16, batch, seq, n_side=0, out_dtype=BF16):
    m = x.shape[0]
    n = w_bf16.shape[1] - n_side
    tm = ROW_TILE
    tn = max(t for t in INPROJ_COL_TILES if n % t == 0)
    lat_tiles = seq // tm
    in_specs = [pl.BlockSpec((tm, D_MODEL), lambda i, j: (i, 0)),
                pl.BlockSpec((1, D_MODEL), lambda i, j: (0, 0)),
                pl.BlockSpec((None, 1, 3 * D_MODEL), lambda i, j: (_mod_row(i, lat_tiles, batch), 0, 0)),
                pl.BlockSpec((D_MODEL, tn), lambda i, j: (0, j))]
    out_specs = [pl.BlockSpec((tm, tn), lambda i, j: (i, j))]
    out_shape = [jax.ShapeDtypeStruct((m, n), out_dtype)]
    args = [x, norm_w.reshape(1, D_MODEL), mod, w_bf16]
    if n_side:
        assert n % n_side == 0
        in_specs.append(pl.BlockSpec((D_MODEL, n_side), lambda i, j: (0, n // n_side)))
        out_specs.append(pl.BlockSpec((tm, n_side), lambda i, j: (i, 0)))
        out_shape.append(jax.ShapeDtypeStruct((m, n_side), F32))
        args.append(w_bf16)
    res = pl.pallas_call(
        _inproj_kernel,
        grid=(m // tm, n // tn),
        in_specs=in_specs,
        out_specs=out_specs,
        out_shape=out_shape,
        scratch_shapes=[pltpu.VMEM((tm, D_MODEL), BF16)],
        compiler_params=_cparams(2),
        name="inproj",
    )(*args)
    return res if n_side else res[0]


Q_PRESCALE = (HEAD_DIM ** -0.5) * 1.4426950408889634


def _qkprep_kernel(q_ref, k_ref, v_ref, cos_ref, sin_ref, qw_ref, kw_ref, qo_ref, ko_ref, vo_ref):
    cos = cos_ref[...]
    sin = sin_ref[...]
    lane = lax.broadcasted_iota(jnp.int32, cos.shape, 1)
    first = (lane % (AXIS_DIM)) < (AXIS_DIM // 2)

    def prep(x, w):
        y = x * lax.rsqrt(jnp.mean(x * x, axis=-1, keepdims=True) + EPS) * w
        rot = jnp.where(first, pltpu.roll(y, HEAD_DIM - AXIS_DIM // 2, 1), pltpu.roll(y, AXIS_DIM // 2, 1))
        return y * cos + rot * sin

    for h in range(B_HEADS):
        sl = slice(h * HEAD_DIM, (h + 1) * HEAD_DIM)
        qo_ref[:, sl] = (prep(q_ref[:, sl].astype(F32), qw_ref[...]) * Q_PRESCALE).astype(qo_ref.dtype)
    for h in range(B_KV_HEADS):
        sl = slice(h * HEAD_DIM, (h + 1) * HEAD_DIM)
        ko_ref[:, sl] = prep(k_ref[:, sl].astype(F32), kw_ref[...]).astype(ko_ref.dtype)
    vo_ref[...] = v_ref[...].astype(vo_ref.dtype)


def _qkprep(proj, cos_tab, sin_tab, qn_w, kn_w, batch, seq):
    m = proj.shape[0]
    tm = CONV_ROW_TILE
    lat_tiles = seq // tm
    n_lat = batch * lat_tiles
    q_blk = (3 * A_WIDTH) // B_Q_W
    k_blk = (3 * A_WIDTH + B_Q_W) // B_KV_W

    def tab_idx(i):
        return (jnp.where(i < n_lat, i % lat_tiles, lat_tiles), 0)

    def kv_idx(i):
        lat_blk = (i // lat_tiles) * (lat_tiles + 1) + 1 + i % lat_tiles
        return (jnp.where(i < n_lat, lat_blk, (i - n_lat) * (lat_tiles + 1)), 0)

    return pl.pallas_call(
        _qkprep_kernel,
        grid=(m // tm,),
        in_specs=[pl.BlockSpec((tm, B_Q_W), lambda i: (i, q_blk)),
                  pl.BlockSpec((tm, B_KV_W), lambda i: (i, k_blk)),
                  pl.BlockSpec((tm, B_KV_W), lambda i: (i, k_blk + 1)),
                  pl.BlockSpec((tm, HEAD_DIM), tab_idx),
                  pl.BlockSpec((tm, HEAD_DIM), tab_idx),
                  pl.BlockSpec((1, HEAD_DIM), lambda i: (0, 0)),
                  pl.BlockSpec((1, HEAD_DIM), lambda i: (0, 0))],
        out_specs=[pl.BlockSpec((tm, B_Q_W), lambda i: (i, 0)),
                   pl.BlockSpec((tm, B_KV_W), kv_idx),
                   pl.BlockSpec((tm, B_KV_W), kv_idx)],
        out_shape=[jax.ShapeDtypeStruct((m, B_Q_W), BF16),
                   jax.ShapeDtypeStruct((m, B_KV_W), BF16),
                   jax.ShapeDtypeStruct((m, B_KV_W), BF16)],
        compiler_params=_cparams(1),
        name="qk_prep",
    )(proj, proj, proj, cos_tab, sin_tab, qn_w.reshape(1, HEAD_DIM), kn_w.reshape(1, HEAD_DIM))


def _rope_tables(seq, tm):
    rows = seq // GRID_W
    row = jnp.repeat(jnp.arange(rows), GRID_W).astype(F32)
    col = jnp.tile(jnp.arange(GRID_W), rows).astype(F32)
    freqs = ROPE_THETA ** (-jnp.arange(0, AXIS_DIM, 2, dtype=F32) / AXIS_DIM)
    ang_r = row[:, None] * freqs[None, :]
    ang_c = col[:, None] * freqs[None, :]
    ang = jnp.concatenate([ang_r, ang_r, ang_c, ang_c], axis=-1)
    sign = jnp.where((jnp.arange(HEAD_DIM) % AXIS_DIM) < AXIS_DIM // 2, -1.0, 1.0).astype(F32)
    cos = jnp.concatenate([jnp.cos(ang), jnp.ones((tm, HEAD_DIM), F32)], axis=0)
    sin = jnp.concatenate([jnp.sin(ang) * sign[None, :], jnp.zeros((tm, HEAD_DIM), F32)], axis=0)
    return cos, sin


ATTN_Q_TILE = 256
ATTN_KV_BODY = 4096
ATTN_KV_TAIL = 768


def _kv_chunks(kv_len):
    tail = min(ATTN_KV_TAIL, kv_len)
    body = kv_len - tail
    sizes = [ATTN_KV_BODY] * (body // ATTN_KV_BODY) + ([body % ATTN_KV_BODY] if body % ATTN_KV_BODY else []) + [tail]
    assert all(c % LANES == 0 for c in sizes) and sum(sizes) == kv_len
    return tuple(sizes)


def _attn_kernel(q_ref, k_ref, v_ref, o_ref, m_scr, l_scr, acc_scr, *, chunks, tq):
    q = jnp.concatenate([q_ref[:, g * HEAD_DIM:(g + 1) * HEAD_DIM] for g in range(B_GROUP)], axis=0)
    m_scr[...] = jnp.full(m_scr.shape, -jnp.inf, F32)
    l_scr[...] = jnp.zeros(l_scr.shape, F32)
    acc_scr[...] = jnp.zeros(acc_scr.shape, F32)
    start = 0
    for tk in chunks:
        k = k_ref[start:start + tk, :]
        v = v_ref[start:start + tk, :]
        start += tk
        s = _dot_nt(q, k)
        m_prev = m_scr[...]
        m_next = jnp.maximum(m_prev, jnp.max(s, axis=1, keepdims=True))
        p = jnp.exp2(s - jnp.concatenate([m_next] * (tk // LANES), axis=1))
        alpha = jnp.exp2(m_prev - m_next)
        pv = _dot(p.astype(BF16), jnp.concatenate([v, jnp.ones_like(v)], axis=1))
        l_scr[...] = alpha * l_scr[...] + pv[:, HEAD_DIM:]
        acc_scr[...] = acc_scr[...] * alpha + pv[:, :HEAD_DIM]
        m_scr[...] = m_next
    out = acc_scr[...] / l_scr[...]
    for g in range(B_GROUP):
        o_ref[:, g * HEAD_DIM:(g + 1) * HEAD_DIM] = out[g * tq:(g + 1) * tq].astype(o_ref.dtype)


def _attention(qr, kr, vr, batch, seq, ctx_len, latent):
    tq = ATTN_Q_TILE
    gw = B_GROUP * HEAD_DIM
    q_len = seq if latent else ctx_len
    q_tiles = q_len // tq
    q_row0 = 0 if latent else batch * seq // tq
    kv_len = ctx_len + seq
    if latent:
        kv_rows = kv_len
        kv_idx = lambda b, h, i: (b, h)
    else:
        kv_rows = ctx_len
        kv_idx = lambda b, h, i: (b * (kv_len // ctx_len), h)
    rows = B_GROUP * tq
    return pl.pallas_call(
        functools.partial(_attn_kernel, chunks=_kv_chunks(kv_rows), tq=tq),
        grid=(batch, B_KV_HEADS, q_tiles),
        in_specs=[pl.BlockSpec((tq, gw), lambda b, h, i: (q_row0 + b * q_tiles + i, h)),
                  pl.BlockSpec((kv_rows, HEAD_DIM), kv_idx),
                  pl.BlockSpec((kv_rows, HEAD_DIM), kv_idx)],
        out_specs=pl.BlockSpec((tq, gw), lambda b, h, i: (b * q_tiles + i, h)),
        out_shape=jax.ShapeDtypeStruct((batch * q_len, B_Q_W), BF16),
        scratch_shapes=[pltpu.VMEM((rows, LANES), F32), pltpu.VMEM((rows, LANES), F32),
                        pltpu.VMEM((rows, HEAD_DIM), F32)],
        compiler_params=_cparams(3),
        name="attn_lat" if latent else "attn_ctx",
    )(qr, kr, vr)


def _ab_out_kernel(u_ref, v_ref, ga_ref, gb0_ref, gb1_ref, aol_ref, aoc_ref, ws_ref, bs_ref, w_ref, x_ref, gate_ref, o_ref,
                   y_ref, acc_ref, *, n_lat_tiles):
    k = pl.program_id(1)

    @pl.when(k == 0)
    def _():
        tm = u_ref.shape[0]
        for c in range(tm // SGU_CHUNK):
            rows = slice(c * SGU_CHUNK, (c + 1) * SGU_CHUNK)
            for g in range(A_GROUPS):
                cols = slice(g * LANES, (g + 1) * LANES)
                vg = v_ref[rows, cols].astype(F32)
                d = vg - jnp.mean(vg, axis=-1, keepdims=True)
                var = jnp.mean(d * d, axis=-1, keepdims=True)
                vn = (d * lax.rsqrt(var + 1e-5)).astype(BF16)
                mixed = _dot(ws_ref[g], vn) + bs_ref[:, g:g + 1]
                y_ref[rows, cols] = (u_ref[rows, cols].astype(F32) * mixed
                                     * _silu(ga_ref[rows, cols].astype(F32))).astype(BF16)
        acc_ref[...] = _dot(y_ref[...], w_ref[0:A_WIDTH, :])

    @pl.when(k == 1)
    def _():
        ao = jnp.where(pl.program_id(0) < n_lat_tiles, aol_ref[...], aoc_ref[...]).astype(F32)
        gb = jnp.concatenate([gb0_ref[...], gb1_ref[...]], axis=1).astype(F32)
        y = (ao * _silu(gb)).astype(BF16)
        o_ref[...] = x_ref[...] + gate_ref[...] * (acc_ref[...] + _dot(y, w_ref[A_WIDTH:, :]))


def _ab_out(proj, ao_lat, ao_ctx, sgu_w_bf16, sgu_b_t, w_out_bf16, x, mod, batch, seq):
    m = x.shape[0]
    tm = ROW_TILE
    lat_tiles = seq // tm
    n_lat_tiles = batch * lat_tiles
    assert ao_ctx.shape[0] == tm
    gb_blk0 = (3 * A_WIDTH + B_Q_W + 2 * B_KV_W) // (B_Q_W // 2)
    return pl.pallas_call(
        functools.partial(_ab_out_kernel, n_lat_tiles=n_lat_tiles),
        grid=(m // tm, 2),
        in_specs=[pl.BlockSpec((tm, A_WIDTH), lambda i, k: (i, 0)),
                  pl.BlockSpec((tm, A_WIDTH), lambda i, k: (i, 1)),
                  pl.BlockSpec((tm, A_WIDTH), lambda i, k: (i, 2)),
                  pl.BlockSpec((tm, B_Q_W // 2), lambda i, k: (i, gb_blk0)),
                  pl.BlockSpec((tm, B_Q_W // 2), lambda i, k: (i, gb_blk0 + 1)),
                  pl.BlockSpec((tm, B_Q_W), lambda i, k: (jnp.minimum(i, n_lat_tiles - 1), 0)),
                  pl.BlockSpec((tm, B_Q_W), lambda i, k: (0, 0)),
                  pl.BlockSpec((A_GROUPS, SGU_CHUNK, SGU_CHUNK), lambda i, k: (0, 0, 0)),
                  pl.BlockSpec((SGU_CHUNK, A_GROUPS), lambda i, k: (0, 0)),
                  pl.BlockSpec((A_WIDTH + B_Q_W, D_MODEL), lambda i, k: (0, 0), pipeline_mode=pl.Buffered(1)),
                  pl.BlockSpec((tm, D_MODEL), lambda i, k: (i, 0)),
                  pl.BlockSpec((None, 1, D_MODEL), lambda i, k: (_mod_row(i, lat_tiles, batch), 0, 2))],
        out_specs=pl.BlockSpec((tm, D_MODEL), lambda i, k: (i, 0)),
        out_shape=jax.ShapeDtypeStruct((m, D_MODEL), F32),
        scratch_shapes=[pltpu.VMEM((tm, A_WIDTH), BF16), pltpu.VMEM((tm, D_MODEL), F32)],
        compiler_params=_cparams(2),
        name="ab_out",
    )(proj, proj, proj, proj, proj, ao_lat, ao_ctx, sgu_w_bf16, sgu_b_t, w_out_bf16, x, mod)


DN_HALO = 2 * SUBLANES
DN_PROJ_COL_TILE = 2048


def _inproj_dn_kernel(xp_ref, x_ref, xn_ref, nw_ref, mod_ref, w_ref, ws_ref, cw_ref, qkv_ref, z_ref, ba_ref, h_ref, p_ref,
                      *, lat_tiles, n_lat, q_tiles, qk_tiles, qkv_tiles, ctx_len):
    i = pl.program_id(0)
    j = pl.program_id(1)
    tm = x_ref.shape[0]
    is_lat = i < n_lat
    pos = i % lat_tiles
    first = jnp.logical_or(jnp.logical_not(is_lat), pos == 0)
    last = jnp.logical_or(jnp.logical_not(is_lat), pos == lat_tiles - 1)
    half = CONV_K // 2

    @pl.when(j == 0)
    def _():
        shift = mod_ref[:, 0:D_MODEL]
        scale = mod_ref[:, D_MODEL:2 * D_MODEL]

        def norm_mod(x):
            y = x * lax.rsqrt(jnp.mean(x * x, axis=-1, keepdims=True) + EPS) * nw_ref[...]
            return y * (1.0 + scale) + shift

        h_ref[0:DN_HALO, :] = jnp.where(first, 0.0, norm_mod(xp_ref[...])).astype(BF16)
        h_ref[DN_HALO:DN_HALO + tm, :] = norm_mod(x_ref[...]).astype(BF16)
        h_ref[DN_HALO + tm:, :] = jnp.where(last, 0.0, norm_mod(xn_ref[...])).astype(BF16)
        ba_ref[...] = _dot(h_ref[DN_HALO:DN_HALO + tm, :], ws_ref[...])

    n_ext = h_ref.shape[0]
    tn = w_ref.shape[1]
    pair_w = 2 * HEAD_DIM
    row = lax.broadcasted_iota(jnp.int32, (tm, HEAD_DIM), 0)
    is_ctx = jnp.logical_not(is_lat)

    def conv_silu(ph, sl):
        acc = cw_ref[half:half + 1, sl] * ph[DN_HALO:DN_HALO + tm]
        for t in range(CONV_K):
            off = t - half
            if off == 0:
                continue
            tap = pltpu.roll(ph, (-off) % n_ext, 0)[DN_HALO:DN_HALO + tm]
            crosses = (jnp.logical_and(row >= ctx_len - off, row < ctx_len) if off > 0
                       else jnp.logical_and(row >= ctx_len, row < ctx_len - off))
            tap = jnp.where(jnp.logical_and(is_ctx, crosses), 0.0, tap)
            acc = acc + cw_ref[t:t + 1, sl] * tap
        return _silu(acc)

    def for_each_head(store):
        for n, c0 in enumerate(range(0, tn, pair_w)):
            p_ref[n % 2] = _dot(h_ref[...], w_ref[:, c0:c0 + pair_w])
            for h0 in range(0, pair_w, HEAD_DIM):
                sl = slice(c0 + h0, c0 + h0 + HEAD_DIM)
                store(sl, conv_silu(p_ref[n % 2, :, h0:h0 + HEAD_DIM], sl))

    @pl.when(j < qk_tiles)
    def _():
        qk_scale = jnp.where(j < q_tiles, HEAD_DIM ** -0.5, 1.0)

        def store(sl, y):
            qkv_ref[:, sl] = (y * (lax.rsqrt(jnp.sum(y * y, axis=-1, keepdims=True) + EPS) * qk_scale)).astype(qkv_ref.dtype)
        for_each_head(store)

    @pl.when(jnp.logical_and(j >= qk_tiles, j < qkv_tiles))
    def _():
        def store(sl, y):
            qkv_ref[:, sl] = y.astype(qkv_ref.dtype)
        for_each_head(store)

    @pl.when(j >= qkv_tiles)
    def _():
        z_ref[...] = _dot(h_ref[DN_HALO:DN_HALO + tm, :], w_ref[...]).astype(z_ref.dtype)


def _inproj_dn(x, norm_w, mod, w_bf16, conv_w, batch, seq, ctx_len):
    m = x.shape[0]
    tm, tn = ROW_TILE, DN_PROJ_COL_TILE
    n_side = 4 * DN_V_HEADS
    n = w_bf16.shape[1] - n_side
    assert n == DN_QKV_W + DN_V_W and tm == 2 * ctx_len
    lat_tiles = seq // tm
    n_lat = batch * lat_tiles
    halo_per_tile = tm // DN_HALO
    n_halo = m // DN_HALO
    qkv_tiles = DN_QKV_W // tn
    kern = functools.partial(_inproj_dn_kernel, lat_tiles=lat_tiles, n_lat=n_lat, q_tiles=DN_K_W // tn,
                             qk_tiles=2 * DN_K_W // tn, qkv_tiles=qkv_tiles, ctx_len=ctx_len)
    return pl.pallas_call(
        kern,
        grid=(m // tm, n // tn),
        in_specs=[pl.BlockSpec((DN_HALO, D_MODEL), lambda i, j: (jnp.maximum(i * halo_per_tile - 1, 0), 0)),
                  pl.BlockSpec((tm, D_MODEL), lambda i, j: (i, 0)),
                  pl.BlockSpec((DN_HALO, D_MODEL), lambda i, j: (jnp.minimum((i + 1) * halo_per_tile, n_halo - 1), 0)),
                  pl.BlockSpec((1, D_MODEL), lambda i, j: (0, 0)),
                  pl.BlockSpec((None, 1, 3 * D_MODEL), lambda i, j: (_mod_row(i, lat_tiles, batch), 0, 0)),
                  pl.BlockSpec((D_MODEL, tn), lambda i, j: (0, j)),
                  pl.BlockSpec((D_MODEL, n_side), lambda i, j: (0, n // n_side)),
                  pl.BlockSpec((SUBLANES, tn), lambda i, j: (0, jnp.minimum(j, qkv_tiles - 1)))],
        out_specs=[pl.BlockSpec((tm, tn), lambda i, j: (i, jnp.minimum(j, qkv_tiles - 1))),
                   pl.BlockSpec((tm, tn), lambda i, j: (i, jnp.maximum(j - qkv_tiles, 0))),
                   pl.BlockSpec((tm, n_side), lambda i, j: (i, 0))],
        out_shape=[jax.ShapeDtypeStruct((m, DN_QKV_W), F32),
                   jax.ShapeDtypeStruct((m, DN_V_W), BF16),
                   jax.ShapeDtypeStruct((m, n_side), F32)],
        scratch_shapes=[pltpu.VMEM((tm + 2 * DN_HALO, D_MODEL), BF16),
                        pltpu.VMEM((2, tm + 2 * DN_HALO, 2 * HEAD_DIM), F32)],
        compiler_params=_cparams(2),
        name="inproj_dn",
    )(x, x, x, norm_w.reshape(1, D_MODEL), mod, w_bf16, w_bf16, conv_w)


def _dngate_kernel(ba_ref, alog_ref, dtb_ref, o_ref):
    ba = ba_ref[...]
    tm = ba.shape[0]
    lane = lax.broadcasted_iota(jnp.int32, (DN_CHUNK, LANES), 1)
    is_beta = (lane // DN_V_HEADS) % 2 == 0
    is_fwd = lane < 2 * DN_V_HEADS
    z = ba + dtb_ref[...]
    softplus = jnp.maximum(z, 0.0) + jnp.log1p(jnp.exp(-jnp.abs(z)))
    g = -jnp.exp(alog_ref[...]) * softplus
    r = lax.broadcasted_iota(jnp.int32, (DN_CHUNK, DN_CHUNK), 0)
    c = lax.broadcasted_iota(jnp.int32, (DN_CHUNK, DN_CHUNK), 1)
    tri_lo = (r >= c).astype(BF16)
    tri_up = (r <= c).astype(BF16)
    beta = jax.nn.sigmoid(ba)
    for ch in range(tm // DN_CHUNK):
        rows = slice(ch * DN_CHUNK, (ch + 1) * DN_CHUNK)
        gch = g[rows]
        g1 = gch.astype(BF16)
        r1 = gch - g1.astype(F32)
        g2 = r1.astype(BF16)
        g3 = (r1 - g2.astype(F32)).astype(BF16)
        pre = _dot(tri_lo, g1) + _dot(tri_lo, g2) + _dot(tri_lo, g3)
        suf = _dot(tri_up, g1) + _dot(tri_up, g2) + _dot(tri_up, g3)
        gc = jnp.where(is_fwd, pre, suf)
        o_ref[rows, :] = jnp.where(is_beta, beta[rows], gc)


def _dn_gate(ba, alog_vec, dtb_vec):
    m = ba.shape[0]
    tm = ROW_TILE
    return pl.pallas_call(
        _dngate_kernel,
        grid=(m // tm,),
        in_specs=[pl.BlockSpec((tm, LANES), lambda i: (i, 0)),
                  pl.BlockSpec((1, LANES), lambda i: (0, 0)),
                  pl.BlockSpec((1, LANES), lambda i: (0, 0))],
        out_specs=pl.BlockSpec((tm, LANES), lambda i: (i, 0)),
        out_shape=jax.ShapeDtypeStruct((m, LANES), F32),
        compiler_params=_cparams(1),
        name="dn_gate",
    )(ba, alog_vec, dtb_vec)


DN_CHAIN_GROUP = 16
DN_KH_PER_STEP = 16


def _dncore_kernel(qf_ref, kf_ref, vf_ref, gf_ref, qb_ref, kb_ref, vb_ref, gb_ref, of_ref, ob_ref, s_ref):
    C = DN_CHUNK
    nvh = 2 * DN_KH_PER_STEP

    @pl.when(pl.program_id(2) == 0)
    def _():
        s_ref[...] = jnp.zeros(s_ref.shape, F32)

    lane = lax.broadcasted_iota(jnp.int32, (C, 2 * C), 1)
    left = lane < C
    row = lax.broadcasted_iota(jnp.int32, (C, 2 * C), 0)
    colp = lane % C
    left_sq = lax.broadcasted_iota(jnp.int32, (2 * C, 2 * C), 1) < C
    eye2 = (row == colp).astype(F32)

    def blockdiag(p):
        z = jnp.zeros_like(p)
        return jnp.concatenate([jnp.where(left, p, z), jnp.where(left, z, p)], axis=0)

    def packed_mm(a, b):
        return _dot(a.astype(BF16), blockdiag(b.astype(BF16)))

    dirs = ((qf_ref, kf_ref, vf_ref, gf_ref, of_ref), (qb_ref, kb_ref, vb_ref, gb_ref, ob_ref))
    chains = [(d, kh) for d in range(2) for kh in range(DN_KH_PER_STEP)]
    G = [dirs[d][3][...] for d in range(2)]
    GT = [jnp.concatenate([g, g], axis=0).T for g in G]

    for g0 in range(0, len(chains), DN_CHAIN_GROUP):
        group = chains[g0:g0 + DN_CHAIN_GROUP]
        st = []
        for d, kh in group:
            q_ref, k_ref, v_ref, _, _ = dirs[d]
            base_beta = d * 2 * nvh
            base_gc = base_beta + nvh
            lv0 = 2 * kh
            q = q_ref[:, kh * HEAD_DIM:(kh + 1) * HEAD_DIM]
            k = k_ref[:, kh * HEAD_DIM:(kh + 1) * HEAD_DIM]
            k2 = jnp.concatenate([k, k], axis=0)
            kT2 = k2.T
            gram = _dot(jnp.concatenate([q, k], axis=0).astype(BF16), kT2.astype(BF16))
            b0, b1 = G[d][:, base_beta + lv0:base_beta + lv0 + 1], G[d][:, base_beta + lv0 + 1:base_beta + lv0 + 2]
            c0, c1 = G[d][:, base_gc + lv0:base_gc + lv0 + 1], G[d][:, base_gc + lv0 + 1:base_gc + lv0 + 2]
            r0, r1 = GT[d][base_gc + lv0:base_gc + lv0 + 1, :], GT[d][base_gc + lv0 + 1:base_gc + lv0 + 2, :]
            st.append(dict(d=d, lv0=lv0, q=q, k2=k2, kT2=kT2, gram=gram, b0=b0, b1=b1, c0=c0, c1=c1, r0=r0, r1=r1))

        for s in st:
            d = s["d"]
            incl = (row >= colp) if d == 0 else (row <= colp)
            strict = (row > colp) if d == 0 else (row < colp)
            gcol_p = jnp.where(left, s["c0"], s["c1"])
            grow_p = jnp.where(left[0:1], s["r0"], s["r1"])
            beta_p = jnp.where(left, s["b0"], s["b1"])
            dec = jnp.exp(jnp.where(incl, gcol_p - grow_p, -1e30))
            s["dec"] = dec
            s["attn"] = s["gram"][0:C] * dec
            s["L"] = jnp.where(strict, s["gram"][C:2 * C] * dec, 0.0) * beta_p

        for s in st:
            n1 = jnp.where(jnp.logical_and(row // 2 == colp // 2, row != colp), s["L"], 0.0)
            s["X"] = eye2 - n1
        blk = 2
        while blk < C:
            mask = jnp.logical_and(row // (2 * blk) == colp // (2 * blk), row // blk != colp // blk)
            for s in st:
                s["Y"] = packed_mm(s["X"], jnp.where(mask, s["L"], 0.0))
            for s in st:
                s["X"] = s["X"] - packed_mm(s["Y"], s["X"])
            blk *= 2

        for s in st:
            d, lv0 = s["d"], s["lv0"]
            v_ref = dirs[d][2]
            beta_r = jnp.concatenate([s["b0"], s["b1"]], axis=0)
            egc_r = jnp.exp(jnp.concatenate([s["c0"], s["c1"]], axis=0))
            v2 = jnp.concatenate([v_ref[:, lv0 * HEAD_DIM:(lv0 + 1) * HEAD_DIM],
                                  v_ref[:, (lv0 + 1) * HEAD_DIM:(lv0 + 2) * HEAD_DIM]], axis=0)
            rhs = jnp.concatenate([v2 * beta_r, s["k2"] * (beta_r * egc_r)], axis=1)
            s["sol"] = _dot(blockdiag(s["X"]).astype(BF16), rhs.astype(BF16))
            s["egc"] = egc_r
            s["qg2"] = jnp.concatenate([s["q"], s["q"]], axis=0) * egc_r

        for s in st:
            d, lv0 = s["d"], s["lv0"]
            w2 = s["sol"][:, HEAD_DIM:]
            s["ws"] = []
            for r in range(2):
                lhs = jnp.concatenate([w2[r * C:(r + 1) * C], s["qg2"][r * C:(r + 1) * C]], axis=0).astype(BF16)
                s["ws"].append(_dot(lhs, s_ref[d, lv0 + r].astype(BF16)))

        for s in st:
            u2 = s["sol"][:, 0:HEAD_DIM]
            vn2 = jnp.concatenate([u2[r * C:(r + 1) * C] - s["ws"][r][0:C] for r in range(2)], axis=0).astype(BF16)
            s["vn2"] = vn2
            s["o2"] = jnp.concatenate([s["ws"][r][C:2 * C] for r in range(2)], axis=0) + _dot(
                blockdiag(s["attn"]).astype(BF16), vn2)

        for s in st:
            d, lv0 = s["d"], s["lv0"]
            o_ref = dirs[d][4]
            last = C - 1 if d == 0 else 0
            kdT_p = s["kT2"] * s["dec"][last:last + 1, :]
            zkd = jnp.zeros_like(kdT_p)
            for r in range(2):
                kd_r = jnp.where(left_sq if r == 0 else jnp.logical_not(left_sq), kdT_p, zkd).astype(BF16)
                glr = jnp.broadcast_to(s["egc"][r * C + last:r * C + last + 1, :], (HEAD_DIM, HEAD_DIM))
                s_ref[d, lv0 + r] = s_ref[d, lv0 + r] * glr + _dot(kd_r, s["vn2"])
                o_ref[:, (lv0 + r) * HEAD_DIM:(lv0 + r + 1) * HEAD_DIM] = s["o2"][r * C:(r + 1) * C].astype(o_ref.dtype)


def _dn_core(qkv, gates, batch, seq, ctx_len):
    m = qkv.shape[0]
    C = DN_CHUNK
    n_lat = seq // C
    n_ctx = ctx_len // C
    n_steps = n_ctx + n_lat
    ctx0 = batch * n_lat
    khs = DN_KH_PER_STEP
    qw = khs * HEAD_DIM
    vw = 2 * khs * HEAD_DIM
    k_blk0 = DN_K_W // qw
    v_blk0 = 2 * DN_K_W // vw

    def rf(b, t):
        return jnp.where(t < n_ctx, ctx0 + b * n_ctx + t, b * n_lat + (t - n_ctx))

    def rb(b, t):
        return jnp.where(t < n_ctx, ctx0 + b * n_ctx + (n_ctx - 1 - t), b * n_lat + (n_lat - 1 - (t - n_ctx)))

    def specs(rfun):
        return [pl.BlockSpec((C, qw), lambda b, h, t: (rfun(b, t), h)),
                pl.BlockSpec((C, qw), lambda b, h, t: (rfun(b, t), k_blk0 + h)),
                pl.BlockSpec((C, vw), lambda b, h, t: (rfun(b, t), v_blk0 + h)),
                pl.BlockSpec((C, LANES), lambda b, h, t: (rfun(b, t), h))]

    return pl.pallas_call(
        _dncore_kernel,
        grid=(batch, DN_K_HEADS // khs, n_steps),
        in_specs=specs(rf) + specs(rb),
        out_specs=[pl.BlockSpec((C, vw), lambda b, h, t: (rf(b, t), h)),
                   pl.BlockSpec((C, vw), lambda b, h, t: (rb(b, t), h))],
        out_shape=[jax.ShapeDtypeStruct((m, DN_V_W), BF16), jax.ShapeDtypeStruct((m, DN_V_W), BF16)],
        scratch_shapes=[pltpu.VMEM((2, 2 * khs, HEAD_DIM, HEAD_DIM), F32)],
        compiler_params=_cparams(3),
        name="dn_core",
    )(qkv, qkv, qkv, gates, qkv, qkv, qkv, gates)


DN_OUT_K_TILE = 1024


def _dn_out_kernel(of_ref, ob_ref, z_ref, nw_ref, w_ref, x_ref, gate_ref, *rest):
    o_ref, acc_ref = rest[-2:]
    k = pl.program_id(1)
    ys = []
    for h in range(of_ref.shape[1] // HEAD_DIM):
        sl = slice(h * HEAD_DIM, (h + 1) * HEAD_DIM)
        o = of_ref[:, sl].astype(F32) + ob_ref[:, sl].astype(F32)
        n = o * lax.rsqrt(jnp.mean(o * o, axis=-1, keepdims=True) + EPS) * nw_ref[...]
        ys.append((n * _silu(z_ref[:, sl].astype(F32))).astype(BF16))
    tk = of_ref.shape[1]
    part = _dot(jnp.concatenate(ys, axis=1), w_ref[pl.ds(pl.multiple_of(k * tk, tk), tk), :])

    @pl.when(k == 0)
    def _():
        acc_ref[...] = part

    @pl.when(jnp.logical_and(k > 0, k < pl.num_programs(1) - 1))
    def _():
        acc_ref[...] += part

    @pl.when(k == pl.num_programs(1) - 1)
    def _():
        y = x_ref[...] + gate_ref[...] * (acc_ref[...] + part)
        if len(rest) == 3:
            y = y * lax.rsqrt(jnp.mean(y * y, axis=-1, keepdims=True) + EPS) * rest[0][...]
        o_ref[...] = y


def _dn_out(o_f, o_b, z, norm_w, w_out_bf16, x, mod, batch, seq, final_norm_w=None):
    tm, tk = ROW_TILE, DN_OUT_K_TILE
    m = x.shape[0] if final_norm_w is None else batch * seq
    lat_tiles = seq // tm
    extra_specs, extra_args = [], []
    if final_norm_w is not None:
        extra_specs = [pl.BlockSpec((1, D_MODEL), lambda i, k: (0, 0))]
        extra_args = [final_norm_w.reshape(1, D_MODEL)]
    return pl.pallas_call(
        _dn_out_kernel,
        grid=(m // tm, DN_V_W // tk),
        in_specs=[pl.BlockSpec((tm, tk), lambda i, k: (i, k)),
                  pl.BlockSpec((tm, tk), lambda i, k: (i, k)),
                  pl.BlockSpec((tm, tk), lambda i, k: (i, k)),
                  pl.BlockSpec((1, HEAD_DIM), lambda i, k: (0, 0)),
                  pl.BlockSpec((DN_V_W, D_MODEL), lambda i, k: (0, 0), pipeline_mode=pl.Buffered(1)),
                  pl.BlockSpec((tm, D_MODEL), lambda i, k: (i, 0)),
                  pl.BlockSpec((None, 1, D_MODEL), lambda i, k: (_mod_row(i, lat_tiles, batch), 0, 2))] + extra_specs,
        out_specs=pl.BlockSpec((tm, D_MODEL), lambda i, k: (i, 0)),
        out_shape=jax.ShapeDtypeStruct((m, D_MODEL), F32),
        scratch_shapes=[pltpu.VMEM((tm, D_MODEL), F32)],
        compiler_params=_cparams(2),
        name="dn_out",
    )(o_f, o_b, z, norm_w.reshape(1, HEAD_DIM), w_out_bf16, x, mod, *extra_args)


def _final_norm_kernel(x_ref, w_ref, o_ref):
    x = x_ref[...]
    o_ref[...] = x * lax.rsqrt(jnp.mean(x * x, axis=-1, keepdims=True) + EPS) * w_ref[...]


def _final_norm(x, w, rows):
    tm = ROW_TILE
    return pl.pallas_call(
        _final_norm_kernel,
        grid=(rows // tm,),
        in_specs=[pl.BlockSpec((tm, D_MODEL), lambda i: (i, 0)),
                  pl.BlockSpec((1, D_MODEL), lambda i: (0, 0))],
        out_specs=pl.BlockSpec((tm, D_MODEL), lambda i: (i, 0)),
        out_shape=jax.ShapeDtypeStruct((rows, D_MODEL), F32),
        compiler_params=_cparams(1),
        name="final_norm",
    )(x, w.reshape(1, D_MODEL))


def kernel(x, c, ctx, c_ctx, norm_w, ada_w, ada_b, ab_w_in, ab_w_out, sgu_w, sgu_b, q_norm_w, k_norm_w,
           dn_w_in, dn_conv_w, dn_a_log, dn_dt_bias, dn_norm_w, dn_w_out, final_norm_w):
    batch, seq, _ = x.shape
    ctx_len = ctx.shape[1]
    depth = norm_w.shape[0]
    assert ctx_len == CONV_ROW_TILE and seq % ROW_TILE == 0 and batch * ctx_len == ROW_TILE
    n_lat_rows = batch * seq

    xs = jnp.concatenate([x.reshape(n_lat_rows, D_MODEL), ctx.reshape(batch * ctx_len, D_MODEL)], axis=0)
    cond = jnp.zeros((SUBLANES, D_MODEL), F32).at[0:batch].set(c).at[batch].set(c_ctx)
    mods = _ada_mod(cond, ada_w, ada_b)
    cos_tab, sin_tab = _rope_tables(seq, CONV_ROW_TILE)

    for i in range(depth):
        j = i // 2
        mod = mods[i].reshape(SUBLANES, 1, 3 * D_MODEL)
        if i % 2 == 0:
            proj = _inproj(xs, norm_w[i], mod, ab_w_in[j].astype(BF16), batch, seq)
            qr, kr, vr = _qkprep(proj, cos_tab, sin_tab, q_norm_w[j], k_norm_w[j], batch, seq)
            ao_lat = _attention(qr, kr, vr, batch, seq, ctx_len, True)
            ao_ctx = _attention(qr, kr, vr, batch, seq, ctx_len, False)
            xs = _ab_out(proj, ao_lat, ao_ctx, sgu_w[j].astype(BF16), sgu_b[j].T, ab_w_out[j].astype(BF16), xs, mod,
                         batch, seq)
        else:
            conv_w = jnp.zeros((SUBLANES, DN_QKV_W), F32).at[0:CONV_K].set(dn_conv_w[j])
            qkv, z, ba = _inproj_dn(xs, norm_w[i], mod, dn_w_in[j].astype(BF16), conv_w, batch, seq, ctx_len)
            zeros = jnp.zeros((2, DN_V_HEADS), F32)
            alog_vec = jnp.concatenate([zeros, dn_a_log[j]], axis=1).reshape(1, LANES)
            dtb_vec = jnp.concatenate([zeros, dn_dt_bias[j]], axis=1).reshape(1, LANES)
            gates = _dn_gate(ba, alog_vec, dtb_vec)
            n_hg = DN_K_HEADS // DN_KH_PER_STEP
            nvh = 2 * DN_KH_PER_STEP
            g4 = gates.reshape(-1, 4, n_hg, nvh).transpose(0, 2, 1, 3).reshape(-1, n_hg, 4 * nvh)
            g4 = jnp.pad(g4, ((0, 0), (0, 0), (0, LANES - 4 * nvh))).reshape(-1, n_hg * LANES)
            o_f, o_b = _dn_core(qkv, g4, batch, seq, ctx_len)
            xs = _dn_out(o_f, o_b, z, dn_norm_w[j], dn_w_out[j].astype(BF16), xs, mod, batch, seq,
                         final_norm_w=final_norm_w if i == depth - 1 else None)

    if depth % 2 == 1:
        xs = _final_norm(xs, final_norm_w, n_lat_rows)
    return xs.reshape(batch, seq, D_MODEL)
```

```python
import functools

import jax
import jax.numpy as jnp
from jax import lax
from jax.experimental import pallas as pl
from jax.experimental.pallas import tpu as pltpu

F32 = jnp.float32
BF16 = jnp.bfloat16

D_MODEL = 2048
GRID_W = 64
EPS = 1e-6
HEAD_DIM = 128
A_WIDTH = D_MODEL // 2
A_GROUPS = A_WIDTH // 128
SGU_CHUNK = 128
B_HEADS = (D_MODEL // 2) // HEAD_DIM
B_KV_HEADS = B_HEADS // 4
B_GROUP = B_HEADS // B_KV_HEADS
B_Q_W = B_HEADS * HEAD_DIM
B_KV_W = B_KV_HEADS * HEAD_DIM
ROPE_THETA = 10000.0
AXIS_DIM = HEAD_DIM // 2
AB_IN_W = 3 * A_WIDTH + 2 * B_Q_W + 2 * B_KV_W
DN_K_HEADS = D_MODEL // HEAD_DIM
DN_V_HEADS = 2 * DN_K_HEADS
DN_K_W = DN_K_HEADS * HEAD_DIM
DN_V_W = DN_V_HEADS * HEAD_DIM
DN_QKV_W = 2 * DN_K_W + DN_V_W
DN_IN_W = DN_QKV_W + DN_V_W + 4 * DN_V_HEADS
DN_CHUNK = 64
CONV_K = 5

V7X_VMEM_LIMIT_BYTES = 56 * 1024 * 1024
LANES = 128
SUBLANES = 8

ROW_TILE = 512
CONV_ROW_TILE = 256


def _cparams(n_axes):
    return pltpu.CompilerParams(dimension_semantics=("arbitrary",) * n_axes,
                                vmem_limit_bytes=V7X_VMEM_LIMIT_BYTES)


def _silu(x):
    return x * jax.nn.sigmoid(x)


def _split_bf16(a):
    hi = a.astype(BF16)
    lo = (a - hi.astype(F32)).astype(BF16)
    return hi, lo


def _dot(a, b):
    return jnp.dot(a, b, preferred_element_type=F32)


def _dot_nt(a, b):
    return lax.dot_general(a, b, (((1,), (1,)), ((), ())), preferred_element_type=F32)


def _dot3(a, b):
    ah, al = _split_bf16(a)
    bh, bl = _split_bf16(b)
    return _dot(ah, bh) + _dot(ah, bl) + _dot(al, bh)


def _ada_kernel(c_ref, w_ref, b_ref, o_ref):
    s = _silu(c_ref[...])
    o_ref[0] = _dot3(s, w_ref[0]) + b_ref[0]


def _ada_mod(cond, ada_w, ada_b):
    depth = ada_w.shape[0]
    tn = 512
    return pl.pallas_call(
        _ada_kernel,
        grid=(depth, 3 * D_MODEL // tn),
        in_specs=[pl.BlockSpec((SUBLANES, D_MODEL), lambda l, j: (0, 0)),
                  pl.BlockSpec((1, D_MODEL, tn), lambda l, j: (l, 0, j)),
                  pl.BlockSpec((1, 1, tn), lambda l, j: (l, 0, j))],
        out_specs=pl.BlockSpec((1, SUBLANES, tn), lambda l, j: (l, 0, j)),
        out_shape=jax.ShapeDtypeStruct((depth, SUBLANES, 3 * D_MODEL), F32),
        compiler_params=_cparams(2),
        name="ada_mod",
    )(cond, ada_w, ada_b.reshape(depth, 1, 3 * D_MODEL))


INPROJ_COL_TILES = (512, 1024, 1408, 1536, 2816)


def _inproj_kernel(*refs, n_lat_tiles):
    nw_ref, mod_ref, w_ref, o_ref, h_ref = refs[-5:]

    @pl.when(pl.program_id(1) == 0)
    def _():
        x = _select_rows(refs[:-5], n_lat_tiles)
        y = x * lax.rsqrt(jnp.mean(x * x, axis=-1, keepdims=True) + EPS) * nw_ref[...]
        shift = mod_ref[:, 0:D_MODEL]
        scale = mod_ref[:, D_MODEL:2 * D_MODEL]
        h_ref[...] = (y * (1.0 + scale) + shift).astype(BF16)

    o_ref[...] = _dot(h_ref[...], w_ref[...]).astype(o_ref.dtype)


def _select_rows(x_refs, n_lat_tiles):
    if len(x_refs) == 1:
        return x_refs[0][...]
    return jnp.where(pl.program_id(0) < n_lat_tiles, x_refs[0][...], x_refs[1][...])


def _row_specs(x_parts, tm, width, n_lat_tiles):
    if len(x_parts) == 1:
        return [pl.BlockSpec((tm, width), lambda i, j: (i, 0))]
    assert x_parts[1].shape[0] == tm
    return [pl.BlockSpec((tm, width), lambda i, j: (jnp.minimum(i, n_lat_tiles - 1), 0)),
            pl.BlockSpec((tm, width), lambda i, j: (0, 0))]


def _mod_row(i, lat_tiles, batch):
    return jnp.minimum(i // lat_tiles, batch)


def _inproj(x_parts, norm_w, mod, w_stack_bf16, layer, batch, seq):
    m = sum(p.shape[0] for p in x_parts)
    n = w_stack_bf16.shape[2]
    tm = ROW_TILE
    tn = max(t for t in INPROJ_COL_TILES if n % t == 0)
    lat_tiles = seq // tm
    n_lat_tiles = batch * lat_tiles
    return pl.pallas_call(
        functools.partial(_inproj_kernel, n_lat_tiles=n_lat_tiles),
        grid=(m // tm, n // tn),
        in_specs=_row_specs(x_parts, tm, D_MODEL, n_lat_tiles) + [
            pl.BlockSpec((1, D_MODEL), lambda i, j: (0, 0)),
            pl.BlockSpec((None, 1, 3 * D_MODEL), lambda i, j: (_mod_row(i, lat_tiles, batch), 0, 0)),
            pl.BlockSpec((None, D_MODEL, tn), lambda i, j: (layer, 0, j))],
        out_specs=pl.BlockSpec((tm, tn), lambda i, j: (i, j)),
        out_shape=jax.ShapeDtypeStruct((m, n), BF16),
        scratch_shapes=[pltpu.VMEM((tm, D_MODEL), BF16)],
        compiler_params=_cparams(2),
        name="inproj",
    )(*x_parts, norm_w.reshape(1, D_MODEL), mod, w_stack_bf16)


Q_PRESCALE = (HEAD_DIM ** -0.5) * 1.4426950408889634


def _qkprep_kernel(q_ref, k_ref, v_ref, cos_ref, sin_ref, qw_ref, kw_ref, qo_ref, ko_ref, vo_ref):
    cos = cos_ref[...]
    sin = sin_ref[...]
    lane = lax.broadcasted_iota(jnp.int32, cos.shape, 1)
    first = (lane % (AXIS_DIM)) < (AXIS_DIM // 2)

    def prep(x, w):
        y = x * lax.rsqrt(jnp.mean(x * x, axis=-1, keepdims=True) + EPS) * w
        rot = jnp.where(first, pltpu.roll(y, HEAD_DIM - AXIS_DIM // 2, 1), pltpu.roll(y, AXIS_DIM // 2, 1))
        return y * cos + rot * sin

    for h in range(B_HEADS):
        sl = slice(h * HEAD_DIM, (h + 1) * HEAD_DIM)
        qo_ref[:, sl] = (prep(q_ref[:, sl].astype(F32), qw_ref[...]) * Q_PRESCALE).astype(qo_ref.dtype)
    for h in range(B_KV_HEADS):
        sl = slice(h * HEAD_DIM, (h + 1) * HEAD_DIM)
        ko_ref[:, sl] = prep(k_ref[:, sl].astype(F32), kw_ref[...]).astype(ko_ref.dtype)
    vo_ref[...] = v_ref[...].astype(vo_ref.dtype)


def _qkprep(proj, cos_tab, sin_tab, qn_w, kn_w, batch, seq):
    m = proj.shape[0]
    tm = CONV_ROW_TILE
    lat_tiles = seq // tm
    n_lat = batch * lat_tiles
    q_blk = (3 * A_WIDTH) // B_Q_W
    k_blk = (3 * A_WIDTH + B_Q_W) // B_KV_W

    def tab_idx(i):
        return (jnp.where(i < n_lat, i % lat_tiles, lat_tiles), 0)

    def kv_idx(i):
        lat_blk = (i // lat_tiles) * (lat_tiles + 1) + 1 + i % lat_tiles
        return (jnp.where(i < n_lat, lat_blk, (i - n_lat) * (lat_tiles + 1)), 0)

    return pl.pallas_call(
        _qkprep_kernel,
        grid=(m // tm,),
        in_specs=[pl.BlockSpec((tm, B_Q_W), lambda i: (i, q_blk)),
                  pl.BlockSpec((tm, B_KV_W), lambda i: (i, k_blk)),
                  pl.BlockSpec((tm, B_KV_W), lambda i: (i, k_blk + 1)),
                  pl.BlockSpec((tm, HEAD_DIM), tab_idx),
                  pl.BlockSpec((tm, HEAD_DIM), tab_idx),
                  pl.BlockSpec((1, HEAD_DIM), lambda i: (0, 0)),
                  pl.BlockSpec((1, HEAD_DIM), lambda i: (0, 0))],
        out_specs=[pl.BlockSpec((tm, B_Q_W), lambda i: (i, 0)),
                   pl.BlockSpec((tm, B_KV_W), kv_idx),
                   pl.BlockSpec((tm, B_KV_W), kv_idx)],
        out_shape=[jax.ShapeDtypeStruct((m, B_Q_W), BF16),
                   jax.ShapeDtypeStruct((m, B_KV_W), BF16),
                   jax.ShapeDtypeStruct((m, B_KV_W), BF16)],
        compiler_params=_cparams(1),
        name="qk_prep",
    )(proj, proj, proj, cos_tab, sin_tab, qn_w.reshape(1, HEAD_DIM), kn_w.reshape(1, HEAD_DIM))


def _rope_tables(seq, tm):
    rows = seq // GRID_W
    row = jnp.repeat(jnp.arange(rows), GRID_W).astype(F32)
    col = jnp.tile(jnp.arange(GRID_W), rows).astype(F32)
    freqs = ROPE_THETA ** (-jnp.arange(0, AXIS_DIM, 2, dtype=F32) / AXIS_DIM)
    ang_r = row[:, None] * freqs[None, :]
    ang_c = col[:, None] * freqs[None, :]
    ang = jnp.concatenate([ang_r, ang_r, ang_c, ang_c], axis=-1)
    sign = jnp.where((jnp.arange(HEAD_DIM) % AXIS_DIM) < AXIS_DIM // 2, -1.0, 1.0).astype(F32)
    cos = jnp.concatenate([jnp.cos(ang), jnp.ones((tm, HEAD_DIM), F32)], axis=0)
    sin = jnp.concatenate([jnp.sin(ang) * sign[None, :], jnp.zeros((tm, HEAD_DIM), F32)], axis=0)
    return cos, sin


ATTN_Q_TILE = 256
ATTN_KV_BODY = 4096
ATTN_KV_TAIL = 768


def _kv_chunks(kv_len):
    tail = min(ATTN_KV_TAIL, kv_len)
    body = kv_len - tail
    sizes = [ATTN_KV_BODY] * (body // ATTN_KV_BODY) + ([body % ATTN_KV_BODY] if body % ATTN_KV_BODY else []) + [tail]
    assert all(c % LANES == 0 for c in sizes) and sum(sizes) == kv_len
    return tuple(sizes)


def _attn_kernel(q_ref, k_ref, v_ref, o_ref, m_scr, l_scr, acc_scr, *, chunks, tq):
    q = jnp.concatenate([q_ref[:, g * HEAD_DIM:(g + 1) * HEAD_DIM] for g in range(B_GROUP)], axis=0)
    m_scr[...] = jnp.full(m_scr.shape, -jnp.inf, F32)
    l_scr[...] = jnp.zeros(l_scr.shape, F32)
    acc_scr[...] = jnp.zeros(acc_scr.shape, F32)
    start = 0
    for tk in chunks:
        k = k_ref[start:start + tk, :]
        v = v_ref[start:start + tk, :]
        start += tk
        s = _dot_nt(q, k)
        m_prev = m_scr[...]
        m_next = jnp.maximum(m_prev, jnp.max(s, axis=1, keepdims=True))
        p = jnp.exp2(s - jnp.concatenate([m_next] * (tk // LANES), axis=1))
        alpha = jnp.exp2(m_prev - m_next)
        pv = _dot(p.astype(BF16), jnp.concatenate([v, jnp.ones_like(v)], axis=1))
        l_scr[...] = alpha * l_scr[...] + pv[:, HEAD_DIM:]
        acc_scr[...] = acc_scr[...] * alpha + pv[:, :HEAD_DIM]
        m_scr[...] = m_next
    out = acc_scr[...] / l_scr[...]
    for g in range(B_GROUP):
        o_ref[:, g * HEAD_DIM:(g + 1) * HEAD_DIM] = out[g * tq:(g + 1) * tq].astype(o_ref.dtype)


def _attention(qr, kr, vr, batch, seq, ctx_len, latent):
    tq = ATTN_Q_TILE
    gw = B_GROUP * HEAD_DIM
    q_len = seq if latent else ctx_len
    q_tiles = q_len // tq
    q_row0 = 0 if latent else batch * seq // tq
    kv_len = ctx_len + seq
    if latent:
        kv_rows = kv_len
        kv_idx = lambda b, h, i: (b, h)
    else:
        kv_rows = ctx_len
        kv_idx = lambda b, h, i: (b * (kv_len // ctx_len), h)
    rows = B_GROUP * tq
    return pl.pallas_call(
        functools.partial(_attn_kernel, chunks=_kv_chunks(kv_rows), tq=tq),
        grid=(batch, B_KV_HEADS, q_tiles),
        in_specs=[pl.BlockSpec((tq, gw), lambda b, h, i: (q_row0 + b * q_tiles + i, h)),
                  pl.BlockSpec((kv_rows, HEAD_DIM), kv_idx),
                  pl.BlockSpec((kv_rows, HEAD_DIM), kv_idx)],
        out_specs=pl.BlockSpec((tq, gw), lambda b, h, i: (b * q_tiles + i, h)),
        out_shape=jax.ShapeDtypeStruct((batch * q_len, B_Q_W), BF16),
        scratch_shapes=[pltpu.VMEM((rows, LANES), F32), pltpu.VMEM((rows, LANES), F32),
                        pltpu.VMEM((rows, HEAD_DIM), F32)],
        compiler_params=_cparams(3),
        name="attn_lat" if latent else "attn_ctx",
    )(qr, kr, vr)


def _ab_out_kernel(u_ref, v_ref, ga_ref, gb0_ref, gb1_ref, aol_ref, aoc_ref, ws_ref, bs_ref, w_ref, gate_ref, *rest,
                   n_lat_tiles):
    o_ref, y_ref, acc_ref = rest[-3:]
    k = pl.program_id(1)

    @pl.when(k == 0)
    def _():
        tm = u_ref.shape[0]
        for c in range(tm // SGU_CHUNK):
            rows = slice(c * SGU_CHUNK, (c + 1) * SGU_CHUNK)
            for g in range(A_GROUPS):
                cols = slice(g * LANES, (g + 1) * LANES)
                vg = v_ref[rows, cols].astype(F32)
                d = vg - jnp.mean(vg, axis=-1, keepdims=True)
                var = jnp.mean(d * d, axis=-1, keepdims=True)
                vn = (d * lax.rsqrt(var + 1e-5)).astype(BF16)
                mixed = _dot(ws_ref[g], vn) + bs_ref[:, g:g + 1]
                y_ref[rows, cols] = (u_ref[rows, cols].astype(F32) * mixed
                                     * _silu(ga_ref[rows, cols].astype(F32))).astype(BF16)
        acc_ref[...] = _dot(y_ref[...], w_ref[0:A_WIDTH, :])

    @pl.when(k == 1)
    def _():
        ao = jnp.where(pl.program_id(0) < n_lat_tiles, aol_ref[...], aoc_ref[...]).astype(F32)
        gb = jnp.concatenate([gb0_ref[...], gb1_ref[...]], axis=1).astype(F32)
        y = (ao * _silu(gb)).astype(BF16)
        o_ref[...] = _select_rows(rest[:-3], n_lat_tiles) + gate_ref[...] * (acc_ref[...] + _dot(y, w_ref[A_WIDTH:, :]))


def _ab_out(proj, ao_lat, ao_ctx, sgu_w_stack_bf16, sgu_b_t, w_out_stack_bf16, layer, x_parts, mod, batch, seq):
    m = proj.shape[0]
    tm = ROW_TILE
    lat_tiles = seq // tm
    n_lat_tiles = batch * lat_tiles
    assert ao_ctx.shape[0] == tm
    gb_blk0 = (3 * A_WIDTH + B_Q_W + 2 * B_KV_W) // (B_Q_W // 2)
    return pl.pallas_call(
        functools.partial(_ab_out_kernel, n_lat_tiles=n_lat_tiles),
        grid=(m // tm, 2),
        in_specs=[pl.BlockSpec((tm, A_WIDTH), lambda i, k: (i, 0)),
                  pl.BlockSpec((tm, A_WIDTH), lambda i, k: (i, 1)),
                  pl.BlockSpec((tm, A_WIDTH), lambda i, k: (i, 2)),
                  pl.BlockSpec((tm, B_Q_W // 2), lambda i, k: (i, gb_blk0)),
                  pl.BlockSpec((tm, B_Q_W // 2), lambda i, k: (i, gb_blk0 + 1)),
                  pl.BlockSpec((tm, B_Q_W), lambda i, k: (jnp.minimum(i, n_lat_tiles - 1), 0)),
                  pl.BlockSpec((tm, B_Q_W), lambda i, k: (0, 0)),
                  pl.BlockSpec((None, A_GROUPS, SGU_CHUNK, SGU_CHUNK), lambda i, k: (layer, 0, 0, 0)),
                  pl.BlockSpec((SGU_CHUNK, A_GROUPS), lambda i, k: (0, 0)),
                  pl.BlockSpec((None, A_WIDTH + B_Q_W, D_MODEL), lambda i, k: (layer, 0, 0),
                               pipeline_mode=pl.Buffered(1)),
                  pl.BlockSpec((None, 1, D_MODEL), lambda i, k: (_mod_row(i, lat_tiles, batch), 0, 2))]
        + _row_specs(x_parts, tm, D_MODEL, n_lat_tiles),
        out_specs=pl.BlockSpec((tm, D_MODEL), lambda i, k: (i, 0)),
        out_shape=jax.ShapeDtypeStruct((m, D_MODEL), F32),
        scratch_shapes=[pltpu.VMEM((tm, A_WIDTH), BF16), pltpu.VMEM((tm, D_MODEL), F32)],
        compiler_params=_cparams(2),
        name="ab_out",
    )(proj, proj, proj, proj, proj, ao_lat, ao_ctx, sgu_w_stack_bf16, sgu_b_t, w_out_stack_bf16, mod, *x_parts)


DN_HALO = 2 * SUBLANES
DN_PROJ_COL_TILE = 2048


def _inproj_dn_kernel(xp_ref, x_ref, xn_ref, nw_ref, mod_ref, w_ref, ws_ref, cw_ref, qkv_ref, z_ref, ba_ref, h_ref, p_ref,
                      *, lat_tiles, n_lat, q_tiles, qk_tiles, qkv_tiles, ctx_len):
    i = pl.program_id(0)
    j = pl.program_id(1)
    tm = x_ref.shape[0]
    is_lat = i < n_lat
    pos = i % lat_tiles
    first = jnp.logical_or(jnp.logical_not(is_lat), pos == 0)
    last = jnp.logical_or(jnp.logical_not(is_lat), pos == lat_tiles - 1)
    half = CONV_K // 2

    @pl.when(j == 0)
    def _():
        shift = mod_ref[:, 0:D_MODEL]
        scale = mod_ref[:, D_MODEL:2 * D_MODEL]

        def norm_mod(x):
            y = x * lax.rsqrt(jnp.mean(x * x, axis=-1, keepdims=True) + EPS) * nw_ref[...]
            return y * (1.0 + scale) + shift

        h_ref[0:DN_HALO, :] = jnp.where(first, 0.0, norm_mod(xp_ref[...])).astype(BF16)
        h_ref[DN_HALO:DN_HALO + tm, :] = norm_mod(x_ref[...]).astype(BF16)
        h_ref[DN_HALO + tm:, :] = jnp.where(last, 0.0, norm_mod(xn_ref[...])).astype(BF16)
        ba_ref[...] = _dot(h_ref[DN_HALO:DN_HALO + tm, :], ws_ref[...])

    n_ext = h_ref.shape[0]
    tn = w_ref.shape[1]
    pair_w = 2 * HEAD_DIM
    row = lax.broadcasted_iota(jnp.int32, (tm, HEAD_DIM), 0)
    is_ctx = jnp.logical_not(is_lat)

    def conv_silu(ph, sl):
        acc = cw_ref[half:half + 1, sl] * ph[DN_HALO:DN_HALO + tm]
        for t in range(CONV_K):
            off = t - half
            if off == 0:
                continue
            tap = pltpu.roll(ph, (-off) % n_ext, 0)[DN_HALO:DN_HALO + tm]
            crosses = (jnp.logical_and(row >= ctx_len - off, row < ctx_len) if off > 0
                       else jnp.logical_and(row >= ctx_len, row < ctx_len - off))
            tap = jnp.where(jnp.logical_and(is_ctx, crosses), 0.0, tap)
            acc = acc + cw_ref[t:t + 1, sl] * tap
        return _silu(acc)

    def for_each_head(store):
        for n, c0 in enumerate(range(0, tn, pair_w)):
            p_ref[n % 2] = _dot(h_ref[...], w_ref[:, c0:c0 + pair_w])
            for h0 in range(0, pair_w, HEAD_DIM):
                sl = slice(c0 + h0, c0 + h0 + HEAD_DIM)
                store(sl, conv_silu(p_ref[n % 2, :, h0:h0 + HEAD_DIM], sl))

    @pl.when(j < qk_tiles)
    def _():
        qk_scale = jnp.where(j < q_tiles, HEAD_DIM ** -0.5, 1.0)

        def store(sl, y):
            qkv_ref[:, sl] = (y * (lax.rsqrt(jnp.sum(y * y, axis=-1, keepdims=True) + EPS) * qk_scale)).astype(qkv_ref.dtype)
        for_each_head(store)

    @pl.when(jnp.logical_and(j >= qk_tiles, j < qkv_tiles))
    def _():
        def store(sl, y):
            qkv_ref[:, sl] = y.astype(qkv_ref.dtype)
        for_each_head(store)

    @pl.when(j >= qkv_tiles)
    def _():
        z_ref[...] = _dot(h_ref[DN_HALO:DN_HALO + tm, :], w_ref[...]).astype(z_ref.dtype)


def _inproj_dn(x, norm_w, mod, w_stack_bf16, layer, conv_w, batch, seq, ctx_len):
    m = x.shape[0]
    tm, tn = ROW_TILE, DN_PROJ_COL_TILE
    n_side = 4 * DN_V_HEADS
    n = w_stack_bf16.shape[2] - n_side
    assert n == DN_QKV_W + DN_V_W and tm == 2 * ctx_len
    lat_tiles = seq // tm
    n_lat = batch * lat_tiles
    halo_per_tile = tm // DN_HALO
    n_halo = m // DN_HALO
    qkv_tiles = DN_QKV_W // tn
    kern = functools.partial(_inproj_dn_kernel, lat_tiles=lat_tiles, n_lat=n_lat, q_tiles=DN_K_W // tn,
                             qk_tiles=2 * DN_K_W // tn, qkv_tiles=qkv_tiles, ctx_len=ctx_len)
    return pl.pallas_call(
        kern,
        grid=(m // tm, n // tn),
        in_specs=[pl.BlockSpec((DN_HALO, D_MODEL), lambda i, j: (jnp.maximum(i * halo_per_tile - 1, 0), 0)),
                  pl.BlockSpec((tm, D_MODEL), lambda i, j: (i, 0)),
                  pl.BlockSpec((DN_HALO, D_MODEL), lambda i, j: (jnp.minimum((i + 1) * halo_per_tile, n_halo - 1), 0)),
                  pl.BlockSpec((1, D_MODEL), lambda i, j: (0, 0)),
                  pl.BlockSpec((None, 1, 3 * D_MODEL), lambda i, j: (_mod_row(i, lat_tiles, batch), 0, 0)),
                  pl.BlockSpec((None, D_MODEL, tn), lambda i, j: (layer, 0, j)),
                  pl.BlockSpec((None, D_MODEL, n_side), lambda i, j: (layer, 0, n // n_side)),
                  pl.BlockSpec((SUBLANES, tn), lambda i, j: (0, jnp.minimum(j, qkv_tiles - 1)))],
        out_specs=[pl.BlockSpec((tm, tn), lambda i, j: (i, jnp.minimum(j, qkv_tiles - 1))),
                   pl.BlockSpec((tm, tn), lambda i, j: (i, jnp.maximum(j - qkv_tiles, 0))),
                   pl.BlockSpec((tm, n_side), lambda i, j: (i, 0))],
        out_shape=[jax.ShapeDtypeStruct((m, DN_QKV_W), F32),
                   jax.ShapeDtypeStruct((m, DN_V_W), BF16),
                   jax.ShapeDtypeStruct((m, n_side), F32)],
        scratch_shapes=[pltpu.VMEM((tm + 2 * DN_HALO, D_MODEL), BF16),
                        pltpu.VMEM((2, tm + 2 * DN_HALO, 2 * HEAD_DIM), F32)],
        compiler_params=_cparams(2),
        name="inproj_dn",
    )(x, x, x, norm_w.reshape(1, D_MODEL), mod, w_stack_bf16, w_stack_bf16, conv_w)


def _dngate_kernel(ba_ref, alog_ref, dtb_ref, o_ref):
    ba = ba_ref[...]
    tm = ba.shape[0]
    lane = lax.broadcasted_iota(jnp.int32, (DN_CHUNK, LANES), 1)
    is_beta = (lane // DN_V_HEADS) % 2 == 0
    is_fwd = lane < 2 * DN_V_HEADS
    z = ba + dtb_ref[...]
    softplus = jnp.maximum(z, 0.0) + jnp.log1p(jnp.exp(-jnp.abs(z)))
    g = -jnp.exp(alog_ref[...]) * softplus
    r = lax.broadcasted_iota(jnp.int32, (DN_CHUNK, DN_CHUNK), 0)
    c = lax.broadcasted_iota(jnp.int32, (DN_CHUNK, DN_CHUNK), 1)
    tri_lo = (r >= c).astype(BF16)
    tri_up = (r <= c).astype(BF16)
    beta = jax.nn.sigmoid(ba)
    for ch in range(tm // DN_CHUNK):
        rows = slice(ch * DN_CHUNK, (ch + 1) * DN_CHUNK)
        gch = g[rows]
        g1 = gch.astype(BF16)
        r1 = gch - g1.astype(F32)
        g2 = r1.astype(BF16)
        g3 = (r1 - g2.astype(F32)).astype(BF16)
        pre = _dot(tri_lo, g1) + _dot(tri_lo, g2) + _dot(tri_lo, g3)
        suf = _dot(tri_up, g1) + _dot(tri_up, g2) + _dot(tri_up, g3)
        gc = jnp.where(is_fwd, pre, suf)
        o_ref[rows, :] = jnp.where(is_beta, beta[rows], gc)


def _dn_gate(ba, alog_vec, dtb_vec):
    m = ba.shape[0]
    tm = ROW_TILE
    return pl.pallas_call(
        _dngate_kernel,
        grid=(m // tm,),
        in_specs=[pl.BlockSpec((tm, LANES), lambda i: (i, 0)),
                  pl.BlockSpec((1, LANES), lambda i: (0, 0)),
                  pl.BlockSpec((1, LANES), lambda i: (0, 0))],
        out_specs=pl.BlockSpec((tm, LANES), lambda i: (i, 0)),
        out_shape=jax.ShapeDtypeStruct((m, LANES), F32),
        compiler_params=_cparams(1),
        name="dn_gate",
    )(ba, alog_vec, dtb_vec)


DN_CHAIN_GROUP = 16
DN_KH_PER_STEP = 16


def _dncore_kernel(qf_ref, kf_ref, vf_ref, gf_ref, qb_ref, kb_ref, vb_ref, gb_ref, of_ref, ob_ref, s_ref):
    C = DN_CHUNK
    nvh = 2 * DN_KH_PER_STEP

    @pl.when(pl.program_id(2) == 0)
    def _():
        s_ref[...] = jnp.zeros(s_ref.shape, F32)

    lane = lax.broadcasted_iota(jnp.int32, (C, 2 * C), 1)
    left = lane < C
    row = lax.broadcasted_iota(jnp.int32, (C, 2 * C), 0)
    colp = lane % C
    left_sq = lax.broadcasted_iota(jnp.int32, (2 * C, 2 * C), 1) < C
    eye2 = (row == colp).astype(F32)

    def blockdiag(p):
        z = jnp.zeros_like(p)
        return jnp.concatenate([jnp.where(left, p, z), jnp.where(left, z, p)], axis=0)

    def packed_mm(a, b):
        return _dot(a.astype(BF16), blockdiag(b.astype(BF16)))

    dirs = ((qf_ref, kf_ref, vf_ref, gf_ref, of_ref), (qb_ref, kb_ref, vb_ref, gb_ref, ob_ref))
    chains = [(d, kh) for d in range(2) for kh in range(DN_KH_PER_STEP)]
    G = [dirs[d][3][...] for d in range(2)]
    GT = [jnp.concatenate([g, g], axis=0).T for g in G]

    for g0 in range(0, len(chains), DN_CHAIN_GROUP):
        group = chains[g0:g0 + DN_CHAIN_GROUP]
        st = []
        for d, kh in group:
            q_ref, k_ref, v_ref, _, _ = dirs[d]
            base_beta = d * 2 * nvh
            base_gc = base_beta + nvh
            lv0 = 2 * kh
            q = q_ref[:, kh * HEAD_DIM:(kh + 1) * HEAD_DIM]
            k = k_ref[:, kh * HEAD_DIM:(kh + 1) * HEAD_DIM]
            k2 = jnp.concatenate([k, k], axis=0)
            kT2 = k2.T
            gram = _dot(jnp.concatenate([q, k], axis=0).astype(BF16), kT2.astype(BF16))
            b0, b1 = G[d][:, base_beta + lv0:base_beta + lv0 + 1], G[d][:, base_beta + lv0 + 1:base_beta + lv0 + 2]
            c0, c1 = G[d][:, base_gc + lv0:base_gc + lv0 + 1], G[d][:, base_gc + lv0 + 1:base_gc + lv0 + 2]
            r0, r1 = GT[d][base_gc + lv0:base_gc + lv0 + 1, :], GT[d][base_gc + lv0 + 1:base_gc + lv0 + 2, :]
            st.append(dict(d=d, lv0=lv0, q=q, k2=k2, kT2=kT2, gram=gram, b0=b0, b1=b1, c0=c0, c1=c1, r0=r0, r1=r1))

        for s in st:
            d = s["d"]
            incl = (row >= colp) if d == 0 else (row <= colp)
            strict = (row > colp) if d == 0 else (row < colp)
            gcol_p = jnp.where(left, s["c0"], s["c1"])
            grow_p = jnp.where(left[0:1], s["r0"], s["r1"])
            beta_p = jnp.where(left, s["b0"], s["b1"])
            dec = jnp.exp(jnp.where(incl, gcol_p - grow_p, -1e30))
            s["dec"] = dec
            s["attn"] = s["gram"][0:C] * dec
            s["L"] = jnp.where(strict, s["gram"][C:2 * C] * dec, 0.0) * beta_p

        for s in st:
            n1 = jnp.where(jnp.logical_and(row // 2 == colp // 2, row != colp), s["L"], 0.0)
            s["X"] = eye2 - n1
        blk = 2
        while blk < C:
            mask = jnp.logical_and(row // (2 * blk) == colp // (2 * blk), row // blk != colp // blk)
            for s in st:
                s["Y"] = packed_mm(s["X"], jnp.where(mask, s["L"], 0.0))
            for s in st:
                s["X"] = s["X"] - packed_mm(s["Y"], s["X"])
            blk *= 2

        for s in st:
            d, lv0 = s["d"], s["lv0"]
            v_ref = dirs[d][2]
            beta_r = jnp.concatenate([s["b0"], s["b1"]], axis=0)
            egc_r = jnp.exp(jnp.concatenate([s["c0"], s["c1"]], axis=0))
            v2 = jnp.concatenate([v_ref[:, lv0 * HEAD_DIM:(lv0 + 1) * HEAD_DIM],
                                  v_ref[:, (lv0 + 1) * HEAD_DIM:(lv0 + 2) * HEAD_DIM]], axis=0)
            rhs = jnp.concatenate([v2 * beta_r, s["k2"] * (beta_r * egc_r)], axis=1)
            s["sol"] = _dot(blockdiag(s["X"]).astype(BF16), rhs.astype(BF16))
            s["egc"] = egc_r
            s["qg2"] = jnp.concatenate([s["q"], s["q"]], axis=0) * egc_r

        for s in st:
            d, lv0 = s["d"], s["lv0"]
            w2 = s["sol"][:, HEAD_DIM:]
            s["ws"] = []
            for r in range(2):
                lhs = jnp.concatenate([w2[r * C:(r + 1) * C], s["qg2"][r * C:(r + 1) * C]], axis=0).astype(BF16)
                s["ws"].append(_dot(lhs, s_ref[d, lv0 + r].astype(BF16)))

        for s in st:
            u2 = s["sol"][:, 0:HEAD_DIM]
            vn2 = jnp.concatenate([u2[r * C:(r + 1) * C] - s["ws"][r][0:C] for r in range(2)], axis=0).astype(BF16)
            s["vn2"] = vn2
            s["o2"] = jnp.concatenate([s["ws"][r][C:2 * C] for r in range(2)], axis=0) + _dot(
                blockdiag(s["attn"]).astype(BF16), vn2)

        for s in st:
            d, lv0 = s["d"], s["lv0"]
            o_ref = dirs[d][4]
            last = C - 1 if d == 0 else 0
            kdT_p = s["kT2"] * s["dec"][last:last + 1, :]
            zkd = jnp.zeros_like(kdT_p)
            for r in range(2):
                kd_r = jnp.where(left_sq if r == 0 else jnp.logical_not(left_sq), kdT_p, zkd).astype(BF16)
                glr = jnp.broadcast_to(s["egc"][r * C + last:r * C + last + 1, :], (HEAD_DIM, HEAD_DIM))
                s_ref[d, lv0 + r] = s_ref[d, lv0 + r] * glr + _dot(kd_r, s["vn2"])
                o_ref[:, (lv0 + r) * HEAD_DIM:(lv0 + r + 1) * HEAD_DIM] = s["o2"][r * C:(r + 1) * C].astype(o_ref.dtype)


def _dn_core(qkv, gates, batch, seq, ctx_len):
    m = qkv.shape[0]
    C = DN_CHUNK
    n_lat = seq // C
    n_ctx = ctx_len // C
    n_steps = n_ctx + n_lat
    ctx0 = batch * n_lat
    khs = DN_KH_PER_STEP
    qw = khs * HEAD_DIM
    vw = 2 * khs * HEAD_DIM
    k_blk0 = DN_K_W // qw
    v_blk0 = 2 * DN_K_W // vw

    def rf(b, t):
        return jnp.where(t < n_ctx, ctx0 + b * n_ctx + t, b * n_lat + (t - n_ctx))

    def rb(b, t):
        return jnp.where(t < n_ctx, ctx0 + b * n_ctx + (n_ctx - 1 - t), b * n_lat + (n_lat - 1 - (t - n_ctx)))

    def specs(rfun):
        return [pl.BlockSpec((C, qw), lambda b, h, t: (rfun(b, t), h)),
                pl.BlockSpec((C, qw), lambda b, h, t: (rfun(b, t), k_blk0 + h)),
                pl.BlockSpec((C, vw), lambda b, h, t: (rfun(b, t), v_blk0 + h)),
                pl.BlockSpec((C, LANES), lambda b, h, t: (rfun(b, t), h))]

    return pl.pallas_call(
        _dncore_kernel,
        grid=(batch, DN_K_HEADS // khs, n_steps),
        in_specs=specs(rf) + specs(rb),
        out_specs=[pl.BlockSpec((C, vw), lambda b, h, t: (rf(b, t), h)),
                   pl.BlockSpec((C, vw), lambda b, h, t: (rb(b, t), h))],
        out_shape=[jax.ShapeDtypeStruct((m, DN_V_W), BF16), jax.ShapeDtypeStruct((m, DN_V_W), BF16)],
        scratch_shapes=[pltpu.VMEM((2, 2 * khs, HEAD_DIM, HEAD_DIM), F32)],
        compiler_params=_cparams(3),
        name="dn_core",
    )(qkv, qkv, qkv, gates, qkv, qkv, qkv, gates)


DN_OUT_K_TILE = 1024


def _dn_out_kernel(of_ref, ob_ref, z_ref, nw_ref, w_ref, x_ref, gate_ref, *rest):
    o_ref, acc_ref = rest[-2:]
    k = pl.program_id(1)
    ys = []
    for h in range(of_ref.shape[1] // HEAD_DIM):
        sl = slice(h * HEAD_DIM, (h + 1) * HEAD_DIM)
        o = of_ref[:, sl].astype(F32) + ob_ref[:, sl].astype(F32)
        n = o * lax.rsqrt(jnp.mean(o * o, axis=-1, keepdims=True) + EPS) * nw_ref[...]
        ys.append((n * _silu(z_ref[:, sl].astype(F32))).astype(BF16))
    tk = of_ref.shape[1]
    part = _dot(jnp.concatenate(ys, axis=1), w_ref[pl.ds(pl.multiple_of(k * tk, tk), tk), :])

    @pl.when(k == 0)
    def _():
        acc_ref[...] = part

    @pl.when(jnp.logical_and(k > 0, k < pl.num_programs(1) - 1))
    def _():
        acc_ref[...] += part

    @pl.when(k == pl.num_programs(1) - 1)
    def _():
        y = x_ref[...] + gate_ref[...] * (acc_ref[...] + part)
        if len(rest) == 3:
            y = y * lax.rsqrt(jnp.mean(y * y, axis=-1, keepdims=True) + EPS) * rest[0][...]
        o_ref[...] = y


def _dn_out(o_f, o_b, z, norm_w, w_out_stack_bf16, layer, x, mod, batch, seq, final_norm_w=None):
    tm, tk = ROW_TILE, DN_OUT_K_TILE
    m = x.shape[0] if final_norm_w is None else batch * seq
    lat_tiles = seq // tm
    extra_specs, extra_args = [], []
    if final_norm_w is not None:
        extra_specs = [pl.BlockSpec((1, D_MODEL), lambda i, k: (0, 0))]
        extra_args = [final_norm_w.reshape(1, D_MODEL)]
    return pl.pallas_call(
        _dn_out_kernel,
        grid=(m // tm, DN_V_W // tk),
        in_specs=[pl.BlockSpec((tm, tk), lambda i, k: (i, k)),
                  pl.BlockSpec((tm, tk), lambda i, k: (i, k)),
                  pl.BlockSpec((tm, tk), lambda i, k: (i, k)),
                  pl.BlockSpec((1, HEAD_DIM), lambda i, k: (0, 0)),
                  pl.BlockSpec((None, DN_V_W, D_MODEL), lambda i, k: (layer, 0, 0), pipeline_mode=pl.Buffered(1)),
                  pl.BlockSpec((tm, D_MODEL), lambda i, k: (i, 0)),
                  pl.BlockSpec((None, 1, D_MODEL), lambda i, k: (_mod_row(i, lat_tiles, batch), 0, 2))] + extra_specs,
        out_specs=pl.BlockSpec((tm, D_MODEL), lambda i, k: (i, 0)),
        out_shape=jax.ShapeDtypeStruct((m, D_MODEL), F32),
        scratch_shapes=[pltpu.VMEM((tm, D_MODEL), F32)],
        compiler_params=_cparams(2),
        name="dn_out",
    )(o_f, o_b, z, norm_w.reshape(1, HEAD_DIM), w_out_stack_bf16, x, mod, *extra_args)


def _final_norm_kernel(x_ref, w_ref, o_ref):
    x = x_ref[...]
    o_ref[...] = x * lax.rsqrt(jnp.mean(x * x, axis=-1, keepdims=True) + EPS) * w_ref[...]


def _final_norm(x, w, rows):
    tm = ROW_TILE
    return pl.pallas_call(
        _final_norm_kernel,
        grid=(rows // tm,),
        in_specs=[pl.BlockSpec((tm, D_MODEL), lambda i: (i, 0)),
                  pl.BlockSpec((1, D_MODEL), lambda i: (0, 0))],
        out_specs=pl.BlockSpec((tm, D_MODEL), lambda i: (i, 0)),
        out_shape=jax.ShapeDtypeStruct((rows, D_MODEL), F32),
        compiler_params=_cparams(1),
        name="final_norm",
    )(x, w.reshape(1, D_MODEL))


def kernel(x, c, ctx, c_ctx, norm_w, ada_w, ada_b, ab_w_in, ab_w_out, sgu_w, sgu_b, q_norm_w, k_norm_w,
           dn_w_in, dn_conv_w, dn_a_log, dn_dt_bias, dn_norm_w, dn_w_out, final_norm_w):
    batch, seq, _ = x.shape
    ctx_len = ctx.shape[1]
    depth = norm_w.shape[0]
    assert ctx_len == CONV_ROW_TILE and seq % ROW_TILE == 0 and batch * ctx_len == ROW_TILE
    n_lat_rows = batch * seq

    x_parts = (x.reshape(n_lat_rows, D_MODEL), ctx.reshape(batch * ctx_len, D_MODEL))
    ab_w_in_bf, ab_w_out_bf, sgu_w_bf = ab_w_in.astype(BF16), ab_w_out.astype(BF16), sgu_w.astype(BF16)
    dn_w_in_bf, dn_w_out_bf = dn_w_in.astype(BF16), dn_w_out.astype(BF16)
    cond = jnp.zeros((SUBLANES, D_MODEL), F32).at[0:batch].set(c).at[batch].set(c_ctx)
    mods = _ada_mod(cond, ada_w, ada_b)
    cos_tab, sin_tab = _rope_tables(seq, CONV_ROW_TILE)

    for i in range(depth):
        j = i // 2
        mod = mods[i].reshape(SUBLANES, 1, 3 * D_MODEL)
        if i % 2 == 0:
            proj = _inproj(x_parts, norm_w[i], mod, ab_w_in_bf, j, batch, seq)
            qr, kr, vr = _qkprep(proj, cos_tab, sin_tab, q_norm_w[j], k_norm_w[j], batch, seq)
            ao_lat = _attention(qr, kr, vr, batch, seq, ctx_len, True)
            ao_ctx = _attention(qr, kr, vr, batch, seq, ctx_len, False)
            x_parts = (_ab_out(proj, ao_lat, ao_ctx, sgu_w_bf, sgu_b[j].T, ab_w_out_bf, j, x_parts, mod, batch, seq),)
        else:
            conv_w = jnp.zeros((SUBLANES, DN_QKV_W), F32).at[0:CONV_K].set(dn_conv_w[j])
            assert len(x_parts) == 1
            qkv, z, ba = _inproj_dn(x_parts[0], norm_w[i], mod, dn_w_in_bf, j, conv_w, batch, seq, ctx_len)
            zeros = jnp.zeros((2, DN_V_HEADS), F32)
            alog_vec = jnp.concatenate([zeros, dn_a_log[j]], axis=1).reshape(1, LANES)
            dtb_vec = jnp.concatenate([zeros, dn_dt_bias[j]], axis=1).reshape(1, LANES)
            gates = _dn_gate(ba, alog_vec, dtb_vec)
            n_hg = DN_K_HEADS // DN_KH_PER_STEP
            nvh = 2 * DN_KH_PER_STEP
            g4 = gates
            if n_hg > 1:
                g4 = gates.reshape(-1, 4, n_hg, nvh).transpose(0, 2, 1, 3).reshape(-1, n_hg, 4 * nvh)
                g4 = jnp.pad(g4, ((0, 0), (0, 0), (0, LANES - 4 * nvh))).reshape(-1, n_hg * LANES)
            o_f, o_b = _dn_core(qkv, g4, batch, seq, ctx_len)
            x_parts = (_dn_out(o_f, o_b, z, dn_norm_w[j], dn_w_out_bf, j, x_parts[0], mod, batch, seq,
                               final_norm_w=final_norm_w if i == depth - 1 else None),)

    xs = x_parts[0]
    if depth % 2 == 1:
        xs = _final_norm(xs, final_norm_w, n_lat_rows)
    return xs.reshape(batch, seq, D_MODEL)
```

```python
import functools

import jax
import jax.numpy as jnp
from jax import lax
from jax.experimental import pallas as pl
from jax.experimental.pallas import tpu as pltpu

F32 = jnp.float32
BF16 = jnp.bfloat16

D_MODEL = 2048
GRID_W = 64
EPS = 1e-6
HEAD_DIM = 128
A_WIDTH = D_MODEL // 2
A_GROUPS = A_WIDTH // 128
SGU_CHUNK = 128
B_HEADS = (D_MODEL // 2) // HEAD_DIM
B_KV_HEADS = B_HEADS // 4
B_GROUP = B_HEADS // B_KV_HEADS
B_Q_W = B_HEADS * HEAD_DIM
B_KV_W = B_KV_HEADS * HEAD_DIM
ROPE_THETA = 10000.0
AXIS_DIM = HEAD_DIM // 2
AB_IN_W = 3 * A_WIDTH + 2 * B_Q_W + 2 * B_KV_W
DN_K_HEADS = D_MODEL // HEAD_DIM
DN_V_HEADS = 2 * DN_K_HEADS
DN_K_W = DN_K_HEADS * HEAD_DIM
DN_V_W = DN_V_HEADS * HEAD_DIM
DN_QKV_W = 2 * DN_K_W + DN_V_W
DN_IN_W = DN_QKV_W + DN_V_W + 4 * DN_V_HEADS
DN_CHUNK = 64
CONV_K = 5

V7X_VMEM_LIMIT_BYTES = 56 * 1024 * 1024
LANES = 128
SUBLANES = 8

ROW_TILE = 512
CONV_ROW_TILE = 256


def _cparams(n_axes):
    return pltpu.CompilerParams(dimension_semantics=("arbitrary",) * n_axes,
                                vmem_limit_bytes=V7X_VMEM_LIMIT_BYTES)


def _silu(x):
    return x * jax.nn.sigmoid(x)


def _split_bf16(a):
    hi = a.astype(BF16)
    lo = (a - hi.astype(F32)).astype(BF16)
    return hi, lo


def _dot(a, b):
    return jnp.dot(a, b, preferred_element_type=F32)


def _dot_nt(a, b):
    return lax.dot_general(a, b, (((1,), (1,)), ((), ())), preferred_element_type=F32)


def _dot3(a, b):
    ah, al = _split_bf16(a)
    bh, bl = _split_bf16(b)
    return _dot(ah, bh) + _dot(ah, bl) + _dot(al, bh)


def _ada_kernel(c_ref, w_ref, b_ref, o_ref):
    s = _silu(c_ref[...])
    o_ref[0] = _dot3(s, w_ref[0]) + b_ref[0]


def _ada_mod(cond, ada_w, ada_b):
    depth = ada_w.shape[0]
    tn = 512
    return pl.pallas_call(
        _ada_kernel,
        grid=(depth, 3 * D_MODEL // tn),
        in_specs=[pl.BlockSpec((SUBLANES, D_MODEL), lambda l, j: (0, 0)),
                  pl.BlockSpec((1, D_MODEL, tn), lambda l, j: (l, 0, j)),
                  pl.BlockSpec((1, 1, tn), lambda l, j: (l, 0, j))],
        out_specs=pl.BlockSpec((1, SUBLANES, tn), lambda l, j: (l, 0, j)),
        out_shape=jax.ShapeDtypeStruct((depth, SUBLANES, 3 * D_MODEL), F32),
        compiler_params=_cparams(2),
        name="ada_mod",
    )(cond, ada_w, ada_b.reshape(depth, 1, 3 * D_MODEL))


INPROJ_COL_TILES = (512, 1024, 1408, 1536, 2816)


def _inproj_kernel(*refs, n_lat_tiles):
    nw_ref, mod_ref, w_ref, o_ref, h_ref = refs[-5:]

    @pl.when(pl.program_id(1) == 0)
    def _():
        x = _select_rows(refs[:-5], n_lat_tiles)
        y = x * lax.rsqrt(jnp.mean(x * x, axis=-1, keepdims=True) + EPS) * nw_ref[...]
        shift = mod_ref[:, 0:D_MODEL]
        scale = mod_ref[:, D_MODEL:2 * D_MODEL]
        h_ref[...] = (y * (1.0 + scale) + shift).astype(BF16)

    o_ref[...] = _dot(h_ref[...], w_ref[...]).astype(o_ref.dtype)


def _select_rows(x_refs, n_lat_tiles):
    if len(x_refs) == 1:
        return x_refs[0][...]
    return jnp.where(pl.program_id(0) < n_lat_tiles, x_refs[0][...], x_refs[1][...])


def _row_specs(x_parts, tm, width, n_lat_tiles):
    if len(x_parts) == 1:
        return [pl.BlockSpec((tm, width), lambda i, j: (i, 0))]
    assert x_parts[1].shape[0] == tm
    return [pl.BlockSpec((tm, width), lambda i, j: (jnp.minimum(i, n_lat_tiles - 1), 0)),
            pl.BlockSpec((tm, width), lambda i, j: (0, 0))]


def _mod_row(i, lat_tiles, batch):
    return jnp.minimum(i // lat_tiles, batch)


def _inproj(x_parts, norm_w, mod, w_stack_bf16, layer, batch, seq):
    m = sum(p.shape[0] for p in x_parts)
    n = w_stack_bf16.shape[2]
    tm = ROW_TILE
    tn = max(t for t in INPROJ_COL_TILES if n % t == 0)
    lat_tiles = seq // tm
    n_lat_tiles = batch * lat_tiles
    return pl.pallas_call(
        functools.partial(_inproj_kernel, n_lat_tiles=n_lat_tiles),
        grid=(m // tm, n // tn),
        in_specs=_row_specs(x_parts, tm, D_MODEL, n_lat_tiles) + [
            pl.BlockSpec((1, D_MODEL), lambda i, j: (0, 0)),
            pl.BlockSpec((None, 1, 3 * D_MODEL), lambda i, j: (_mod_row(i, lat_tiles, batch), 0, 0)),
            pl.BlockSpec((None, D_MODEL, tn), lambda i, j: (layer, 0, j))],
        out_specs=pl.BlockSpec((tm, tn), lambda i, j: (i, j)),
        out_shape=jax.ShapeDtypeStruct((m, n), BF16),
        scratch_shapes=[pltpu.VMEM((tm, D_MODEL), BF16)],
        compiler_params=_cparams(2),
        name="inproj",
    )(*x_parts, norm_w.reshape(1, D_MODEL), mod, w_stack_bf16)


Q_PRESCALE = (HEAD_DIM ** -0.5) * 1.4426950408889634


def _qkprep_kernel(q_ref, k_ref, v_ref, cos_ref, sin_ref, qw_ref, kw_ref, qo_ref, ko_ref, vo_ref):
    cos = cos_ref[...]
    sin = sin_ref[...]
    lane = lax.broadcasted_iota(jnp.int32, cos.shape, 1)
    first = (lane % (AXIS_DIM)) < (AXIS_DIM // 2)

    def prep(x, w):
        y = x * lax.rsqrt(jnp.mean(x * x, axis=-1, keepdims=True) + EPS) * w
        rot = jnp.where(first, pltpu.roll(y, HEAD_DIM - AXIS_DIM // 2, 1), pltpu.roll(y, AXIS_DIM // 2, 1))
        return y * cos + rot * sin

    for h in range(B_HEADS):
        sl = slice(h * HEAD_DIM, (h + 1) * HEAD_DIM)
        qo_ref[:, sl] = (prep(q_ref[:, sl].astype(F32), qw_ref[...]) * Q_PRESCALE).astype(qo_ref.dtype)
    for h in range(B_KV_HEADS):
        sl = slice(h * HEAD_DIM, (h + 1) * HEAD_DIM)
        ko_ref[:, sl] = prep(k_ref[:, sl].astype(F32), kw_ref[...]).astype(ko_ref.dtype)
    vo_ref[...] = v_ref[...].astype(vo_ref.dtype)


def _qkprep(proj, cos_tab, sin_tab, qn_w, kn_w, batch, seq):
    m = proj.shape[0]
    tm = CONV_ROW_TILE
    lat_tiles = seq // tm
    n_lat = batch * lat_tiles
    q_blk = (3 * A_WIDTH) // B_Q_W
    k_blk = (3 * A_WIDTH + B_Q_W) // B_KV_W

    def tab_idx(i):
        return (jnp.where(i < n_lat, i % lat_tiles, lat_tiles), 0)

    def kv_idx(i):
        lat_blk = (i // lat_tiles) * (lat_tiles + 1) + 1 + i % lat_tiles
        return (jnp.where(i < n_lat, lat_blk, (i - n_lat) * (lat_tiles + 1)), 0)

    return pl.pallas_call(
        _qkprep_kernel,
        grid=(m // tm,),
        in_specs=[pl.BlockSpec((tm, B_Q_W), lambda i: (i, q_blk)),
                  pl.BlockSpec((tm, B_KV_W), lambda i: (i, k_blk)),
                  pl.BlockSpec((tm, B_KV_W), lambda i: (i, k_blk + 1)),
                  pl.BlockSpec((tm, HEAD_DIM), tab_idx),
                  pl.BlockSpec((tm, HEAD_DIM), tab_idx),
                  pl.BlockSpec((1, HEAD_DIM), lambda i: (0, 0)),
                  pl.BlockSpec((1, HEAD_DIM), lambda i: (0, 0))],
        out_specs=[pl.BlockSpec((tm, B_Q_W), lambda i: (i, 0)),
                   pl.BlockSpec((tm, B_KV_W), kv_idx),
                   pl.BlockSpec((tm, B_KV_W), kv_idx)],
        out_shape=[jax.ShapeDtypeStruct((m, B_Q_W), BF16),
                   jax.ShapeDtypeStruct((m, B_KV_W), BF16),
                   jax.ShapeDtypeStruct((m, B_KV_W), BF16)],
        compiler_params=_cparams(1),
        name="qk_prep",
    )(proj, proj, proj, cos_tab, sin_tab, qn_w.reshape(1, HEAD_DIM), kn_w.reshape(1, HEAD_DIM))


def _rope_tables(seq, tm):
    rows = seq // GRID_W
    row = jnp.repeat(jnp.arange(rows), GRID_W).astype(F32)
    col = jnp.tile(jnp.arange(GRID_W), rows).astype(F32)
    freqs = ROPE_THETA ** (-jnp.arange(0, AXIS_DIM, 2, dtype=F32) / AXIS_DIM)
    ang_r = row[:, None] * freqs[None, :]
    ang_c = col[:, None] * freqs[None, :]
    ang = jnp.concatenate([ang_r, ang_r, ang_c, ang_c], axis=-1)
    sign = jnp.where((jnp.arange(HEAD_DIM) % AXIS_DIM) < AXIS_DIM // 2, -1.0, 1.0).astype(F32)
    cos = jnp.concatenate([jnp.cos(ang), jnp.ones((tm, HEAD_DIM), F32)], axis=0)
    sin = jnp.concatenate([jnp.sin(ang) * sign[None, :], jnp.zeros((tm, HEAD_DIM), F32)], axis=0)
    return cos, sin


ATTN_Q_TILE = 256
ATTN_KV_BODY = 4096
ATTN_KV_TAIL = 768


def _kv_chunks(kv_len):
    tail = min(ATTN_KV_TAIL, kv_len)
    body = kv_len - tail
    sizes = [ATTN_KV_BODY] * (body // ATTN_KV_BODY) + ([body % ATTN_KV_BODY] if body % ATTN_KV_BODY else []) + [tail]
    assert all(c % LANES == 0 for c in sizes) and sum(sizes) == kv_len
    return tuple(sizes)


def _attn_kernel(q_ref, k_ref, v_ref, o_ref, m_scr, l_scr, acc_scr, *, chunks, tq):
    q = jnp.concatenate([q_ref[:, g * HEAD_DIM:(g + 1) * HEAD_DIM] for g in range(B_GROUP)], axis=0)
    m_scr[...] = jnp.full(m_scr.shape, -jnp.inf, F32)
    l_scr[...] = jnp.zeros(l_scr.shape, F32)
    acc_scr[...] = jnp.zeros(acc_scr.shape, F32)
    start = 0
    for tk in chunks:
        k = k_ref[start:start + tk, :]
        v = v_ref[start:start + tk, :]
        start += tk
        s = _dot_nt(q, k)
        m_prev = m_scr[...]
        m_next = jnp.maximum(m_prev, jnp.max(s, axis=1, keepdims=True))
        p = jnp.exp2(s - jnp.concatenate([m_next] * (tk // LANES), axis=1))
        alpha = jnp.exp2(m_prev - m_next)
        pv = _dot(p.astype(BF16), jnp.concatenate([v, jnp.ones_like(v)], axis=1))
        l_scr[...] = alpha * l_scr[...] + pv[:, HEAD_DIM:]
        acc_scr[...] = acc_scr[...] * alpha + pv[:, :HEAD_DIM]
        m_scr[...] = m_next
    out = acc_scr[...] / l_scr[...]
    for g in range(B_GROUP):
        o_ref[:, g * HEAD_DIM:(g + 1) * HEAD_DIM] = out[g * tq:(g + 1) * tq].astype(o_ref.dtype)


def _attention(qr, kr, vr, batch, seq, ctx_len, latent):
    tq = ATTN_Q_TILE
    gw = B_GROUP * HEAD_DIM
    q_len = seq if latent else ctx_len
    q_tiles = q_len // tq
    q_row0 = 0 if latent else batch * seq // tq
    kv_len = ctx_len + seq
    if latent:
        kv_rows = kv_len
        kv_idx = lambda b, h, i: (b, h)
    else:
        kv_rows = ctx_len
        kv_idx = lambda b, h, i: (b * (kv_len // ctx_len), h)
    rows = B_GROUP * tq
    return pl.pallas_call(
        functools.partial(_attn_kernel, chunks=_kv_chunks(kv_rows), tq=tq),
        grid=(batch, B_KV_HEADS, q_tiles),
        in_specs=[pl.BlockSpec((tq, gw), lambda b, h, i: (q_row0 + b * q_tiles + i, h)),
                  pl.BlockSpec((kv_rows, HEAD_DIM), kv_idx),
                  pl.BlockSpec((kv_rows, HEAD_DIM), kv_idx)],
        out_specs=pl.BlockSpec((tq, gw), lambda b, h, i: (b * q_tiles + i, h)),
        out_shape=jax.ShapeDtypeStruct((batch * q_len, B_Q_W), BF16),
        scratch_shapes=[pltpu.VMEM((rows, LANES), F32), pltpu.VMEM((rows, LANES), F32),
                        pltpu.VMEM((rows, HEAD_DIM), F32)],
        compiler_params=_cparams(3),
        name="attn_lat" if latent else "attn_ctx",
    )(qr, kr, vr)


def _ab_out_kernel(u_ref, v_ref, ga_ref, gb0_ref, gb1_ref, aol_ref, aoc_ref, ws_ref, bs_ref, w_ref, gate_ref, *rest,
                   n_lat_tiles):
    o_ref, y_ref, acc_ref = rest[-3:]
    k = pl.program_id(1)

    @pl.when(k == 0)
    def _():
        tm = u_ref.shape[0]
        for c in range(tm // SGU_CHUNK):
            rows = slice(c * SGU_CHUNK, (c + 1) * SGU_CHUNK)
            for g in range(A_GROUPS):
                cols = slice(g * LANES, (g + 1) * LANES)
                vg = v_ref[rows, cols].astype(F32)
                d = vg - jnp.mean(vg, axis=-1, keepdims=True)
                var = jnp.mean(d * d, axis=-1, keepdims=True)
                vn = (d * lax.rsqrt(var + 1e-5)).astype(BF16)
                mixed = _dot(ws_ref[g], vn) + bs_ref[:, g:g + 1]
                y_ref[rows, cols] = (u_ref[rows, cols].astype(F32) * mixed
                                     * _silu(ga_ref[rows, cols].astype(F32))).astype(BF16)
        acc_ref[...] = _dot(y_ref[...], w_ref[0:A_WIDTH, :])

    @pl.when(k == 1)
    def _():
        ao = jnp.where(pl.program_id(0) < n_lat_tiles, aol_ref[...], aoc_ref[...]).astype(F32)
        gb = jnp.concatenate([gb0_ref[...], gb1_ref[...]], axis=1).astype(F32)
        y = (ao * _silu(gb)).astype(BF16)
        o_ref[...] = _select_rows(rest[:-3], n_lat_tiles) + gate_ref[...] * (acc_ref[...] + _dot(y, w_ref[A_WIDTH:, :]))


def _ab_out(proj, ao_lat, ao_ctx, sgu_w_stack_bf16, sgu_b_t, w_out_stack_bf16, layer, x_parts, mod, batch, seq):
    m = proj.shape[0]
    tm = ROW_TILE
    lat_tiles = seq // tm
    n_lat_tiles = batch * lat_tiles
    assert ao_ctx.shape[0] == tm
    gb_blk0 = (3 * A_WIDTH + B_Q_W + 2 * B_KV_W) // (B_Q_W // 2)
    return pl.pallas_call(
        functools.partial(_ab_out_kernel, n_lat_tiles=n_lat_tiles),
        grid=(m // tm, 2),
        in_specs=[pl.BlockSpec((tm, A_WIDTH), lambda i, k: (i, 0)),
                  pl.BlockSpec((tm, A_WIDTH), lambda i, k: (i, 1)),
                  pl.BlockSpec((tm, A_WIDTH), lambda i, k: (i, 2)),
                  pl.BlockSpec((tm, B_Q_W // 2), lambda i, k: (i, gb_blk0)),
                  pl.BlockSpec((tm, B_Q_W // 2), lambda i, k: (i, gb_blk0 + 1)),
                  pl.BlockSpec((tm, B_Q_W), lambda i, k: (jnp.minimum(i, n_lat_tiles - 1), 0)),
                  pl.BlockSpec((tm, B_Q_W), lambda i, k: (0, 0)),
                  pl.BlockSpec((None, A_GROUPS, SGU_CHUNK, SGU_CHUNK), lambda i, k: (layer, 0, 0, 0)),
                  pl.BlockSpec((SGU_CHUNK, A_GROUPS), lambda i, k: (0, 0)),
                  pl.BlockSpec((None, A_WIDTH + B_Q_W, D_MODEL), lambda i, k: (layer, 0, 0),
                               pipeline_mode=pl.Buffered(1)),
                  pl.BlockSpec((None, 1, D_MODEL), lambda i, k: (_mod_row(i, lat_tiles, batch), 0, 2))]
        + _row_specs(x_parts, tm, D_MODEL, n_lat_tiles),
        out_specs=pl.BlockSpec((tm, D_MODEL), lambda i, k: (i, 0)),
        out_shape=jax.ShapeDtypeStruct((m, D_MODEL), F32),
        scratch_shapes=[pltpu.VMEM((tm, A_WIDTH), BF16), pltpu.VMEM((tm, D_MODEL), F32)],
        compiler_params=_cparams(2),
        name="ab_out",
    )(proj, proj, proj, proj, proj, ao_lat, ao_ctx, sgu_w_stack_bf16, sgu_b_t, w_out_stack_bf16, mod, *x_parts)


DN_HALO = 2 * SUBLANES
DN_PROJ_COL_TILE = 2048


def _inproj_dn_kernel(xp_ref, x_ref, xn_ref, nw_ref, mod_ref, w_ref, ws_ref, cw_ref, qkv_ref, z_ref, ba_ref, h_ref, p_ref,
                      *, lat_tiles, n_lat, q_tiles, qk_tiles, qkv_tiles, ctx_len):
    i = pl.program_id(0)
    j = pl.program_id(1)
    tm = x_ref.shape[0]
    is_lat = i < n_lat
    pos = i % lat_tiles
    first = jnp.logical_or(jnp.logical_not(is_lat), pos == 0)
    last = jnp.logical_or(jnp.logical_not(is_lat), pos == lat_tiles - 1)
    half = CONV_K // 2

    @pl.when(j == 0)
    def _():
        shift = mod_ref[:, 0:D_MODEL]
        scale = mod_ref[:, D_MODEL:2 * D_MODEL]

        def norm_mod(x):
            y = x * lax.rsqrt(jnp.mean(x * x, axis=-1, keepdims=True) + EPS) * nw_ref[...]
            return y * (1.0 + scale) + shift

        h_ref[0:DN_HALO, :] = jnp.where(first, 0.0, norm_mod(xp_ref[...])).astype(BF16)
        h_ref[DN_HALO:DN_HALO + tm, :] = norm_mod(x_ref[...]).astype(BF16)
        h_ref[DN_HALO + tm:, :] = jnp.where(last, 0.0, norm_mod(xn_ref[...])).astype(BF16)
        ba_ref[...] = _dot(h_ref[DN_HALO:DN_HALO + tm, :], ws_ref[...])

    n_ext = h_ref.shape[0]
    tn = w_ref.shape[1]
    pair_w = 2 * HEAD_DIM
    row = lax.broadcasted_iota(jnp.int32, (tm, HEAD_DIM), 0)
    is_ctx = jnp.logical_not(is_lat)

    def conv_silu(ph, sl):
        acc = cw_ref[half:half + 1, sl] * ph[DN_HALO:DN_HALO + tm]
        for t in range(CONV_K):
            off = t - half
            if off == 0:
                continue
            tap = pltpu.roll(ph, (-off) % n_ext, 0)[DN_HALO:DN_HALO + tm]
            crosses = (jnp.logical_and(row >= ctx_len - off, row < ctx_len) if off > 0
                       else jnp.logical_and(row >= ctx_len, row < ctx_len - off))
            tap = jnp.where(jnp.logical_and(is_ctx, crosses), 0.0, tap)
            acc = acc + cw_ref[t:t + 1, sl] * tap
        return _silu(acc)

    def for_each_head(store):
        for n, c0 in enumerate(range(0, tn, pair_w)):
            p_ref[n % 2] = _dot(h_ref[...], w_ref[:, c0:c0 + pair_w])
            for h0 in range(0, pair_w, HEAD_DIM):
                sl = slice(c0 + h0, c0 + h0 + HEAD_DIM)
                store(sl, conv_silu(p_ref[n % 2, :, h0:h0 + HEAD_DIM], sl))

    @pl.when(j < qk_tiles)
    def _():
        qk_scale = jnp.where(j < q_tiles, HEAD_DIM ** -0.5, 1.0)

        def store(sl, y):
            qkv_ref[:, sl] = (y * (lax.rsqrt(jnp.sum(y * y, axis=-1, keepdims=True) + EPS) * qk_scale)).astype(qkv_ref.dtype)
        for_each_head(store)

    @pl.when(jnp.logical_and(j >= qk_tiles, j < qkv_tiles))
    def _():
        def store(sl, y):
            qkv_ref[:, sl] = y.astype(qkv_ref.dtype)
        for_each_head(store)

    @pl.when(j >= qkv_tiles)
    def _():
        z_ref[...] = _dot(h_ref[DN_HALO:DN_HALO + tm, :], w_ref[...]).astype(z_ref.dtype)


def _inproj_dn(x, norm_w, mod, w_stack_bf16, layer, conv_w, batch, seq, ctx_len):
    m = x.shape[0]
    tm, tn = ROW_TILE, DN_PROJ_COL_TILE
    n_side = 4 * DN_V_HEADS
    n = w_stack_bf16.shape[2] - n_side
    assert n == DN_QKV_W + DN_V_W and tm == 2 * ctx_len
    lat_tiles = seq // tm
    n_lat = batch * lat_tiles
    halo_per_tile = tm // DN_HALO
    n_halo = m // DN_HALO
    qkv_tiles = DN_QKV_W // tn
    kern = functools.partial(_inproj_dn_kernel, lat_tiles=lat_tiles, n_lat=n_lat, q_tiles=DN_K_W // tn,
                             qk_tiles=2 * DN_K_W // tn, qkv_tiles=qkv_tiles, ctx_len=ctx_len)
    return pl.pallas_call(
        kern,
        grid=(m // tm, n // tn),
        in_specs=[pl.BlockSpec((DN_HALO, D_MODEL), lambda i, j: (jnp.maximum(i * halo_per_tile - 1, 0), 0)),
                  pl.BlockSpec((tm, D_MODEL), lambda i, j: (i, 0)),
                  pl.BlockSpec((DN_HALO, D_MODEL), lambda i, j: (jnp.minimum((i + 1) * halo_per_tile, n_halo - 1), 0)),
                  pl.BlockSpec((1, D_MODEL), lambda i, j: (0, 0)),
                  pl.BlockSpec((None, 1, 3 * D_MODEL), lambda i, j: (_mod_row(i, lat_tiles, batch), 0, 0)),
                  pl.BlockSpec((None, D_MODEL, tn), lambda i, j: (layer, 0, j)),
                  pl.BlockSpec((None, D_MODEL, n_side), lambda i, j: (layer, 0, n // n_side)),
                  pl.BlockSpec((SUBLANES, tn), lambda i, j: (0, jnp.minimum(j, qkv_tiles - 1)))],
        out_specs=[pl.BlockSpec((tm, tn), lambda i, j: (i, jnp.minimum(j, qkv_tiles - 1))),
                   pl.BlockSpec((tm, tn), lambda i, j: (i, jnp.maximum(j - qkv_tiles, 0))),
                   pl.BlockSpec((tm, n_side), lambda i, j: (i, 0))],
        out_shape=[jax.ShapeDtypeStruct((m, DN_QKV_W), F32),
                   jax.ShapeDtypeStruct((m, DN_V_W), BF16),
                   jax.ShapeDtypeStruct((m, n_side), F32)],
        scratch_shapes=[pltpu.VMEM((tm + 2 * DN_HALO, D_MODEL), BF16),
                        pltpu.VMEM((2, tm + 2 * DN_HALO, 2 * HEAD_DIM), F32)],
        compiler_params=_cparams(2),
        name="inproj_dn",
    )(x, x, x, norm_w.reshape(1, D_MODEL), mod, w_stack_bf16, w_stack_bf16, conv_w)


def _dngate_kernel(ba_ref, alog_ref, dtb_ref, o_ref):
    ba = ba_ref[...]
    tm = ba.shape[0]
    lane = lax.broadcasted_iota(jnp.int32, (DN_CHUNK, LANES), 1)
    is_beta = (lane // DN_V_HEADS) % 2 == 0
    is_fwd = lane < 2 * DN_V_HEADS
    z = ba + dtb_ref[...]
    softplus = jnp.maximum(z, 0.0) + jnp.log1p(jnp.exp(-jnp.abs(z)))
    g = -jnp.exp(alog_ref[...]) * softplus
    r = lax.broadcasted_iota(jnp.int32, (DN_CHUNK, DN_CHUNK), 0)
    c = lax.broadcasted_iota(jnp.int32, (DN_CHUNK, DN_CHUNK), 1)
    tri_lo = (r >= c).astype(BF16)
    tri_up = (r <= c).astype(BF16)
    beta = jax.nn.sigmoid(ba)
    for ch in range(tm // DN_CHUNK):
        rows = slice(ch * DN_CHUNK, (ch + 1) * DN_CHUNK)
        gch = g[rows]
        g1 = gch.astype(BF16)
        r1 = gch - g1.astype(F32)
        g2 = r1.astype(BF16)
        g3 = (r1 - g2.astype(F32)).astype(BF16)
        pre = _dot(tri_lo, g1) + _dot(tri_lo, g2) + _dot(tri_lo, g3)
        suf = _dot(tri_up, g1) + _dot(tri_up, g2) + _dot(tri_up, g3)
        gc = jnp.where(is_fwd, pre, suf)
        o_ref[rows, :] = jnp.where(is_beta, beta[rows], gc)


def _dn_gate(ba, alog_vec, dtb_vec):
    m = ba.shape[0]
    tm = ROW_TILE
    return pl.pallas_call(
        _dngate_kernel,
        grid=(m // tm,),
        in_specs=[pl.BlockSpec((tm, LANES), lambda i: (i, 0)),
                  pl.BlockSpec((1, LANES), lambda i: (0, 0)),
                  pl.BlockSpec((1, LANES), lambda i: (0, 0))],
        out_specs=pl.BlockSpec((tm, LANES), lambda i: (i, 0)),
        out_shape=jax.ShapeDtypeStruct((m, LANES), F32),
        compiler_params=_cparams(1),
        name="dn_gate",
    )(ba, alog_vec, dtb_vec)


DN_CHAIN_GROUP = 8
DN_GROUP_SKEW = 4
DN_KH_PER_STEP = 16


def _dncore_kernel(qf_ref, kf_ref, vf_ref, gf_ref, qb_ref, kb_ref, vb_ref, gb_ref, of_ref, ob_ref, s_ref):
    C = DN_CHUNK
    nvh = 2 * DN_KH_PER_STEP

    @pl.when(pl.program_id(2) == 0)
    def _():
        s_ref[...] = jnp.zeros(s_ref.shape, F32)

    lane = lax.broadcasted_iota(jnp.int32, (C, 2 * C), 1)
    left = lane < C
    row = lax.broadcasted_iota(jnp.int32, (C, 2 * C), 0)
    colp = lane % C
    left_sq = lax.broadcasted_iota(jnp.int32, (2 * C, 2 * C), 1) < C
    eye2 = (row == colp).astype(F32)

    def blockdiag(p):
        z = jnp.zeros_like(p)
        return jnp.concatenate([jnp.where(left, p, z), jnp.where(left, z, p)], axis=0)

    def packed_mm(a, b):
        return _dot(a.astype(BF16), blockdiag(b.astype(BF16)))

    dirs = ((qf_ref, kf_ref, vf_ref, gf_ref, of_ref), (qb_ref, kb_ref, vb_ref, gb_ref, ob_ref))
    chains = [(d, kh) for d in range(2) for kh in range(DN_KH_PER_STEP)]
    G = [dirs[d][3][...] for d in range(2)]
    GT = [jnp.concatenate([g, g], axis=0).T for g in G]

    def run_group(group):
        st = []
        for d, kh in group:
            q_ref, k_ref, v_ref, _, _ = dirs[d]
            base_beta = d * 2 * nvh
            base_gc = base_beta + nvh
            lv0 = 2 * kh
            q = q_ref[:, kh * HEAD_DIM:(kh + 1) * HEAD_DIM]
            k = k_ref[:, kh * HEAD_DIM:(kh + 1) * HEAD_DIM]
            k2 = jnp.concatenate([k, k], axis=0)
            kT2 = k2.T
            gram = _dot(jnp.concatenate([q, k], axis=0).astype(BF16), kT2.astype(BF16))
            b0, b1 = G[d][:, base_beta + lv0:base_beta + lv0 + 1], G[d][:, base_beta + lv0 + 1:base_beta + lv0 + 2]
            c0, c1 = G[d][:, base_gc + lv0:base_gc + lv0 + 1], G[d][:, base_gc + lv0 + 1:base_gc + lv0 + 2]
            r0, r1 = GT[d][base_gc + lv0:base_gc + lv0 + 1, :], GT[d][base_gc + lv0 + 1:base_gc + lv0 + 2, :]
            st.append(dict(d=d, lv0=lv0, q=q, k2=k2, kT2=kT2, gram=gram, b0=b0, b1=b1, c0=c0, c1=c1, r0=r0, r1=r1))

        yield
        for s in st:
            d = s["d"]
            incl = (row >= colp) if d == 0 else (row <= colp)
            strict = (row > colp) if d == 0 else (row < colp)
            gcol_p = jnp.where(left, s["c0"], s["c1"])
            grow_p = jnp.where(left[0:1], s["r0"], s["r1"])
            beta_p = jnp.where(left, s["b0"], s["b1"])
            dec = jnp.exp(jnp.where(incl, gcol_p - grow_p, -1e30))
            s["dec"] = dec
            s["attn"] = s["gram"][0:C] * dec
            s["L"] = jnp.where(strict, s["gram"][C:2 * C] * dec, 0.0) * beta_p

        yield
        for s in st:
            n1 = jnp.where(jnp.logical_and(row // 2 == colp // 2, row != colp), s["L"], 0.0)
            s["X"] = eye2 - n1
        blk = 2
        while blk < C:
            mask = jnp.logical_and(row // (2 * blk) == colp // (2 * blk), row // blk != colp // blk)
            yield
            for s in st:
                s["Y"] = packed_mm(s["X"], jnp.where(mask, s["L"], 0.0))
            yield
            for s in st:
                s["X"] = s["X"] - packed_mm(s["Y"], s["X"])
            blk *= 2

        yield
        for s in st:
            d, lv0 = s["d"], s["lv0"]
            v_ref = dirs[d][2]
            beta_r = jnp.concatenate([s["b0"], s["b1"]], axis=0)
            egc_r = jnp.exp(jnp.concatenate([s["c0"], s["c1"]], axis=0))
            v2 = jnp.concatenate([v_ref[:, lv0 * HEAD_DIM:(lv0 + 1) * HEAD_DIM],
                                  v_ref[:, (lv0 + 1) * HEAD_DIM:(lv0 + 2) * HEAD_DIM]], axis=0)
            rhs = jnp.concatenate([v2 * beta_r, s["k2"] * (beta_r * egc_r)], axis=1)
            s["sol"] = _dot(blockdiag(s["X"]).astype(BF16), rhs.astype(BF16))
            s["egc"] = egc_r
            s["qg2"] = jnp.concatenate([s["q"], s["q"]], axis=0) * egc_r

        yield
        for s in st:
            d, lv0 = s["d"], s["lv0"]
            w2 =s["sol"][:, HEAD_DIM:]
            s["ws"] = []
            for r in range(2):
                lhs = jnp.concatenate([w2[r * C:(r + 1) * C], s["qg2"][r * C:(r + 1) * C]], axis=0).astype(BF16)
                s["ws"].append(_dot(lhs, s_ref[d, lv0 + r].astype(BF16)))

        yield
        for s in st:
            u2 = s["sol"][:, 0:HEAD_DIM]
            vn2 = jnp.concatenate([u2[r * C:(r + 1) * C] - s["ws"][r][0:C] for r in range(2)], axis=0).astype(BF16)
            s["vn2"] = vn2
            s["o2"] = jnp.concatenate([s["ws"][r][C:2 * C] for r in range(2)], axis=0) + _dot(
                blockdiag(s["attn"]).astype(BF16), vn2)

        yield
        for s in st:
            d, lv0 = s["d"], s["lv0"]
            o_ref = dirs[d][4]
            last = C - 1 if d == 0 else 0
            kdT_p = s["kT2"] * s["dec"][last:last + 1, :]
            zkd = jnp.zeros_like(kdT_p)
            for r in range(2):
                kd_r = jnp.where(left_sq if r == 0 else jnp.logical_not(left_sq), kdT_p, zkd).astype(BF16)
                glr = jnp.broadcast_to(s["egc"][r * C + last:r * C + last + 1, :], (HEAD_DIM, HEAD_DIM))
                s_ref[d, lv0 + r] = s_ref[d, lv0 + r] * glr + _dot(kd_r, s["vn2"])
                o_ref[:, (lv0 + r) * HEAD_DIM:(lv0 + r + 1) * HEAD_DIM] = s["o2"][r * C:(r + 1) * C].astype(o_ref.dtype)

    pending = [run_group(chains[g0:g0 + DN_CHAIN_GROUP]) for g0 in range(0, len(chains), DN_CHAIN_GROUP)]
    active, tick = [], 0
    while pending or active:
        if pending and tick % DN_GROUP_SKEW == 0:
            active.append(pending.pop(0))
        active = [g for g in active if next(g, "done") != "done"]
        tick += 1


def _dn_core(qkv, gates, batch, seq, ctx_len):
    m = qkv.shape[0]
    C = DN_CHUNK
    n_lat = seq // C
    n_ctx = ctx_len // C
    n_steps = n_ctx + n_lat
    ctx0 = batch * n_lat
    khs = DN_KH_PER_STEP
    qw = khs * HEAD_DIM
    vw = 2 * khs * HEAD_DIM
    k_blk0 = DN_K_W // qw
    v_blk0 = 2 * DN_K_W // vw

    def rf(b, t):
        return jnp.where(t < n_ctx, ctx0 + b * n_ctx + t, b * n_lat + (t - n_ctx))

    def rb(b, t):
        return jnp.where(t < n_ctx, ctx0 + b * n_ctx + (n_ctx - 1 - t), b * n_lat + (n_lat - 1 - (t - n_ctx)))

    def specs(rfun):
        return [pl.BlockSpec((C, qw), lambda b, h, t: (rfun(b, t), h)),
                pl.BlockSpec((C, qw), lambda b, h, t: (rfun(b, t), k_blk0 + h)),
                pl.BlockSpec((C, vw), lambda b, h, t: (rfun(b, t), v_blk0 + h)),
                pl.BlockSpec((C, LANES), lambda b, h, t: (rfun(b, t), h))]

    return pl.pallas_call(
        _dncore_kernel,
        grid=(batch, DN_K_HEADS // khs, n_steps),
        in_specs=specs(rf) + specs(rb),
        out_specs=[pl.BlockSpec((C, vw), lambda b, h, t: (rf(b, t), h)),
                   pl.BlockSpec((C, vw), lambda b, h, t: (rb(b, t), h))],
        out_shape=[jax.ShapeDtypeStruct((m, DN_V_W), BF16), jax.ShapeDtypeStruct((m, DN_V_W), BF16)],
        scratch_shapes=[pltpu.VMEM((2, 2 * khs, HEAD_DIM, HEAD_DIM), F32)],
        compiler_params=_cparams(3),
        name="dn_core",
    )(qkv, qkv, qkv, gates, qkv, qkv, qkv, gates)


DN_OUT_K_TILE = 1024


def _dn_out_kernel(of_ref, ob_ref, z_ref, nw_ref, w_ref, x_ref, gate_ref, *rest):
    o_ref, acc_ref = rest[-2:]
    k = pl.program_id(1)
    ys = []
    for h in range(of_ref.shape[1] // HEAD_DIM):
        sl = slice(h * HEAD_DIM, (h + 1) * HEAD_DIM)
        o = of_ref[:, sl].astype(F32) + ob_ref[:, sl].astype(F32)
        n = o * lax.rsqrt(jnp.mean(o * o, axis=-1, keepdims=True) + EPS) * nw_ref[...]
        ys.append((n * _silu(z_ref[:, sl].astype(F32))).astype(BF16))
    tk = of_ref.shape[1]
    part = _dot(jnp.concatenate(ys, axis=1), w_ref[pl.ds(pl.multiple_of(k * tk, tk), tk), :])

    @pl.when(k == 0)
    def _():
        acc_ref[...] = part

    @pl.when(jnp.logical_and(k > 0, k < pl.num_programs(1) - 1))
    def _():
        acc_ref[...] += part

    @pl.when(k == pl.num_programs(1) - 1)
    def _():
        y = x_ref[...] + gate_ref[...] * (acc_ref[...] + part)
        if len(rest) == 3:
            y = y * lax.rsqrt(jnp.mean(y * y, axis=-1, keepdims=True) + EPS) * rest[0][...]
        o_ref[...] = y


def _dn_out(o_f, o_b, z, norm_w, w_out_stack_bf16, layer, x, mod, batch, seq, final_norm_w=None):
    tm, tk = ROW_TILE, DN_OUT_K_TILE
    m = x.shape[0] if final_norm_w is None else batch * seq
    lat_tiles = seq // tm
    extra_specs, extra_args = [], []
    if final_norm_w is not None:
        extra_specs = [pl.BlockSpec((1, D_MODEL), lambda i, k: (0, 0))]
        extra_args = [final_norm_w.reshape(1, D_MODEL)]
    return pl.pallas_call(
        _dn_out_kernel,
        grid=(m // tm, DN_V_W // tk),
        in_specs=[pl.BlockSpec((tm, tk), lambda i, k: (i, k)),
                  pl.BlockSpec((tm, tk), lambda i, k: (i, k)),
                  pl.BlockSpec((tm, tk), lambda i, k: (i, k)),
                  pl.BlockSpec((1, HEAD_DIM), lambda i, k: (0, 0)),
                  pl.BlockSpec((None, DN_V_W, D_MODEL), lambda i, k: (layer, 0, 0), pipeline_mode=pl.Buffered(1)),
                  pl.BlockSpec((tm, D_MODEL), lambda i, k: (i, 0)),
                  pl.BlockSpec((None, 1, D_MODEL), lambda i, k: (_mod_row(i, lat_tiles, batch), 0, 2))] + extra_specs,
        out_specs=pl.BlockSpec((tm, D_MODEL), lambda i, k: (i, 0)),
        out_shape=jax.ShapeDtypeStruct((m, D_MODEL), F32),
        scratch_shapes=[pltpu.VMEM((tm, D_MODEL), F32)],
        compiler_params=_cparams(2),
        name="dn_out",
    )(o_f, o_b, z, norm_w.reshape(1, HEAD_DIM), w_out_stack_bf16, x, mod, *extra_args)


def _final_norm_kernel(x_ref, w_ref, o_ref):
    x = x_ref[...]
    o_ref[...] = x * lax.rsqrt(jnp.mean(x * x, axis=-1, keepdims=True) + EPS) * w_ref[...]


def _final_norm(x, w, rows):
    tm = ROW_TILE
    return pl.pallas_call(
        _final_norm_kernel,
        grid=(rows // tm,),
        in_specs=[pl.BlockSpec((tm, D_MODEL), lambda i: (i, 0)),
                  pl.BlockSpec((1, D_MODEL), lambda i: (0, 0))],
        out_specs=pl.BlockSpec((tm, D_MODEL), lambda i: (i, 0)),
        out_shape=jax.ShapeDtypeStruct((rows, D_MODEL), F32),
        compiler_params=_cparams(1),
        name="final_norm",
    )(x, w.reshape(1, D_MODEL))


def kernel(x, c, ctx, c_ctx, norm_w, ada_w, ada_b, ab_w_in, ab_w_out, sgu_w, sgu_b, q_norm_w, k_norm_w,
           dn_w_in, dn_conv_w, dn_a_log, dn_dt_bias, dn_norm_w, dn_w_out, final_norm_w):
    batch, seq, _ = x.shape
    ctx_len = ctx.shape[1]
    depth = norm_w.shape[0]
    assert ctx_len == CONV_ROW_TILE and seq % ROW_TILE == 0 and batch * ctx_len == ROW_TILE
    n_lat_rows = batch * seq

    x_parts = (x.reshape(n_lat_rows, D_MODEL), ctx.reshape(batch * ctx_len, D_MODEL))
    ab_w_in_bf, ab_w_out_bf, sgu_w_bf = ab_w_in.astype(BF16), ab_w_out.astype(BF16), sgu_w.astype(BF16)
    dn_w_in_bf, dn_w_out_bf = dn_w_in.astype(BF16), dn_w_out.astype(BF16)
    cond = jnp.zeros((SUBLANES, D_MODEL), F32).at[0:batch].set(c).at[batch].set(c_ctx)
    mods = _ada_mod(cond, ada_w, ada_b)
    cos_tab, sin_tab = _rope_tables(seq, CONV_ROW_TILE)

    for i in range(depth):
        j = i // 2
        mod = mods[i].reshape(SUBLANES, 1, 3 * D_MODEL)
        if i % 2 == 0:
            proj = _inproj(x_parts, norm_w[i], mod, ab_w_in_bf, j, batch, seq)
            qr, kr, vr = _qkprep(proj, cos_tab, sin_tab, q_norm_w[j], k_norm_w[j], batch, seq)
            ao_lat = _attention(qr, kr, vr, batch, seq, ctx_len, True)
            ao_ctx = _attention(qr, kr, vr, batch, seq, ctx_len, False)
            x_parts = (_ab_out(proj, ao_lat, ao_ctx, sgu_w_bf, sgu_b[j].T, ab_w_out_bf, j, x_parts, mod, batch, seq),)
        else:
            conv_w = jnp.zeros((SUBLANES, DN_QKV_W), F32).at[0:CONV_K].set(dn_conv_w[j])
            assert len(x_parts) == 1
            qkv, z, ba = _inproj_dn(x_parts[0], norm_w[i], mod, dn_w_in_bf, j, conv_w, batch, seq, ctx_len)
            zeros = jnp.zeros((2, DN_V_HEADS), F32)
            alog_vec = jnp.concatenate([zeros, dn_a_log[j]], axis=1).reshape(1, LANES)
            dtb_vec = jnp.concatenate([zeros, dn_dt_bias[j]], axis=1).reshape(1, LANES)
            gates = _dn_gate(ba, alog_vec, dtb_vec)
            n_hg = DN_K_HEADS // DN_KH_PER_STEP
            nvh = 2 * DN_KH_PER_STEP
            g4 = gates
            if n_hg > 1:
                g4 = gates.reshape(-1, 4, n_hg, nvh).transpose(0, 2, 1, 3).reshape(-1, n_hg, 4 * nvh)
                g4 = jnp.pad(g4, ((0, 0), (0, 0), (0, LANES - 4 * nvh))).reshape(-1, n_hg * LANES)
            o_f, o_b = _dn_core(qkv, g4, batch, seq, ctx_len)
            x_parts = (_dn_out(o_f, o_b, z, dn_norm_w[j], dn_w_out_bf, j, x_parts[0], mod, batch, seq,
                               final_norm_w=final_norm_w if i == depth - 1 else None),)

    xs = x_parts[0]
    if depth % 2 == 1:
        xs = _final_norm(xs, final_norm_w, n_lat_rows)
    return xs.reshape(batch, seq, D_MODEL)
```

```python
import functools

import jax
import jax.numpy as jnp
from jax import lax
from jax.experimental import pallas as pl
from jax.experimental.pallas import tpu as pltpu

F32 = jnp.float32
BF16 = jnp.bfloat16

D_MODEL = 2048
GRID_W = 64
EPS = 1e-6
HEAD_DIM = 128
A_WIDTH = D_MODEL // 2
A_GROUPS = A_WIDTH // 128
SGU_CHUNK = 128
B_HEADS = (D_MODEL // 2) // HEAD_DIM
B_KV_HEADS = B_HEADS // 4
B_GROUP = B_HEADS // B_KV_HEADS
B_Q_W = B_HEADS * HEAD_DIM
B_KV_W = B_KV_HEADS * HEAD_DIM
ROPE_THETA = 10000.0
AXIS_DIM = HEAD_DIM // 2
AB_IN_W = 3 * A_WIDTH + 2 * B_Q_W + 2 * B_KV_W
DN_K_HEADS = D_MODEL // HEAD_DIM
DN_V_HEADS = 2 * DN_K_HEADS
DN_K_W = DN_K_HEADS * HEAD_DIM
DN_V_W = DN_V_HEADS * HEAD_DIM
DN_QKV_W = 2 * DN_K_W + DN_V_W
DN_IN_W = DN_QKV_W + DN_V_W + 4 * DN_V_HEADS
DN_CHUNK = 64
CONV_K = 5

V7X_VMEM_LIMIT_BYTES = 56 * 1024 * 1024
LANES = 128
SUBLANES = 8

ROW_TILE = 512
CONV_ROW_TILE = 256


def _cparams(n_axes):
    return pltpu.CompilerParams(dimension_semantics=("arbitrary",) * n_axes,
                                vmem_limit_bytes=V7X_VMEM_LIMIT_BYTES)


def _silu(x):
    return x * jax.nn.sigmoid(x)


def _split_bf16(a):
    hi = a.astype(BF16)
    lo = (a - hi.astype(F32)).astype(BF16)
    return hi, lo


def _dot(a, b):
    return jnp.dot(a, b, preferred_element_type=F32)


def _dot_nt(a, b):
    return lax.dot_general(a, b, (((1,), (1,)), ((), ())), preferred_element_type=F32)


def _dot3(a, b):
    ah, al = _split_bf16(a)
    bh, bl = _split_bf16(b)
    return _dot(ah, bh) + _dot(ah, bl) + _dot(al, bh)


def _ada_kernel(c_ref, w_ref, b_ref, o_ref):
    s = _silu(c_ref[...])
    o_ref[0] = _dot3(s, w_ref[0]) + b_ref[0]


def _ada_mod(cond, ada_w, ada_b):
    depth = ada_w.shape[0]
    tn = 512
    return pl.pallas_call(
        _ada_kernel,
        grid=(depth, 3 * D_MODEL // tn),
        in_specs=[pl.BlockSpec((SUBLANES, D_MODEL), lambda l, j: (0, 0)),
                  pl.BlockSpec((1, D_MODEL, tn), lambda l, j: (l, 0, j)),
                  pl.BlockSpec((1, 1, tn), lambda l, j: (l, 0, j))],
        out_specs=pl.BlockSpec((1, SUBLANES, tn), lambda l, j: (l, 0, j)),
        out_shape=jax.ShapeDtypeStruct((depth, SUBLANES, 3 * D_MODEL), F32),
        compiler_params=_cparams(2),
        name="ada_mod",
    )(cond, ada_w, ada_b.reshape(depth, 1, 3 * D_MODEL))


INPROJ_COL_TILES = (512, 1024, 1408, 1536, 2816)


def _inproj_kernel(*refs, n_lat_tiles):
    nw_ref, mod_ref, w_ref, o_ref, h_ref = refs[-5:]

    @pl.when(pl.program_id(1) == 0)
    def _():
        x = _select_rows(refs[:-5], n_lat_tiles)
        y = x * lax.rsqrt(jnp.mean(x * x, axis=-1, keepdims=True) + EPS) * nw_ref[...]
        shift = mod_ref[:, 0:D_MODEL]
        scale = mod_ref[:, D_MODEL:2 * D_MODEL]
        h_ref[...] = (y * (1.0 + scale) + shift).astype(BF16)

    o_ref[...] = _dot(h_ref[...], w_ref[...]).astype(o_ref.dtype)


def _select_rows(x_refs, n_lat_tiles):
    if len(x_refs) == 1:
        return x_refs[0][...]
    return jnp.where(pl.program_id(0) < n_lat_tiles, x_refs[0][...], x_refs[1][...])


def _row_specs(x_parts, tm, width, n_lat_tiles):
    if len(x_parts) == 1:
        return [pl.BlockSpec((tm, width), lambda i, j: (i, 0))]
    assert x_parts[1].shape[0] == tm
    return [pl.BlockSpec((tm, width), lambda i, j: (jnp.minimum(i, n_lat_tiles - 1), 0)),
            pl.BlockSpec((tm, width), lambda i, j: (0, 0))]


def _mod_row(i, lat_tiles, batch):
    return jnp.minimum(i // lat_tiles, batch)


def _inproj(x_parts, norm_w, mod, w_stack_bf16, layer, batch, seq):
    m = sum(p.shape[0] for p in x_parts)
    n = w_stack_bf16.shape[2]
    tm = ROW_TILE
    tn = max(t for t in INPROJ_COL_TILES if n % t == 0)
    lat_tiles = seq // tm
    n_lat_tiles = batch * lat_tiles
    return pl.pallas_call(
        functools.partial(_inproj_kernel, n_lat_tiles=n_lat_tiles),
        grid=(m // tm, n // tn),
        in_specs=_row_specs(x_parts, tm, D_MODEL, n_lat_tiles) + [
            pl.BlockSpec((1, D_MODEL), lambda i, j: (0, 0)),
            pl.BlockSpec((None, 1, 3 * D_MODEL), lambda i, j: (_mod_row(i, lat_tiles, batch), 0, 0)),
            pl.BlockSpec((None, D_MODEL, tn), lambda i, j: (layer, 0, j))],
        out_specs=pl.BlockSpec((tm, tn), lambda i, j: (i, j)),
        out_shape=jax.ShapeDtypeStruct((m, n), BF16),
        scratch_shapes=[pltpu.VMEM((tm, D_MODEL), BF16)],
        compiler_params=_cparams(2),
        name="inproj",
    )(*x_parts, norm_w.reshape(1, D_MODEL), mod, w_stack_bf16)


Q_PRESCALE = (HEAD_DIM ** -0.5) * 1.4426950408889634


def _qkprep_kernel(q_ref, k_ref, v_ref, cos_ref, sin_ref, qw_ref, kw_ref, qo_ref, ko_ref, vo_ref):
    cos = cos_ref[...]
    sin = sin_ref[...]
    lane = lax.broadcasted_iota(jnp.int32, cos.shape, 1)
    first = (lane % (AXIS_DIM)) < (AXIS_DIM // 2)

    def prep(x, w):
        y = x * lax.rsqrt(jnp.mean(x * x, axis=-1, keepdims=True) + EPS) * w
        rot = jnp.where(first, pltpu.roll(y, HEAD_DIM - AXIS_DIM // 2, 1), pltpu.roll(y, AXIS_DIM // 2, 1))
        return y * cos + rot * sin

    for h in range(B_HEADS):
        sl = slice(h * HEAD_DIM, (h + 1) * HEAD_DIM)
        qo_ref[:, sl] = (prep(q_ref[:, sl].astype(F32), qw_ref[...]) * Q_PRESCALE).astype(qo_ref.dtype)
    for h in range(B_KV_HEADS):
        sl = slice(h * HEAD_DIM, (h + 1) * HEAD_DIM)
        ko_ref[:, sl] = prep(k_ref[:, sl].astype(F32), kw_ref[...]).astype(ko_ref.dtype)
    vo_ref[...] = v_ref[...].astype(vo_ref.dtype)


def _qkprep(proj, cos_tab, sin_tab, qn_w, kn_w, batch, seq):
    m = proj.shape[0]
    tm = CONV_ROW_TILE
    lat_tiles = seq // tm
    n_lat = batch * lat_tiles
    q_blk = (3 * A_WIDTH) // B_Q_W
    k_blk = (3 * A_WIDTH + B_Q_W) // B_KV_W

    def tab_idx(i):
        return (jnp.where(i < n_lat, i % lat_tiles, lat_tiles), 0)

    def kv_idx(i):
        lat_blk = (i // lat_tiles) * (lat_tiles + 1) + 1 + i % lat_tiles
        return (jnp.where(i < n_lat, lat_blk, (i - n_lat) * (lat_tiles + 1)), 0)

    return pl.pallas_call(
        _qkprep_kernel,
        grid=(m // tm,),
        in_specs=[pl.BlockSpec((tm, B_Q_W), lambda i: (i, q_blk)),
                  pl.BlockSpec((tm, B_KV_W), lambda i: (i, k_blk)),
                  pl.BlockSpec((tm, B_KV_W), lambda i: (i, k_blk + 1)),
                  pl.BlockSpec((tm, HEAD_DIM), tab_idx),
                  pl.BlockSpec((tm, HEAD_DIM), tab_idx),
                  pl.BlockSpec((1, HEAD_DIM), lambda i: (0, 0)),
                  pl.BlockSpec((1, HEAD_DIM), lambda i: (0, 0))],
        out_specs=[pl.BlockSpec((tm, B_Q_W), lambda i: (i, 0)),
                   pl.BlockSpec((tm, B_KV_W), kv_idx),
                   pl.BlockSpec((tm, B_KV_W), kv_idx)],
        out_shape=[jax.ShapeDtypeStruct((m, B_Q_W), BF16),
                   jax.ShapeDtypeStruct((m, B_KV_W), BF16),
                   jax.ShapeDtypeStruct((m, B_KV_W), BF16)],
        compiler_params=_cparams(1),
        name="qk_prep",
    )(proj, proj, proj, cos_tab, sin_tab, qn_w.reshape(1, HEAD_DIM), kn_w.reshape(1, HEAD_DIM))


def _rope_tables(seq, tm):
    rows = seq // GRID_W
    row = jnp.repeat(jnp.arange(rows), GRID_W).astype(F32)
    col = jnp.tile(jnp.arange(GRID_W), rows).astype(F32)
    freqs = ROPE_THETA ** (-jnp.arange(0, AXIS_DIM, 2, dtype=F32) / AXIS_DIM)
    ang_r = row[:, None] * freqs[None, :]
    ang_c = col[:, None] * freqs[None, :]
    ang = jnp.concatenate([ang_r, ang_r, ang_c, ang_c], axis=-1)
    sign = jnp.where((jnp.arange(HEAD_DIM) % AXIS_DIM) < AXIS_DIM // 2, -1.0, 1.0).astype(F32)
    cos = jnp.concatenate([jnp.cos(ang), jnp.ones((tm, HEAD_DIM), F32)], axis=0)
    sin = jnp.concatenate([jnp.sin(ang) * sign[None, :], jnp.zeros((tm, HEAD_DIM), F32)], axis=0)
    return cos, sin


ATTN_Q_TILE = 256
ATTN_KV_BODY = 4096
ATTN_KV_TAIL = 768


def _kv_chunks(kv_len):
    tail = min(ATTN_KV_TAIL, kv_len)
    body = kv_len - tail
    sizes = [ATTN_KV_BODY] * (body // ATTN_KV_BODY) + ([body % ATTN_KV_BODY] if body % ATTN_KV_BODY else []) + [tail]
    assert all(c % LANES == 0 for c in sizes) and sum(sizes) == kv_len
    return tuple(sizes)


def _attn_kernel(q_ref, k_ref, v_ref, o_ref, m_scr, l_scr, acc_scr, *, chunks, tq):
    q = jnp.concatenate([q_ref[:, g * HEAD_DIM:(g + 1) * HEAD_DIM] for g in range(B_GROUP)], axis=0)
    m_scr[...] = jnp.full(m_scr.shape, -jnp.inf, F32)
    l_scr[...] = jnp.zeros(l_scr.shape, F32)
    acc_scr[...] = jnp.zeros(acc_scr.shape, F32)
    start = 0
    for tk in chunks:
        k = k_ref[start:start + tk, :]
        v = v_ref[start:start + tk, :]
        start += tk
        s = _dot_nt(q, k)
        m_prev = m_scr[...]
        m_next = jnp.maximum(m_prev, jnp.max(s, axis=1, keepdims=True))
        p = jnp.exp2(s - jnp.concatenate([m_next] * (tk // LANES), axis=1))
        alpha = jnp.exp2(m_prev - m_next)
        pv = _dot(p.astype(BF16), jnp.concatenate([v, jnp.ones_like(v)], axis=1))
        l_scr[...] = alpha * l_scr[...] + pv[:, HEAD_DIM:]
        acc_scr[...] = acc_scr[...] * alpha + pv[:, :HEAD_DIM]
        m_scr[...] = m_next
    out = acc_scr[...] / l_scr[...]
    for g in range(B_GROUP):
        o_ref[:, g * HEAD_DIM:(g + 1) * HEAD_DIM] = out[g * tq:(g + 1) * tq].astype(o_ref.dtype)


def _attention(qr, kr, vr, batch, seq, ctx_len, latent):
    tq = ATTN_Q_TILE
    gw = B_GROUP * HEAD_DIM
    q_len = seq if latent else ctx_len
    q_tiles = q_len // tq
    q_row0 = 0 if latent else batch * seq // tq
    kv_len = ctx_len + seq
    if latent:
        kv_rows = kv_len
        kv_idx = lambda b, h, i: (b, h)
    else:
        kv_rows = ctx_len
        kv_idx = lambda b, h, i: (b * (kv_len // ctx_len), h)
    rows = B_GROUP * tq
    return pl.pallas_call(
        functools.partial(_attn_kernel, chunks=_kv_chunks(kv_rows), tq=tq),
        grid=(batch, B_KV_HEADS, q_tiles),
        in_specs=[pl.BlockSpec((tq, gw), lambda b, h, i: (q_row0 + b * q_tiles + i, h)),
                  pl.BlockSpec((kv_rows, HEAD_DIM), kv_idx),
                  pl.BlockSpec((kv_rows, HEAD_DIM), kv_idx)],
        out_specs=pl.BlockSpec((tq, gw), lambda b, h, i: (b * q_tiles + i, h)),
        out_shape=jax.ShapeDtypeStruct((batch * q_len, B_Q_W), BF16),
        scratch_shapes=[pltpu.VMEM((rows, LANES), F32), pltpu.VMEM((rows, LANES), F32),
                        pltpu.VMEM((rows, HEAD_DIM), F32)],
        compiler_params=_cparams(3),
        name="attn_lat" if latent else "attn_ctx",
    )(qr, kr, vr)


def _ab_out_kernel(u_ref, v_ref, ga_ref, gb0_ref, gb1_ref, aol_ref, aoc_ref, ws_ref, bs_ref, w_ref, gate_ref, *rest,
                   n_lat_tiles):
    o_ref, y_ref, acc_ref = rest[-3:]
    k = pl.program_id(1)

    @pl.when(k == 0)
    def _():
        tm = u_ref.shape[0]
        for c in range(tm // SGU_CHUNK):
            rows = slice(c * SGU_CHUNK, (c + 1) * SGU_CHUNK)
            for g in range(A_GROUPS):
                cols = slice(g * LANES, (g + 1) * LANES)
                vg = v_ref[rows, cols].astype(F32)
                d = vg - jnp.mean(vg, axis=-1, keepdims=True)
                var = jnp.mean(d * d, axis=-1, keepdims=True)
                vn = (d * lax.rsqrt(var + 1e-5)).astype(BF16)
                mixed = _dot(ws_ref[g], vn) + bs_ref[:, g:g + 1]
                y_ref[rows, cols] = (u_ref[rows, cols].astype(F32) * mixed
                                     * _silu(ga_ref[rows, cols].astype(F32))).astype(BF16)
        acc_ref[...] = _dot(y_ref[...], w_ref[0:A_WIDTH, :])

    @pl.when(k == 1)
    def _():
        ao = jnp.where(pl.program_id(0) < n_lat_tiles, aol_ref[...], aoc_ref[...]).astype(F32)
        gb = jnp.concatenate([gb0_ref[...], gb1_ref[...]], axis=1).astype(F32)
        y = (ao * _silu(gb)).astype(BF16)
        o_ref[...] = _select_rows(rest[:-3], n_lat_tiles) + gate_ref[...] * (acc_ref[...] + _dot(y, w_ref[A_WIDTH:, :]))


def _ab_out(proj, ao_lat, ao_ctx, sgu_w_stack_bf16, sgu_b_t, w_out_stack_bf16, layer, x_parts, mod, batch, seq):
    m = proj.shape[0]
    tm = ROW_TILE
    lat_tiles = seq // tm
    n_lat_tiles = batch * lat_tiles
    assert ao_ctx.shape[0] == tm
    gb_blk0 = (3 * A_WIDTH + B_Q_W + 2 * B_KV_W) // (B_Q_W // 2)
    return pl.pallas_call(
        functools.partial(_ab_out_kernel, n_lat_tiles=n_lat_tiles),
        grid=(m // tm, 2),
        in_specs=[pl.BlockSpec((tm, A_WIDTH), lambda i, k: (i, 0)),
                  pl.BlockSpec((tm, A_WIDTH), lambda i, k: (i, 1)),
                  pl.BlockSpec((tm, A_WIDTH), lambda i, k: (i, 2)),
                  pl.BlockSpec((tm, B_Q_W // 2), lambda i, k: (i, gb_blk0)),
                  pl.BlockSpec((tm, B_Q_W // 2), lambda i, k: (i, gb_blk0 + 1)),
                  pl.BlockSpec((tm, B_Q_W), lambda i, k: (jnp.minimum(i, n_lat_tiles - 1), 0)),
                  pl.BlockSpec((tm, B_Q_W), lambda i, k: (0, 0)),
                  pl.BlockSpec((None, A_GROUPS, SGU_CHUNK, SGU_CHUNK), lambda i, k: (layer, 0, 0, 0)),
                  pl.BlockSpec((SGU_CHUNK, A_GROUPS), lambda i, k: (0, 0)),
                  pl.BlockSpec((None, A_WIDTH + B_Q_W, D_MODEL), lambda i, k: (layer, 0, 0),
                               pipeline_mode=pl.Buffered(1)),
                  pl.BlockSpec((None, 1, D_MODEL), lambda i, k: (_mod_row(i, lat_tiles, batch), 0, 2))]
        + _row_specs(x_parts, tm, D_MODEL, n_lat_tiles),
        out_specs=pl.BlockSpec((tm, D_MODEL), lambda i, k: (i, 0)),
        out_shape=jax.ShapeDtypeStruct((m, D_MODEL), F32),
        scratch_shapes=[pltpu.VMEM((tm, A_WIDTH), BF16), pltpu.VMEM((tm, D_MODEL), F32)],
        compiler_params=_cparams(2),
        name="ab_out",
    )(proj, proj, proj, proj, proj, ao_lat, ao_ctx, sgu_w_stack_bf16, sgu_b_t, w_out_stack_bf16, mod, *x_parts)


DN_HALO = 2 * SUBLANES
DN_PROJ_COL_TILE = 2048


def _inproj_dn_kernel(xp_ref, x_ref, xn_ref, nw_ref, mod_ref, w_ref, ws_ref, cw_ref, qkv_ref, z_ref, ba_ref, h_ref, p_ref,
                      *, lat_tiles, n_lat, q_tiles, qk_tiles, qkv_tiles, ctx_len):
    i = pl.program_id(0)
    j = pl.program_id(1)
    tm = x_ref.shape[0]
    is_lat = i < n_lat
    pos = i % lat_tiles
    first = jnp.logical_or(jnp.logical_not(is_lat), pos == 0)
    last = jnp.logical_or(jnp.logical_not(is_lat), pos == lat_tiles - 1)
    half = CONV_K // 2

    @pl.when(j == 0)
    def _():
        shift = mod_ref[:, 0:D_MODEL]
        scale = mod_ref[:, D_MODEL:2 * D_MODEL]

        def norm_mod(x):
            y = x * lax.rsqrt(jnp.mean(x * x, axis=-1, keepdims=True) + EPS) * nw_ref[...]
            return y * (1.0 + scale) + shift

        h_ref[0:DN_HALO, :] = jnp.where(first, 0.0, norm_mod(xp_ref[...])).astype(BF16)
        h_ref[DN_HALO:DN_HALO + tm, :] = norm_mod(x_ref[...]).astype(BF16)
        h_ref[DN_HALO + tm:, :] = jnp.where(last, 0.0, norm_mod(xn_ref[...])).astype(BF16)
        ba_ref[...] = _dot(h_ref[DN_HALO:DN_HALO + tm, :], ws_ref[...])

    n_ext = h_ref.shape[0]
    tn = w_ref.shape[1]
    pair_w = 2 * HEAD_DIM
    row = lax.broadcasted_iota(jnp.int32, (tm, HEAD_DIM), 0)
    is_ctx = jnp.logical_not(is_lat)

    def conv_silu(ph, sl):
        acc = cw_ref[half:half + 1, sl] * ph[DN_HALO:DN_HALO + tm]
        for t in range(CONV_K):
            off = t - half
            if off == 0:
                continue
            tap = pltpu.roll(ph, (-off) % n_ext, 0)[DN_HALO:DN_HALO + tm]
            crosses = (jnp.logical_and(row >= ctx_len - off, row < ctx_len) if off > 0
                       else jnp.logical_and(row >= ctx_len, row < ctx_len - off))
            tap = jnp.where(jnp.logical_and(is_ctx, crosses), 0.0, tap)
            acc = acc + cw_ref[t:t + 1, sl] * tap
        return _silu(acc)

    def for_each_head(store):
        for n, c0 in enumerate(range(0, tn, pair_w)):
            p_ref[n % 2] = _dot(h_ref[...], w_ref[:, c0:c0 + pair_w])
            for h0 in range(0, pair_w, HEAD_DIM):
                sl = slice(c0 + h0, c0 + h0 + HEAD_DIM)
                store(sl, conv_silu(p_ref[n % 2, :, h0:h0 + HEAD_DIM], sl))

    @pl.when(j < qk_tiles)
    def _():
        qk_scale = jnp.where(j < q_tiles, HEAD_DIM ** -0.5, 1.0)

        def store(sl, y):
            qkv_ref[:, sl] = (y * (lax.rsqrt(jnp.sum(y * y, axis=-1, keepdims=True) + EPS) * qk_scale)).astype(qkv_ref.dtype)
        for_each_head(store)

    @pl.when(jnp.logical_and(j >= qk_tiles, j < qkv_tiles))
    def _():
        def store(sl, y):
            qkv_ref[:, sl] = y.astype(qkv_ref.dtype)
        for_each_head(store)

    @pl.when(j >= qkv_tiles)
    def _():
        z_ref[...] = _dot(h_ref[DN_HALO:DN_HALO + tm, :], w_ref[...]).astype(z_ref.dtype)


def _inproj_dn(x, norm_w, mod, w_stack_bf16, layer, conv_w, batch, seq, ctx_len):
    m = x.shape[0]
    tm, tn = ROW_TILE, DN_PROJ_COL_TILE
    n_side = 4 * DN_V_HEADS
    n = w_stack_bf16.shape[2] - n_side
    assert n == DN_QKV_W + DN_V_W and tm == 2 * ctx_len
    lat_tiles = seq // tm
    n_lat = batch * lat_tiles
    halo_per_tile = tm // DN_HALO
    n_halo = m // DN_HALO
    qkv_tiles = DN_QKV_W // tn
    kern = functools.partial(_inproj_dn_kernel, lat_tiles=lat_tiles, n_lat=n_lat, q_tiles=DN_K_W // tn,
                             qk_tiles=2 * DN_K_W // tn, qkv_tiles=qkv_tiles, ctx_len=ctx_len)
    return pl.pallas_call(
        kern,
        grid=(m // tm, n // tn),
        in_specs=[pl.BlockSpec((DN_HALO, D_MODEL), lambda i, j: (jnp.maximum(i * halo_per_tile - 1, 0), 0)),
                  pl.BlockSpec((tm, D_MODEL), lambda i, j: (i, 0)),
                  pl.BlockSpec((DN_HALO, D_MODEL), lambda i, j: (jnp.minimum((i + 1) * halo_per_tile, n_halo - 1), 0)),
                  pl.BlockSpec((1, D_MODEL), lambda i, j: (0, 0)),
                  pl.BlockSpec((None, 1, 3 * D_MODEL), lambda i, j: (_mod_row(i, lat_tiles, batch), 0, 0)),
                  pl.BlockSpec((None, D_MODEL, tn), lambda i, j: (layer, 0, j)),
                  pl.BlockSpec((None, D_MODEL, n_side), lambda i, j: (layer, 0, n // n_side)),
                  pl.BlockSpec((SUBLANES, tn), lambda i, j: (0, jnp.minimum(j, qkv_tiles - 1)))],
        out_specs=[pl.BlockSpec((tm, tn), lambda i, j: (i, jnp.minimum(j, qkv_tiles - 1))),
                   pl.BlockSpec((tm, tn), lambda i, j: (i, jnp.maximum(j - qkv_tiles, 0))),
                   pl.BlockSpec((tm, n_side), lambda i, j: (i, 0))],
        out_shape=[jax.ShapeDtypeStruct((m, DN_QKV_W), F32),
                   jax.ShapeDtypeStruct((m, DN_V_W), BF16),
                   jax.ShapeDtypeStruct((m, n_side), F32)],
        scratch_shapes=[pltpu.VMEM((tm + 2 * DN_HALO, D_MODEL), BF16),
                        pltpu.VMEM((2, tm + 2 * DN_HALO, 2 * HEAD_DIM), F32)],
        compiler_params=_cparams(2),
        name="inproj_dn",
    )(x, x, x, norm_w.reshape(1, D_MODEL), mod, w_stack_bf16, w_stack_bf16, conv_w)


def _dngate_kernel(ba_ref, alog_ref, dtb_ref, o_ref):
    ba = ba_ref[...]
    tm = ba.shape[0]
    lane = lax.broadcasted_iota(jnp.int32, (DN_CHUNK, LANES), 1)
    is_beta = (lane // DN_V_HEADS) % 2 == 0
    is_fwd = lane < 2 * DN_V_HEADS
    z = ba + dtb_ref[...]
    softplus = jnp.maximum(z, 0.0) + jnp.log1p(jnp.exp(-jnp.abs(z)))
    g = -jnp.exp(alog_ref[...]) * softplus
    r = lax.broadcasted_iota(jnp.int32, (DN_CHUNK, DN_CHUNK), 0)
    c = lax.broadcasted_iota(jnp.int32, (DN_CHUNK, DN_CHUNK), 1)
    tri_lo = (r >= c).astype(BF16)
    tri_up = (r <= c).astype(BF16)
    beta = jax.nn.sigmoid(ba)
    for ch in range(tm // DN_CHUNK):
        rows = slice(ch * DN_CHUNK, (ch + 1) * DN_CHUNK)
        gch = g[rows]
        g1 = gch.astype(BF16)
        r1 = gch - g1.astype(F32)
        g2 = r1.astype(BF16)
        g3 = (r1 - g2.astype(F32)).astype(BF16)
        pre = _dot(tri_lo, g1) + _dot(tri_lo, g2) + _dot(tri_lo, g3)
        suf = _dot(tri_up, g1) + _dot(tri_up, g2) + _dot(tri_up, g3)
        gc = jnp.where(is_fwd, pre, suf)
        o_ref[rows, :] = jnp.where(is_beta, beta[rows], gc)


def _dn_gate(ba, alog_vec, dtb_vec):
    m = ba.shape[0]
    tm = ROW_TILE
    return pl.pallas_call(
        _dngate_kernel,
        grid=(m // tm,),
        in_specs=[pl.BlockSpec((tm, LANES), lambda i: (i, 0)),
                  pl.BlockSpec((1, LANES), lambda i: (0, 0)),
                  pl.BlockSpec((1, LANES), lambda i: (0, 0))],
        out_specs=pl.BlockSpec((tm, LANES), lambda i: (i, 0)),
        out_shape=jax.ShapeDtypeStruct((m, LANES), F32),
        compiler_params=_cparams(1),
        name="dn_gate",
    )(ba, alog_vec, dtb_vec)


DN_CHAIN_GROUP = 16
DN_KH_PER_STEP = 16


def _dncore_kernel(qf_ref, kf_ref, vf_ref, gf_ref, qb_ref, kb_ref, vb_ref, gb_ref, of_ref, ob_ref, s_ref):
    C = DN_CHUNK
    nvh = 2 * DN_KH_PER_STEP

    @pl.when(pl.program_id(2) == 0)
    def _():
        s_ref[...] = jnp.zeros(s_ref.shape, F32)

    lane = lax.broadcasted_iota(jnp.int32, (C, 2 * C), 1)
    left = lane < C
    row = lax.broadcasted_iota(jnp.int32, (C, 2 * C), 0)
    colp = lane % C
    left_sq = lax.broadcasted_iota(jnp.int32, (2 * C, 2 * C), 1) < C
    eye2 = (row == colp).astype(F32)

    def blockdiag(p):
        z = jnp.zeros_like(p)
        return jnp.concatenate([jnp.where(left, p, z), jnp.where(left, z, p)], axis=0)

    def packed_mm(a, b):
        return _dot(a.astype(BF16), blockdiag(b.astype(BF16)))

    dirs = ((qf_ref, kf_ref, vf_ref, gf_ref, of_ref), (qb_ref, kb_ref, vb_ref, gb_ref, ob_ref))
    chains = [(d, kh) for d in range(2) for kh in range(DN_KH_PER_STEP)]
    G = [dirs[d][3][...] for d in range(2)]
    GT = [jnp.concatenate([g, g], axis=0).T for g in G]

    for g0 in range(0, len(chains), DN_CHAIN_GROUP):
        group = chains[g0:g0 + DN_CHAIN_GROUP]
        st = []
        for d, kh in group:
            q_ref, k_ref, v_ref, _, _ = dirs[d]
            base_beta = d * 2 * nvh
            base_gc = base_beta + nvh
            lv0 = 2 * kh
            q = q_ref[:, kh * HEAD_DIM:(kh + 1) * HEAD_DIM]
            k = k_ref[:, kh * HEAD_DIM:(kh + 1) * HEAD_DIM]
            k2 = jnp.concatenate([k, k], axis=0)
            kT2 = k2.T
            gram = _dot(jnp.concatenate([q, k], axis=0).astype(BF16), kT2.astype(BF16))
            b0, b1 = G[d][:, base_beta + lv0:base_beta + lv0 + 1], G[d][:, base_beta + lv0 + 1:base_beta + lv0 + 2]
            c0, c1 = G[d][:, base_gc + lv0:base_gc + lv0 + 1], G[d][:, base_gc + lv0 + 1:base_gc + lv0 + 2]
            r0, r1 = GT[d][base_gc + lv0:base_gc + lv0 + 1, :], GT[d][base_gc + lv0 + 1:base_gc + lv0 + 2, :]
            st.append(dict(d=d, lv0=lv0, q=q, k2=k2, kT2=kT2, gram=gram, b0=b0, b1=b1, c0=c0, c1=c1, r0=r0, r1=r1))

        for s in st:
            d = s["d"]
            incl = (row >= colp) if d == 0 else (row <= colp)
            strict = (row > colp) if d == 0 else (row < colp)
            gcol_p = jnp.where(left, s["c0"], s["c1"])
            grow_p = jnp.where(left[0:1], s["r0"], s["r1"])
            beta_p = jnp.where(left, s["b0"], s["b1"])
            dec = jnp.exp(jnp.where(incl, gcol_p - grow_p, -1e30))
            s["dec"] = dec
            s["attn"] = s["gram"][0:C] * dec
            s["L"] = jnp.where(strict, s["gram"][C:2 * C] * dec, 0.0) * beta_p

        for s in st:
            n1 = jnp.where(jnp.logical_and(row // 2 == colp // 2, row != colp), s["L"], 0.0)
            s["X"] = eye2 - n1
        blk = 2
        while blk < C:
            mask = jnp.logical_and(row // (2 * blk) == colp // (2 * blk), row // blk != colp // blk)
            for s in st:
                s["Y"] = packed_mm(s["X"], jnp.where(mask, s["L"], 0.0))
            for s in st:
                s["X"] = s["X"] - packed_mm(s["Y"], s["X"])
            blk *= 2

        for s in st:
            d, lv0 = s["d"], s["lv0"]
            v_ref = dirs[d][2]
            beta_r = jnp.concatenate([s["b0"], s["b1"]], axis=0)
            egc_r = jnp.exp(jnp.concatenate([s["c0"], s["c1"]], axis=0))
            v2 = jnp.concatenate([v_ref[:, lv0 * HEAD_DIM:(lv0 + 1) * HEAD_DIM],
                                  v_ref[:, (lv0 + 1) * HEAD_DIM:(lv0 + 2) * HEAD_DIM]], axis=0)
            rhs = jnp.concatenate([v2 * beta_r, s["k2"] * (beta_r * egc_r)], axis=1)
            s["sol"] = _dot(blockdiag(s["X"]).astype(BF16), rhs.astype(BF16))
            s["egc"] = egc_r
            s["qg2"] = jnp.concatenate([s["q"], s["q"]], axis=0) * egc_r

        for s in st:
            d, lv0 = s["d"], s["lv0"]
            w2 = s["sol"][:, HEAD_DIM:]
            s["ws"] = []
            for r in range(2):
                lhs = jnp.concatenate([w2[r * C:(r + 1) * C], s["qg2"][r * C:(r + 1) * C]], axis=0).astype(BF16)
                s["ws"].append(_dot(lhs, s_ref[d, lv0 + r].astype(BF16)))

        for s in st:
            u2 = s["sol"][:, 0:HEAD_DIM]
            vn2 = jnp.concatenate([u2[r * C:(r + 1) * C] - s["ws"][r][0:C] for r in range(2)], axis=0).astype(BF16)
            s["vn2"] = vn2
            s["o2"] = jnp.concatenate([s["ws"][r][C:2 * C] for r in range(2)], axis=0) + _dot(
                blockdiag(s["attn"]).astype(BF16), vn2)

        for s in st:
            d, lv0 = s["d"], s["lv0"]
            o_ref = dirs[d][4]
            last = C - 1 if d == 0 else 0
            kdT_p = s["kT2"] * s["dec"][last:last + 1, :]
            zkd = jnp.zeros_like(kdT_p)
            for r in range(2):
                kd_r = jnp.where(left_sq if r == 0 else jnp.logical_not(left_sq), kdT_p, zkd).astype(BF16)
                glr = jnp.broadcast_to(s["egc"][r * C + last:r * C + last + 1, :], (HEAD_DIM, HEAD_DIM))
                s_ref[d, lv0 + r] = s_ref[d, lv0 + r] * glr + _dot(kd_r, s["vn2"])
                o_ref[:, (lv0 + r) * HEAD_DIM:(lv0 + r + 1) * HEAD_DIM] = s["o2"][r * C:(r + 1) * C].astype(o_ref.dtype)


def _dn_core(qkv, gates, batch, seq, ctx_len):
    m = qkv.shape[0]
    C = DN_CHUNK
    n_lat = seq // C
    n_ctx = ctx_len // C
    n_steps = n_ctx + n_lat
    ctx0 = batch * n_lat
    khs = DN_KH_PER_STEP
    qw = khs * HEAD_DIM
    vw = 2 * khs * HEAD_DIM
    k_blk0 = DN_K_W // qw
    v_blk0 = 2 * DN_K_W // vw

    def rf(b, t):
        return jnp.where(t < n_ctx, ctx0 + b * n_ctx + t, b * n_lat + (t - n_ctx))

    def rb(b, t):
        return jnp.where(t < n_ctx, ctx0 + b * n_ctx + (n_ctx - 1 - t), b * n_lat + (n_lat - 1 - (t - n_ctx)))

    def specs(rfun):
        return [pl.BlockSpec((C, qw), lambda b, h, t: (rfun(b, t), h)),
                pl.BlockSpec((C, qw), lambda b, h, t: (rfun(b, t), k_blk0 + h)),
                pl.BlockSpec((C, vw), lambda b, h, t: (rfun(b, t), v_blk0 + h)),
                pl.BlockSpec((C, LANES), lambda b, h, t: (rfun(b, t), h))]

    return pl.pallas_call(
        _dncore_kernel,
        grid=(batch, DN_K_HEADS // khs, n_steps),
        in_specs=specs(rf) + specs(rb),
        out_specs=[pl.BlockSpec((C, vw), lambda b, h, t: (rf(b, t), h)),
                   pl.BlockSpec((C, vw), lambda b, h, t: (rb(b, t), h))],
        out_shape=[jax.ShapeDtypeStruct((m, DN_V_W), BF16), jax.ShapeDtypeStruct((m, DN_V_W), BF16)],
        scratch_shapes=[pltpu.VMEM((2, 2 * khs, HEAD_DIM, HEAD_DIM), F32)],
        compiler_params=_cparams(3),
        name="dn_core",
    )(qkv, qkv, qkv, gates, qkv, qkv, qkv, gates)


DN_OUT_ROW_TILE = 256


def _dn_out_kernel(of_ref, ob_ref, z_ref, nw_ref, w_ref, x_ref, gate_ref, *rest):
    o_ref, y_ref = rest[-2:]
    for h in range(of_ref.shape[1] // HEAD_DIM):
        sl = slice(h * HEAD_DIM, (h + 1) * HEAD_DIM)
        o = of_ref[:, sl].astype(F32) + ob_ref[:, sl].astype(F32)
        n = o * lax.rsqrt(jnp.mean(o * o, axis=-1, keepdims=True) + EPS) * nw_ref[...]
        y_ref[:, sl] = (n * _silu(z_ref[:, sl].astype(F32))).astype(BF16)
    y = x_ref[...] + gate_ref[...] * _dot(y_ref[...], w_ref[...])
    if len(rest) == 3:
        y = y * lax.rsqrt(jnp.mean(y * y, axis=-1, keepdims=True) + EPS) * rest[0][...]
    o_ref[...] = y


def _dn_out(o_f, o_b, z, norm_w, w_out_stack_bf16, layer, x, mod, batch, seq, final_norm_w=None):
    tm = DN_OUT_ROW_TILE
    m = x.shape[0] if final_norm_w is None else batch * seq
    lat_tiles = seq // tm
    extra_specs, extra_args = [], []
    if final_norm_w is not None:
        extra_specs = [pl.BlockSpec((1, D_MODEL), lambda i: (0, 0))]
        extra_args = [final_norm_w.reshape(1, D_MODEL)]
    return pl.pallas_call(
        _dn_out_kernel,
        grid=(m // tm,),
        in_specs=[pl.BlockSpec((tm, DN_V_W), lambda i: (i, 0)),
                  pl.BlockSpec((tm, DN_V_W), lambda i: (i, 0)),
                  pl.BlockSpec((tm, DN_V_W), lambda i: (i, 0)),
                  pl.BlockSpec((1, HEAD_DIM), lambda i: (0, 0)),
                  pl.BlockSpec((None, DN_V_W, D_MODEL), lambda i: (layer, 0, 0), pipeline_mode=pl.Buffered(1)),
                  pl.BlockSpec((tm, D_MODEL), lambda i: (i, 0)),
                  pl.BlockSpec((None, 1, D_MODEL), lambda i: (_mod_row(i, lat_tiles, batch), 0, 2))] + extra_specs,
        out_specs=pl.BlockSpec((tm, D_MODEL), lambda i: (i, 0)),
        out_shape=jax.ShapeDtypeStruct((m, D_MODEL), F32),
        scratch_shapes=[pltpu.VMEM((tm, DN_V_W), BF16)],
        compiler_params=_cparams(1),
        name="dn_out",
    )(o_f, o_b, z, norm_w.reshape(1, HEAD_DIM), w_out_stack_bf16, x, mod, *extra_args)


def _final_norm_kernel(x_ref, w_ref, o_ref):
    x = x_ref[...]
    o_ref[...] = x * lax.rsqrt(jnp.mean(x * x, axis=-1, keepdims=True) + EPS) * w_ref[...]


def _final_norm(x, w, rows):
    tm = ROW_TILE
    return pl.pallas_call(
        _final_norm_kernel,
        grid=(rows // tm,),
        in_specs=[pl.BlockSpec((tm, D_MODEL), lambda i: (i, 0)),
                  pl.BlockSpec((1, D_MODEL), lambda i: (0, 0))],
        out_specs=pl.BlockSpec((tm, D_MODEL), lambda i: (i, 0)),
        out_shape=jax.ShapeDtypeStruct((rows, D_MODEL), F32),
        compiler_params=_cparams(1),
        name="final_norm",
    )(x, w.reshape(1, D_MODEL))


def kernel(x, c, ctx, c_ctx, norm_w, ada_w, ada_b, ab_w_in, ab_w_out, sgu_w, sgu_b, q_norm_w, k_norm_w,
           dn_w_in, dn_conv_w, dn_a_log, dn_dt_bias, dn_norm_w, dn_w_out, final_norm_w):
    batch, seq, _ = x.shape
    ctx_len = ctx.shape[1]
    depth = norm_w.shape[0]
    assert ctx_len == CONV_ROW_TILE and seq % ROW_TILE == 0 and batch * ctx_len == ROW_TILE
    n_lat_rows = batch * seq

    x_parts = (x.reshape(n_lat_rows, D_MODEL), ctx.reshape(batch * ctx_len, D_MODEL))
    ab_w_in_bf, ab_w_out_bf, sgu_w_bf = ab_w_in.astype(BF16), ab_w_out.astype(BF16), sgu_w.astype(BF16)
    dn_w_in_bf, dn_w_out_bf = dn_w_in.astype(BF16), dn_w_out.astype(BF16)
    cond = jnp.zeros((SUBLANES, D_MODEL), F32).at[0:batch].set(c).at[batch].set(c_ctx)
    mods = _ada_mod(cond, ada_w, ada_b)
    cos_tab, sin_tab = _rope_tables(seq, CONV_ROW_TILE)

    for i in range(depth):
        j = i // 2
        mod = mods[i].reshape(SUBLANES, 1, 3 * D_MODEL)
        if i % 2 == 0:
            proj = _inproj(x_parts, norm_w[i], mod, ab_w_in_bf, j, batch, seq)
            qr, kr, vr = _qkprep(proj, cos_tab, sin_tab, q_norm_w[j], k_norm_w[j], batch, seq)
            ao_lat = _attention(qr, kr, vr, batch, seq, ctx_len, True)
            ao_ctx = _attention(qr, kr, vr, batch, seq, ctx_len, False)
            x_parts = (_ab_out(proj, ao_lat, ao_ctx, sgu_w_bf, sgu_b[j].T, ab_w_out_bf, j, x_parts, mod, batch, seq),)
        else:
            conv_w = jnp.zeros((SUBLANES, DN_QKV_W), F32).at[0:CONV_K].set(dn_conv_w[j])
            assert len(x_parts) == 1
            qkv, z, ba = _inproj_dn(x_parts[0], norm_w[i], mod, dn_w_in_bf, j, conv_w, batch, seq, ctx_len)
            zeros = jnp.zeros((2, DN_V_HEADS), F32)
            alog_vec = jnp.concatenate([zeros, dn_a_log[j]], axis=1).reshape(1, LANES)
            dtb_vec = jnp.concatenate([zeros, dn_dt_bias[j]], axis=1).reshape(1, LANES)
            gates = _dn_gate(ba, alog_vec, dtb_vec)
            n_hg = DN_K_HEADS // DN_KH_PER_STEP
            nvh = 2 * DN_KH_PER_STEP
            g4 = gates
            if n_hg > 1:
                g4 = gates.reshape(-1, 4, n_hg, nvh).transpose(0, 2, 1, 3).reshape(-1, n_hg, 4 * nvh)
                g4 = jnp.pad(g4, ((0, 0), (0, 0), (0, LANES - 4 * nvh))).reshape(-1, n_hg * LANES)
            o_f, o_b = _dn_core(qkv, g4, batch, seq, ctx_len)
            x_parts = (_dn_out(o_f, o_b, z, dn_norm_w[j], dn_w_out_bf, j, x_parts[0], mod, batch, seq,
                               final_norm_w=final_norm_w if i == depth - 1 else None),)

    xs = x_parts[0]
    if depth % 2 == 1:
        xs = _final_norm(xs, final_norm_w, n_lat_rows)
    return xs.reshape(batch, seq, D_MODEL)
```

```python
import functools

import jax
import jax.numpy as jnp
from jax import lax
from jax.experimental import pallas as pl
from jax.experimental.pallas import tpu as pltpu

F32 = jnp.float32
BF16 = jnp.bfloat16

D_MODEL = 2048
GRID_W = 64
EPS = 1e-6
HEAD_DIM = 128
A_WIDTH = D_MODEL // 2
A_GROUPS = A_WIDTH // 128
SGU_CHUNK = 128
B_HEADS = (D_MODEL // 2) // HEAD_DIM
B_KV_HEADS = B_HEADS // 4
B_GROUP = B_HEADS // B_KV_HEADS
B_Q_W = B_HEADS * HEAD_DIM
B_KV_W = B_KV_HEADS * HEAD_DIM
ROPE_THETA = 10000.0
AXIS_DIM = HEAD_DIM // 2
AB_IN_W = 3 * A_WIDTH + 2 * B_Q_W + 2 * B_KV_W
DN_K_HEADS = D_MODEL // HEAD_DIM
DN_V_HEADS = 2 * DN_K_HEADS
DN_K_W = DN_K_HEADS * HEAD_DIM
DN_V_W = DN_V_HEADS * HEAD_DIM
DN_QKV_W = 2 * DN_K_W + DN_V_W
DN_IN_W = DN_QKV_W + DN_V_W + 4 * DN_V_HEADS
DN_CHUNK = 64
CONV_K = 5

V7X_VMEM_LIMIT_BYTES = 56 * 1024 * 1024
LANES = 128
SUBLANES = 8

ROW_TILE = 512
CONV_ROW_TILE = 256


def _cparams(n_axes):
    return pltpu.CompilerParams(dimension_semantics=("arbitrary",) * n_axes,
                                vmem_limit_bytes=V7X_VMEM_LIMIT_BYTES)


def _silu(x):
    return x * jax.nn.sigmoid(x)


def _split_bf16(a):
    hi = a.astype(BF16)
    lo = (a - hi.astype(F32)).astype(BF16)
    return hi, lo


def _dot(a, b):
    return jnp.dot(a, b, preferred_element_type=F32)


def _dot_nt(a, b):
    return lax.dot_general(a, b, (((1,), (1,)), ((), ())), preferred_element_type=F32)


def _dot3(a, b):
    ah, al = _split_bf16(a)
    bh, bl = _split_bf16(b)
    return _dot(ah, bh) + _dot(ah, bl) + _dot(al, bh)


def _ada_kernel(c_ref, w_ref, b_ref, o_ref):
    s = _silu(c_ref[...])
    o_ref[0] = _dot3(s, w_ref[0]) + b_ref[0]


def _ada_mod(cond, ada_w, ada_b):
    depth = ada_w.shape[0]
    tn = 1536
    return pl.pallas_call(
        _ada_kernel,
        grid=(depth, 3 * D_MODEL // tn),
        in_specs=[pl.BlockSpec((SUBLANES, D_MODEL), lambda l, j: (0, 0)),
                  pl.BlockSpec((1, D_MODEL, tn), lambda l, j: (l, 0, j)),
                  pl.BlockSpec((1, 1, tn), lambda l, j: (l, 0, j))],
        out_specs=pl.BlockSpec((1, SUBLANES, tn), lambda l, j: (l, 0, j)),
        out_shape=jax.ShapeDtypeStruct((depth, SUBLANES, 3 * D_MODEL), F32),
        compiler_params=_cparams(2),
        name="ada_mod",
    )(cond, ada_w, ada_b.reshape(depth, 1, 3 * D_MODEL))


INPROJ_COL_TILES = (512, 1024, 1408, 1536, 2816)


def _inproj_kernel(*refs, n_lat_tiles):
    nw_ref, mod_ref, w_ref, o_ref, h_ref = refs[-5:]

    @pl.when(pl.program_id(1) == 0)
    def _():
        x = _select_rows(refs[:-5], n_lat_tiles)
        y = x * lax.rsqrt(jnp.mean(x * x, axis=-1, keepdims=True) + EPS) * nw_ref[...]
        shift = mod_ref[:, 0:D_MODEL]
        scale = mod_ref[:, D_MODEL:2 * D_MODEL]
        h_ref[...] = (y * (1.0 + scale) + shift).astype(BF16)

    o_ref[...] = _dot(h_ref[...], w_ref[...]).astype(o_ref.dtype)


def _select_rows(x_refs, n_lat_tiles):
    if len(x_refs) == 1:
        return x_refs[0][...]
    return jnp.where(pl.program_id(0) < n_lat_tiles, x_refs[0][...], x_refs[1][...])


def _row_specs(x_parts, tm, width, n_lat_tiles):
    if len(x_parts) == 1:
        return [pl.BlockSpec((tm, width), lambda i, j: (i, 0))]
    assert x_parts[1].shape[0] == tm
    return [pl.BlockSpec((tm, width), lambda i, j: (jnp.minimum(i, n_lat_tiles - 1), 0)),
            pl.BlockSpec((tm, width), lambda i, j: (0, 0))]


def _mod_row(i, lat_tiles, batch):
    return jnp.minimum(i // lat_tiles, batch)


def _inproj(x_parts, norm_w, mod, w_stack_bf16, layer, batch, seq):
    m = sum(p.shape[0] for p in x_parts)
    n = w_stack_bf16.shape[2]
    tm = ROW_TILE
    tn = max(t for t in INPROJ_COL_TILES if n % t == 0)
    lat_tiles = seq // tm
    n_lat_tiles = batch * lat_tiles
    return pl.pallas_call(
        functools.partial(_inproj_kernel, n_lat_tiles=n_lat_tiles),
        grid=(m // tm, n // tn),
        in_specs=_row_specs(x_parts, tm, D_MODEL, n_lat_tiles) + [
            pl.BlockSpec((1, D_MODEL), lambda i, j: (0, 0)),
            pl.BlockSpec((None, 1, 3 * D_MODEL), lambda i, j: (_mod_row(i, lat_tiles, batch), 0, 0)),
            pl.BlockSpec((None, D_MODEL, tn), lambda i, j: (layer, 0, j))],
        out_specs=pl.BlockSpec((tm, tn), lambda i, j: (i, j)),
        out_shape=jax.ShapeDtypeStruct((m, n), BF16),
        scratch_shapes=[pltpu.VMEM((tm, D_MODEL), BF16)],
        compiler_params=_cparams(2),
        name="inproj",
    )(*x_parts, norm_w.reshape(1, D_MODEL), mod, w_stack_bf16)


Q_PRESCALE = (HEAD_DIM ** -0.5) * 1.4426950408889634


def _qkprep_kernel(q_ref, k_ref, v_ref, cos_ref, sin_ref, qw_ref, kw_ref, qo_ref, ko_ref, vo_ref):
    cos = cos_ref[...]
    sin = sin_ref[...]
    lane = lax.broadcasted_iota(jnp.int32, cos.shape, 1)
    first = (lane % (AXIS_DIM)) < (AXIS_DIM // 2)

    def prep(x, w):
        y = x * lax.rsqrt(jnp.mean(x * x, axis=-1, keepdims=True) + EPS) * w
        rot = jnp.where(first, pltpu.roll(y, HEAD_DIM - AXIS_DIM // 2, 1), pltpu.roll(y, AXIS_DIM // 2, 1))
        return y * cos + rot * sin

    for h in range(B_HEADS):
        sl = slice(h * HEAD_DIM, (h + 1) * HEAD_DIM)
        qo_ref[:, sl] = (prep(q_ref[:, sl].astype(F32), qw_ref[...]) * Q_PRESCALE).astype(qo_ref.dtype)
    for h in range(B_KV_HEADS):
        sl = slice(h * HEAD_DIM, (h + 1) * HEAD_DIM)
        ko_ref[:, sl] = prep(k_ref[:, sl].astype(F32), kw_ref[...]).astype(ko_ref.dtype)
    vo_ref[...] = v_ref[...].astype(vo_ref.dtype)


def _qkprep(proj, cos_tab, sin_tab, qn_w, kn_w, batch, seq):
    m = proj.shape[0]
    tm = CONV_ROW_TILE
    lat_tiles = seq // tm
    n_lat = batch * lat_tiles
    q_blk = (3 * A_WIDTH) // B_Q_W
    k_blk = (3 * A_WIDTH + B_Q_W) // B_KV_W

    def tab_idx(i):
        return (jnp.where(i < n_lat, i % lat_tiles, lat_tiles), 0)

    def kv_idx(i):
        lat_blk = (i // lat_tiles) * (lat_tiles + 1) + 1 + i % lat_tiles
        return (jnp.where(i < n_lat, lat_blk, (i - n_lat) * (lat_tiles + 1)), 0)

    return pl.pallas_call(
        _qkprep_kernel,
        grid=(m // tm,),
        in_specs=[pl.BlockSpec((tm, B_Q_W), lambda i: (i, q_blk)),
                  pl.BlockSpec((tm, B_KV_W), lambda i: (i, k_blk)),
                  pl.BlockSpec((tm, B_KV_W), lambda i: (i, k_blk + 1)),
                  pl.BlockSpec((tm, HEAD_DIM), tab_idx),
                  pl.BlockSpec((tm, HEAD_DIM), tab_idx),
                  pl.BlockSpec((1, HEAD_DIM), lambda i: (0, 0)),
                  pl.BlockSpec((1, HEAD_DIM), lambda i: (0, 0))],
        out_specs=[pl.BlockSpec((tm, B_Q_W), lambda i: (i, 0)),
                   pl.BlockSpec((tm, B_KV_W), kv_idx),
                   pl.BlockSpec((tm, B_KV_W), kv_idx)],
        out_shape=[jax.ShapeDtypeStruct((m, B_Q_W), BF16),
                   jax.ShapeDtypeStruct((m, B_KV_W), BF16),
                   jax.ShapeDtypeStruct((m, B_KV_W), BF16)],
        compiler_params=_cparams(1),
        name="qk_prep",
    )(proj, proj, proj, cos_tab, sin_tab, qn_w.reshape(1, HEAD_DIM), kn_w.reshape(1, HEAD_DIM))


def _rope_tables(seq, tm):
    rows = seq // GRID_W
    row = jnp.repeat(jnp.arange(rows), GRID_W).astype(F32)
    col = jnp.tile(jnp.arange(GRID_W), rows).astype(F32)
    freqs = ROPE_THETA ** (-jnp.arange(0, AXIS_DIM, 2, dtype=F32) / AXIS_DIM)
    ang_r = row[:, None] * freqs[None, :]
    ang_c = col[:, None] * freqs[None, :]
    ang = jnp.concatenate([ang_r, ang_r, ang_c, ang_c], axis=-1)
    sign = jnp.where((jnp.arange(HEAD_DIM) % AXIS_DIM) < AXIS_DIM // 2, -1.0, 1.0).astype(F32)
    cos = jnp.concatenate([jnp.cos(ang), jnp.ones((tm, HEAD_DIM), F32)], axis=0)
    sin = jnp.concatenate([jnp.sin(ang) * sign[None, :], jnp.zeros((tm, HEAD_DIM), F32)], axis=0)
    return cos, sin


ATTN_Q_TILE = 256
ATTN_KV_BODY = 4096
ATTN_KV_TAIL = 768


def _kv_chunks(kv_len):
    tail = min(ATTN_KV_TAIL, kv_len)
    body = kv_len - tail
    sizes = [ATTN_KV_BODY] * (body // ATTN_KV_BODY) + ([body % ATTN_KV_BODY] if body % ATTN_KV_BODY else []) + [tail]
    assert all(c % LANES == 0 for c in sizes) and sum(sizes) == kv_len
    return tuple(sizes)


def _attn_kernel(q_ref, k_ref, v_ref, o_ref, m_scr, l_scr, acc_scr, *, chunks, tq):
    q = jnp.concatenate([q_ref[:, g * HEAD_DIM:(g + 1) * HEAD_DIM] for g in range(B_GROUP)], axis=0)
    m_scr[...] = jnp.full(m_scr.shape, -jnp.inf, F32)
    l_scr[...] = jnp.zeros(l_scr.shape, F32)
    acc_scr[...] = jnp.zeros(acc_scr.shape, F32)
    start = 0
    for tk in chunks:
        k = k_ref[start:start + tk, :]
        v = v_ref[start:start + tk, :]
        start += tk
        s = _dot_nt(q, k)
        m_prev = m_scr[...]
        m_next = jnp.maximum(m_prev, jnp.max(s, axis=1, keepdims=True))
        p = jnp.exp2(s - jnp.concatenate([m_next] * (tk // LANES), axis=1))
        alpha = jnp.exp2(m_prev - m_next)
        pv = _dot(p.astype(BF16), jnp.concatenate([v, jnp.ones_like(v)], axis=1))
        l_scr[...] = alpha * l_scr[...] + pv[:, HEAD_DIM:]
        acc_scr[...] = acc_scr[...] * alpha + pv[:, :HEAD_DIM]
        m_scr[...] = m_next
    out = acc_scr[...] / l_scr[...]
    for g in range(B_GROUP):
        o_ref[:, g * HEAD_DIM:(g + 1) * HEAD_DIM] = out[g * tq:(g + 1) * tq].astype(o_ref.dtype)


def _attention(qr, kr, vr, batch, seq, ctx_len, latent):
    tq = ATTN_Q_TILE
    gw = B_GROUP * HEAD_DIM
    q_len = seq if latent else ctx_len
    q_tiles = q_len // tq
    q_row0 = 0 if latent else batch * seq // tq
    kv_len = ctx_len + seq
    if latent:
        kv_rows = kv_len
        kv_idx = lambda b, h, i: (b, h)
    else:
        kv_rows = ctx_len
        kv_idx = lambda b, h, i: (b * (kv_len // ctx_len), h)
    rows = B_GROUP * tq
    return pl.pallas_call(
        functools.partial(_attn_kernel, chunks=_kv_chunks(kv_rows), tq=tq),
        grid=(batch, B_KV_HEADS, q_tiles),
        in_specs=[pl.BlockSpec((tq, gw), lambda b, h, i: (q_row0 + b * q_tiles + i, h)),
                  pl.BlockSpec((kv_rows, HEAD_DIM), kv_idx),
                  pl.BlockSpec((kv_rows, HEAD_DIM), kv_idx)],
        out_specs=pl.BlockSpec((tq, gw), lambda b, h, i: (b * q_tiles + i, h)),
        out_shape=jax.ShapeDtypeStruct((batch * q_len, B_Q_W), BF16),
        scratch_shapes=[pltpu.VMEM((rows, LANES), F32), pltpu.VMEM((rows, LANES), F32),
                        pltpu.VMEM((rows, HEAD_DIM), F32)],
        compiler_params=_cparams(3),
        name="attn_lat" if latent else "attn_ctx",
    )(qr, kr, vr)


def _ab_out_kernel(u_ref, v_ref, ga_ref, gb0_ref, gb1_ref, aol_ref, aoc_ref, ws_ref, bs_ref, w_ref, gate_ref, *rest,
                   n_lat_tiles):
    o_ref, y_ref, acc_ref = rest[-3:]
    k = pl.program_id(1)

    @pl.when(k == 0)
    def _():
        tm = u_ref.shape[0]
        for c in range(tm // SGU_CHUNK):
            rows = slice(c * SGU_CHUNK, (c + 1) * SGU_CHUNK)
            for g in range(A_GROUPS):
                cols = slice(g * LANES, (g + 1) * LANES)
                vg = v_ref[rows, cols].astype(F32)
                d = vg - jnp.mean(vg, axis=-1, keepdims=True)
                var = jnp.mean(d * d, axis=-1, keepdims=True)
                vn = (d * lax.rsqrt(var + 1e-5)).astype(BF16)
                mixed = _dot(ws_ref[g], vn) + bs_ref[:, g:g + 1]
                y_ref[rows, cols] = (u_ref[rows, cols].astype(F32) * mixed
                                     * _silu(ga_ref[rows, cols].astype(F32))).astype(BF16)
        acc_ref[...] = _dot(y_ref[...], w_ref[0:A_WIDTH, :])

    @pl.when(k == 1)
    def _():
        ao = jnp.where(pl.program_id(0) < n_lat_tiles, aol_ref[...], aoc_ref[...]).astype(F32)
        gb = jnp.concatenate([gb0_ref[...], gb1_ref[...]], axis=1).astype(F32)
        y = (ao * _silu(gb)).astype(BF16)
        o_ref[...] = _select_rows(rest[:-3], n_lat_tiles) + gate_ref[...] * (acc_ref[...] + _dot(y, w_ref[A_WIDTH:, :]))


def _ab_out(proj, ao_lat, ao_ctx, sgu_w_stack_bf16, sgu_b_t, w_out_stack_bf16, layer, x_parts, mod, batch, seq):
    m = proj.shape[0]
    tm = ROW_TILE
    lat_tiles = seq // tm
    n_lat_tiles = batch * lat_tiles
    assert ao_ctx.shape[0] == tm
    gb_blk0 = (3 * A_WIDTH + B_Q_W + 2 * B_KV_W) // (B_Q_W // 2)
    return pl.pallas_call(
        functools.partial(_ab_out_kernel, n_lat_tiles=n_lat_tiles),
        grid=(m // tm, 2),
        in_specs=[pl.BlockSpec((tm, A_WIDTH), lambda i, k: (i, 0)),
                  pl.BlockSpec((tm, A_WIDTH), lambda i, k: (i, 1)),
                  pl.BlockSpec((tm, A_WIDTH), lambda i, k: (i, 2)),
                  pl.BlockSpec((tm, B_Q_W // 2), lambda i, k: (i, gb_blk0)),
                  pl.BlockSpec((tm, B_Q_W // 2), lambda i, k: (i, gb_blk0 + 1)),
                  pl.BlockSpec((tm, B_Q_W), lambda i, k: (jnp.minimum(i, n_lat_tiles - 1), 0)),
                  pl.BlockSpec((tm, B_Q_W), lambda i, k: (0, 0)),
                  pl.BlockSpec((None, A_GROUPS, SGU_CHUNK, SGU_CHUNK), lambda i, k: (layer, 0, 0, 0)),
                  pl.BlockSpec((SGU_CHUNK, A_GROUPS), lambda i, k: (0, 0)),
                  pl.BlockSpec((None, A_WIDTH + B_Q_W, D_MODEL), lambda i, k: (layer, 0, 0),
                               pipeline_mode=pl.Buffered(1)),
                  pl.BlockSpec((None, 1, D_MODEL), lambda i, k: (_mod_row(i, lat_tiles, batch), 0, 2))]
        + _row_specs(x_parts, tm, D_MODEL, n_lat_tiles),
        out_specs=pl.BlockSpec((tm, D_MODEL), lambda i, k: (i, 0)),
        out_shape=jax.ShapeDtypeStruct((m, D_MODEL), F32),
        scratch_shapes=[pltpu.VMEM((tm, A_WIDTH), BF16), pltpu.VMEM((tm, D_MODEL), F32)],
        compiler_params=_cparams(2),
        name="ab_out",
    )(proj, proj, proj, proj, proj, ao_lat, ao_ctx, sgu_w_stack_bf16, sgu_b_t, w_out_stack_bf16, mod, *x_parts)


DN_HALO = 2 * SUBLANES
DN_PROJ_COL_TILE = 2048


def _inproj_dn_kernel(xp_ref, x_ref, xn_ref, nw_ref, mod_ref, w_ref, ws_ref, cw_ref, qkv_ref, z_ref, ba_ref, h_ref, p_ref,
                      *, lat_tiles, n_lat, q_tiles, qk_tiles, qkv_tiles, ctx_len):
    i = pl.program_id(0)
    j = pl.program_id(1)
    tm = x_ref.shape[0]
    is_lat = i < n_lat
    pos = i % lat_tiles
    first = jnp.logical_or(jnp.logical_not(is_lat), pos == 0)
    last = jnp.logical_or(jnp.logical_not(is_lat), pos == lat_tiles - 1)
    half = CONV_K // 2

    @pl.when(j == 0)
    def _():
        shift = mod_ref[:, 0:D_MODEL]
        scale = mod_ref[:, D_MODEL:2 * D_MODEL]

        def norm_mod(x):
            y = x * lax.rsqrt(jnp.mean(x * x, axis=-1, keepdims=True) + EPS) * nw_ref[...]
            return y * (1.0 + scale) + shift

        h_ref[0:DN_HALO, :] = jnp.where(first, 0.0, norm_mod(xp_ref[...])).astype(BF16)
        h_ref[DN_HALO:DN_HALO + tm, :] = norm_mod(x_ref[...]).astype(BF16)
        h_ref[DN_HALO + tm:, :] = jnp.where(last, 0.0, norm_mod(xn_ref[...])).astype(BF16)
        ba_ref[...] = _dot(h_ref[DN_HALO:DN_HALO + tm, :], ws_ref[...])

    n_ext = h_ref.shape[0]
    tn = w_ref.shape[1]
    pair_w = 2 * HEAD_DIM
    row = lax.broadcasted_iota(jnp.int32, (tm, HEAD_DIM), 0)
    is_ctx = jnp.logical_not(is_lat)

    def conv_silu(ph, sl):
        acc = cw_ref[half:half + 1, sl] * ph[DN_HALO:DN_HALO + tm]
        for t in range(CONV_K):
            off = t - half
            if off == 0:
                continue
            tap = pltpu.roll(ph, (-off) % n_ext, 0)[DN_HALO:DN_HALO + tm]
            crosses = (jnp.logical_and(row >= ctx_len - off, row < ctx_len) if off > 0
                       else jnp.logical_and(row >= ctx_len, row < ctx_len - off))
            tap = jnp.where(jnp.logical_and(is_ctx, crosses), 0.0, tap)
            acc = acc + cw_ref[t:t + 1, sl] * tap
        return _silu(acc)

    def for_each_head(store):
        for n, c0 in enumerate(range(0, tn, pair_w)):
            p_ref[n % 2] = _dot(h_ref[...], w_ref[:, c0:c0 + pair_w])
            for h0 in range(0, pair_w, HEAD_DIM):
                sl = slice(c0 + h0, c0 + h0 + HEAD_DIM)
                store(sl, conv_silu(p_ref[n % 2, :, h0:h0 + HEAD_DIM], sl))

    @pl.when(j < qk_tiles)
    def _():
        qk_scale = jnp.where(j < q_tiles, HEAD_DIM ** -0.5, 1.0)

        def store(sl, y):
            qkv_ref[:, sl] = (y * (lax.rsqrt(jnp.sum(y * y, axis=-1, keepdims=True) + EPS) * qk_scale)).astype(qkv_ref.dtype)
        for_each_head(store)

    @pl.when(jnp.logical_and(j >= qk_tiles, j < qkv_tiles))
    def _():
        def store(sl, y):
            qkv_ref[:, sl] = y.astype(qkv_ref.dtype)
        for_each_head(store)

    @pl.when(j >= qkv_tiles)
    def _():
        z_ref[...] = _dot(h_ref[DN_HALO:DN_HALO + tm, :], w_ref[...]).astype(z_ref.dtype)


def _inproj_dn(x, norm_w, mod, w_stack_bf16, layer, conv_w, batch, seq, ctx_len):
    m = x.shape[0]
    tm, tn = ROW_TILE, DN_PROJ_COL_TILE
    n_side = 4 * DN_V_HEADS
    n = w_stack_bf16.shape[2] - n_side
    assert n == DN_QKV_W + DN_V_W and tm == 2 * ctx_len
    lat_tiles = seq // tm
    n_lat = batch * lat_tiles
    halo_per_tile = tm // DN_HALO
    n_halo = m // DN_HALO
    qkv_tiles = DN_QKV_W // tn
    kern = functools.partial(_inproj_dn_kernel, lat_tiles=lat_tiles, n_lat=n_lat, q_tiles=DN_K_W // tn,
                             qk_tiles=2 * DN_K_W // tn, qkv_tiles=qkv_tiles, ctx_len=ctx_len)
    return pl.pallas_call(
        kern,
        grid=(m // tm, n // tn),
        in_specs=[pl.BlockSpec((DN_HALO, D_MODEL), lambda i, j: (jnp.maximum(i * halo_per_tile - 1, 0), 0)),
                  pl.BlockSpec((tm, D_MODEL), lambda i, j: (i, 0)),
                  pl.BlockSpec((DN_HALO, D_MODEL), lambda i, j: (jnp.minimum((i + 1) * halo_per_tile, n_halo - 1), 0)),
                  pl.BlockSpec((1, D_MODEL), lambda i, j: (0, 0)),
                  pl.BlockSpec((None, 1, 3 * D_MODEL), lambda i, j: (_mod_row(i, lat_tiles, batch), 0, 0)),
                  pl.BlockSpec((None, D_MODEL, tn), lambda i, j: (layer, 0, j)),
                  pl.BlockSpec((None, D_MODEL, n_side), lambda i, j: (layer, 0, n // n_side)),
                  pl.BlockSpec((SUBLANES, tn), lambda i, j: (0, jnp.minimum(j, qkv_tiles - 1)))],
        out_specs=[pl.BlockSpec((tm, tn), lambda i, j: (i, jnp.minimum(j, qkv_tiles - 1))),
                   pl.BlockSpec((tm, tn), lambda i, j: (i, jnp.maximum(j - qkv_tiles, 0))),
                   pl.BlockSpec((tm, n_side), lambda i, j: (i, 0))],
        out_shape=[jax.ShapeDtypeStruct((m, DN_QKV_W), F32),
                   jax.ShapeDtypeStruct((m, DN_V_W), BF16),
                   jax.ShapeDtypeStruct((m, n_side), F32)],
        scratch_shapes=[pltpu.VMEM((tm + 2 * DN_HALO, D_MODEL), BF16),
                        pltpu.VMEM((2, tm + 2 * DN_HALO, 2 * HEAD_DIM), F32)],
        compiler_params=_cparams(2),
        name="inproj_dn",
    )(x, x, x, norm_w.reshape(1, D_MODEL), mod, w_stack_bf16, w_stack_bf16, conv_w)


def _dngate_kernel(ba_ref, alog_ref, dtb_ref, o_ref):
    ba = ba_ref[...]
    tm = ba.shape[0]
    lane = lax.broadcasted_iota(jnp.int32, (DN_CHUNK, LANES), 1)
    is_beta = (lane // DN_V_HEADS) % 2 == 0
    is_fwd = lane < 2 * DN_V_HEADS
    z = ba + dtb_ref[...]
    softplus = jnp.maximum(z, 0.0) + jnp.log1p(jnp.exp(-jnp.abs(z)))
    g = -jnp.exp(alog_ref[...]) * softplus
    r = lax.broadcasted_iota(jnp.int32, (DN_CHUNK, DN_CHUNK), 0)
    c = lax.broadcasted_iota(jnp.int32, (DN_CHUNK, DN_CHUNK), 1)
    tri_lo = (r >= c).astype(BF16)
    tri_up = (r <= c).astype(BF16)
    beta = jax.nn.sigmoid(ba)
    for ch in range(tm // DN_CHUNK):
        rows = slice(ch * DN_CHUNK, (ch + 1) * DN_CHUNK)
        gch = g[rows]
        g1 = gch.astype(BF16)
        r1 = gch - g1.astype(F32)
        g2 = r1.astype(BF16)
        g3 = (r1 - g2.astype(F32)).astype(BF16)
        pre = _dot(tri_lo, g1) + _dot(tri_lo, g2) + _dot(tri_lo, g3)
        suf = _dot(tri_up, g1) + _dot(tri_up, g2) + _dot(tri_up, g3)
        gc = jnp.where(is_fwd, pre, suf)
        o_ref[rows, :] = jnp.where(is_beta, beta[rows], gc)


def _dn_gate(ba, alog_vec, dtb_vec):
    m = ba.shape[0]
    tm = ROW_TILE
    return pl.pallas_call(
        _dngate_kernel,
        grid=(m // tm,),
        in_specs=[pl.BlockSpec((tm, LANES), lambda i: (i, 0)),
                  pl.BlockSpec((1, LANES), lambda i: (0, 0)),
                  pl.BlockSpec((1, LANES), lambda i: (0, 0))],
        out_specs=pl.BlockSpec((tm, LANES), lambda i: (i, 0)),
        out_shape=jax.ShapeDtypeStruct((m, LANES), F32),
        compiler_params=_cparams(1),
        name="dn_gate",
    )(ba, alog_vec, dtb_vec)


DN_CHAIN_GROUP = 16
DN_KH_PER_STEP = 16


def _dncore_kernel(qf_ref, kf_ref, vf_ref, gf_ref, qb_ref, kb_ref, vb_ref, gb_ref, of_ref, ob_ref, s_ref):
    C = DN_CHUNK
    nvh = 2 * DN_KH_PER_STEP

    @pl.when(pl.program_id(2) == 0)
    def _():
        s_ref[...] = jnp.zeros(s_ref.shape, F32)

    lane = lax.broadcasted_iota(jnp.int32, (C, 2 * C), 1)
    left = lane < C
    row = lax.broadcasted_iota(jnp.int32, (C, 2 * C), 0)
    colp = lane % C
    left_sq = lax.broadcasted_iota(jnp.int32, (2 * C, 2 * C), 1) < C
    eye2 = (row == colp).astype(F32)

    def blockdiag(p):
        z = jnp.zeros_like(p)
        return jnp.concatenate([jnp.where(left, p, z), jnp.where(left, z, p)], axis=0)

    def packed_mm(a, b):
        return _dot(a.astype(BF16), blockdiag(b.astype(BF16)))

    dirs = ((qf_ref, kf_ref, vf_ref, gf_ref, of_ref), (qb_ref, kb_ref, vb_ref, gb_ref, ob_ref))
    chains = [(d, kh) for d in range(2) for kh in range(DN_KH_PER_STEP)]
    G = [dirs[d][3][...] for d in range(2)]
    GT = [jnp.concatenate([g, g], axis=0).T for g in G]

    for g0 in range(0, len(chains), DN_CHAIN_GROUP):
        group = chains[g0:g0 + DN_CHAIN_GROUP]
        st = []
        for d, kh in group:
            q_ref, k_ref, v_ref, _, _ = dirs[d]
            base_beta = d * 2 * nvh
            base_gc = base_beta + nvh
            lv0 = 2 * kh
            q = q_ref[:, kh * HEAD_DIM:(kh + 1) * HEAD_DIM]
            k = k_ref[:, kh * HEAD_DIM:(kh + 1) * HEAD_DIM]
            k2 = jnp.concatenate([k, k], axis=0)
            kT2 = k2.T
            gram = _dot(jnp.concatenate([q, k], axis=0).astype(BF16), kT2.astype(BF16))
            b0, b1 = G[d][:, base_beta + lv0:base_beta + lv0 + 1], G[d][:, base_beta + lv0 + 1:base_beta + lv0 + 2]
            c0, c1 = G[d][:, base_gc + lv0:base_gc + lv0 + 1], G[d][:, base_gc + lv0 + 1:base_gc + lv0 + 2]
            r0, r1 = GT[d][base_gc + lv0:base_gc + lv0 + 1, :], GT[d][base_gc + lv0 + 1:base_gc + lv0 + 2, :]
            st.append(dict(d=d, lv0=lv0, q=q, k2=k2, kT2=kT2, gram=gram, b0=b0, b1=b1, c0=c0, c1=c1, r0=r0, r1=r1))

        for s in st:
            d = s["d"]
            incl = (row >= colp) if d == 0 else (row <= colp)
            strict = (row > colp) if d == 0 else (row < colp)
            gcol_p = jnp.where(left, s["c0"], s["c1"])
            grow_p = jnp.where(left[0:1], s["r0"], s["r1"])
            beta_p = jnp.where(left, s["b0"], s["b1"])
            dec = jnp.exp(jnp.where(incl, gcol_p - grow_p, -1e30))
            s["dec"] = dec
            s["attn"] = s["gram"][0:C] * dec
            s["L"] = jnp.where(strict, s["gram"][C:2 * C] * dec, 0.0) * beta_p

        for s in st:
            n1 = jnp.where(jnp.logical_and(row // 2 == colp // 2, row != colp), s["L"], 0.0)
            s["X"] = eye2 - n1
        blk = 2
        while blk < C:
            mask = jnp.logical_and(row // (2 * blk) == colp // (2 * blk), row // blk != colp // blk)
            for s in st:
                s["Y"] = packed_mm(s["X"], jnp.where(mask, s["L"], 0.0))
            for s in st:
                s["X"] = s["X"] - packed_mm(s["Y"], s["X"])
            blk *= 2

        for s in st:
            d, lv0 = s["d"], s["lv0"]
            v_ref = dirs[d][2]
            beta_r = jnp.concatenate([s["b0"], s["b1"]], axis=0)
            egc_r = jnp.exp(jnp.concatenate([s["c0"], s["c1"]], axis=0))
            v2 = jnp.concatenate([v_ref[:, lv0 * HEAD_DIM:(lv0 + 1) * HEAD_DIM],
                                  v_ref[:, (lv0 + 1) * HEAD_DIM:(lv0 + 2) * HEAD_DIM]], axis=0)
            rhs = jnp.concatenate([v2 * beta_r, s["k2"] * (beta_r * egc_r)], axis=1)
            s["sol"] = _dot(blockdiag(s["X"]).astype(BF16), rhs.astype(BF16))
            s["egc"] = egc_r
            s["qg2"] = jnp.concatenate([s["q"], s["q"]], axis=0) * egc_r

        for s in st:
            d, lv0 = s["d"], s["lv0"]
            w2 = s["sol"][:, HEAD_DIM:]
            s["ws"] = []
            for r in range(2):
                lhs = jnp.concatenate([w2[r * C:(r + 1) * C], s["qg2"][r * C:(r + 1) * C]], axis=0).astype(BF16)
                s["ws"].append(_dot(lhs, s_ref[d, lv0 + r].astype(BF16)))

        for s in st:
            u2 = s["sol"][:, 0:HEAD_DIM]
            vn2 = jnp.concatenate([u2[r * C:(r + 1) * C] - s["ws"][r][0:C] for r in range(2)], axis=0).astype(BF16)
            s["vn2"] = vn2
            s["o2"] = jnp.concatenate([s["ws"][r][C:2 * C] for r in range(2)], axis=0) + _dot(
                blockdiag(s["attn"]).astype(BF16), vn2)

        for s in st:
            d, lv0 = s["d"], s["lv0"]
            o_ref = dirs[d][4]
            last = C - 1 if d == 0 else 0
            kdT_p = s["kT2"] * s["dec"][last:last + 1, :]
            zkd = jnp.zeros_like(kdT_p)
            for r in range(2):
                kd_r = jnp.where(left_sq if r == 0 else jnp.logical_not(left_sq), kdT_p, zkd).astype(BF16)
                glr = jnp.broadcast_to(s["egc"][r * C + last:r * C + last + 1, :], (HEAD_DIM, HEAD_DIM))
                s_ref[d, lv0 + r] = s_ref[d, lv0 + r] * glr + _dot(kd_r, s["vn2"])
                o_ref[:, (lv0 + r) * HEAD_DIM:(lv0 + r + 1) * HEAD_DIM] = s["o2"][r * C:(r + 1) * C].astype(o_ref.dtype)


def _dn_core(qkv, gates, batch, seq, ctx_len):
    m = qkv.shape[0]
    C = DN_CHUNK
    n_lat = seq // C
    n_ctx = ctx_len // C
    n_steps = n_ctx + n_lat
    ctx0 = batch * n_lat
    khs = DN_KH_PER_STEP
    qw = khs * HEAD_DIM
    vw = 2 * khs * HEAD_DIM
    k_blk0 = DN_K_W // qw
    v_blk0 = 2 * DN_K_W // vw

    def rf(b, t):
        return jnp.where(t < n_ctx, ctx0 + b * n_ctx + t, b * n_lat + (t - n_ctx))

    def rb(b, t):
        return jnp.where(t < n_ctx, ctx0 + b * n_ctx + (n_ctx - 1 - t), b * n_lat + (n_lat - 1 - (t - n_ctx)))

    def specs(rfun):
        return [pl.BlockSpec((C, qw), lambda b, h, t: (rfun(b, t), h)),
                pl.BlockSpec((C, qw), lambda b, h, t: (rfun(b, t), k_blk0 + h)),
                pl.BlockSpec((C, vw), lambda b, h, t: (rfun(b, t), v_blk0 + h)),
                pl.BlockSpec((C, LANES), lambda b, h, t: (rfun(b, t), h))]

    return pl.pallas_call(
        _dncore_kernel,
        grid=(batch, DN_K_HEADS // khs, n_steps),
        in_specs=specs(rf) + specs(rb),
        out_specs=[pl.BlockSpec((C, vw), lambda b, h, t: (rf(b, t), h)),
                   pl.BlockSpec((C, vw), lambda b, h, t: (rb(b, t), h))],
        out_shape=[jax.ShapeDtypeStruct((m, DN_V_W), BF16), jax.ShapeDtypeStruct((m, DN_V_W), BF16)],
        scratch_shapes=[pltpu.VMEM((2, 2 * khs, HEAD_DIM, HEAD_DIM), F32)],
        compiler_params=_cparams(3),
        name="dn_core",
    )(qkv, qkv, qkv, gates, qkv, qkv, qkv, gates)


DN_OUT_ROW_TILE = 256


def _dn_out_kernel(of_ref, ob_ref, z_ref, nw_ref, w_ref, x_ref, gate_ref, *rest):
    o_ref, y_ref = rest[-2:]
    for h in range(of_ref.shape[1] // HEAD_DIM):
        sl = slice(h * HEAD_DIM, (h + 1) * HEAD_DIM)
        o = of_ref[:, sl].astype(F32) + ob_ref[:, sl].astype(F32)
        n = o * lax.rsqrt(jnp.mean(o * o, axis=-1, keepdims=True) + EPS) * nw_ref[...]
        y_ref[:, sl] = (n * _silu(z_ref[:, sl].astype(F32))).astype(BF16)
    y = x_ref[...] + gate_ref[...] * _dot(y_ref[...], w_ref[...])
    if len(rest) == 3:
        y = y * lax.rsqrt(jnp.mean(y * y, axis=-1, keepdims=True) + EPS) * rest[0][...]
    o_ref[...] = y


def _dn_out(o_f, o_b, z, norm_w, w_out_stack_bf16, layer, x, mod, batch, seq, final_norm_w=None):
    tm = DN_OUT_ROW_TILE
    m = x.shape[0] if final_norm_w is None else batch * seq
    lat_tiles = seq // tm
    extra_specs, extra_args = [], []
    if final_norm_w is not None:
        extra_specs = [pl.BlockSpec((1, D_MODEL), lambda i: (0, 0))]
        extra_args = [final_norm_w.reshape(1, D_MODEL)]
    return pl.pallas_call(
        _dn_out_kernel,
        grid=(m // tm,),
        in_specs=[pl.BlockSpec((tm, DN_V_W), lambda i: (i, 0)),
                  pl.BlockSpec((tm, DN_V_W), lambda i: (i, 0)),
                  pl.BlockSpec((tm, DN_V_W), lambda i: (i, 0)),
                  pl.BlockSpec((1, HEAD_DIM), lambda i: (0, 0)),
                  pl.BlockSpec((None, DN_V_W, D_MODEL), lambda i: (layer, 0, 0), pipeline_mode=pl.Buffered(1)),
                  pl.BlockSpec((tm, D_MODEL), lambda i: (i, 0)),
                  pl.BlockSpec((None, 1, D_MODEL), lambda i: (_mod_row(i, lat_tiles, batch), 0, 2))] + extra_specs,
        out_specs=pl.BlockSpec((tm, D_MODEL), lambda i: (i, 0)),
        out_shape=jax.ShapeDtypeStruct((m, D_MODEL), F32),
        scratch_shapes=[pltpu.VMEM((tm, DN_V_W), BF16)],
        compiler_params=_cparams(1),
        name="dn_out",
    )(o_f, o_b, z, norm_w.reshape(1, HEAD_DIM), w_out_stack_bf16, x, mod, *extra_args)


def _final_norm_kernel(x_ref, w_ref, o_ref):
    x = x_ref[...]
    o_ref[...] = x * lax.rsqrt(jnp.mean(x * x, axis=-1, keepdims=True) + EPS) * w_ref[...]


def _final_norm(x, w, rows):
    tm = ROW_TILE
    return pl.pallas_call(
        _final_norm_kernel,
        grid=(rows // tm,),
        in_specs=[pl.BlockSpec((tm, D_MODEL), lambda i: (i, 0)),
                  pl.BlockSpec((1, D_MODEL), lambda i: (0, 0))],
        out_specs=pl.BlockSpec((tm, D_MODEL), lambda i: (i, 0)),
        out_shape=jax.ShapeDtypeStruct((rows, D_MODEL), F32),
        compiler_params=_cparams(1),
        name="final_norm",
    )(x, w.reshape(1, D_MODEL))


def kernel(x, c, ctx, c_ctx, norm_w, ada_w, ada_b, ab_w_in, ab_w_out, sgu_w, sgu_b, q_norm_w, k_norm_w,
           dn_w_in, dn_conv_w, dn_a_log, dn_dt_bias, dn_norm_w, dn_w_out, final_norm_w):
    batch, seq, _ = x.shape
    ctx_len = ctx.shape[1]
    depth = norm_w.shape[0]
    assert ctx_len == CONV_ROW_TILE and seq % ROW_TILE == 0 and batch * ctx_len == ROW_TILE
    n_lat_rows = batch * seq

    x_parts = (x.reshape(n_lat_rows, D_MODEL), ctx.reshape(batch * ctx_len, D_MODEL))
    ab_w_in_bf, ab_w_out_bf, sgu_w_bf = ab_w_in.astype(BF16), ab_w_out.astype(BF16), sgu_w.astype(BF16)
    dn_w_in_bf, dn_w_out_bf = dn_w_in.astype(BF16), dn_w_out.astype(BF16)
    cond = jnp.zeros((SUBLANES, D_MODEL), F32).at[0:batch].set(c).at[batch].set(c_ctx)
    mods = _ada_mod(cond, ada_w, ada_b)
    cos_tab, sin_tab = _rope_tables(seq, CONV_ROW_TILE)

    for i in range(depth):
        j = i // 2
        mod = mods[i].reshape(SUBLANES, 1, 3 * D_MODEL)
        if i % 2 == 0:
            proj = _inproj(x_parts, norm_w[i], mod, ab_w_in_bf, j, batch, seq)
            qr, kr, vr = _qkprep(proj, cos_tab, sin_tab, q_norm_w[j], k_norm_w[j], batch, seq)
            ao_lat = _attention(qr, kr, vr, batch, seq, ctx_len, True)
            ao_ctx = _attention(qr, kr, vr, batch, seq, ctx_len, False)
            x_parts = (_ab_out(proj, ao_lat, ao_ctx, sgu_w_bf, sgu_b[j].T, ab_w_out_bf, j, x_parts, mod, batch, seq),)
        else:
            conv_w = jnp.zeros((SUBLANES, DN_QKV_W), F32).at[0:CONV_K].set(dn_conv_w[j])
            assert len(x_parts) == 1
            qkv, z, ba = _inproj_dn(x_parts[0], norm_w[i], mod, dn_w_in_bf, j, conv_w, batch, seq, ctx_len)
            zeros = jnp.zeros((2, DN_V_HEADS), F32)
            alog_vec = jnp.concatenate([zeros, dn_a_log[j]], axis=1).reshape(1, LANES)
            dtb_vec = jnp.concatenate([zeros, dn_dt_bias[j]], axis=1).reshape(1, LANES)
            gates = _dn_gate(ba, alog_vec, dtb_vec)
            n_hg = DN_K_HEADS // DN_KH_PER_STEP
            nvh = 2 * DN_KH_PER_STEP
            g4 = gates
            if n_hg > 1:
                g4 = gates.reshape(-1, 4, n_hg, nvh).transpose(0, 2, 1, 3).reshape(-1, n_hg, 4 * nvh)
                g4 = jnp.pad(g4, ((0, 0), (0, 0), (0, LANES - 4 * nvh))).reshape(-1, n_hg * LANES)
            o_f, o_b = _dn_core(qkv, g4, batch, seq, ctx_len)
            x_parts = (_dn_out(o_f, o_b, z, dn_norm_w[j], dn_w_out_bf, j, x_parts[0], mod, batch, seq,
                               final_norm_w=final_norm_w if i == depth - 1 else None),)

    xs = x_parts[0]
    if depth % 2 == 1:
        xs = _final_norm(xs, final_norm_w, n_lat_rows)
    return xs.reshape(batch, seq, D_MODEL)
```

```python
import functools

import jax
import jax.numpy as jnp
from jax import lax
from jax.experimental import pallas as pl
from jax.experimental.pallas import tpu as pltpu

F32 = jnp.float32
BF16 = jnp.bfloat16

D_MODEL = 2048
GRID_W = 64
EPS = 1e-6
HEAD_DIM = 128
A_WIDTH = D_MODEL // 2
A_GROUPS = A_WIDTH // 128
SGU_CHUNK = 128
B_HEADS = (D_MODEL // 2) // HEAD_DIM
B_KV_HEADS = B_HEADS // 4
B_GROUP = B_HEADS // B_KV_HEADS
B_Q_W = B_HEADS * HEAD_DIM
B_KV_W = B_KV_HEADS * HEAD_DIM
ROPE_THETA = 10000.0
AXIS_DIM = HEAD_DIM // 2
AB_IN_W = 3 * A_WIDTH + 2 * B_Q_W + 2 * B_KV_W
DN_K_HEADS = D_MODEL // HEAD_DIM
DN_V_HEADS = 2 * DN_K_HEADS
DN_K_W = DN_K_HEADS * HEAD_DIM
DN_V_W = DN_V_HEADS * HEAD_DIM
DN_QKV_W = 2 * DN_K_W + DN_V_W
DN_IN_W = DN_QKV_W + DN_V_W + 4 * DN_V_HEADS
DN_CHUNK = 64
CONV_K = 5

V7X_VMEM_LIMIT_BYTES = 56 * 1024 * 1024
LANES = 128
SUBLANES = 8

ROW_TILE = 512
CONV_ROW_TILE = 256


def _cparams(n_axes):
    return pltpu.CompilerParams(dimension_semantics=("arbitrary",) * n_axes,
                                vmem_limit_bytes=V7X_VMEM_LIMIT_BYTES)


def _silu(x):
    return x * jax.nn.sigmoid(x)


def _split_bf16(a):
    hi = a.astype(BF16)
    lo = (a - hi.astype(F32)).astype(BF16)
    return hi, lo


def _dot(a, b):
    return jnp.dot(a, b, preferred_element_type=F32)


def _dot_nt(a, b):
    return lax.dot_general(a, b, (((1,), (1,)), ((), ())), preferred_element_type=F32)


def _dot3(a, b):
    ah, al = _split_bf16(a)
    bh, bl = _split_bf16(b)
    return _dot(ah, bh) + _dot(ah, bl) + _dot(al, bh)


def _ada_kernel(c_ref, w_ref, b_ref, o_ref):
    s = _silu(c_ref[...])
    o_ref[0] = _dot3(s, w_ref[0]) + b_ref[0]


def _ada_mod(cond, ada_w, ada_b):
    depth = ada_w.shape[0]
    tn = 512
    return pl.pallas_call(
        _ada_kernel,
        grid=(depth, 3 * D_MODEL // tn),
        in_specs=[pl.BlockSpec((SUBLANES, D_MODEL), lambda l, j: (0, 0)),
                  pl.BlockSpec((1, D_MODEL, tn), lambda l, j: (l, 0, j)),
                  pl.BlockSpec((1, 1, tn), lambda l, j: (l, 0, j))],
        out_specs=pl.BlockSpec((1, SUBLANES, tn), lambda l, j: (l, 0, j)),
        out_shape=jax.ShapeDtypeStruct((depth, SUBLANES, 3 * D_MODEL), F32),
        compiler_params=_cparams(2),
        name="ada_mod",
    )(cond, ada_w, ada_b.reshape(depth, 1, 3 * D_MODEL))


INPROJ_COL_TILES = (512, 1024, 1408, 1536, 2816)


def _inproj_kernel(*refs, n_lat_tiles):
    nw_ref, mod_ref, w_ref, o_ref, h_ref = refs[-5:]

    @pl.when(pl.program_id(1) == 0)
    def _():
        x = _select_rows(refs[:-5], n_lat_tiles)
        y = x * lax.rsqrt(jnp.mean(x * x, axis=-1, keepdims=True) + EPS) * nw_ref[...]
        shift = mod_ref[:, 0:D_MODEL]
        scale = mod_ref[:, D_MODEL:2 * D_MODEL]
        h_ref[...] = (y * (1.0 + scale) + shift).astype(BF16)

    o_ref[...] = _dot(h_ref[...], w_ref[...]).astype(o_ref.dtype)


def _select_rows(x_refs, n_lat_tiles):
    if len(x_refs) == 1:
        return x_refs[0][...]
    return jnp.where(pl.program_id(0) < n_lat_tiles, x_refs[0][...], x_refs[1][...])


def _row_specs(x_parts, tm, width, n_lat_tiles):
    if len(x_parts) == 1:
        return [pl.BlockSpec((tm, width), lambda i, j: (i, 0))]
    assert x_parts[1].shape[0] == tm
    return [pl.BlockSpec((tm, width), lambda i, j: (jnp.minimum(i, n_lat_tiles - 1), 0)),
            pl.BlockSpec((tm, width), lambda i, j: (0, 0))]


def _mod_row(i, lat_tiles, batch):
    return jnp.minimum(i // lat_tiles, batch)


def _inproj(x_parts, norm_w, mod, w_stack_bf16, layer, batch, seq):
    m = sum(p.shape[0] for p in x_parts)
    n = w_stack_bf16.shape[2]
    tm = ROW_TILE
    tn = max(t for t in INPROJ_COL_TILES if n % t == 0)
    lat_tiles = seq // tm
    n_lat_tiles = batch * lat_tiles
    return pl.pallas_call(
        functools.partial(_inproj_kernel, n_lat_tiles=n_lat_tiles),
        grid=(m // tm, n // tn),
        in_specs=_row_specs(x_parts, tm, D_MODEL, n_lat_tiles) + [
            pl.BlockSpec((1, D_MODEL), lambda i, j: (0, 0)),
            pl.BlockSpec((None, 1, 3 * D_MODEL), lambda i, j: (_mod_row(i, lat_tiles, batch), 0, 0)),
            pl.BlockSpec((None, D_MODEL, tn), lambda i, j: (layer, 0, j))],
        out_specs=pl.BlockSpec((tm, tn), lambda i, j: (i, j)),
        out_shape=jax.ShapeDtypeStruct((m, n), BF16),
        scratch_shapes=[pltpu.VMEM((tm, D_MODEL), BF16)],
        compiler_params=_cparams(2),
        name="inproj",
    )(*x_parts, norm_w.reshape(1, D_MODEL), mod, w_stack_bf16)


Q_PRESCALE = (HEAD_DIM ** -0.5) * 1.4426950408889634


def _qkprep_kernel(q_ref, k_ref, v_ref, cos_ref, sin_ref, qw_ref, kw_ref, qo_ref, ko_ref, vo_ref):
    cos = cos_ref[...]
    sin = sin_ref[...]
    lane = lax.broadcasted_iota(jnp.int32, cos.shape, 1)
    first = (lane % (AXIS_DIM)) < (AXIS_DIM // 2)

    def prep(x, w):
        y = x * lax.rsqrt(jnp.mean(x * x, axis=-1, keepdims=True) + EPS) * w
        rot = jnp.where(first, pltpu.roll(y, HEAD_DIM - AXIS_DIM // 2, 1), pltpu.roll(y, AXIS_DIM // 2, 1))
        return y * cos + rot * sin

    for h in range(B_HEADS):
        sl = slice(h * HEAD_DIM, (h + 1) * HEAD_DIM)
        qo_ref[:, sl] = (prep(q_ref[:, sl].astype(F32), qw_ref[...]) * Q_PRESCALE).astype(qo_ref.dtype)
    for h in range(B_KV_HEADS):
        sl = slice(h * HEAD_DIM, (h + 1) * HEAD_DIM)
        ko_ref[:, sl] = prep(k_ref[:, sl].astype(F32), kw_ref[...]).astype(ko_ref.dtype)
    vo_ref[...] = v_ref[...].astype(vo_ref.dtype)


def _qkprep(proj, cos_tab, sin_tab, qn_w, kn_w, batch, seq):
    m = proj.shape[0]
    tm = CONV_ROW_TILE
    lat_tiles = seq // tm
    n_lat = batch * lat_tiles
    q_blk = (3 * A_WIDTH) // B_Q_W
    k_blk = (3 * A_WIDTH + B_Q_W) // B_KV_W

    def tab_idx(i):
        return (jnp.where(i < n_lat, i % lat_tiles, lat_tiles), 0)

    def kv_idx(i):
        lat_blk = (i // lat_tiles) * (lat_tiles + 1) + 1 + i % lat_tiles
        return (jnp.where(i < n_lat, lat_blk, (i - n_lat) * (lat_tiles + 1)), 0)

    return pl.pallas_call(
        _qkprep_kernel,
        grid=(m // tm,),
        in_specs=[pl.BlockSpec((tm, B_Q_W), lambda i: (i, q_blk)),
                  pl.BlockSpec((tm, B_KV_W), lambda i: (i, k_blk)),
                  pl.BlockSpec((tm, B_KV_W), lambda i: (i, k_blk + 1)),
                  pl.BlockSpec((tm, HEAD_DIM), tab_idx),
                  pl.BlockSpec((tm, HEAD_DIM), tab_idx),
                  pl.BlockSpec((1, HEAD_DIM), lambda i: (0, 0)),
                  pl.BlockSpec((1, HEAD_DIM), lambda i: (0, 0))],
        out_specs=[pl.BlockSpec((tm, B_Q_W), lambda i: (i, 0)),
                   pl.BlockSpec((tm, B_KV_W), kv_idx),
                   pl.BlockSpec((tm, B_KV_W), kv_idx)],
        out_shape=[jax.ShapeDtypeStruct((m, B_Q_W), BF16),
                   jax.ShapeDtypeStruct((m, B_KV_W), BF16),
                   jax.ShapeDtypeStruct((m, B_KV_W), BF16)],
        compiler_params=_cparams(1),
        name="qk_prep",
    )(proj, proj, proj, cos_tab, sin_tab, qn_w.reshape(1, HEAD_DIM), kn_w.reshape(1, HEAD_DIM))


def _rope_tables(seq, tm):
    rows = seq // GRID_W
    row = jnp.repeat(jnp.arange(rows), GRID_W).astype(F32)
    col = jnp.tile(jnp.arange(GRID_W), rows).astype(F32)
    freqs = ROPE_THETA ** (-jnp.arange(0, AXIS_DIM, 2, dtype=F32) / AXIS_DIM)
    ang_r = row[:, None] * freqs[None, :]
    ang_c = col[:, None] * freqs[None, :]
    ang = jnp.concatenate([ang_r, ang_r, ang_c, ang_c], axis=-1)
    sign = jnp.where((jnp.arange(HEAD_DIM) % AXIS_DIM) < AXIS_DIM // 2, -1.0, 1.0).astype(F32)
    cos = jnp.concatenate([jnp.cos(ang), jnp.ones((tm, HEAD_DIM), F32)], axis=0)
    sin = jnp.concatenate([jnp.sin(ang) * sign[None, :], jnp.zeros((tm, HEAD_DIM), F32)], axis=0)
    return cos, sin


ATTN_Q_TILE = 256
ATTN_KV_BODY = 4096
ATTN_KV_TAIL = 768


def _kv_chunks(kv_len):
    tail = min(ATTN_KV_TAIL, kv_len)
    body = kv_len - tail
    sizes = [ATTN_KV_BODY] * (body // ATTN_KV_BODY) + ([body % ATTN_KV_BODY] if body % ATTN_KV_BODY else []) + [tail]
    assert all(c % LANES == 0 for c in sizes) and sum(sizes) == kv_len
    return tuple(sizes)


def _attn_kernel(q_ref, k_ref, v_ref, o_ref, m_scr, l_scr, acc_scr, *, chunks, tq):
    q = jnp.concatenate([q_ref[:, g * HEAD_DIM:(g + 1) * HEAD_DIM] for g in range(B_GROUP)], axis=0)
    m_scr[...] = jnp.full(m_scr.shape, -jnp.inf, F32)
    l_scr[...] = jnp.zeros(l_scr.shape, F32)
    acc_scr[...] = jnp.zeros(acc_scr.shape, F32)
    start = 0
    for tk in chunks:
        k = k_ref[start:start + tk, :]
        v = v_ref[start:start + tk, :]
        start += tk
        s = _dot_nt(q, k)
        m_prev = m_scr[...]
        m_next = jnp.maximum(m_prev, jnp.max(s, axis=1, keepdims=True))
        p = jnp.exp2(s - jnp.concatenate([m_next] * (tk // LANES), axis=1))
        alpha = jnp.exp2(m_prev - m_next)
        pv = _dot(p.astype(BF16), jnp.concatenate([v, jnp.ones_like(v)], axis=1))
        l_scr[...] = alpha * l_scr[...] + pv[:, HEAD_DIM:]
        acc_scr[...] = acc_scr[...] * alpha + pv[:, :HEAD_DIM]
        m_scr[...] = m_next
    out = acc_scr[...] / l_scr[...]
    for g in range(B_GROUP):
        o_ref[:, g * HEAD_DIM:(g + 1) * HEAD_DIM] = out[g * tq:(g + 1) * tq].astype(o_ref.dtype)


def _attention(qr, kr, vr, batch, seq, ctx_len, latent):
    tq = ATTN_Q_TILE
    gw = B_GROUP * HEAD_DIM
    q_len = seq if latent else ctx_len
    q_tiles = q_len // tq
    q_row0 = 0 if latent else batch * seq // tq
    kv_len = ctx_len + seq
    if latent:
        kv_rows = kv_len
        kv_idx = lambda b, h, i: (b, h)
    else:
        kv_rows = ctx_len
        kv_idx = lambda b, h, i: (b * (kv_len // ctx_len), h)
    rows = B_GROUP * tq
    return pl.pallas_call(
        functools.partial(_attn_kernel, chunks=_kv_chunks(kv_rows), tq=tq),
        grid=(batch, B_KV_HEADS, q_tiles),
        in_specs=[pl.BlockSpec((tq, gw), lambda b, h, i: (q_row0 + b * q_tiles + i, h)),
                  pl.BlockSpec((kv_rows, HEAD_DIM), kv_idx),
                  pl.BlockSpec((kv_rows, HEAD_DIM), kv_idx)],
        out_specs=pl.BlockSpec((tq, gw), lambda b, h, i: (b * q_tiles + i, h)),
        out_shape=jax.ShapeDtypeStruct((batch * q_len, B_Q_W), BF16),
        scratch_shapes=[pltpu.VMEM((rows, LANES), F32), pltpu.VMEM((rows, LANES), F32),
                        pltpu.VMEM((rows, HEAD_DIM), F32)],
        compiler_params=_cparams(3),
        name="attn_lat" if latent else "attn_ctx",
    )(qr, kr, vr)


def _ab_out_kernel(u_ref, v_ref, ga_ref, gb0_ref, gb1_ref, aol_ref, aoc_ref, ws_ref, bs_ref, w_ref, gate_ref, *rest,
                   n_lat_tiles):
    o_ref, y_ref, acc_ref = rest[-3:]
    k = pl.program_id(1)

    @pl.when(k == 0)
    def _():
        tm = u_ref.shape[0]
        for c in range(tm // SGU_CHUNK):
            rows = slice(c * SGU_CHUNK, (c + 1) * SGU_CHUNK)
            for g in range(A_GROUPS):
                cols = slice(g * LANES, (g + 1) * LANES)
                vg = v_ref[rows, cols].astype(F32)
                d = vg - jnp.mean(vg, axis=-1, keepdims=True)
                var = jnp.mean(d * d, axis=-1, keepdims=True)
                vn = (d * lax.rsqrt(var + 1e-5)).astype(BF16)
                mixed = _dot(ws_ref[g], vn) + bs_ref[:, g:g + 1]
                y_ref[rows, cols] = (u_ref[rows, cols].astype(F32) * mixed
                                     * _silu(ga_ref[rows, cols].astype(F32))).astype(BF16)
        acc_ref[...] = _dot(y_ref[...], w_ref[0:A_WIDTH, :])

    @pl.when(k == 1)
    def _():
        ao = jnp.where(pl.program_id(0) < n_lat_tiles, aol_ref[...], aoc_ref[...]).astype(F32)
        gb = jnp.concatenate([gb0_ref[...], gb1_ref[...]], axis=1).astype(F32)
        y = (ao * _silu(gb)).astype(BF16)
        o_ref[...] = _select_rows(rest[:-3], n_lat_tiles) + gate_ref[...] * (acc_ref[...] + _dot(y, w_ref[A_WIDTH:, :]))


def _ab_out(proj, ao_lat, ao_ctx, sgu_w_stack_bf16, sgu_b_t, w_out_stack_bf16, layer, x_parts, mod, batch, seq):
    m = proj.shape[0]
    tm = ROW_TILE
    lat_tiles = seq // tm
    n_lat_tiles = batch * lat_tiles
    assert ao_ctx.shape[0] == tm
    gb_blk0 = (3 * A_WIDTH + B_Q_W + 2 * B_KV_W) // (B_Q_W // 2)
    return pl.pallas_call(
        functools.partial(_ab_out_kernel, n_lat_tiles=n_lat_tiles),
        grid=(m // tm, 2),
        in_specs=[pl.BlockSpec((tm, A_WIDTH), lambda i, k: (i, 0)),
                  pl.BlockSpec((tm, A_WIDTH), lambda i, k: (i, 1)),
                  pl.BlockSpec((tm, A_WIDTH), lambda i, k: (i, 2)),
                  pl.BlockSpec((tm, B_Q_W // 2), lambda i, k: (i, gb_blk0)),
                  pl.BlockSpec((tm, B_Q_W // 2), lambda i, k: (i, gb_blk0 + 1)),
                  pl.BlockSpec((tm, B_Q_W), lambda i, k: (jnp.minimum(i, n_lat_tiles - 1), 0)),
                  pl.BlockSpec((tm, B_Q_W), lambda i, k: (0, 0)),
                  pl.BlockSpec((None, A_GROUPS, SGU_CHUNK, SGU_CHUNK), lambda i, k: (layer, 0, 0, 0)),
                  pl.BlockSpec((SGU_CHUNK, A_GROUPS), lambda i, k: (0, 0)),
                  pl.BlockSpec((None, A_WIDTH + B_Q_W, D_MODEL), lambda i, k: (layer, 0, 0),
                               pipeline_mode=pl.Buffered(1)),
                  pl.BlockSpec((None, 1, D_MODEL), lambda i, k: (_mod_row(i, lat_tiles, batch), 0, 2))]
        + _row_specs(x_parts, tm, D_MODEL, n_lat_tiles),
        out_specs=pl.BlockSpec((tm, D_MODEL), lambda i, k: (i, 0)),
        out_shape=jax.ShapeDtypeStruct((m, D_MODEL), F32),
        scratch_shapes=[pltpu.VMEM((tm, A_WIDTH), BF16), pltpu.VMEM((tm, D_MODEL), F32)],
        compiler_params=_cparams(2),
        name="ab_out",
    )(proj, proj, proj, proj, proj, ao_lat, ao_ctx, sgu_w_stack_bf16, sgu_b_t, w_out_stack_bf16, mod, *x_parts)


DN_HALO = 2 * SUBLANES
DN_PROJ_COL_TILE = 2048


def _inproj_dn_kernel(xp_ref, x_ref, xn_ref, nw_ref, mod_ref, w_ref, ws_ref, cw_ref, alog_ref, dtb_ref, qkv_ref, z_ref,
                      gates_ref, h_ref, p_ref,
                      *, lat_tiles, n_lat, q_tiles, qk_tiles, qkv_tiles, ctx_len):
    i = pl.program_id(0)
    j = pl.program_id(1)
    tm = x_ref.shape[0]
    is_lat = i < n_lat
    pos = i % lat_tiles
    first = jnp.logical_or(jnp.logical_not(is_lat), pos == 0)
    last = jnp.logical_or(jnp.logical_not(is_lat), pos == lat_tiles - 1)
    half = CONV_K // 2

    @pl.when(j == 0)
    def _():
        shift = mod_ref[:, 0:D_MODEL]
        scale = mod_ref[:, D_MODEL:2 * D_MODEL]

        def norm_mod(x):
            y = x * lax.rsqrt(jnp.mean(x * x, axis=-1, keepdims=True) + EPS) * nw_ref[...]
            return y * (1.0 + scale) + shift

        h_ref[0:DN_HALO, :] = jnp.where(first, 0.0, norm_mod(xp_ref[...])).astype(BF16)
        h_ref[DN_HALO:DN_HALO + tm, :] = norm_mod(x_ref[...]).astype(BF16)
        h_ref[DN_HALO + tm:, :] = jnp.where(last, 0.0, norm_mod(xn_ref[...])).astype(BF16)
        _write_gates(_dot(h_ref[DN_HALO:DN_HALO + tm, :], ws_ref[...]), alog_ref[...], dtb_ref[...], gates_ref)

    n_ext = h_ref.shape[0]
    tn = w_ref.shape[1]
    pair_w = 2 * HEAD_DIM
    row = lax.broadcasted_iota(jnp.int32, (tm, HEAD_DIM), 0)
    is_ctx = jnp.logical_not(is_lat)

    def conv_silu(ph, sl):
        acc = cw_ref[half:half + 1, sl] * ph[DN_HALO:DN_HALO + tm]
        for t in range(CONV_K):
            off = t - half
            if off == 0:
                continue
            tap = pltpu.roll(ph, (-off) % n_ext, 0)[DN_HALO:DN_HALO + tm]
            crosses = (jnp.logical_and(row >= ctx_len - off, row < ctx_len) if off > 0
                       else jnp.logical_and(row >= ctx_len, row < ctx_len - off))
            tap = jnp.where(jnp.logical_and(is_ctx, crosses), 0.0, tap)
            acc = acc + cw_ref[t:t + 1, sl] * tap
        return _silu(acc)

    def for_each_head(store):
        for n, c0 in enumerate(range(0, tn, pair_w)):
            p_ref[n % 2] = _dot(h_ref[...], w_ref[:, c0:c0 + pair_w])
            for h0 in range(0, pair_w, HEAD_DIM):
                sl = slice(c0 + h0, c0 + h0 + HEAD_DIM)
                store(sl, conv_silu(p_ref[n % 2, :, h0:h0 + HEAD_DIM], sl))

    @pl.when(j < qk_tiles)
    def _():
        qk_scale = jnp.where(j < q_tiles, HEAD_DIM ** -0.5, 1.0)

        def store(sl, y):
            qkv_ref[:, sl] = (y * (lax.rsqrt(jnp.sum(y * y, axis=-1, keepdims=True) + EPS) * qk_scale)).astype(qkv_ref.dtype)
        for_each_head(store)

    @pl.when(jnp.logical_and(j >= qk_tiles, j < qkv_tiles))
    def _():
        def store(sl, y):
            qkv_ref[:, sl] = y.astype(qkv_ref.dtype)
        for_each_head(store)

    @pl.when(j >= qkv_tiles)
    def _():
        z_ref[...] = _dot(h_ref[DN_HALO:DN_HALO + tm, :], w_ref[...]).astype(z_ref.dtype)


def _inproj_dn(x, norm_w, mod, w_stack_bf16, layer, conv_w, alog_vec, dtb_vec, batch, seq, ctx_len):
    m = x.shape[0]
    tm, tn = ROW_TILE, DN_PROJ_COL_TILE
    n_side = 4 * DN_V_HEADS
    n = w_stack_bf16.shape[2] - n_side
    assert n == DN_QKV_W + DN_V_W and tm == 2 * ctx_len
    lat_tiles = seq // tm
    n_lat = batch * lat_tiles
    halo_per_tile = tm // DN_HALO
    n_halo = m // DN_HALO
    qkv_tiles = DN_QKV_W // tn
    kern = functools.partial(_inproj_dn_kernel, lat_tiles=lat_tiles, n_lat=n_lat, q_tiles=DN_K_W // tn,
                             qk_tiles=2 * DN_K_W // tn, qkv_tiles=qkv_tiles, ctx_len=ctx_len)
    return pl.pallas_call(
        kern,
        grid=(m // tm, n // tn),
        in_specs=[pl.BlockSpec((DN_HALO, D_MODEL), lambda i, j: (jnp.maximum(i * halo_per_tile - 1, 0), 0)),
                  pl.BlockSpec((tm, D_MODEL), lambda i, j: (i, 0)),
                  pl.BlockSpec((DN_HALO, D_MODEL), lambda i, j: (jnp.minimum((i + 1) * halo_per_tile, n_halo - 1), 0)),
                  pl.BlockSpec((1, D_MODEL), lambda i, j: (0, 0)),
                  pl.BlockSpec((None, 1, 3 * D_MODEL), lambda i, j: (_mod_row(i, lat_tiles, batch), 0, 0)),
                  pl.BlockSpec((None, D_MODEL, tn), lambda i, j: (layer, 0, j)),
                  pl.BlockSpec((None, D_MODEL, n_side), lambda i, j: (layer, 0, n // n_side)),
                  pl.BlockSpec((SUBLANES, tn), lambda i, j: (0, jnp.minimum(j, qkv_tiles - 1))),
                  pl.BlockSpec((1, n_side), lambda i, j: (0, 0)),
                  pl.BlockSpec((1, n_side), lambda i, j: (0, 0))],
        out_specs=[pl.BlockSpec((tm, tn), lambda i, j: (i, jnp.minimum(j, qkv_tiles - 1))),
                   pl.BlockSpec((tm, tn), lambda i, j: (i, jnp.maximum(j - qkv_tiles, 0))),
                   pl.BlockSpec((tm, n_side), lambda i, j: (i, 0))],
        out_shape=[jax.ShapeDtypeStruct((m, DN_QKV_W), F32),
                   jax.ShapeDtypeStruct((m, DN_V_W), BF16),
                   jax.ShapeDtypeStruct((m, n_side), F32)],
        scratch_shapes=[pltpu.VMEM((tm + 2 * DN_HALO, D_MODEL), BF16),
                        pltpu.VMEM((2, tm + 2 * DN_HALO, 2 * HEAD_DIM), F32)],
        compiler_params=_cparams(2),
        name="inproj_dn",
    )(x, x, x, norm_w.reshape(1, D_MODEL), mod, w_stack_bf16, w_stack_bf16, conv_w, alog_vec, dtb_vec)


def _write_gates(ba, alog, dtb, o_ref):
    tm = ba.shape[0]
    lane = lax.broadcasted_iota(jnp.int32, (DN_CHUNK, LANES), 1)
    is_beta = (lane // DN_V_HEADS) % 2 == 0
    is_fwd = lane < 2 * DN_V_HEADS
    z = ba + dtb
    softplus = jnp.maximum(z, 0.0) + jnp.log1p(jnp.exp(-jnp.abs(z)))
    g = -jnp.exp(alog) * softplus
    r = lax.broadcasted_iota(jnp.int32, (DN_CHUNK, DN_CHUNK), 0)
    c = lax.broadcasted_iota(jnp.int32, (DN_CHUNK, DN_CHUNK), 1)
    tri_lo = (r >= c).astype(BF16)
    tri_up = (r <= c).astype(BF16)
    beta = jax.nn.sigmoid(ba)
    for ch in range(tm // DN_CHUNK):
        rows = slice(ch * DN_CHUNK, (ch + 1) * DN_CHUNK)
        gch = g[rows]
        g1 = gch.astype(BF16)
        r1 = gch - g1.astype(F32)
        g2 = r1.astype(BF16)
        g3 = (r1 - g2.astype(F32)).astype(BF16)
        pre = _dot(tri_lo, g1) + _dot(tri_lo, g2) + _dot(tri_lo, g3)
        suf = _dot(tri_up, g1) + _dot(tri_up, g2) + _dot(tri_up, g3)
        gc = jnp.where(is_fwd, pre, suf)
        o_ref[rows, :] = jnp.where(is_beta, beta[rows], gc)


DN_CHAIN_GROUP = 16
DN_KH_PER_STEP = 16


def _dncore_kernel(qf_ref, kf_ref, vf_ref, gf_ref, qb_ref, kb_ref, vb_ref, gb_ref, of_ref, ob_ref, s_ref):
    C = DN_CHUNK
    nvh = 2 * DN_KH_PER_STEP

    @pl.when(pl.program_id(2) == 0)
    def _():
        s_ref[...] = jnp.zeros(s_ref.shape, F32)

    lane = lax.broadcasted_iota(jnp.int32, (C, 2 * C), 1)
    left = lane < C
    row = lax.broadcasted_iota(jnp.int32, (C, 2 * C), 0)
    colp = lane % C
    left_sq = lax.broadcasted_iota(jnp.int32, (2 * C, 2 * C), 1) < C
    eye2 = (row == colp).astype(F32)

    def blockdiag(p):
        z = jnp.zeros_like(p)
        return jnp.concatenate([jnp.where(left, p, z), jnp.where(left, z, p)], axis=0)

    def packed_mm(a, b):
        return _dot(a.astype(BF16), blockdiag(b.astype(BF16)))

    dirs = ((qf_ref, kf_ref, vf_ref, gf_ref, of_ref), (qb_ref, kb_ref, vb_ref, gb_ref, ob_ref))
    chains = [(d, kh) for d in range(2) for kh in range(DN_KH_PER_STEP)]
    G = [dirs[d][3][...] for d in range(2)]
    GT = [jnp.concatenate([g, g], axis=0).T for g in G]

    for g0 in range(0, len(chains), DN_CHAIN_GROUP):
        group = chains[g0:g0 + DN_CHAIN_GROUP]
        st = []
        for d, kh in group:
            q_ref, k_ref, v_ref, _, _ = dirs[d]
            base_beta = d * 2 * nvh
            base_gc = base_beta + nvh
            lv0 = 2 * kh
            q = q_ref[:, kh * HEAD_DIM:(kh + 1) * HEAD_DIM]
            k = k_ref[:, kh * HEAD_DIM:(kh + 1) * HEAD_DIM]
            k2 = jnp.concatenate([k, k], axis=0)
            kT2 = k2.T
            gram = _dot(jnp.concatenate([q, k], axis=0).astype(BF16), kT2.astype(BF16))
            b0, b1 = G[d][:, base_beta + lv0:base_beta + lv0 + 1], G[d][:, base_beta + lv0 + 1:base_beta + lv0 + 2]
            c0, c1 = G[d][:, base_gc + lv0:base_gc + lv0 + 1], G[d][:, base_gc + lv0 + 1:base_gc + lv0 + 2]
            r0, r1 = GT[d][base_gc + lv0:base_gc + lv0 + 1, :], GT[d][base_gc + lv0 + 1:base_gc + lv0 + 2, :]
            st.append(dict(d=d, lv0=lv0, q=q, k2=k2, kT2=kT2, gram=gram, b0=b0, b1=b1, c0=c0, c1=c1, r0=r0, r1=r1))

        for s in st:
            d = s["d"]
            incl = (row >= colp) if d == 0 else (row <= colp)
            strict = (row > colp) if d == 0 else (row < colp)
            gcol_p = jnp.where(left, s["c0"], s["c1"])
            grow_p = jnp.where(left[0:1], s["r0"], s["r1"])
            beta_p = jnp.where(left, s["b0"], s["b1"])
            dec = jnp.exp(jnp.where(incl, gcol_p - grow_p, -1e30))
            s["dec"] = dec
            s["attn"] = s["gram"][0:C] * dec
            s["L"] = jnp.where(strict, s["gram"][C:2 * C] * dec, 0.0) * beta_p

        for s in st:
            n1 = jnp.where(jnp.logical_and(row // 2 == colp // 2, row != colp), s["L"], 0.0)
            s["X"] = eye2 - n1
        blk = 2
        while blk < C:
            mask = jnp.logical_and(row // (2 * blk) == colp // (2 * blk), row // blk != colp // blk)
            for s in st:
                s["Y"] = packed_mm(s["X"], jnp.where(mask, s["L"], 0.0))
            for s in st:
                s["X"] = s["X"] - packed_mm(s["Y"], s["X"])
            blk *= 2

        for s in st:
            d, lv0 = s["d"], s["lv0"]
            v_ref = dirs[d][2]
            beta_r = jnp.concatenate([s["b0"], s["b1"]], axis=0)
            egc_r = jnp.exp(jnp.concatenate([s["c0"], s["c1"]], axis=0))
            v2 = jnp.concatenate([v_ref[:, lv0 * HEAD_DIM:(lv0 + 1) * HEAD_DIM],
                                  v_ref[:, (lv0 + 1) * HEAD_DIM:(lv0 + 2) * HEAD_DIM]], axis=0)
            rhs = jnp.concatenate([v2 * beta_r, s["k2"] * (beta_r * egc_r)], axis=1)
            s["sol"] = _dot(blockdiag(s["X"]).astype(BF16), rhs.astype(BF16))
            s["egc"] = egc_r
            s["qg2"] = jnp.concatenate([s["q"], s["q"]], axis=0) * egc_r

        for s in st:
            d, lv0 = s["d"], s["lv0"]
            w2 = s["sol"][:, HEAD_DIM:]
            s["ws"] = []
            for r in range(2):
                lhs = jnp.concatenate([w2[r * C:(r + 1) * C], s["qg2"][r * C:(r + 1) * C]], axis=0).astype(BF16)
                s["ws"].append(_dot(lhs, s_ref[d, lv0 + r].astype(BF16)))

        for s in st:
            u2 = s["sol"][:, 0:HEAD_DIM]
            vn2 = jnp.concatenate([u2[r * C:(r + 1) * C] - s["ws"][r][0:C] for r in range(2)], axis=0).astype(BF16)
            s["vn2"] = vn2
            s["o2"] = jnp.concatenate([s["ws"][r][C:2 * C] for r in range(2)], axis=0) + _dot(
                blockdiag(s["attn"]).astype(BF16), vn2)

        for s in st:
            d, lv0 = s["d"], s["lv0"]
            o_ref = dirs[d][4]
            last = C - 1 if d == 0 else 0
            kdT_p = s["kT2"] * s["dec"][last:last + 1, :]
            zkd = jnp.zeros_like(kdT_p)
            for r in range(2):
                kd_r = jnp.where(left_sq if r == 0 else jnp.logical_not(left_sq), kdT_p, zkd).astype(BF16)
                glr = jnp.broadcast_to(s["egc"][r * C + last:r * C + last + 1, :], (HEAD_DIM, HEAD_DIM))
                s_ref[d, lv0 + r] = s_ref[d, lv0 + r] * glr + _dot(kd_r, s["vn2"])
                o_ref[:, (lv0 + r) * HEAD_DIM:(lv0 + r + 1) * HEAD_DIM] = s["o2"][r * C:(r + 1) * C].astype(o_ref.dtype)


def _dn_core(qkv, gates, batch, seq, ctx_len):
    m = qkv.shape[0]
    C = DN_CHUNK
    n_lat = seq // C
    n_ctx = ctx_len // C
    n_steps = n_ctx + n_lat
    ctx0 = batch * n_lat
    khs = DN_KH_PER_STEP
    qw = khs * HEAD_DIM
    vw = 2 * khs * HEAD_DIM
    k_blk0 = DN_K_W // qw
    v_blk0 = 2 * DN_K_W // vw

    def rf(b, t):
        return jnp.where(t < n_ctx, ctx0 + b * n_ctx + t, b * n_lat + (t - n_ctx))

    def rb(b, t):
        return jnp.where(t < n_ctx, ctx0 + b * n_ctx + (n_ctx - 1 - t), b * n_lat + (n_lat - 1 - (t - n_ctx)))

    def specs(rfun):
        return [pl.BlockSpec((C, qw), lambda b, h, t: (rfun(b, t), h)),
                pl.BlockSpec((C, qw), lambda b, h, t: (rfun(b, t), k_blk0 + h)),
                pl.BlockSpec((C, vw), lambda b, h, t: (rfun(b, t), v_blk0 + h)),
                pl.BlockSpec((C, LANES), lambda b, h, t: (rfun(b, t), h))]

    return pl.pallas_call(
        _dncore_kernel,
        grid=(batch, DN_K_HEADS // khs, n_steps),
        in_specs=specs(rf) + specs(rb),
        out_specs=[pl.BlockSpec((C, vw), lambda b, h, t: (rf(b, t), h)),
                   pl.BlockSpec((C, vw), lambda b, h, t: (rb(b, t), h))],
        out_shape=[jax.ShapeDtypeStruct((m, DN_V_W), BF16), jax.ShapeDtypeStruct((m, DN_V_W), BF16)],
        scratch_shapes=[pltpu.VMEM((2, 2 * khs, HEAD_DIM, HEAD_DIM), F32)],
        compiler_params=_cparams(3),
        name="dn_core",
    )(qkv, qkv, qkv, gates, qkv, qkv, qkv, gates)


DN_OUT_ROW_TILE = 256


def _dn_out_kernel(of_ref, ob_ref, z_ref, nw_ref, w_ref, x_ref, gate_ref, *rest):
    o_ref, y_ref = rest[-2:]
    for h in range(of_ref.shape[1] // HEAD_DIM):
        sl = slice(h * HEAD_DIM, (h + 1) * HEAD_DIM)
        o = of_ref[:, sl].astype(F32) + ob_ref[:, sl].astype(F32)
        n = o * lax.rsqrt(jnp.mean(o * o, axis=-1, keepdims=True) + EPS) * nw_ref[...]
        y_ref[:, sl] = (n * _silu(z_ref[:, sl].astype(F32))).astype(BF16)
    y = x_ref[...] + gate_ref[...] * _dot(y_ref[...], w_ref[...])
    if len(rest) == 3:
        y = y * lax.rsqrt(jnp.mean(y * y, axis=-1, keepdims=True) + EPS) * rest[0][...]
    o_ref[...] = y


def _dn_out(o_f, o_b, z, norm_w, w_out_stack_bf16, layer, x, mod, batch, seq, final_norm_w=None):
    tm = DN_OUT_ROW_TILE
    m = x.shape[0] if final_norm_w is None else batch * seq
    lat_tiles = seq // tm
    extra_specs, extra_args = [], []
    if final_norm_w is not None:
        extra_specs = [pl.BlockSpec((1, D_MODEL), lambda i: (0, 0))]
        extra_args = [final_norm_w.reshape(1, D_MODEL)]
    return pl.pallas_call(
        _dn_out_kernel,
        grid=(m // tm,),
        in_specs=[pl.BlockSpec((tm, DN_V_W), lambda i: (i, 0)),
                  pl.BlockSpec((tm, DN_V_W), lambda i: (i, 0)),
                  pl.BlockSpec((tm, DN_V_W), lambda i: (i, 0)),
                  pl.BlockSpec((1, HEAD_DIM), lambda i: (0, 0)),
                  pl.BlockSpec((None, DN_V_W, D_MODEL), lambda i: (layer, 0, 0), pipeline_mode=pl.Buffered(1)),
                  pl.BlockSpec((tm, D_MODEL), lambda i: (i, 0)),
                  pl.BlockSpec((None, 1, D_MODEL), lambda i: (_mod_row(i, lat_tiles, batch), 0, 2))] + extra_specs,
        out_specs=pl.BlockSpec((tm, D_MODEL), lambda i: (i, 0)),
        out_shape=jax.ShapeDtypeStruct((m, D_MODEL), F32),
        scratch_shapes=[pltpu.VMEM((tm, DN_V_W), BF16)],
        compiler_params=_cparams(1),
        name="dn_out",
    )(o_f, o_b, z, norm_w.reshape(1, HEAD_DIM), w_out_stack_bf16, x, mod, *extra_args)


def _final_norm_kernel(x_ref, w_ref, o_ref):
    x = x_ref[...]
    o_ref[...] = x * lax.rsqrt(jnp.mean(x * x, axis=-1, keepdims=True) + EPS) * w_ref[...]


def _final_norm(x, w, rows):
    tm = ROW_TILE
    return pl.pallas_call(
        _final_norm_kernel,
        grid=(rows // tm,),
        in_specs=[pl.BlockSpec((tm, D_MODEL), lambda i: (i, 0)),
                  pl.BlockSpec((1, D_MODEL), lambda i: (0, 0))],
        out_specs=pl.BlockSpec((tm, D_MODEL), lambda i: (i, 0)),
        out_shape=jax.ShapeDtypeStruct((rows, D_MODEL), F32),
        compiler_params=_cparams(1),
        name="final_norm",
    )(x, w.reshape(1, D_MODEL))


def kernel(x, c, ctx, c_ctx, norm_w, ada_w, ada_b, ab_w_in, ab_w_out, sgu_w, sgu_b, q_norm_w, k_norm_w,
           dn_w_in, dn_conv_w, dn_a_log, dn_dt_bias, dn_norm_w, dn_w_out, final_norm_w):
    batch, seq, _ = x.shape
    ctx_len = ctx.shape[1]
    depth = norm_w.shape[0]
    assert ctx_len == CONV_ROW_TILE and seq % ROW_TILE == 0 and batch * ctx_len == ROW_TILE
    n_lat_rows = batch * seq

    x_parts = (x.reshape(n_lat_rows, D_MODEL), ctx.reshape(batch * ctx_len, D_MODEL))
    ab_w_in_bf, ab_w_out_bf, sgu_w_bf = ab_w_in.astype(BF16), ab_w_out.astype(BF16), sgu_w.astype(BF16)
    dn_w_in_bf, dn_w_out_bf = dn_w_in.astype(BF16), dn_w_out.astype(BF16)
    cond = jnp.zeros((SUBLANES, D_MODEL), F32).at[0:batch].set(c).at[batch].set(c_ctx)
    mods = _ada_mod(cond, ada_w, ada_b)
    cos_tab, sin_tab = _rope_tables(seq, CONV_ROW_TILE)

    for i in range(depth):
        j = i // 2
        mod = mods[i].reshape(SUBLANES, 1, 3 * D_MODEL)
        if i % 2 == 0:
            proj = _inproj(x_parts, norm_w[i], mod, ab_w_in_bf, j, batch, seq)
            qr, kr, vr = _qkprep(proj, cos_tab, sin_tab, q_norm_w[j], k_norm_w[j], batch, seq)
            ao_lat = _attention(qr, kr, vr, batch, seq, ctx_len, True)
            ao_ctx = _attention(qr, kr, vr, batch, seq, ctx_len, False)
            x_parts = (_ab_out(proj, ao_lat, ao_ctx, sgu_w_bf, sgu_b[j].T, ab_w_out_bf, j, x_parts, mod, batch, seq),)
        else:
            conv_w = jnp.zeros((SUBLANES, DN_QKV_W), F32).at[0:CONV_K].set(dn_conv_w[j])
            assert len(x_parts) == 1
            zeros = jnp.zeros((2, DN_V_HEADS), F32)
            alog_vec = jnp.concatenate([zeros, dn_a_log[j]], axis=1).reshape(1, LANES)
            dtb_vec = jnp.concatenate([zeros, dn_dt_bias[j]], axis=1).reshape(1, LANES)
            qkv, z, gates = _inproj_dn(x_parts[0], norm_w[i], mod, dn_w_in_bf, j, conv_w, alog_vec, dtb_vec, batch, seq,
                                       ctx_len)
            n_hg = DN_K_HEADS // DN_KH_PER_STEP
            nvh = 2 * DN_KH_PER_STEP
            g4 = gates
            if n_hg > 1:
                g4 = gates.reshape(-1, 4, n_hg, nvh).transpose(0, 2, 1, 3).reshape(-1, n_hg, 4 * nvh)
                g4 = jnp.pad(g4, ((0, 0), (0, 0), (0, LANES - 4 * nvh))).reshape(-1, n_hg * LANES)
            o_f, o_b = _dn_core(qkv, g4, batch, seq, ctx_len)
            x_parts = (_dn_out(o_f, o_b, z, dn_norm_w[j], dn_w_out_bf, j, x_parts[0], mod, batch, seq,
                               final_norm_w=final_norm_w if i == depth - 1 else None),)

    xs = x_parts[0]
    if depth % 2 == 1:
        xs = _final_norm(xs, final_norm_w, n_lat_rows)
    return xs.reshape(batch, seq, D_MODEL)
```

```python
import functools

import jax
import jax.numpy as jnp
from jax import lax
from jax.experimental import pallas as pl
from jax.experimental.pallas import tpu as pltpu

F32 = jnp.float32
BF16 = jnp.bfloat16

D_MODEL = 2048
GRID_W = 64
EPS = 1e-6
HEAD_DIM = 128
A_WIDTH = D_MODEL // 2
A_GROUPS = A_WIDTH // 128
SGU_CHUNK = 128
B_HEADS = (D_MODEL // 2) // HEAD_DIM
B_KV_HEADS = B_HEADS // 4
B_GROUP = B_HEADS // B_KV_HEADS
B_Q_W = B_HEADS * HEAD_DIM
B_KV_W = B_KV_HEADS * HEAD_DIM
ROPE_THETA = 10000.0
AXIS_DIM = HEAD_DIM // 2
AB_IN_W = 3 * A_WIDTH + 2 * B_Q_W + 2 * B_KV_W
DN_K_HEADS = D_MODEL // HEAD_DIM
DN_V_HEADS = 2 * DN_K_HEADS
DN_K_W = DN_K_HEADS * HEAD_DIM
DN_V_W = DN_V_HEADS * HEAD_DIM
DN_QKV_W = 2 * DN_K_W + DN_V_W
DN_IN_W = DN_QKV_W + DN_V_W + 4 * DN_V_HEADS
DN_CHUNK = 64
CONV_K = 5

V7X_VMEM_LIMIT_BYTES = 56 * 1024 * 1024
LANES = 128
SUBLANES = 8

ROW_TILE = 512
CONV_ROW_TILE = 256


def _cparams(n_axes):
    return pltpu.CompilerParams(dimension_semantics=("arbitrary",) * n_axes,
                                vmem_limit_bytes=V7X_VMEM_LIMIT_BYTES)


def _silu(x):
    return x * jax.nn.sigmoid(x)


def _split_bf16(a):
    hi = a.astype(BF16)
    lo = (a - hi.astype(F32)).astype(BF16)
    return hi, lo


def _dot(a, b):
    return jnp.dot(a, b, preferred_element_type=F32)


def _dot_nt(a, b):
    return lax.dot_general(a, b, (((1,), (1,)), ((), ())), preferred_element_type=F32)


def _dot3(a, b):
    ah, al = _split_bf16(a)
    bh, bl = _split_bf16(b)
    return _dot(ah, bh) + _dot(ah, bl) + _dot(al, bh)


def _ada_kernel(c_ref, w_ref, b_ref, o_ref):
    s = _silu(c_ref[...])
    o_ref[0] = _dot3(s, w_ref[0]) + b_ref[0]


def _ada_mod(cond, ada_w, ada_b):
    depth = ada_w.shape[0]
    tn = 1536
    return pl.pallas_call(
        _ada_kernel,
        grid=(depth, 3 * D_MODEL // tn),
        in_specs=[pl.BlockSpec((SUBLANES, D_MODEL), lambda l, j: (0, 0)),
                  pl.BlockSpec((1, D_MODEL, tn), lambda l, j: (l, 0, j)),
                  pl.BlockSpec((1, 1, tn), lambda l, j: (l, 0, j))],
        out_specs=pl.BlockSpec((1, SUBLANES, tn), lambda l, j: (l, 0, j)),
        out_shape=jax.ShapeDtypeStruct((depth, SUBLANES, 3 * D_MODEL), F32),
        compiler_params=_cparams(2),
        name="ada_mod",
    )(cond, ada_w, ada_b.reshape(depth, 1, 3 * D_MODEL))


INPROJ_COL_TILES = (512, 1024, 1408, 1536, 2816)


def _inproj_kernel(*refs, n_lat_tiles):
    nw_ref, mod_ref, w_ref, o_ref, h_ref = refs[-5:]

    @pl.when(pl.program_id(1) == 0)
    def _():
        x = _select_rows(refs[:-5], n_lat_tiles)
        y = x * lax.rsqrt(jnp.mean(x * x, axis=-1, keepdims=True) + EPS) * nw_ref[...]
        shift = mod_ref[:, 0:D_MODEL]
        scale = mod_ref[:, D_MODEL:2 * D_MODEL]
        h_ref[...] = (y * (1.0 + scale) + shift).astype(BF16)

    o_ref[...] = _dot(h_ref[...], w_ref[...]).astype(o_ref.dtype)


def _select_rows(x_refs, n_lat_tiles):
    if len(x_refs) == 1:
        return x_refs[0][...]
    return jnp.where(pl.program_id(0) < n_lat_tiles, x_refs[0][...], x_refs[1][...])


def _row_specs(x_parts, tm, width, n_lat_tiles):
    if len(x_parts) == 1:
        return [pl.BlockSpec((tm, width), lambda i, j: (i, 0))]
    assert x_parts[1].shape[0] == tm
    return [pl.BlockSpec((tm, width), lambda i, j: (jnp.minimum(i, n_lat_tiles - 1), 0)),
            pl.BlockSpec((tm, width), lambda i, j: (0, 0))]


def _mod_row(i, lat_tiles, batch):
    return jnp.minimum(i // lat_tiles, batch)


def _inproj(x_parts, norm_w, mod, w_stack_bf16, layer, batch, seq):
    m = sum(p.shape[0] for p in x_parts)
    n = w_stack_bf16.shape[2]
    tm = ROW_TILE
    tn = max(t for t in INPROJ_COL_TILES if n % t == 0)
    lat_tiles = seq // tm
    n_lat_tiles = batch * lat_tiles
    return pl.pallas_call(
        functools.partial(_inproj_kernel, n_lat_tiles=n_lat_tiles),
        grid=(m // tm, n // tn),
        in_specs=_row_specs(x_parts, tm, D_MODEL, n_lat_tiles) + [
            pl.BlockSpec((1, D_MODEL), lambda i, j: (0, 0)),
            pl.BlockSpec((None, 1, 3 * D_MODEL), lambda i, j: (_mod_row(i, lat_tiles, batch), 0, 0)),
            pl.BlockSpec((None, D_MODEL, tn), lambda i, j: (layer, 0, j))],
        out_specs=pl.BlockSpec((tm, tn), lambda i, j: (i, j)),
        out_shape=jax.ShapeDtypeStruct((m, n), BF16),
        scratch_shapes=[pltpu.VMEM((tm, D_MODEL), BF16)],
        compiler_params=_cparams(2),
        name="inproj",
    )(*x_parts, norm_w.reshape(1, D_MODEL), mod, w_stack_bf16)


Q_PRESCALE = (HEAD_DIM ** -0.5) * 1.4426950408889634


def _qkprep_kernel(q_ref, k_ref, v_ref, cos_ref, sin_ref, qw_ref, kw_ref, qo_ref, ko_ref, vo_ref):
    cos = cos_ref[...]
    sin = sin_ref[...]
    lane = lax.broadcasted_iota(jnp.int32, cos.shape, 1)
    first = (lane % (AXIS_DIM)) < (AXIS_DIM // 2)

    def prep(x, w):
        y = x * lax.rsqrt(jnp.mean(x * x, axis=-1, keepdims=True) + EPS) * w
        rot = jnp.where(first, pltpu.roll(y, HEAD_DIM - AXIS_DIM // 2, 1), pltpu.roll(y, AXIS_DIM // 2, 1))
        return y * cos + rot * sin

    for h in range(B_HEADS):
        sl = slice(h * HEAD_DIM, (h + 1) * HEAD_DIM)
        qo_ref[:, sl] = (prep(q_ref[:, sl].astype(F32), qw_ref[...]) * Q_PRESCALE).astype(qo_ref.dtype)
    for h in range(B_KV_HEADS):
        sl = slice(h * HEAD_DIM, (h + 1) * HEAD_DIM)
        ko_ref[:, sl] = prep(k_ref[:, sl].astype(F32), kw_ref[...]).astype(ko_ref.dtype)
    vo_ref[...] = v_ref[...].astype(vo_ref.dtype)


def _qkprep(proj, cos_tab, sin_tab, qn_w, kn_w, batch, seq):
    m = proj.shape[0]
    tm = CONV_ROW_TILE
    lat_tiles = seq // tm
    n_lat = batch * lat_tiles
    q_blk = (3 * A_WIDTH) // B_Q_W
    k_blk = (3 * A_WIDTH + B_Q_W) // B_KV_W

    def tab_idx(i):
        return (jnp.where(i < n_lat, i % lat_tiles, lat_tiles), 0)

    def kv_idx(i):
        lat_blk = (i // lat_tiles) * (lat_tiles + 1) + 1 + i % lat_tiles
        return (jnp.where(i < n_lat, lat_blk, (i - n_lat) * (lat_tiles + 1)), 0)

    return pl.pallas_call(
        _qkprep_kernel,
        grid=(m // tm,),
        in_specs=[pl.BlockSpec((tm, B_Q_W), lambda i: (i, q_blk)),
                  pl.BlockSpec((tm, B_KV_W), lambda i: (i, k_blk)),
                  pl.BlockSpec((tm, B_KV_W), lambda i: (i, k_blk + 1)),
                  pl.BlockSpec((tm, HEAD_DIM), tab_idx),
                  pl.BlockSpec((tm, HEAD_DIM), tab_idx),
                  pl.BlockSpec((1, HEAD_DIM), lambda i: (0, 0)),
                  pl.BlockSpec((1, HEAD_DIM), lambda i: (0, 0))],
        out_specs=[pl.BlockSpec((tm, B_Q_W), lambda i: (i, 0)),
                   pl.BlockSpec((tm, B_KV_W), kv_idx),
                   pl.BlockSpec((tm, B_KV_W), kv_idx)],
        out_shape=[jax.ShapeDtypeStruct((m, B_Q_W), BF16),
                   jax.ShapeDtypeStruct((m, B_KV_W), BF16),
                   jax.ShapeDtypeStruct((m, B_KV_W), BF16)],
        compiler_params=_cparams(1),
        name="qk_prep",
    )(proj, proj, proj, cos_tab, sin_tab, qn_w.reshape(1, HEAD_DIM), kn_w.reshape(1, HEAD_DIM))


def _rope_tables(seq, tm):
    rows = seq // GRID_W
    row = jnp.repeat(jnp.arange(rows), GRID_W).astype(F32)
    col = jnp.tile(jnp.arange(GRID_W), rows).astype(F32)
    freqs = ROPE_THETA ** (-jnp.arange(0, AXIS_DIM, 2, dtype=F32) / AXIS_DIM)
    ang_r = row[:, None] * freqs[None, :]
    ang_c = col[:, None] * freqs[None, :]
    ang = jnp.concatenate([ang_r, ang_r, ang_c, ang_c], axis=-1)
    sign = jnp.where((jnp.arange(HEAD_DIM) % AXIS_DIM) < AXIS_DIM // 2, -1.0, 1.0).astype(F32)
    cos = jnp.concatenate([jnp.cos(ang), jnp.ones((tm, HEAD_DIM), F32)], axis=0)
    sin = jnp.concatenate([jnp.sin(ang) * sign[None, :], jnp.zeros((tm, HEAD_DIM), F32)], axis=0)
    return cos, sin


ATTN_Q_TILE = 256
ATTN_KV_BODY = 4096
ATTN_KV_TAIL = 768


def _kv_chunks(kv_len):
    tail = min(ATTN_KV_TAIL, kv_len)
    body = kv_len - tail
    sizes = [ATTN_KV_BODY] * (body // ATTN_KV_BODY) + ([body % ATTN_KV_BODY] if body % ATTN_KV_BODY else []) + [tail]
    assert all(c % LANES == 0 for c in sizes) and sum(sizes) == kv_len
    return tuple(sizes)


def _attn_kernel(q_ref, k_ref, v_ref, o_ref, m_scr, l_scr, acc_scr, *, chunks, tq):
    q = jnp.concatenate([q_ref[:, g * HEAD_DIM:(g + 1) * HEAD_DIM] for g in range(B_GROUP)], axis=0)
    m_scr[...] = jnp.full(m_scr.shape, -jnp.inf, F32)
    l_scr[...] = jnp.zeros(l_scr.shape, F32)
    acc_scr[...] = jnp.zeros(acc_scr.shape, F32)
    start = 0
    for tk in chunks:
        k = k_ref[start:start + tk, :]
        v = v_ref[start:start + tk, :]
        start += tk
        s = _dot_nt(q, k)
        m_prev = m_scr[...]
        m_next = jnp.maximum(m_prev, jnp.max(s, axis=1, keepdims=True))
        p = jnp.exp2(s - jnp.concatenate([m_next] * (tk // LANES), axis=1))
        alpha = jnp.exp2(m_prev - m_next)
        pv = _dot(p.astype(BF16), jnp.concatenate([v, jnp.ones_like(v)], axis=1))
        l_scr[...] = alpha * l_scr[...] + pv[:, HEAD_DIM:]
        acc_scr[...] = acc_scr[...] * alpha + pv[:, :HEAD_DIM]
        m_scr[...] = m_next
    out = acc_scr[...] / l_scr[...]
    for g in range(B_GROUP):
        o_ref[:, g * HEAD_DIM:(g + 1) * HEAD_DIM] = out[g * tq:(g + 1) * tq].astype(o_ref.dtype)


def _attention(qr, kr, vr, batch, seq, ctx_len, latent):
    tq = ATTN_Q_TILE
    gw = B_GROUP * HEAD_DIM
    q_len = seq if latent else ctx_len
    q_tiles = q_len // tq
    q_row0 = 0 if latent else batch * seq // tq
    kv_len = ctx_len + seq
    if latent:
        kv_rows = kv_len
        kv_idx = lambda b, h, i: (b, h)
    else:
        kv_rows = ctx_len
        kv_idx = lambda b, h, i: (b * (kv_len // ctx_len), h)
    rows = B_GROUP * tq
    return pl.pallas_call(
        functools.partial(_attn_kernel, chunks=_kv_chunks(kv_rows), tq=tq),
        grid=(batch, B_KV_HEADS, q_tiles),
        in_specs=[pl.BlockSpec((tq, gw), lambda b, h, i: (q_row0 + b * q_tiles + i, h)),
                  pl.BlockSpec((kv_rows, HEAD_DIM), kv_idx),
                  pl.BlockSpec((kv_rows, HEAD_DIM), kv_idx)],
        out_specs=pl.BlockSpec((tq, gw), lambda b, h, i: (b * q_tiles + i, h)),
        out_shape=jax.ShapeDtypeStruct((batch * q_len, B_Q_W), BF16),
        scratch_shapes=[pltpu.VMEM((rows, LANES), F32), pltpu.VMEM((rows, LANES), F32),
                        pltpu.VMEM((rows, HEAD_DIM), F32)],
        compiler_params=_cparams(3),
        name="attn_lat" if latent else "attn_ctx",
    )(qr, kr, vr)


def _ab_out_kernel(u_ref, v_ref, ga_ref, gb0_ref, gb1_ref, aol_ref, aoc_ref, ws_ref, bs_ref, w_ref, gate_ref, *rest,
                   n_lat_tiles):
    o_ref, y_ref, acc_ref = rest[-3:]
    k = pl.program_id(1)

    @pl.when(k == 0)
    def _():
        tm = u_ref.shape[0]
        for c in range(tm // SGU_CHUNK):
            rows = slice(c * SGU_CHUNK, (c + 1) * SGU_CHUNK)
            for g in range(A_GROUPS):
                cols = slice(g * LANES, (g + 1) * LANES)
                vg = v_ref[rows, cols].astype(F32)
                d = vg - jnp.mean(vg, axis=-1, keepdims=True)
                var = jnp.mean(d * d, axis=-1, keepdims=True)
                vn = (d * lax.rsqrt(var + 1e-5)).astype(BF16)
                mixed = _dot(ws_ref[g], vn) + bs_ref[:, g:g + 1]
                y_ref[rows, cols] = (u_ref[rows, cols].astype(F32) * mixed
                                     * _silu(ga_ref[rows, cols].astype(F32))).astype(BF16)
        acc_ref[...] = _dot(y_ref[...], w_ref[0:A_WIDTH, :])

    @pl.when(k == 1)
    def _():
        ao = jnp.where(pl.program_id(0) < n_lat_tiles, aol_ref[...], aoc_ref[...]).astype(F32)
        gb = jnp.concatenate([gb0_ref[...], gb1_ref[...]], axis=1).astype(F32)
        y = (ao * _silu(gb)).astype(BF16)
        o_ref[...] = _select_rows(rest[:-3], n_lat_tiles) + gate_ref[...] * (acc_ref[...] + _dot(y, w_ref[A_WIDTH:, :]))


def _ab_out(proj, ao_lat, ao_ctx, sgu_w_stack_bf16, sgu_b_t, w_out_stack_bf16, layer, x_parts, mod, batch, seq):
    m = proj.shape[0]
    tm = ROW_TILE
    lat_tiles = seq // tm
    n_lat_tiles = batch * lat_tiles
    assert ao_ctx.shape[0] == tm
    gb_blk0 = (3 * A_WIDTH + B_Q_W + 2 * B_KV_W) // (B_Q_W // 2)
    return pl.pallas_call(
        functools.partial(_ab_out_kernel, n_lat_tiles=n_lat_tiles),
        grid=(m // tm, 2),
        in_specs=[pl.BlockSpec((tm, A_WIDTH), lambda i, k: (i, 0)),
                  pl.BlockSpec((tm, A_WIDTH), lambda i, k: (i, 1)),
                  pl.BlockSpec((tm, A_WIDTH), lambda i, k: (i, 2)),
                  pl.BlockSpec((tm, B_Q_W // 2), lambda i, k: (i, gb_blk0)),
                  pl.BlockSpec((tm, B_Q_W // 2), lambda i, k: (i, gb_blk0 + 1)),
                  pl.BlockSpec((tm, B_Q_W), lambda i, k: (jnp.minimum(i, n_lat_tiles - 1), 0)),
                  pl.BlockSpec((tm, B_Q_W), lambda i, k: (0, 0)),
                  pl.BlockSpec((None, A_GROUPS, SGU_CHUNK, SGU_CHUNK), lambda i, k: (layer, 0, 0, 0)),
                  pl.BlockSpec((SGU_CHUNK, A_GROUPS), lambda i, k: (0, 0)),
                  pl.BlockSpec((None, A_WIDTH + B_Q_W, D_MODEL), lambda i, k: (layer, 0, 0),
                               pipeline_mode=pl.Buffered(1)),
                  pl.BlockSpec((None, 1, D_MODEL), lambda i, k: (_mod_row(i, lat_tiles, batch), 0, 2))]
        + _row_specs(x_parts, tm, D_MODEL, n_lat_tiles),
        out_specs=pl.BlockSpec((tm, D_MODEL), lambda i, k: (i, 0)),
        out_shape=jax.ShapeDtypeStruct((m, D_MODEL), F32),
        scratch_shapes=[pltpu.VMEM((tm, A_WIDTH), BF16), pltpu.VMEM((tm, D_MODEL), F32)],
        compiler_params=_cparams(2),
        name="ab_out",
    )(proj, proj, proj, proj, proj, ao_lat, ao_ctx, sgu_w_stack_bf16, sgu_b_t, w_out_stack_bf16, mod, *x_parts)


DN_HALO = 2 * SUBLANES
DN_PROJ_COL_TILE = 2048


def _inproj_dn_kernel(xp_ref, x_ref, xn_ref, nw_ref, mod_ref, w_ref, ws_ref, cw_ref, alog_ref, dtb_ref, qkv_ref, z_ref,
                      gates_ref, h_ref, p_ref,
                      *, lat_tiles, n_lat, q_tiles, qk_tiles, qkv_tiles, ctx_len):
    i = pl.program_id(0)
    j = pl.program_id(1)
    tm = x_ref.shape[0]
    is_lat = i < n_lat
    pos = i % lat_tiles
    first = jnp.logical_or(jnp.logical_not(is_lat), pos == 0)
    last = jnp.logical_or(jnp.logical_not(is_lat), pos == lat_tiles - 1)
    half = CONV_K // 2

    @pl.when(j == 0)
    def _():
        shift = mod_ref[:, 0:D_MODEL]
        scale = mod_ref[:, D_MODEL:2 * D_MODEL]

        def norm_mod(x):
            y = x * lax.rsqrt(jnp.mean(x * x, axis=-1, keepdims=True) + EPS) * nw_ref[...]
            return y * (1.0 + scale) + shift

        h_ref[0:DN_HALO, :] = jnp.where(first, 0.0, norm_mod(xp_ref[...])).astype(BF16)
        h_ref[DN_HALO:DN_HALO + tm, :] = norm_mod(x_ref[...]).astype(BF16)
        h_ref[DN_HALO + tm:, :] = jnp.where(last, 0.0, norm_mod(xn_ref[...])).astype(BF16)
        _write_gates(_dot(h_ref[DN_HALO:DN_HALO + tm, :], ws_ref[...]), alog_ref[...], dtb_ref[...], gates_ref)

    n_ext = h_ref.shape[0]
    tn = w_ref.shape[1]
    pair_w = 2 * HEAD_DIM
    row = lax.broadcasted_iota(jnp.int32, (tm, HEAD_DIM), 0)
    is_ctx = jnp.logical_not(is_lat)

    def conv_silu(ph, sl):
        acc = cw_ref[half:half + 1, sl] * ph[DN_HALO:DN_HALO + tm]
        for t in range(CONV_K):
            off = t - half
            if off == 0:
                continue
            tap = pltpu.roll(ph, (-off) % n_ext, 0)[DN_HALO:DN_HALO + tm]
            crosses = (jnp.logical_and(row >= ctx_len - off, row < ctx_len) if off > 0
                       else jnp.logical_and(row >= ctx_len, row < ctx_len - off))
            tap = jnp.where(jnp.logical_and(is_ctx, crosses), 0.0, tap)
            acc = acc + cw_ref[t:t + 1, sl] * tap
        return _silu(acc)

    def for_each_head(store):
        for n, c0 in enumerate(range(0, tn, pair_w)):
            p_ref[n % 2] = _dot(h_ref[...], w_ref[:, c0:c0 + pair_w])
            for h0 in range(0, pair_w, HEAD_DIM):
                sl = slice(c0 + h0, c0 + h0 + HEAD_DIM)
                store(sl, conv_silu(p_ref[n % 2, :, h0:h0 + HEAD_DIM], sl))

    @pl.when(j < qk_tiles)
    def _():
        qk_scale = jnp.where(j < q_tiles, HEAD_DIM ** -0.5, 1.0)

        def store(sl, y):
            qkv_ref[:, sl] = (y * (lax.rsqrt(jnp.sum(y * y, axis=-1, keepdims=True) + EPS) * qk_scale)).astype(qkv_ref.dtype)
        for_each_head(store)

    @pl.when(jnp.logical_and(j >= qk_tiles, j < qkv_tiles))
    def _():
        def store(sl, y):
            qkv_ref[:, sl] = y.astype(qkv_ref.dtype)
        for_each_head(store)

    @pl.when(j >= qkv_tiles)
    def _():
        z_ref[...] = _dot(h_ref[DN_HALO:DN_HALO + tm, :], w_ref[...]).astype(z_ref.dtype)


def _inproj_dn(x, norm_w, mod, w_stack_bf16, layer, conv_w, alog_vec, dtb_vec, batch, seq, ctx_len):
    m = x.shape[0]
    tm, tn = ROW_TILE, DN_PROJ_COL_TILE
    n_side = 4 * DN_V_HEADS
    n = w_stack_bf16.shape[2] - n_side
    assert n == DN_QKV_W + DN_V_W and tm == 2 * ctx_len
    lat_tiles = seq // tm
    n_lat = batch * lat_tiles
    halo_per_tile = tm // DN_HALO
    n_halo = m // DN_HALO
    qkv_tiles = DN_QKV_W // tn
    kern = functools.partial(_inproj_dn_kernel, lat_tiles=lat_tiles, n_lat=n_lat, q_tiles=DN_K_W // tn,
                             qk_tiles=2 * DN_K_W // tn, qkv_tiles=qkv_tiles, ctx_len=ctx_len)
    return pl.pallas_call(
        kern,
        grid=(m // tm, n // tn),
        in_specs=[pl.BlockSpec((DN_HALO, D_MODEL), lambda i, j: (jnp.maximum(i * halo_per_tile - 1, 0), 0)),
                  pl.BlockSpec((tm, D_MODEL), lambda i, j: (i, 0)),
                  pl.BlockSpec((DN_HALO, D_MODEL), lambda i, j: (jnp.minimum((i + 1) * halo_per_tile, n_halo - 1), 0)),
                  pl.BlockSpec((1, D_MODEL), lambda i, j: (0, 0)),
                  pl.BlockSpec((None, 1, 3 * D_MODEL), lambda i, j: (_mod_row(i, lat_tiles, batch), 0, 0)),
                  pl.BlockSpec((None, D_MODEL, tn), lambda i, j: (layer, 0, j)),
                  pl.BlockSpec((None, D_MODEL, n_side), lambda i, j: (layer, 0, n // n_side)),
                  pl.BlockSpec((SUBLANES, tn), lambda i, j: (0, jnp.minimum(j, qkv_tiles - 1))),
                  pl.BlockSpec((1, n_side), lambda i, j: (0, 0)),
                  pl.BlockSpec((1, n_side), lambda i, j: (0, 0))],
        out_specs=[pl.BlockSpec((tm, tn), lambda i, j: (i, jnp.minimum(j, qkv_tiles - 1))),
                   pl.BlockSpec((tm, tn), lambda i, j: (i, jnp.maximum(j - qkv_tiles, 0))),
                   pl.BlockSpec((tm, n_side), lambda i, j: (i, 0))],
        out_shape=[jax.ShapeDtypeStruct((m, DN_QKV_W), F32),
                   jax.ShapeDtypeStruct((m, DN_V_W), BF16),
                   jax.ShapeDtypeStruct((m, n_side), F32)],
        scratch_shapes=[pltpu.VMEM((tm + 2 * DN_HALO, D_MODEL), BF16),
                        pltpu.VMEM((2, tm + 2 * DN_HALO, 2 * HEAD_DIM), F32)],
        compiler_params=_cparams(2),
        name="inproj_dn",
    )(x, x, x, norm_w.reshape(1, D_MODEL), mod, w_stack_bf16, w_stack_bf16, conv_w, alog_vec, dtb_vec)


def _write_gates(ba, alog, dtb, o_ref):
    tm = ba.shape[0]
    lane = lax.broadcasted_iota(jnp.int32, (DN_CHUNK, LANES), 1)
    is_beta = (lane // DN_V_HEADS) % 2 == 0
    is_fwd = lane < 2 * DN_V_HEADS
    z = ba + dtb
    softplus = jnp.maximum(z, 0.0) + jnp.log1p(jnp.exp(-jnp.abs(z)))
    g = -jnp.exp(alog) * softplus
    r = lax.broadcasted_iota(jnp.int32, (DN_CHUNK, DN_CHUNK), 0)
    c = lax.broadcasted_iota(jnp.int32, (DN_CHUNK, DN_CHUNK), 1)
    tri_lo = (r >= c).astype(BF16)
    tri_up = (r <= c).astype(BF16)
    beta = jax.nn.sigmoid(ba)
    for ch in range(tm // DN_CHUNK):
        rows = slice(ch * DN_CHUNK, (ch + 1) * DN_CHUNK)
        gch = g[rows]
        g1 = gch.astype(BF16)
        r1 = gch - g1.astype(F32)
        g2 = r1.astype(BF16)
        g3 = (r1 - g2.astype(F32)).astype(BF16)
        pre = _dot(tri_lo, g1) + _dot(tri_lo, g2) + _dot(tri_lo, g3)
        suf = _dot(tri_up, g1) + _dot(tri_up, g2) + _dot(tri_up, g3)
        gc = jnp.where(is_fwd, pre, suf)
        o_ref[rows, :] = jnp.where(is_beta, beta[rows], gc)


DN_CHAIN_GROUP = 16
DN_KH_PER_STEP = 16


def _dncore_kernel(qf_ref, kf_ref, vf_ref, gf_ref, qb_ref, kb_ref, vb_ref, gb_ref, of_ref, ob_ref, s_ref):
    C = DN_CHUNK
    nvh = 2 * DN_KH_PER_STEP

    @pl.when(pl.program_id(2) == 0)
    def _():
        s_ref[...] = jnp.zeros(s_ref.shape, F32)

    lane = lax.broadcasted_iota(jnp.int32, (C, 2 * C), 1)
    left = lane < C
    row = lax.broadcasted_iota(jnp.int32, (C, 2 * C), 0)
    colp = lane % C
    left_sq = lax.broadcasted_iota(jnp.int32, (2 * C, 2 * C), 1) < C
    eye2 = (row == colp).astype(F32)

    def blockdiag(p):
        z = jnp.zeros_like(p)
        return jnp.concatenate([jnp.where(left, p, z), jnp.where(left, z, p)], axis=0)

    def packed_mm(a, b):
        return _dot(a.astype(BF16), blockdiag(b.astype(BF16)))

    dirs = ((qf_ref, kf_ref, vf_ref, gf_ref, of_ref), (qb_ref, kb_ref, vb_ref, gb_ref, ob_ref))
    chains = [(d, kh) for d in range(2) for kh in range(DN_KH_PER_STEP)]
    G = [dirs[d][3][...] for d in range(2)]
    GT = [jnp.concatenate([g, g], axis=0).T for g in G]

    for g0 in range(0, len(chains), DN_CHAIN_GROUP):
        group = chains[g0:g0 + DN_CHAIN_GROUP]
        st = []
        for d, kh in group:
            q_ref, k_ref, v_ref, _, _ = dirs[d]
            base_beta = d * 2 * nvh
            base_gc = base_beta + nvh
            lv0 = 2 * kh
            q = q_ref[:, kh * HEAD_DIM:(kh + 1) * HEAD_DIM]
            k = k_ref[:, kh * HEAD_DIM:(kh + 1) * HEAD_DIM]
            k2 = jnp.concatenate([k, k], axis=0)
            kT2 = k2.T
            gram = _dot(jnp.concatenate([q, k], axis=0).astype(BF16), kT2.astype(BF16))
            b0, b1 = G[d][:, base_beta + lv0:base_beta + lv0 + 1], G[d][:, base_beta + lv0 + 1:base_beta + lv0 + 2]
            c0, c1 = G[d][:, base_gc + lv0:base_gc + lv0 + 1], G[d][:, base_gc + lv0 + 1:base_gc + lv0 + 2]
            r0, r1 = GT[d][base_gc + lv0:base_gc + lv0 + 1, :], GT[d][base_gc + lv0 + 1:base_gc + lv0 + 2, :]
            st.append(dict(d=d, lv0=lv0, q=q, k2=k2, kT2=kT2, gram=gram, b0=b0, b1=b1, c0=c0, c1=c1, r0=r0, r1=r1))

        for s in st:
            d = s["d"]
            incl = (row >= colp) if d == 0 else (row <= colp)
            strict = (row > colp) if d == 0 else (row < colp)
            gcol_p = jnp.where(left, s["c0"], s["c1"])
            grow_p = jnp.where(left[0:1], s["r0"], s["r1"])
            beta_p = jnp.where(left, s["b0"], s["b1"])
            dec = jnp.exp(jnp.where(incl, gcol_p - grow_p, -1e30))
            s["dec"] = dec
            s["attn"] = s["gram"][0:C] * dec
            s["L"] = jnp.where(strict, s["gram"][C:2 * C] * dec, 0.0) * beta_p

        for s in st:
            n1 = jnp.where(jnp.logical_and(row // 2 == colp // 2, row != colp), s["L"], 0.0)
            s["X"] = eye2 - n1
        blk = 2
        while blk < C:
            mask = jnp.logical_and(row // (2 * blk) == colp // (2 * blk), row // blk != colp // blk)
            for s in st:
                s["Y"] = packed_mm(s["X"], jnp.where(mask, s["L"], 0.0))
            for s in st:
                s["X"] = s["X"] - packed_mm(s["Y"], s["X"])
            blk *= 2

        for s in st:
            d, lv0 = s["d"], s["lv0"]
            v_ref = dirs[d][2]
            beta_r = jnp.concatenate([s["b0"], s["b1"]], axis=0)
            egc_r = jnp.exp(jnp.concatenate([s["c0"], s["c1"]], axis=0))
            v2 = jnp.concatenate([v_ref[:, lv0 * HEAD_DIM:(lv0 + 1) * HEAD_DIM],
                                  v_ref[:, (lv0 + 1) * HEAD_DIM:(lv0 + 2) * HEAD_DIM]], axis=0)
            rhs = jnp.concatenate([v2 * beta_r, s["k2"] * (beta_r * egc_r)], axis=1)
            s["sol"] = _dot(blockdiag(s["X"]).astype(BF16), rhs.astype(BF16))
            s["egc"] = egc_r
            s["qg2"] = jnp.concatenate([s["q"], s["q"]], axis=0) * egc_r

        for s in st:
            d, lv0 = s["d"], s["lv0"]
            w2 = s["sol"][:, HEAD_DIM:]
            s["ws"] = []
            for r in range(2):
                lhs = jnp.concatenate([w2[r * C:(r + 1) * C], s["qg2"][r * C:(r + 1) * C]], axis=0).astype(BF16)
                s["ws"].append(_dot(lhs, s_ref[d, lv0 + r].astype(BF16)))

        for s in st:
            u2 = s["sol"][:, 0:HEAD_DIM]
            vn2 = jnp.concatenate([u2[r * C:(r + 1) * C] - s["ws"][r][0:C] for r in range(2)], axis=0).astype(BF16)
            s["vn2"] = vn2
            s["o2"] = jnp.concatenate([s["ws"][r][C:2 * C] for r in range(2)], axis=0) + _dot(
                blockdiag(s["attn"]).astype(BF16), vn2)

        for s in st:
            d, lv0 = s["d"], s["lv0"]
            o_ref = dirs[d][4]
            last = C - 1 if d == 0 else 0
            kdT_p = s["kT2"] * s["dec"][last:last + 1, :]
            zkd = jnp.zeros_like(kdT_p)
            for r in range(2):
                kd_r = jnp.where(left_sq if r == 0 else jnp.logical_not(left_sq), kdT_p, zkd).astype(BF16)
                glr = jnp.broadcast_to(s["egc"][r * C + last:r * C + last + 1, :], (HEAD_DIM, HEAD_DIM))
                s_ref[d, lv0 + r] = s_ref[d, lv0 + r] * glr + _dot(kd_r, s["vn2"])
                o_ref[:, (lv0 + r) * HEAD_DIM:(lv0 + r + 1) * HEAD_DIM] = s["o2"][r * C:(r + 1) * C].astype(o_ref.dtype)


def _dn_core(qkv, gates, batch, seq, ctx_len):
    m = qkv.shape[0]
    C = DN_CHUNK
    n_lat = seq // C
    n_ctx = ctx_len // C
    n_steps = n_ctx + n_lat
    ctx0 = batch * n_lat
    khs = DN_KH_PER_STEP
    qw = khs * HEAD_DIM
    vw = 2 * khs * HEAD_DIM
    k_blk0 = DN_K_W // qw
    v_blk0 = 2 * DN_K_W // vw

    def rf(b, t):
        return jnp.where(t < n_ctx, ctx0 + b * n_ctx + t, b * n_lat + (t - n_ctx))

    def rb(b, t):
        return jnp.where(t < n_ctx, ctx0 + b * n_ctx + (n_ctx - 1 - t), b * n_lat + (n_lat - 1 - (t - n_ctx)))

    def specs(rfun):
        return [pl.BlockSpec((C, qw), lambda b, h, t: (rfun(b, t), h)),
                pl.BlockSpec((C, qw), lambda b, h, t: (rfun(b, t), k_blk0 + h)),
                pl.BlockSpec((C, vw), lambda b, h, t: (rfun(b, t), v_blk0 + h)),
                pl.BlockSpec((C, LANES), lambda b, h, t: (rfun(b, t), h))]

    return pl.pallas_call(
        _dncore_kernel,
        grid=(batch, DN_K_HEADS // khs, n_steps),
        in_specs=specs(rf) + specs(rb),
        out_specs=[pl.BlockSpec((C, vw), lambda b, h, t: (rf(b, t), h)),
                   pl.BlockSpec((C, vw), lambda b, h, t: (rb(b, t), h))],
        out_shape=[jax.ShapeDtypeStruct((m, DN_V_W), BF16), jax.ShapeDtypeStruct((m, DN_V_W), BF16)],
        scratch_shapes=[pltpu.VMEM((2, 2 * khs, HEAD_DIM, HEAD_DIM), F32)],
        compiler_params=_cparams(3),
        name="dn_core",
    )(qkv, qkv, qkv, gates, qkv, qkv, qkv, gates)


DN_OUT_ROW_TILE = 256


def _dn_out_kernel(of_ref, ob_ref, z_ref, nw_ref, w_ref, x_ref, gate_ref, *rest):
    o_ref, y_ref = rest[-2:]
    for h in range(of_ref.shape[1] // HEAD_DIM):
        sl = slice(h * HEAD_DIM, (h + 1) * HEAD_DIM)
        o = of_ref[:, sl].astype(F32) + ob_ref[:, sl].astype(F32)
        n = o * lax.rsqrt(jnp.mean(o * o, axis=-1, keepdims=True) + EPS) * nw_ref[...]
        y_ref[:, sl] = (n * _silu(z_ref[:, sl].astype(F32))).astype(BF16)
    y = x_ref[...] + gate_ref[...] * _dot(y_ref[...], w_ref[...])
    if len(rest) == 3:
        y = y * lax.rsqrt(jnp.mean(y * y, axis=-1, keepdims=True) + EPS) * rest[0][...]
    o_ref[...] = y


def _dn_out(o_f, o_b, z, norm_w, w_out_stack_bf16, layer, x, mod, batch, seq, final_norm_w=None):
    tm = DN_OUT_ROW_TILE
    m = x.shape[0] if final_norm_w is None else batch * seq
    lat_tiles = seq // tm
    extra_specs, extra_args = [], []
    if final_norm_w is not None:
        extra_specs = [pl.BlockSpec((1, D_MODEL), lambda i: (0, 0))]
        extra_args = [final_norm_w.reshape(1, D_MODEL)]
    return pl.pallas_call(
        _dn_out_kernel,
        grid=(m // tm,),
        in_specs=[pl.BlockSpec((tm, DN_V_W), lambda i: (i, 0)),
                  pl.BlockSpec((tm, DN_V_W), lambda i: (i, 0)),
                  pl.BlockSpec((tm, DN_V_W), lambda i: (i, 0)),
                  pl.BlockSpec((1, HEAD_DIM), lambda i: (0, 0)),
                  pl.BlockSpec((None, DN_V_W, D_MODEL), lambda i: (layer, 0, 0), pipeline_mode=pl.Buffered(1)),
                  pl.BlockSpec((tm, D_MODEL), lambda i: (i, 0)),
                  pl.BlockSpec((None, 1, D_MODEL), lambda i: (_mod_row(i, lat_tiles, batch), 0, 2))] + extra_specs,
        out_specs=pl.BlockSpec((tm, D_MODEL), lambda i: (i, 0)),
        out_shape=jax.ShapeDtypeStruct((m, D_MODEL), F32),
        scratch_shapes=[pltpu.VMEM((tm, DN_V_W), BF16)],
        compiler_params=_cparams(1),
        name="dn_out",
    )(o_f, o_b, z, norm_w.reshape(1, HEAD_DIM), w_out_stack_bf16, x, mod, *extra_args)


def _final_norm_kernel(x_ref, w_ref, o_ref):
    x = x_ref[...]
    o_ref[...] = x * lax.rsqrt(jnp.mean(x * x, axis=-1, keepdims=True) + EPS) * w_ref[...]


def _final_norm(x, w, rows):
    tm = ROW_TILE
    return pl.pallas_call(
        _final_norm_kernel,
        grid=(rows // tm,),
        in_specs=[pl.BlockSpec((tm, D_MODEL), lambda i: (i, 0)),
                  pl.BlockSpec((1, D_MODEL), lambda i: (0, 0))],
        out_specs=pl.BlockSpec((tm, D_MODEL), lambda i: (i, 0)),
        out_shape=jax.ShapeDtypeStruct((rows, D_MODEL), F32),
        compiler_params=_cparams(1),
        name="final_norm",
    )(x, w.reshape(1, D_MODEL))


def kernel(x, c, ctx, c_ctx, norm_w, ada_w, ada_b, ab_w_in, ab_w_out, sgu_w, sgu_b, q_norm_w, k_norm_w,
           dn_w_in, dn_conv_w, dn_a_log, dn_dt_bias, dn_norm_w, dn_w_out, final_norm_w):
    batch, seq, _ = x.shape
    ctx_len = ctx.shape[1]
    depth = norm_w.shape[0]
    assert ctx_len == CONV_ROW_TILE and seq % ROW_TILE == 0 and batch * ctx_len == ROW_TILE
    n_lat_rows = batch * seq

    x_parts = (x.reshape(n_lat_rows, D_MODEL), ctx.reshape(batch * ctx_len, D_MODEL))
    ab_w_in_bf, ab_w_out_bf, sgu_w_bf = ab_w_in.astype(BF16), ab_w_out.astype(BF16), sgu_w.astype(BF16)
    dn_w_in_bf, dn_w_out_bf = dn_w_in.astype(BF16), dn_w_out.astype(BF16)
    cond = jnp.zeros((SUBLANES, D_MODEL), F32).at[0:batch].set(c).at[batch].set(c_ctx)
    mods = _ada_mod(cond, ada_w, ada_b)
    cos_tab, sin_tab = _rope_tables(seq, CONV_ROW_TILE)

    for i in range(depth):
        j = i // 2
        mod = mods[i].reshape(SUBLANES, 1, 3 * D_MODEL)
        if i % 2 == 0:
            proj = _inproj(x_parts, norm_w[i], mod, ab_w_in_bf, j, batch, seq)
            qr, kr, vr = _qkprep(proj, cos_tab, sin_tab, q_norm_w[j], k_norm_w[j], batch, seq)
            ao_lat = _attention(qr, kr, vr, batch, seq, ctx_len, True)
            ao_ctx = _attention(qr, kr, vr, batch, seq, ctx_len, False)
            x_parts = (_ab_out(proj, ao_lat, ao_ctx, sgu_w_bf, sgu_b[j].T, ab_w_out_bf, j, x_parts, mod, batch, seq),)
        else:
            conv_w = jnp.zeros((SUBLANES, DN_QKV_W), F32).at[0:CONV_K].set(dn_conv_w[j])
            assert len(x_parts) == 1
            zeros = jnp.zeros((2, DN_V_HEADS), F32)
            alog_vec = jnp.concatenate([zeros, dn_a_log[j]], axis=1).reshape(1, LANES)
            dtb_vec = jnp.concatenate([zeros, dn_dt_bias[j]], axis=1).reshape(1, LANES)
            qkv, z, gates = _inproj_dn(x_parts[0], norm_w[i], mod, dn_w_in_bf, j, conv_w, alog_vec, dtb_vec, batch, seq,
                                       ctx_len)
            n_hg = DN_K_HEADS // DN_KH_PER_STEP
            nvh = 2 * DN_KH_PER_STEP
            g4 = gates
            if n_hg > 1:
                g4 = gates.reshape(-1, 4, n_hg, nvh).transpose(0, 2, 1, 3).reshape(-1, n_hg, 4 * nvh)
                g4 = jnp.pad(g4, ((0, 0), (0, 0), (0, LANES - 4 * nvh))).reshape(-1, n_hg * LANES)
            o_f, o_b = _dn_core(qkv, g4, batch, seq, ctx_len)
            x_parts = (_dn_out(o_f, o_b, z, dn_norm_w[j], dn_w_out_bf, j, x_parts[0], mod, batch, seq,
                               final_norm_w=final_norm_w if i == depth - 1 else None),)

    xs = x_parts[0]
    if depth % 2 == 1:
        xs = _final_norm(xs, final_norm_w, n_lat_rows)
    return xs.reshape(batch, seq, D_MODEL)
```
